```python
import math
import jax, jax.numpy as jnp
from jax import lax
import numpy as np

D_MODEL = 1024
BATCH = 8
SEQ = 2048
DEPTH = 1
DEC_BATCH = 32
DEC_SEQ = 32
PAST_LEN = 1024

CHUNK = 64
RMS_EPS = 1e-6
SSD_EXPAND = 2
D_INNER = SSD_EXPAND * D_MODEL
SSD_HEADDIM = 64
SSD_HEADS = D_INNER // SSD_HEADDIM
SSD_GROUPS = 4
SSD_STATE = 128
CONV_W = 4
CONV_CH = D_INNER + 2 * SSD_GROUPS * SSD_STATE
SSD_NORM_EPS = 1e-5
POOL_DIM = D_MODEL
POOL_WINDOWS = (2, 4, 8, 16)
POOL_GROUPS = len(POOL_WINDOWS)
POOL_GDIM = POOL_DIM // POOL_GROUPS
POOL_STATE = max(POOL_WINDOWS) - 1
N_BRANCH = 2
IN_DIM = D_INNER + CONV_CH + SSD_HEADS + POOL_DIM + N_BRANCH * D_MODEL
N_EXPERTS = 256
TOP_K = 8
N_EXPERT_GROUPS = 8
TOPK_GROUPS = 4
MOE_FF = 256
SHARED_FF = 256
ROUTED_SCALE = 2.5
EXPERT_BLOCK = 64

kernel_name = 'hybrid_ssd_pool_moe_stream_step'


def rmsnorm(x, g, eps=RMS_EPS):
    xf = x.astype(jnp.float32)
    y = xf * lax.rsqrt(jnp.mean(xf * xf, axis=-1, keepdims=True) + eps)
    return (y * g.astype(jnp.float32)).astype(x.dtype)


def causal_dwconv(u, state, w, b):
    L = u.shape[1]
    up = jnp.concatenate([state.astype(u.dtype), u], axis=1)
    out = up[:, 0:L] * w[0]
    for k in range(1, CONV_W):
        out = out + up[:, k:k + L] * w[k]
    return jax.nn.silu(out + b), up[:, up.shape[1] - (CONV_W - 1):]


def ssd_chunked(x, dt, a, bm, cm, init_state):
    b, l, h, p = x.shape
    g, n = bm.shape[2], bm.shape[3]
    r = h // g
    q = min(CHUNK, l)
    nc = l // q
    xc = x.reshape(b, nc, q, g, r, p)
    dtc = dt.reshape(b, nc, q, g, r)
    bc = bm.reshape(b, nc, q, g, n)
    cc = cm.reshape(b, nc, q, g, n)
    cs = jnp.cumsum(dtc * a.reshape(g, r), axis=2)
    causal = jnp.tril(jnp.ones((q, q), dtype=bool))[None, None, :, :, None, None]
    seg = cs[:, :, :, None] - cs[:, :, None, :]
    decay_qk = jnp.exp(jnp.where(causal, seg, -jnp.inf))
    xdt = xc * dtc[..., None]
    cb = jnp.einsum('bcqgn,bckgn->bcqkg', cc, bc)
    y_diag = jnp.einsum('bcqkg,bcqkgr,bckgrp->bcqgrp', cb, decay_qk, xdt)
    decay_end = jnp.exp(cs[:, :, -1:] - cs)
    chunk_states = jnp.einsum('bckgn,bckgr,bckgrp->bcgrpn', bc, decay_end, xdt)
    chunk_decay = jnp.exp(cs[:, :, -1])

    def step(s, inp):
        st, dec = inp
        return s * dec[..., None, None] + st, s

    final, prev = lax.scan(step, init_state.reshape(b, g, r, p, n),
                           (jnp.moveaxis(chunk_states, 1, 0), jnp.moveaxis(chunk_decay, 1, 0)))
    prev = jnp.moveaxis(prev, 0, 1)
    y_off = jnp.einsum('bcqgn,bcgrpn,bcqgr->bcqgrp', cc, prev, jnp.exp(cs))
    y = (y_diag + y_off).reshape(b, l, h, p)
    return y, final.reshape(b, h, p, n)


def gated_group_rmsnorm(y, z, g):
    v = y.astype(jnp.float32) * jax.nn.silu(z.astype(jnp.float32))
    shp = v.shape
    v = v.reshape(shp[:-1] + (SSD_GROUPS, shp[-1] // SSD_GROUPS))
    v = v * lax.rsqrt(jnp.mean(v * v, axis=-1, keepdims=True) + SSD_NORM_EPS)
    return v.reshape(shp) * g.astype(jnp.float32)


def multiscale_pool(u, state, pos0):
    L = u.shape[1]
    up = jnp.concatenate([state.astype(u.dtype), u], axis=1)
    uf = up.astype(jnp.float32)
    cs = jnp.concatenate([jnp.zeros_like(uf[:, :1]), jnp.cumsum(uf, axis=1)], axis=1)
    end = cs[:, POOL_STATE + 1:POOL_STATE + 1 + L]
    pos = pos0 + jnp.arange(L)
    outs = []
    for gi, w in enumerate(POOL_WINDOWS):
        lo, hi = gi * POOL_GDIM, (gi + 1) * POOL_GDIM
        start = cs[:, POOL_STATE + 1 - w:POOL_STATE + 1 - w + L, lo:hi]
        cnt = jnp.minimum(w, pos + 1).astype(jnp.float32)[None, :, None]
        outs.append((end[..., lo:hi] - start) / cnt)
    mean = jnp.concatenate(outs, axis=-1)
    return (mean - u.astype(jnp.float32)).astype(u.dtype), up[:, up.shape[1] - POOL_STATE:]


def mixer(h, conv_state, ssm_state, pool_state, pos0, p):
    bsz, L, _ = h.shape
    proj = h @ p['w_in']
    o1 = D_INNER
    o2 = o1 + CONV_CH
    o3 = o2 + SSD_HEADS
    o4 = o3 + POOL_DIM
    z = proj[..., :o1]
    xbc = proj[..., o1:o2]
    dt_raw = proj[..., o2:o3]
    u = proj[..., o3:o4]
    gate_logits = proj[..., o4:]
    xbc, new_conv = causal_dwconv(xbc, conv_state, p['conv_w'], p['conv_b'])
    gn = SSD_GROUPS * SSD_STATE
    xs = xbc[..., :D_INNER].reshape(bsz, L, SSD_HEADS, SSD_HEADDIM).astype(jnp.float32)
    bm = xbc[..., D_INNER:D_INNER + gn].reshape(bsz, L, SSD_GROUPS, SSD_STATE).astype(jnp.float32)
    cm = xbc[..., D_INNER + gn:].reshape(bsz, L, SSD_GROUPS, SSD_STATE).astype(jnp.float32)
    dt = jax.nn.softplus(dt_raw.astype(jnp.float32) + p['dt_bias'].astype(jnp.float32))
    a = -jnp.exp(p['a_log'].astype(jnp.float32))
    y, new_ssm = ssd_chunked(xs, dt, a, bm, cm, ssm_state.astype(jnp.float32))
    y = y + xs * p['d_skip'].astype(jnp.float32)[:, None]
    y = gated_group_rmsnorm(y.reshape(bsz, L, D_INNER), z, p['ssd_norm_g']).astype(h.dtype)
    a_branch = y @ p['w_ssd_out']
    pooled, new_pool = multiscale_pool(u, pool_state, pos0)
    b_branch = jnp.einsum('blgc,gcd->blgd', pooled.reshape(bsz, L, POOL_GROUPS, POOL_GDIM),
                          p['pool_w']).reshape(bsz, L, POOL_DIM) * p['pool_scale']
    gates = jax.nn.sigmoid(gate_logits.astype(jnp.float32)).astype(h.dtype)
    merged = gates[..., :D_MODEL] * a_branch + gates[..., D_MODEL:] * b_branch
    return merged @ p['w_out'], new_conv, new_ssm.astype(ssm_state.dtype), new_pool


def routed_experts(t, eidx, wts, wg, wu, wd):
    T, K = eidx.shape
    A = T * K
    flat_e = eidx.reshape(-1)
    flat_t = jnp.repeat(jnp.arange(T, dtype=jnp.int32), K)
    flat_w = wts.reshape(-1)
    order = jnp.argsort(flat_e)
    se, st, sw = flat_e[order], flat_t[order], flat_w[order]
    counts = jnp.bincount(flat_e, length=N_EXPERTS)
    starts = jnp.cumsum(counts) - counts
    padded = ((counts + EXPERT_BLOCK - 1) // EXPERT_BLOCK) * EXPERT_BLOCK
    pend = jnp.cumsum(padded)
    pstarts = pend - padded
    dest = pstarts[se] + (jnp.arange(A) - starts[se])
    n_blocks = -(-(A + N_EXPERTS * (EXPERT_BLOCK - 1)) // EXPERT_BLOCK)
    S = n_blocks * EXPERT_BLOCK
    slot_tok = jnp.full((S,), T, dtype=jnp.int32).at[dest].set(st)
    slot_w = jnp.zeros((S,), jnp.float32).at[dest].set(sw)
    block_e = jnp.minimum(jnp.searchsorted(pend, jnp.arange(n_blocks) * EXPERT_BLOCK, side='right'),
                          N_EXPERTS - 1)
    t_pad = jnp.concatenate([t, jnp.zeros((1, t.shape[1]), t.dtype)], axis=0)

    def step(acc, blk):
        tok, wt, e = blk
        xb = t_pad[tok]
        hb = jax.nn.silu(xb @ wg[e]) * (xb @ wu[e])
        yb = (hb @ wd[e]).astype(jnp.float32) * wt[:, None]
        return acc.at[tok].add(yb), None

    acc, _ = lax.scan(step, jnp.zeros((T + 1, t.shape[1]), jnp.float32),
                      (slot_tok.reshape(n_blocks, EXPERT_BLOCK), slot_w.reshape(n_blocks, EXPERT_BLOCK), block_e))
    return acc[:T]


def moe(h, p):
    shp = h.shape
    t = h.reshape(-1, shp[-1])
    T = t.shape[0]
    scores = jax.nn.sigmoid((t @ p['w_router']).astype(jnp.float32))
    biased = scores + p['router_bias'].astype(jnp.float32)
    per_g = N_EXPERTS // N_EXPERT_GROUPS
    gscore = lax.top_k(biased.reshape(T, N_EXPERT_GROUPS, per_g), 2)[0].sum(-1)
    _, gidx = lax.top_k(gscore, TOPK_GROUPS)
    gmask = jax.nn.one_hot(gidx, N_EXPERT_GROUPS, dtype=jnp.float32).sum(1) > 0
    emask = jnp.repeat(gmask, per_g, axis=1)
    _, eidx = lax.top_k(jnp.where(emask, biased, -jnp.inf), TOP_K)
    w = jnp.take_along_axis(scores, eidx, axis=1)
    w = w / (jnp.sum(w, axis=-1, keepdims=True) + 1e-20) * ROUTED_SCALE
    routed = routed_experts(t, eidx, w, p['moe_w_gate'], p['moe_w_up'], p['moe_w_down'])
    shared = (jax.nn.silu(t @ p['shared_w_gate']) * (t @ p['shared_w_up'])) @ p['shared_w_down']
    return (routed.astype(t.dtype) + shared).reshape(shp)


def layer(x, c, conv_state, ssm_state, pool_state, pos0, p):
    mod = jax.nn.silu(c) @ p['w_ada'] + p['b_ada']
    sh1, sc1, g1, sh2, sc2, g2 = [m[:, None, :] for m in jnp.split(mod, 6, axis=-1)]
    h = rmsnorm(x, p['ln1_g']) * (1 + sc1) + sh1
    mix, new_conv, new_ssm, new_pool = mixer(h, conv_state, ssm_state, pool_state, pos0, p)
    x = x + g1 * mix
    h = rmsnorm(x, p['ln2_g']) * (1 + sc2) + sh2
    x = x + g2 * moe(h, p)
    return x, new_conv, new_ssm, new_pool


def setup_inputs(seed: int = 0) -> dict:
    key = jax.random.key(seed)
    ks = jax.random.split(key, 40)
    f32 = jnp.float32

    def nrm(k, shape, scale):
        return jax.random.normal(k, shape, f32) * scale

    def gain(k, shape):
        return 1.0 + 0.02 * jax.random.normal(k, shape, f32)

    dt0 = jnp.exp(jax.random.uniform(ks[14], (DEPTH, SSD_HEADS), f32, math.log(1e-3), math.log(1e-1)))
    return {
        'x_prompt': nrm(ks[0], (BATCH, SEQ, D_MODEL), 1.0),
        'x_sample': nrm(ks[1], (DEC_BATCH, DEC_SEQ, D_MODEL), 1.0),
        'state_ssm': nrm(ks[2], (DEPTH, DEC_BATCH, SSD_HEADS, SSD_HEADDIM, SSD_STATE), 0.1),
        'state_conv': nrm(ks[3], (DEPTH, DEC_BATCH, CONV_W - 1, CONV_CH), 1.0),
        'state_pool': nrm(ks[4], (DEPTH, DEC_BATCH, POOL_STATE, POOL_DIM), 1.0),
        'c_prompt': nrm(ks[5], (BATCH, D_MODEL), 1.0),
        'c_sample': nrm(ks[6], (DEC_BATCH, D_MODEL), 1.0),
        'ln1_g': gain(ks[7], (DEPTH, D_MODEL)),
        'ln2_g': gain(ks[8], (DEPTH, D_MODEL)),
        'w_ada': nrm(ks[9], (DEPTH, D_MODEL, 6 * D_MODEL), 0.5 * D_MODEL ** -0.5),
        'b_ada': nrm(ks[10], (DEPTH, 6 * D_MODEL), 0.02),
        'w_in': nrm(ks[11], (DEPTH, D_MODEL, IN_DIM), D_MODEL ** -0.5),
        'conv_w': nrm(ks[12], (DEPTH, CONV_W, CONV_CH), CONV_W ** -0.5),
        'conv_b': nrm(ks[13], (DEPTH, CONV_CH), 0.02),
        'dt_bias': dt0 + jnp.log(-jnp.expm1(-dt0)),
        'a_log': jnp.log(jax.random.uniform(ks[15], (DEPTH, SSD_HEADS), f32, 1.0, 16.0)),
        'd_skip': gain(ks[16], (DEPTH, SSD_HEADS)),
        'ssd_norm_g': gain(ks[17], (DEPTH, D_INNER)),
        'w_ssd_out': nrm(ks[18], (DEPTH, D_INNER, D_MODEL), D_INNER ** -0.5),
        'pool_w': nrm(ks[19], (DEPTH, POOL_GROUPS, POOL_GDIM, POOL_GDIM), POOL_GDIM ** -0.5),
        'pool_scale': gain(ks[20], (DEPTH, POOL_DIM)),
        'w_out': nrm(ks[21], (DEPTH, D_MODEL, D_MODEL), D_MODEL ** -0.5),
        'w_router': nrm(ks[22], (DEPTH, D_MODEL, N_EXPERTS), D_MODEL ** -0.5),
        'router_bias': nrm(ks[23], (DEPTH, N_EXPERTS), 0.01),
        'moe_w_gate': nrm(ks[24], (DEPTH, N_EXPERTS, D_MODEL, MOE_FF), D_MODEL ** -0.5),
        'moe_w_up': nrm(ks[25], (DEPTH, N_EXPERTS, D_MODEL, MOE_FF), D_MODEL ** -0.5),
        'moe_w_down': nrm(ks[26], (DEPTH, N_EXPERTS, MOE_FF, D_MODEL), MOE_FF ** -0.5),
        'shared_w_gate': nrm(ks[27], (DEPTH, D_MODEL, SHARED_FF), D_MODEL ** -0.5),
        'shared_w_up': nrm(ks[28], (DEPTH, D_MODEL, SHARED_FF), D_MODEL ** -0.5),
        'shared_w_down': nrm(ks[29], (DEPTH, SHARED_FF, D_MODEL), SHARED_FF ** -0.5),
        'final_g': gain(ks[30], (D_MODEL,)),
    }


def reference(x_prompt, x_sample, state_ssm, state_conv, state_pool, c_prompt, c_sample,
              ln1_g, ln2_g, w_ada, b_ada, w_in, conv_w, conv_b, dt_bias, a_log, d_skip,
              ssd_norm_g, w_ssd_out, pool_w, pool_scale, w_out, w_router, router_bias,
              moe_w_gate, moe_w_up, moe_w_down, shared_w_gate, shared_w_up, shared_w_down, final_g):
    bp = x_prompt.shape[0]
    xp, xs = x_prompt, x_sample
    ssm_p, conv_p, pool_p, ssm_s, conv_s, pool_s = [], [], [], [], [], []
    for l in range(DEPTH):
        p = {
            'ln1_g': ln1_g[l], 'ln2_g': ln2_g[l], 'w_ada': w_ada[l], 'b_ada': b_ada[l],
            'w_in': w_in[l], 'conv_w': conv_w[l], 'conv_b': conv_b[l], 'dt_bias': dt_bias[l],
            'a_log': a_log[l], 'd_skip': d_skip[l], 'ssd_norm_g': ssd_norm_g[l],
            'w_ssd_out': w_ssd_out[l], 'pool_w': pool_w[l], 'pool_scale': pool_scale[l],
            'w_out': w_out[l], 'w_router': w_router[l], 'router_bias': router_bias[l],
            'moe_w_gate': moe_w_gate[l], 'moe_w_up': moe_w_up[l], 'moe_w_down': moe_w_down[l],
            'shared_w_gate': shared_w_gate[l], 'shared_w_up': shared_w_up[l],
            'shared_w_down': shared_w_down[l],
        }
        zc = jnp.zeros((bp, CONV_W - 1, CONV_CH), x_prompt.dtype)
        zs = jnp.zeros((bp, SSD_HEADS, SSD_HEADDIM, SSD_STATE), jnp.float32)
        zp = jnp.zeros((bp, POOL_STATE, POOL_DIM), x_prompt.dtype)
        xp, nc, ns, npl = layer(xp, c_prompt, zc, zs, zp, 0, p)
        conv_p.append(nc); ssm_p.append(ns); pool_p.append(npl)
        xs, nc, ns, npl = layer(xs, c_sample, state_conv[l], state_ssm[l], state_pool[l], PAST_LEN, p)
        conv_s.append(nc); ssm_s.append(ns); pool_s.append(npl)
    y_prompt = rmsnorm(xp, final_g)
    y_sample = rmsnorm(xs, final_g)
    return (y_prompt, y_sample, jnp.stack(ssm_p), jnp.stack(conv_p), jnp.stack(pool_p),
            jnp.stack(ssm_s), jnp.stack(conv_s), jnp.stack(pool_s))
```

```python
import functools

import jax
import jax.numpy as jnp
from jax import lax
from jax.experimental import pallas as pl
from jax.experimental.pallas import tpu as pltpu

F32 = jnp.float32
BF16 = jnp.bfloat16
HIGHEST = lax.Precision.HIGHEST

RMS_EPS = 1e-6
SSD_NORM_EPS = 1e-5
CHUNK = 64
SSD_GROUPS = 4
POOL_WINDOWS = (2, 4, 8, 16)
PAST_LEN = 1024
TOP_K = 8
N_EXPERT_GROUPS = 8
TOPK_GROUPS = 4
ROUTED_SCALE = 2.5

LANES = 128
SUBLANES = 8
MOD_ROWS = 32
VMEM_LIMIT = 56 * 1024 * 1024
EXPERT_ROWS = 128


def _silu(x):
    return x * jax.nn.sigmoid(x)


def _softplus(x):
    return jnp.maximum(x, 0.0) + jnp.log1p(jnp.exp(-jnp.abs(x)))


def _row_tile(n, prefs):
    for t in prefs:
        if n % t == 0:
            return t
    return n


def _modulate(y, sc_ref, sh_ref):
    rows, d = y.shape
    y3 = y.reshape(rows // MOD_ROWS, MOD_ROWS, d)
    y3 = y3 * (1.0 + sc_ref[...][:, None, :]) + sh_ref[...][:, None, :]
    return y3.reshape(rows, d)


def _rms(x, g, eps):
    return x * lax.rsqrt(jnp.mean(x * x, axis=-1, keepdims=True) + eps) * g


def _ada_kernel(c_ref, w_ref, b_ref, o_ref):
    s = _silu(c_ref[...])
    o_ref[...] = jnp.dot(s, w_ref[...], preferred_element_type=F32, precision=HIGHEST) + b_ref[...]


def _ada(c_all, w_ada, b_ada):
    n, d = c_all.shape
    dout = w_ada.shape[1]
    tn = _row_tile(dout, (1024, 512, 256, 128))
    return pl.pallas_call(
        _ada_kernel,
        grid=(dout // tn,),
        in_specs=[pl.BlockSpec((n, d), lambda j: (0, 0)),
                  pl.BlockSpec((d, tn), lambda j: (0, j)),
                  pl.BlockSpec((1, tn), lambda j: (0, j))],
        out_specs=pl.BlockSpec((n, tn), lambda j: (0, j)),
        out_shape=jax.ShapeDtypeStruct((n, dout), F32),
        name="ada",
    )(c_all, w_ada, b_ada.reshape(1, dout))


def _inproj_kernel(x_ref, sc_ref, sh_ref, g_ref, w_ref, o_ref, h_scr):
    @pl.when(pl.program_id(1) == 0)
    def _():
        y = _rms(x_ref[...], g_ref[...], RMS_EPS)
        h_scr[...] = _modulate(y, sc_ref, sh_ref).astype(BF16)

    o_ref[...] = jnp.dot(h_scr[...], w_ref[...], preferred_element_type=F32)


def _inproj(x_all, sc_rows, sh_rows, g, wcat):
    t, d = x_all.shape
    n = wcat.shape[1]
    tm = _row_tile(t, (512, 256))
    tn = _row_tile(n, (1664, 1280, 1024, 640, 512, 256, 128))
    mr = tm // MOD_ROWS
    return pl.pallas_call(
        _inproj_kernel,
        grid=(t // tm, n // tn),
        in_specs=[pl.BlockSpec((tm, d), lambda i, j: (i, 0)),
                  pl.BlockSpec((mr, d), lambda i, j: (i, 0)),
                  pl.BlockSpec((mr, d), lambda i, j: (i, 0)),
                  pl.BlockSpec((1, d), lambda i, j: (0, 0)),
                  pl.BlockSpec((d, tn), lambda i, j: (0, j))],
        out_specs=pl.BlockSpec((tm, tn), lambda i, j: (i, j)),
        out_shape=jax.ShapeDtypeStruct((t, n), F32),
        scratch_shapes=[pltpu.VMEM((tm, d), BF16)],
        compiler_params=pltpu.CompilerParams(
            dimension_semantics=("parallel", "arbitrary"), vmem_limit_bytes=VMEM_LIMIT),
        name="inproj",
    )(x_all, sc_rows, sh_rows, g.reshape(1, d), wcat)


def _seq_kernel(z_ref, xp_ref, bc_ref, u_ref, gt_ref, dt_ref, x_ref, g1_ref,
                cst_ref, sst_ref, pst_ref, cw_ref, cb_ref, dtb_ref, alog_ref, dx_ref, ng_ref,
                wssd_ref, pw_ref, ps_ref, wout_ref,
                xo_ref, so_ref,
                cbuf, pbuf, s_scr, y_scr, xbc_scr, dt_scr,
                *, tq, q, pos0, heads, hd, ns, dm):
    j = pl.program_id(1)
    di = heads * hd
    gn = SSD_GROUPS * ns
    hpg = heads // SSD_GROUPS
    cw = cbuf.shape[0] - tq
    kw = cw_ref.shape[0]
    ph = pbuf.shape[0] - tq

    @pl.when(j == 0)
    def _():
        cbuf[cw - (kw - 1):cw, :] = cst_ref[0]
        pbuf[1:ph, :] = pst_ref[0]
        s_scr[...] = sst_ref[0]

    cbuf[cw:cw + tq, 0:di] = xp_ref[...]
    cbuf[cw:cw + tq, di:di + 2 * gn] = bc_ref[...]
    acc = cb_ref[...] + cbuf[cw:cw + tq, :] * cw_ref[kw - 1:kw, :]
    for k in range(kw - 1):
        off = cw - (kw - 1) + k
        acc = acc + cbuf[off:off + tq, :] * cw_ref[k:k + 1, :]
    xbc_scr[...] = _silu(acc)
    cbuf[cw - (kw - 1):cw, :] = cbuf[cw + tq - (kw - 1):cw + tq, :]

    dt_scr[...] = _softplus(dt_ref[...] + dtb_ref[...])
    a_row = -jnp.exp(alog_ref[...])

    r_i = lax.broadcasted_iota(jnp.int32, (q, q), 0)
    c_i = lax.broadcasted_iota(jnp.int32, (q, q), 1)
    causal = r_i >= c_i
    tril = causal.astype(F32)

    def chunk(c, carry):
        r0 = pl.multiple_of(c * q, q)
        dt_c = dt_scr[pl.ds(r0, q), :]
        cs = jnp.dot(tril, dt_c * a_row, preferred_element_type=F32, precision=HIGHEST)
        cs_end = cs[q - 1:q, :]
        wv = dt_c * jnp.exp(cs_end - cs)
        ecs = jnp.exp(cs)
        dec_end = jnp.exp(cs_end)
        cs_t = cs.T
        dt_t = dt_c.T
        for g in range(SSD_GROUPS):
            bg = xbc_scr[pl.ds(r0, q), di + g * ns:di + (g + 1) * ns]
            cg = xbc_scr[pl.ds(r0, q), di + gn + g * ns:di + gn + (g + 1) * ns]
            bgb = bg.astype(BF16)
            cb = lax.dot_general(cg.astype(BF16), bgb, (((1,), (1,)), ((), ())),
                                 preferred_element_type=F32)
            for r in range(hpg):
                h = g * hpg + r
                seg = cs[:, h:h + 1] - cs_t[h:h + 1, :]
                lm = jnp.exp(jnp.where(causal, seg, -jnp.inf))
                m = cb * lm * dt_t[h:h + 1, :]
                xh = xbc_scr[pl.ds(r0, q), h * hd:(h + 1) * hd]
                sh = s_scr[h * hd:(h + 1) * hd, :]
                yd = jnp.dot(m.astype(BF16), xh.astype(BF16), preferred_element_type=F32)
                yo = lax.dot_general((cg * ecs[:, h:h + 1]).astype(BF16), sh.astype(BF16),
                                     (((1,), (1,)), ((), ())), preferred_element_type=F32)
                y_scr[pl.ds(r0, q), h * hd:(h + 1) * hd] = yd + yo
                xw = (xh * wv[:, h:h + 1]).astype(BF16)
                upd = lax.dot_general(xw, bgb, (((0,), (0,)), ((), ())),
                                      preferred_element_type=F32)
                s_scr[h * hd:(h + 1) * hd, :] = sh * dec_end[:, h:h + 1] + upd
        return carry

    lax.fori_loop(0, tq // q, chunk, 0)

    @pl.when(j == pl.num_programs(1) - 1)
    def _():
        so_ref[0] = s_scr[...]

    y = y_scr[...] + xbc_scr[:, 0:di] * dx_ref[...]
    v = y * _silu(z_ref[...])
    gw = di // SSD_GROUPS
    parts = []
    for g in range(SSD_GROUPS):
        vg = v[:, g * gw:(g + 1) * gw]
        parts.append(vg * lax.rsqrt(jnp.mean(vg * vg, axis=-1, keepdims=True) + SSD_NORM_EPS))
    yn = (jnp.concatenate(parts, axis=-1) * ng_ref[...]).astype(BF16)
    a_br = jnp.dot(yn, wssd_ref[...], preferred_element_type=F32)

    pbuf[ph:ph + tq, :] = u_ref[...]
    pos = pos0 + j * tq + lax.broadcasted_iota(jnp.int32, (tq, 1), 0)
    pgd = dm // len(POOL_WINDOWS)
    b_parts = []
    for gi, w in enumerate(POOL_WINDOWS):
        lo = gi * pgd
        s = pbuf[ph:ph + tq, lo:lo + pgd]
        for i in range(1, w):
            s = s + pbuf[ph - i:ph - i + tq, lo:lo + pgd]
        cnt = jnp.minimum(w, pos + 1).astype(F32)
        pooled = s / cnt - pbuf[ph:ph + tq, lo:lo + pgd]
        b_parts.append(jnp.dot(pooled.astype(BF16), pw_ref[gi], preferred_element_type=F32))
    b_br = jnp.concatenate(b_parts, axis=-1) * ps_ref[...]
    pbuf[1:ph, :] = pbuf[tq + 1:tq + ph, :]

    gates = jax.nn.sigmoid(gt_ref[...])
    merged = gates[:, 0:dm] * a_br + gates[:, dm:2 * dm] * b_br
    mix = jnp.dot(merged.astype(BF16), wout_ref[...], preferred_element_type=F32)
    xo_ref[...] = x_ref[...] + g1_ref[0] * mix


def _seq(proj, x_all, g1, conv_state, ssm_state, pool_state, wts, *, row0, nb, seqlen, pos0, cols):
    dm = x_all.shape[1]
    heads, hd, ns = ssm_state.shape[1], ssm_state.shape[2], ssm_state.shape[3]
    di = heads * hd
    gn = SSD_GROUPS * ns
    q = min(CHUNK, seqlen)
    tq = _row_tile(seqlen, (128, 64, 32))
    nj = seqlen // tq
    rb0 = row0 // tq
    oz, ox, obc, ou, og, odt = cols

    def rows(b, j):
        return rb0 + b * nj + j

    full = lambda a: pl.BlockSpec(a.shape, lambda b, j: (0,) * a.ndim)
    kern = functools.partial(_seq_kernel, tq=tq, q=q, pos0=pos0, heads=heads, hd=hd, ns=ns, dm=dm)
    kw1 = conv_state.shape[1]
    pst = pool_state.shape[1]
    xo, so = pl.pallas_call(
        kern,
        grid=(nb, nj),
        in_specs=[
            pl.BlockSpec((tq, di), lambda b, j: (rows(b, j), oz // di)),
            pl.BlockSpec((tq, di), lambda b, j: (rows(b, j), ox // di)),
            pl.BlockSpec((tq, 2 * gn), lambda b, j: (rows(b, j), obc // (2 * gn))),
            pl.BlockSpec((tq, dm), lambda b, j: (rows(b, j), ou // dm)),
            pl.BlockSpec((tq, 2 * dm), lambda b, j: (rows(b, j), og // (2 * dm))),
            pl.BlockSpec((tq, LANES), lambda b, j: (rows(b, j), odt // LANES)),
            pl.BlockSpec((tq, dm), lambda b, j: (rows(b, j), 0)),
            pl.BlockSpec((1, 1, dm), lambda b, j: (b, 0, 0)),
            pl.BlockSpec((1, kw1, di + 2 * gn), lambda b, j: (b, 0, 0)),
            pl.BlockSpec((1, di, ns), lambda b, j: (b, 0, 0)),
            pl.BlockSpec((1, pst, dm), lambda b, j: (b, 0, 0)),
        ] + [full(w) for w in wts],
        out_specs=[pl.BlockSpec((tq, dm), lambda b, j: (b * nj + j, 0)),
                   pl.BlockSpec((1, di, ns), lambda b, j: (b, 0, 0))],
        out_shape=[jax.ShapeDtypeStruct((nb * seqlen, dm), F32),
                   jax.ShapeDtypeStruct((nb, di, ns), F32)],
        scratch_shapes=[pltpu.VMEM((tq + SUBLANES, di + 2 * gn), F32),
                        pltpu.VMEM((tq + 2 * SUBLANES, dm), F32),
                        pltpu.VMEM((di, ns), F32),
                        pltpu.VMEM((tq, di), F32),
                        pltpu.VMEM((tq, di + 2 * gn), F32),
                        pltpu.VMEM((tq, LANES), F32)],
        compiler_params=pltpu.CompilerParams(
            dimension_semantics=("parallel", "arbitrary"), vmem_limit_bytes=VMEM_LIMIT),
        name="seq",
    )(proj, proj, proj, proj, proj, proj, x_all, g1.reshape(nb, 1, dm),
      conv_state, ssm_state.reshape(nb, di, ns), pool_state, *wts)
    return xo, so.reshape(nb, heads, hd, ns)


def _moe_pre_kernel(x_ref, sc_ref, sh_ref, g_ref, wr_ref, rb_ref, swg_ref, swu_ref, swd_ref,
                    h_ref, s_ref, e_ref, w_ref):
    y = _rms(x_ref[...], g_ref[...], RMS_EPS)
    h = _modulate(y, sc_ref, sh_ref)
    h_ref[...] = h
    hb = h.astype(BF16)

    sg = jnp.dot(hb, swg_ref[...], preferred_element_type=F32)
    su = jnp.dot(hb, swu_ref[...], preferred_element_type=F32)
    s_ref[...] = jnp.dot((_silu(sg) * su).astype(BF16), swd_ref[...], preferred_element_type=F32)

    scores = jax.nn.sigmoid(jnp.dot(hb, wr_ref[...], preferred_element_type=F32))
    biased = scores + rb_ref[...]
    tm, ne = scores.shape
    per_g = ne // N_EXPERT_GROUPS
    neg = -jnp.inf
    lane_i = lax.broadcasted_iota(jnp.int32, (tm, ne), 1)
    lane = lane_i.astype(F32)
    lgroup = lax.shift_right_logical(lane_i, per_g.bit_length() - 1).astype(F32)
    lane_o = lax.broadcasted_iota(jnp.int32, (tm, LANES), 1).astype(F32)

    def first_argmax(v, idx, big):
        top = jnp.max(v, axis=-1, keepdims=True)
        return top, jnp.min(jnp.where(v == top, idx, big), axis=-1, keepdims=True)

    gs = jnp.full((tm, LANES), neg, F32)
    for g in range(N_EXPERT_GROUPS):
        mg = jnp.where(lgroup == g, biased, neg)
        t1, i1 = first_argmax(mg, lane, float(ne))
        t2 = jnp.max(jnp.where(lane == i1, neg, mg), axis=-1, keepdims=True)
        gs = jnp.where(lane_o == g, t1 + t2, gs)
    allowed = jnp.zeros((tm, ne), F32)
    for _ in range(TOPK_GROUPS):
        _, gi = first_argmax(gs, lane_o, float(LANES))
        allowed = jnp.where(lgroup == gi, 1.0, allowed)
        gs = jnp.where(lane_o == gi, neg, gs)
    mb = jnp.where(allowed > 0.0, biased, neg)
    eacc = jnp.zeros((tm, LANES), F32)
    wacc = jnp.zeros((tm, LANES), F32)
    for k in range(TOP_K):
        _, ik = first_argmax(mb, lane, float(ne))
        sel = lane == ik
        wk = jnp.sum(jnp.where(sel, scores, 0.0), axis=-1, keepdims=True)
        eacc = jnp.where(lane_o == k, ik, eacc)
        wacc = jnp.where(lane_o == k, wk, wacc)
        mb = jnp.where(sel, neg, mb)
    wsum = jnp.sum(wacc, axis=-1, keepdims=True)
    e_ref[...] = eacc.astype(jnp.int32)
    w_ref[...] = wacc / (wsum + 1e-20) * ROUTED_SCALE


def _moe_pre(x1, sc_rows, sh_rows, g, wr, rb, swg, swu, swd):
    t, d = x1.shape
    ne = wr.shape[1]
    ff = swg.shape[1]
    tm = _row_tile(t, (256,))
    mr = tm // MOD_ROWS
    row = lambda i: (i, 0)
    const = lambda i: (0, 0)
    return pl.pallas_call(
        _moe_pre_kernel,
        grid=(t // tm,),
        in_specs=[pl.BlockSpec((tm, d), row), pl.BlockSpec((mr, d), row), pl.BlockSpec((mr, d), row),
                  pl.BlockSpec((1, d), const), pl.BlockSpec((d, ne), const), pl.BlockSpec((1, ne), const),
                  pl.BlockSpec((d, ff), const), pl.BlockSpec((d, ff), const), pl.BlockSpec((ff, d), const)],
        out_specs=[pl.BlockSpec((tm, d), row), pl.BlockSpec((tm, d), row),
                   pl.BlockSpec((tm, LANES), row), pl.BlockSpec((tm, LANES), row)],
        out_shape=[jax.ShapeDtypeStruct((t, d), F32), jax.ShapeDtypeStruct((t, d), F32),
                   jax.ShapeDtypeStruct((t, LANES), jnp.int32), jax.ShapeDtypeStruct((t, LANES), F32)],
        compiler_params=pltpu.CompilerParams(
            dimension_semantics=("parallel",), vmem_limit_bytes=VMEM_LIMIT),
        name="moe_pre",
    )(x1, sc_rows, sh_rows, g.reshape(1, d), wr, rb.reshape(1, ne), swg, swu, swd)


def _read_rows(buf, nrows, nsub):
    return jnp.concatenate([buf[pl.ds(c, nrows, stride=nsub), :] for c in range(nsub)], axis=-1)


def _grouped_kernel(be_ref, nu_ref, tok_ref, tokn_ref, h_hbm, wg_ref, wu_ref, wd_ref, y_ref,
                    xbuf, sem, wgb, wub, wdb, *, bm, nsub):
    i = pl.program_id(0)
    nu = nu_ref[0]

    def row_copy(tref, r, slot):
        src = h_hbm.at[pl.ds(pl.multiple_of(tref[0, 0, r] * nsub, nsub), nsub)]
        return pltpu.make_async_copy(src, xbuf.at[slot, pl.ds(r * nsub, nsub)], sem.at[slot])

    def gather(tref, slot):
        for r in range(bm):
            row_copy(tref, r, slot).start()

    @pl.when(i == 0)
    def _():
        gather(tok_ref, 0)

    @pl.when(i + 1 < nu)
    def _():
        gather(tokn_ref, (i + 1) % 2)

    @pl.when(i < nu)
    def _():
        slot = i % 2
        pltpu.make_async_copy(h_hbm.at[pl.ds(0, bm * nsub)], xbuf.at[slot], sem.at[slot]).wait()

        @pl.when(jnp.logical_or(i == 0, be_ref[i] != be_ref[jnp.maximum(i - 1, 0)]))
        def _():
            wgb[...] = wg_ref[0].astype(BF16)
            wub[...] = wu_ref[0].astype(BF16)
            wdb[...] = wd_ref[0].astype(BF16)

        x = _read_rows(xbuf.at[slot], bm, nsub).astype(BF16)
        hg = jnp.dot(x, wgb[...], preferred_element_type=F32)
        hu = jnp.dot(x, wub[...], preferred_element_type=F32)
        y = jnp.dot((_silu(hg) * hu).astype(BF16), wdb[...], preferred_element_type=F32)
        for c in range(nsub):
            y_ref[pl.ds(c, bm, stride=nsub), :] = y[:, c * LANES:(c + 1) * LANES]

    @pl.when(i >= nu)
    def _():
        y_ref[...] = jnp.zeros_like(y_ref)


def _grouped(h2_tiles, slot_tok, block_e, n_used, wg, wu, wd):
    ne, d, ff = wg.shape
    nsub = d // LANES
    bm = EXPERT_ROWS
    nb = block_e.shape[0]
    tok3 = slot_tok.reshape(nb, 1, bm)
    kern = functools.partial(_grouped_kernel, bm=bm, nsub=nsub)
    smem_blk = lambda f: pl.BlockSpec((1, 1, bm), f, memory_space=pltpu.SMEM)
    gs = pltpu.PrefetchScalarGridSpec(
        num_scalar_prefetch=2,
        grid=(nb,),
        in_specs=[smem_blk(lambda i, be, nu: (i, 0, 0)),
                  smem_blk(lambda i, be, nu: (jnp.minimum(i + 1, nb - 1), 0, 0)),
                  pl.BlockSpec(memory_space=pl.ANY),
                  pl.BlockSpec((1, d, ff), lambda i, be, nu: (be[i], 0, 0)),
                  pl.BlockSpec((1, d, ff), lambda i, be, nu: (be[i], 0, 0)),
                  pl.BlockSpec((1, ff, d), lambda i, be, nu: (be[i], 0, 0))],
        out_specs=pl.BlockSpec((bm * nsub, LANES), lambda i, be, nu: (i, 0)),
        scratch_shapes=[pltpu.VMEM((2, bm * nsub, LANES), F32),
                        pltpu.SemaphoreType.DMA((2,)),
                        pltpu.VMEM((d, ff), BF16), pltpu.VMEM((d, ff), BF16), pltpu.VMEM((ff, d), BF16)],
    )
    return pl.pallas_call(
        kern,
        grid_spec=gs,
        out_shape=jax.ShapeDtypeStruct((nb * bm * nsub, LANES), F32),
        compiler_params=pltpu.CompilerParams(
            dimension_semantics=("arbitrary",), vmem_limit_bytes=VMEM_LIMIT),
        name="grouped",
    )(block_e, n_used, tok3, tok3, h2_tiles, wg, wu, wd)


def _combine_kernel(d_ref, dn_ref, y_hbm, w_ref, s_ref, x_ref, g2_ref, fg_ref, o_ref, buf, sem,
                    *, tt, nsub):
    i = pl.program_id(0)
    n = pl.num_programs(0)

    def gather(dref, slot):
        def body(r, carry):
            for k in range(TOP_K):
                src = y_hbm.at[pl.ds(pl.multiple_of(dref[0, 0, r * TOP_K + k] * nsub, nsub), nsub)]
                pltpu.make_async_copy(src, buf.at[slot, k, pl.ds(pl.multiple_of(r * nsub, nsub), nsub)],
                                      sem.at[slot]).start()
            return carry
        lax.fori_loop(0, tt, body, 0)

    @pl.when(i == 0)
    def _():
        gather(d_ref, 0)

    @pl.when(i + 1 < n)
    def _():
        gather(dn_ref, (i + 1) % 2)

    slot = i % 2
    for k in range(TOP_K):
        pltpu.make_async_copy(y_hbm.at[pl.ds(0, tt * nsub)], buf.at[slot, k], sem.at[slot]).wait()

    w = w_ref[...]
    routed = w[:, 0:1] * _read_rows(buf.at[slot, 0], tt, nsub)
    for k in range(1, TOP_K):
        routed = routed + w[:, k:k + 1] * _read_rows(buf.at[slot, k], tt, nsub)
    moe = routed + s_ref[...]
    d = moe.shape[1]
    m3 = moe.reshape(tt // MOD_ROWS, MOD_ROWS, d) * g2_ref[...][:, None, :]
    x2 = x_ref[...] + m3.reshape(tt, d)
    o_ref[...] = _rms(x2, fg_ref[...], RMS_EPS)


def _combine(dest, y_tiles, wts, shared, x1, g2_rows, final_g):
    t, d = x1.shape
    nsub = d // LANES
    tt = _row_tile(t, (256,))
    mr = tt // MOD_ROWS
    nt = t // tt
    d3 = dest.reshape(nt, 1, tt * TOP_K)
    kern = functools.partial(_combine_kernel, tt=tt, nsub=nsub)
    smem_blk = lambda f: pl.BlockSpec((1, 1, tt * TOP_K), f, memory_space=pltpu.SMEM)
    row = lambda i: (i, 0)
    return pl.pallas_call(
        kern,
        grid=(nt,),
        in_specs=[smem_blk(lambda i: (i, 0, 0)),
                  smem_blk(lambda i: (jnp.minimum(i + 1, nt - 1), 0, 0)),
                  pl.BlockSpec(memory_space=pl.ANY),
                  pl.BlockSpec((tt, LANES), row), pl.BlockSpec((tt, d), row), pl.BlockSpec((tt, d), row),
                  pl.BlockSpec((mr, d), row), pl.BlockSpec((1, d), lambda i: (0, 0))],
        out_specs=pl.BlockSpec((tt, d), row),
        out_shape=jax.ShapeDtypeStruct((t, d), F32),
        scratch_shapes=[pltpu.VMEM((2, TOP_K, tt * nsub, LANES), F32), pltpu.SemaphoreType.DMA((2,))],
        compiler_params=pltpu.CompilerParams(
            dimension_semantics=("arbitrary",), vmem_limit_bytes=VMEM_LIMIT),
        name="combine",
    )(d3, d3, y_tiles, wts, shared, x1, g2_rows, final_g.reshape(1, d))


def _plan(eidx, n_experts):
    t, k = eidx.shape
    a = t * k
    bm = EXPERT_ROWS
    flat_e = eidx.reshape(-1)
    order = jnp.argsort(flat_e)
    se = flat_e[order]
    st = (order // k).astype(jnp.int32)
    counts = jnp.bincount(flat_e, length=n_experts)
    starts = jnp.cumsum(counts) - counts
    padded = ((counts + bm - 1) // bm) * bm
    pend = jnp.cumsum(padded)
    pstarts = pend - padded
    dest_sorted = (pstarts[se] + (jnp.arange(a) - starts[se])).astype(jnp.int32)
    nb = -(-(a + n_experts * (bm - 1)) // bm)
    slot_tok = jnp.zeros((nb * bm,), jnp.int32).at[dest_sorted].set(st)
    dest = jnp.zeros((a,), jnp.int32).at[order].set(dest_sorted)
    block_e = jnp.minimum(jnp.searchsorted(pend, jnp.arange(nb) * bm, side='right'),
                          n_experts - 1).astype(jnp.int32)
    n_used = (pend[-1] // bm).astype(jnp.int32).reshape(1)
    return slot_tok, dest, block_e, n_used


def per_g_pow2(ne):
    per_g = ne // N_EXPERT_GROUPS
    return per_g * N_EXPERT_GROUPS == ne and per_g & (per_g - 1) == 0


def _mod_rows(m, nbp, lp):
    return jnp.concatenate([jnp.repeat(m[:nbp], lp // MOD_ROWS, axis=0), m[nbp:]], axis=0)


def kernel(x_prompt, x_sample, state_ssm, state_conv, state_pool, c_prompt, c_sample, ln1_g, ln2_g, w_ada, b_ada, w_in, conv_w, conv_b, dt_bias, a_log, d_skip, ssd_norm_g, w_ssd_out, pool_w, pool_scale, w_out, w_router, router_bias, moe_w_gate, moe_w_up, moe_w_down, shared_w_gate, shared_w_up, shared_w_down, final_g):
    bp, lp, dm = x_prompt.shape
    bs, ls, _ = x_sample.shape
    depth = ln1_g.shape[0]
    heads, hd, ns = state_ssm.shape[2], state_ssm.shape[3], state_ssm.shape[4]
    di = heads * hd
    gn = SSD_GROUPS * ns
    cch = di + 2 * gn
    assert depth == 1 and ls == MOD_ROWS and lp % MOD_ROWS == 0 and heads <= LANES
    assert per_g_pow2(w_router.shape[2])
    tp, ts = bp * lp, bs * ls

    x_all = jnp.concatenate([x_prompt.reshape(tp, dm), x_sample.reshape(ts, dm)], axis=0)
    c_all = jnp.concatenate([c_prompt, c_sample], axis=0)

    o1, o2, o3, o4 = di, di + cch, di + cch + heads, di + cch + heads + dm
    cols = (0, di, 2 * di, di + cch, di + cch + dm, di + cch + 3 * dm)

    ssm_p, conv_p, pool_p, ssm_s, conv_s, pool_s = [], [], [], [], [], []
    for l in range(depth):
        wi = w_in[l]
        wcat = jnp.concatenate(
            [wi[:, :o1], wi[:, o1:o2], wi[:, o3:o4], wi[:, o4:],
             jnp.pad(wi[:, o2:o3], ((0, 0), (0, LANES - heads)))], axis=1).astype(BF16)
        pad_h = lambda v: jnp.pad(v.reshape(1, heads), ((0, 0), (0, LANES - heads)))
        seq_w = (conv_w[l], conv_b[l].reshape(1, cch), pad_h(dt_bias[l]), pad_h(a_log[l]),
                 jnp.repeat(d_skip[l], hd).reshape(1, di), ssd_norm_g[l].reshape(1, di),
                 w_ssd_out[l].astype(BF16), pool_w[l].astype(BF16), pool_scale[l].reshape(1, dm),
                 w_out[l].astype(BF16))

        mod = _ada(c_all, w_ada[l], b_ada[l])
        sh1, sc1, g1, sh2, sc2, g2 = jnp.split(mod, 6, axis=-1)

        proj = _inproj(x_all, _mod_rows(sc1, bp, lp), _mod_rows(sh1, bp, lp), ln1_g[l], wcat)

        zc = jnp.zeros((bp,) + state_conv.shape[2:], F32)
        zs = jnp.zeros((bp, heads, hd, ns), F32)
        zp = jnp.zeros((bp,) + state_pool.shape[2:], F32)
        xp1, ns_p = _seq(proj, x_all, g1[:bp], zc, zs, zp, seq_w,
                         row0=0, nb=bp, seqlen=lp, pos0=0, cols=cols)
        xs1, ns_s = _seq(proj, x_all, g1[bp:], state_conv[l], state_ssm[l], state_pool[l], seq_w,
                         row0=tp, nb=bs, seqlen=ls, pos0=PAST_LEN, cols=cols)
        x1 = jnp.concatenate([xp1, xs1], axis=0)

        kc = state_conv.shape[2]
        kp = state_pool.shape[2]
        pj_p = proj[:tp].reshape(bp, lp, -1)
        pj_s = proj[tp:].reshape(bs, ls, -1)
        conv_p.append(pj_p[:, lp - kc:, cols[1]:cols[1] + cch])
        conv_s.append(pj_s[:, ls - kc:, cols[1]:cols[1] + cch])
        pool_p.append(pj_p[:, lp - kp:, cols[3]:cols[3] + dm])
        pool_s.append(pj_s[:, ls - kp:, cols[3]:cols[3] + dm])
        ssm_p.append(ns_p)
        ssm_s.append(ns_s)

        h2, shared, eidx_l, wts = _moe_pre(
            x1, _mod_rows(sc2, bp, lp), _mod_rows(sh2, bp, lp), ln2_g[l],
            w_router[l].astype(BF16), router_bias[l],
            shared_w_gate[l].astype(BF16), shared_w_up[l].astype(BF16), shared_w_down[l].astype(BF16))
        slot_tok, dest, block_e, n_used = _plan(eidx_l[:, :TOP_K], w_router.shape[2])
        nsub = dm // LANES
        y_tiles = _grouped(h2.reshape((tp + ts) * nsub, LANES), slot_tok, block_e, n_used,
                           moe_w_gate[l], moe_w_up[l], moe_w_down[l])
        x_all = _combine(dest, y_tiles, wts, shared, x1, _mod_rows(g2, bp, lp), final_g)

    y_prompt = x_all[:tp].reshape(bp, lp, dm)
    y_sample = x_all[tp:].reshape(bs, ls, dm)
    return (y_prompt, y_sample, jnp.stack(ssm_p), jnp.stack(conv_p), jnp.stack(pool_p),
            jnp.stack(ssm_s), jnp.stack(conv_s), jnp.stack(pool_s))
```

```python
import functools

import jax
import jax.numpy as jnp
from jax import lax
from jax.experimental import pallas as pl
from jax.experimental.pallas import tpu as pltpu

F32 = jnp.float32
BF16 = jnp.bfloat16
HIGHEST = lax.Precision.HIGHEST

RMS_EPS = 1e-6
SSD_NORM_EPS = 1e-5
CHUNK = 64
SSD_GROUPS = 4
POOL_WINDOWS = (2, 4, 8, 16)
PAST_LEN = 1024
TOP_K = 8
N_EXPERT_GROUPS = 8
TOPK_GROUPS = 4
ROUTED_SCALE = 2.5

LANES = 128
SUBLANES = 8
MOD_ROWS = 32
VMEM_LIMIT = 56 * 1024 * 1024
EXPERT_ROWS = 128


def _silu(x):
    return x * jax.nn.sigmoid(x)


def _softplus(x):
    return jnp.maximum(x, 0.0) + jnp.log1p(jnp.exp(-jnp.abs(x)))


def _row_tile(n, prefs):
    for t in prefs:
        if n % t == 0:
            return t
    return n


def _modulate(y, sc_ref, sh_ref):
    rows, d = y.shape
    y3 = y.reshape(rows // MOD_ROWS, MOD_ROWS, d)
    y3 = y3 * (1.0 + sc_ref[...][:, None, :]) + sh_ref[...][:, None, :]
    return y3.reshape(rows, d)


def _rms(x, g, eps):
    return x * lax.rsqrt(jnp.mean(x * x, axis=-1, keepdims=True) + eps) * g


def _ada_kernel(c_ref, w_ref, b_ref, o_ref):
    s = _silu(c_ref[...])
    o_ref[...] = jnp.dot(s, w_ref[...], preferred_element_type=F32, precision=HIGHEST) + b_ref[...]


def _ada(c_all, w_ada, b_ada):
    n, d = c_all.shape
    dout = w_ada.shape[1]
    tn = _row_tile(dout, (1024, 512, 256, 128))
    return pl.pallas_call(
        _ada_kernel,
        grid=(dout // tn,),
        in_specs=[pl.BlockSpec((n, d), lambda j: (0, 0)),
                  pl.BlockSpec((d, tn), lambda j: (0, j)),
                  pl.BlockSpec((1, tn), lambda j: (0, j))],
        out_specs=pl.BlockSpec((n, tn), lambda j: (0, j)),
        out_shape=jax.ShapeDtypeStruct((n, dout), F32),
        name="ada",
    )(c_all, w_ada, b_ada.reshape(1, dout))


def _inproj_kernel(x_ref, sc_ref, sh_ref, g_ref, w_ref, o_ref, h_scr):
    @pl.when(pl.program_id(1) == 0)
    def _():
        y = _rms(x_ref[...], g_ref[...], RMS_EPS)
        h_scr[...] = _modulate(y, sc_ref, sh_ref).astype(BF16)

    o_ref[...] = jnp.dot(h_scr[...], w_ref[...], preferred_element_type=F32)


def _inproj(x_all, sc_rows, sh_rows, g, wcat):
    t, d = x_all.shape
    n = wcat.shape[1]
    tm = _row_tile(t, (512, 256))
    tn = _row_tile(n, (1664, 1280, 1024, 640, 512, 256, 128))
    mr = tm // MOD_ROWS
    return pl.pallas_call(
        _inproj_kernel,
        grid=(t // tm, n // tn),
        in_specs=[pl.BlockSpec((tm, d), lambda i, j: (i, 0)),
                  pl.BlockSpec((mr, d), lambda i, j: (i, 0)),
                  pl.BlockSpec((mr, d), lambda i, j: (i, 0)),
                  pl.BlockSpec((1, d), lambda i, j: (0, 0)),
                  pl.BlockSpec((d, tn), lambda i, j: (0, j))],
        out_specs=pl.BlockSpec((tm, tn), lambda i, j: (i, j)),
        out_shape=jax.ShapeDtypeStruct((t, n), F32),
        scratch_shapes=[pltpu.VMEM((tm, d), BF16)],
        compiler_params=pltpu.CompilerParams(
            dimension_semantics=("parallel", "arbitrary"), vmem_limit_bytes=VMEM_LIMIT),
        name="inproj",
    )(x_all, sc_rows, sh_rows, g.reshape(1, d), wcat)


def _seq_kernel(z_ref, xp_ref, bc_ref, u_ref, gt_ref, dt_ref, x_ref, g1_ref,
                cst_ref, sst_ref, pst_ref, cw_ref, cb_ref, dtb_ref, alog_ref, dx_ref, ng_ref,
                wssd_ref, pw_ref, ps_ref, wout_ref,
                xo_ref, so_ref, co_ref, po_ref,
                cbuf, pbuf, s_scr, y_scr, xbc_scr, dt_scr,
                *, tq, q, pos0, heads, hd, ns, dm):
    j = pl.program_id(1)
    di = heads * hd
    gn = SSD_GROUPS * ns
    hpg = heads // SSD_GROUPS
    cw = cbuf.shape[0] - tq
    kw = cw_ref.shape[0]
    ph = pbuf.shape[0] - tq

    @pl.when(j == 0)
    def _():
        cbuf[cw - (kw - 1):cw, :] = cst_ref[0]
        pbuf[1:ph, :] = pst_ref[0]
        s_scr[...] = sst_ref[0]

    cbuf[cw:cw + tq, 0:di] = xp_ref[...]
    cbuf[cw:cw + tq, di:di + 2 * gn] = bc_ref[...]
    acc = cb_ref[...] + cbuf[cw:cw + tq, :] * cw_ref[kw - 1:kw, :]
    for k in range(kw - 1):
        off = cw - (kw - 1) + k
        acc = acc + cbuf[off:off + tq, :] * cw_ref[k:k + 1, :]
    xbc_scr[...] = _silu(acc)
    cbuf[cw - (kw - 1):cw, :] = cbuf[cw + tq - (kw - 1):cw + tq, :]

    dt_scr[...] = _softplus(dt_ref[...] + dtb_ref[...])
    a_row = -jnp.exp(alog_ref[...])

    r_i = lax.broadcasted_iota(jnp.int32, (q, q), 0)
    c_i = lax.broadcasted_iota(jnp.int32, (q, q), 1)
    causal = r_i >= c_i
    tril = causal.astype(F32)

    def chunk(c, carry):
        r0 = pl.multiple_of(c * q, q)
        dt_c = dt_scr[pl.ds(r0, q), :]
        cs = jnp.dot(tril, dt_c * a_row, preferred_element_type=F32, precision=HIGHEST)
        cs_end = cs[q - 1:q, :]
        wv = dt_c * jnp.exp(cs_end - cs)
        ecs = jnp.exp(cs)
        dec_end = jnp.exp(cs_end)
        cs_t = cs.T
        dt_t = dt_c.T
        for g in range(SSD_GROUPS):
            bg = xbc_scr[pl.ds(r0, q), di + g * ns:di + (g + 1) * ns]
            cg = xbc_scr[pl.ds(r0, q), di + gn + g * ns:di + gn + (g + 1) * ns]
            bgb = bg.astype(BF16)
            cb = lax.dot_general(cg.astype(BF16), bgb, (((1,), (1,)), ((), ())),
                                 preferred_element_type=F32)
            for r in range(hpg):
                h = g * hpg + r
                seg = cs[:, h:h + 1] - cs_t[h:h + 1, :]
                lm = jnp.exp(jnp.where(causal, seg, -jnp.inf))
                m = cb * lm * dt_t[h:h + 1, :]
                xh = xbc_scr[pl.ds(r0, q), h * hd:(h + 1) * hd]
                sh = s_scr[h * hd:(h + 1) * hd, :]
                yd = jnp.dot(m.astype(BF16), xh.astype(BF16), preferred_element_type=F32)
                yo = lax.dot_general((cg * ecs[:, h:h + 1]).astype(BF16), sh.astype(BF16),
                                     (((1,), (1,)), ((), ())), preferred_element_type=F32)
                y_scr[pl.ds(r0, q), h * hd:(h + 1) * hd] = yd + yo
                xw = (xh * wv[:, h:h + 1]).astype(BF16)
                upd = lax.dot_general(xw, bgb, (((0,), (0,)), ((), ())),
                                      preferred_element_type=F32)
                s_scr[h * hd:(h + 1) * hd, :] = sh * dec_end[:, h:h + 1] + upd
        return carry

    lax.fori_loop(0, tq // q, chunk, 0)

    @pl.when(j == pl.num_programs(1) - 1)
    def _():
        so_ref[0] = s_scr[...]
        co_ref[0] = cbuf[cw - (kw - 1):cw, :]

    y = y_scr[...] + xbc_scr[:, 0:di] * dx_ref[...]
    v = y * _silu(z_ref[...])
    gw = di // SSD_GROUPS
    parts = []
    for g in range(SSD_GROUPS):
        vg = v[:, g * gw:(g + 1) * gw]
        parts.append(vg * lax.rsqrt(jnp.mean(vg * vg, axis=-1, keepdims=True) + SSD_NORM_EPS))
    yn = (jnp.concatenate(parts, axis=-1) * ng_ref[...]).astype(BF16)
    a_br = jnp.dot(yn, wssd_ref[...], preferred_element_type=F32)

    pbuf[ph:ph + tq, :] = u_ref[...]
    pos = pos0 + j * tq + lax.broadcasted_iota(jnp.int32, (tq, 1), 0)
    pgd = dm // len(POOL_WINDOWS)
    b_parts = []
    for gi, w in enumerate(POOL_WINDOWS):
        lo = gi * pgd
        s = pbuf[ph:ph + tq, lo:lo + pgd]
        for i in range(1, w):
            s = s + pbuf[ph - i:ph - i + tq, lo:lo + pgd]
        cnt = jnp.minimum(w, pos + 1).astype(F32)
        pooled = s / cnt - pbuf[ph:ph + tq, lo:lo + pgd]
        b_parts.append(jnp.dot(pooled.astype(BF16), pw_ref[gi], preferred_element_type=F32))
    b_br = jnp.concatenate(b_parts, axis=-1) * ps_ref[...]
    pbuf[1:ph, :] = pbuf[tq + 1:tq + ph, :]

    @pl.when(j == pl.num_programs(1) - 1)
    def _():
        po_ref[0] = pbuf[1:ph, :]

    gates = jax.nn.sigmoid(gt_ref[...])
    merged = gates[:, 0:dm] * a_br + gates[:, dm:2 * dm] * b_br
    mix = jnp.dot(merged.astype(BF16), wout_ref[...], preferred_element_type=F32)
    xo_ref[...] = x_ref[...] + g1_ref[0] * mix


def _seq(proj, x_all, g1, conv_state, ssm_state, pool_state, wts, *, row0, nb, seqlen, pos0, cols):
    dm = x_all.shape[1]
    heads, hd, ns = ssm_state.shape[1], ssm_state.shape[2], ssm_state.shape[3]
    di = heads * hd
    gn = SSD_GROUPS * ns
    q = min(CHUNK, seqlen)
    tq = _row_tile(seqlen, (128, 64, 32))
    nj = seqlen // tq
    rb0 = row0 // tq
    oz, ox, obc, ou, og, odt = cols

    def rows(b, j):
        return rb0 + b * nj + j

    full = lambda a: pl.BlockSpec(a.shape, lambda b, j: (0,) * a.ndim)
    kern = functools.partial(_seq_kernel, tq=tq, q=q, pos0=pos0, heads=heads, hd=hd, ns=ns, dm=dm)
    kw1 = conv_state.shape[1]
    pst = pool_state.shape[1]
    xo, so, co, po = pl.pallas_call(
        kern,
        grid=(nb, nj),
        in_specs=[
            pl.BlockSpec((tq, di), lambda b, j: (rows(b, j), oz // di)),
            pl.BlockSpec((tq, di), lambda b, j: (rows(b, j), ox // di)),
            pl.BlockSpec((tq, 2 * gn), lambda b, j: (rows(b, j), obc // (2 * gn))),
            pl.BlockSpec((tq, dm), lambda b, j: (rows(b, j), ou // dm)),
            pl.BlockSpec((tq, 2 * dm), lambda b, j: (rows(b, j), og // (2 * dm))),
            pl.BlockSpec((tq, LANES), lambda b, j: (rows(b, j), odt // LANES)),
            pl.BlockSpec((tq, dm), lambda b, j: (rows(b, j), 0)),
            pl.BlockSpec((1, 1, dm), lambda b, j: (b, 0, 0)),
            pl.BlockSpec((1, kw1, di + 2 * gn), lambda b, j: (b, 0, 0)),
            pl.BlockSpec((1, di, ns), lambda b, j: (b, 0, 0)),
            pl.BlockSpec((1, pst, dm), lambda b, j: (b, 0, 0)),
        ] + [full(w) for w in wts],
        out_specs=[pl.BlockSpec((tq, dm), lambda b, j: (b * nj + j, 0)),
                   pl.BlockSpec((1, di, ns), lambda b, j: (b, 0, 0)),
                   pl.BlockSpec((1, kw1, di + 2 * gn), lambda b, j: (b, 0, 0)),
                   pl.BlockSpec((1, pst, dm), lambda b, j: (b, 0, 0))],
        out_shape=[jax.ShapeDtypeStruct((nb * seqlen, dm), F32),
                   jax.ShapeDtypeStruct((nb, di, ns), F32),
                   jax.ShapeDtypeStruct((nb, kw1, di + 2 * gn), F32),
                   jax.ShapeDtypeStruct((nb, pst, dm), F32)],
        scratch_shapes=[pltpu.VMEM((tq + SUBLANES, di + 2 * gn), F32),
                        pltpu.VMEM((tq + 2 * SUBLANES, dm), F32),
                        pltpu.VMEM((di, ns), F32),
                        pltpu.VMEM((tq, di), F32),
                        pltpu.VMEM((tq, di + 2 * gn), F32),
                        pltpu.VMEM((tq, LANES), F32)],
        compiler_params=pltpu.CompilerParams(
            dimension_semantics=("parallel", "arbitrary"), vmem_limit_bytes=VMEM_LIMIT),
        name="seq",
    )(proj, proj, proj, proj, proj, proj, x_all, g1.reshape(nb, 1, dm),
      conv_state, ssm_state.reshape(nb, di, ns), pool_state, *wts)
    return xo, so.reshape(nb, heads, hd, ns), co, po


def _moe_pre_kernel(x_ref, sc_ref, sh_ref, g_ref, wr_ref, rb_ref, swg_ref, swu_ref, swd_ref,
                    h_ref, s_ref, e_ref, w_ref, m_ref, cnt_ref):
    y = _rms(x_ref[...], g_ref[...], RMS_EPS)
    h = _modulate(y, sc_ref, sh_ref)
    _write_rows(h_ref, h)
    hb = h.astype(BF16)

    sg = jnp.dot(hb, swg_ref[...], preferred_element_type=F32)
    su = jnp.dot(hb, swu_ref[...], preferred_element_type=F32)
    s_ref[...] = jnp.dot((_silu(sg) * su).astype(BF16), swd_ref[...], preferred_element_type=F32)

    scores = jax.nn.sigmoid(jnp.dot(hb, wr_ref[...], preferred_element_type=F32))
    biased = scores + rb_ref[...]
    tm, ne = scores.shape
    per_g = ne // N_EXPERT_GROUPS
    neg = -jnp.inf
    lane_i = lax.broadcasted_iota(jnp.int32, (tm, ne), 1)
    lane = lane_i.astype(F32)
    lgroup = lax.shift_right_logical(lane_i, per_g.bit_length() - 1).astype(F32)
    lane_o = lax.broadcasted_iota(jnp.int32, (tm, LANES), 1).astype(F32)

    def first_argmax(v, idx, big):
        top = jnp.max(v, axis=-1, keepdims=True)
        return top, jnp.min(jnp.where(v == top, idx, big), axis=-1, keepdims=True)

    gs = jnp.full((tm, LANES), neg, F32)
    for g in range(N_EXPERT_GROUPS):
        mg = jnp.where(lgroup == g, biased, neg)
        t1, i1 = first_argmax(mg, lane, float(ne))
        t2 = jnp.max(jnp.where(lane == i1, neg, mg), axis=-1, keepdims=True)
        gs = jnp.where(lane_o == g, t1 + t2, gs)
    allowed = jnp.zeros((tm, ne), F32)
    for _ in range(TOPK_GROUPS):
        _, gi = first_argmax(gs, lane_o, float(LANES))
        allowed = jnp.where(lgroup == gi, 1.0, allowed)
        gs = jnp.where(lane_o == gi, neg, gs)
    mb = jnp.where(allowed > 0.0, biased, neg)
    eacc = jnp.zeros((tm, LANES), F32)
    wacc = jnp.zeros((tm, LANES), F32)
    chosen = jnp.zeros((tm, ne), F32)
    for k in range(TOP_K):
        _, ik = first_argmax(mb, lane, float(ne))
        sel = lane == ik
        wk = jnp.sum(jnp.where(sel, scores, 0.0), axis=-1, keepdims=True)
        eacc = jnp.where(lane_o == k, ik, eacc)
        wacc = jnp.where(lane_o == k, wk, wacc)
        chosen = jnp.where(sel, 1.0, chosen)
        mb = jnp.where(sel, neg, mb)
    wsum = jnp.sum(wacc, axis=-1, keepdims=True)
    e_ref[...] = eacc.astype(jnp.int32)
    w_ref[...] = wacc / (wsum + 1e-20) * ROUTED_SCALE
    m_ref[...] = chosen.astype(BF16)

    @pl.when(pl.program_id(0) == 0)
    def _():
        cnt_ref[...] = jnp.zeros_like(cnt_ref)

    cnt_ref[0:1, :] += jnp.sum(chosen, axis=0, keepdims=True)


def _moe_pre(x1, sc_rows, sh_rows, g, wr, rb, swg, swu, swd):
    t, d = x1.shape
    ne = wr.shape[1]
    ff = swg.shape[1]
    tm = _row_tile(t, (256,))
    mr = tm // MOD_ROWS
    nsub = d // LANES
    row = lambda i: (i, 0)
    const = lambda i: (0, 0)
    return pl.pallas_call(
        _moe_pre_kernel,
        grid=(t // tm,),
        in_specs=[pl.BlockSpec((tm, d), row), pl.BlockSpec((mr, d), row), pl.BlockSpec((mr, d), row),
                  pl.BlockSpec((1, d), const), pl.BlockSpec((d, ne), const), pl.BlockSpec((1, ne), const),
                  pl.BlockSpec((d, ff), const), pl.BlockSpec((d, ff), const), pl.BlockSpec((ff, d), const)],
        out_specs=[pl.BlockSpec((tm * nsub, LANES), row), pl.BlockSpec((tm, d), row),
                   pl.BlockSpec((tm, LANES), row), pl.BlockSpec((tm, LANES), row),
                   pl.BlockSpec((tm, ne), row), pl.BlockSpec((SUBLANES, ne), const)],
        out_shape=[jax.ShapeDtypeStruct((t * nsub, LANES), F32), jax.ShapeDtypeStruct((t, d), F32),
                   jax.ShapeDtypeStruct((t, LANES), jnp.int32), jax.ShapeDtypeStruct((t, LANES), F32),
                   jax.ShapeDtypeStruct((t, ne), BF16), jax.ShapeDtypeStruct((SUBLANES, ne), F32)],
        compiler_params=pltpu.CompilerParams(
            dimension_semantics=("arbitrary",), vmem_limit_bytes=VMEM_LIMIT),
        name="moe_pre",
    )(x1, sc_rows, sh_rows, g.reshape(1, d), wr, rb.reshape(1, ne), swg, swu, swd)


def _read_rows(buf, nrows, nsub):
    return jnp.concatenate([buf[pl.ds(c, nrows, stride=nsub), :] for c in range(nsub)], axis=-1)


def _write_rows(ref, v):
    nrows, width = v.shape
    nsub = width // LANES
    for c in range(nsub):
        ref[pl.ds(c, nrows, stride=nsub), :] = v[:, c * LANES:(c + 1) * LANES]


def _rank_kernel(m_ref, e_ref, ps_ref, d_ref, carry):
    i = pl.program_id(0)
    ne = m_ref.shape[1]

    @pl.when(i == 0)
    def _():
        carry[...] = jnp.zeros_like(carry)

    m = m_ref[...]
    tr = m.shape[0]
    r_i = lax.broadcasted_iota(jnp.int32, (tr, tr), 0)
    c_i = lax.broadcasted_iota(jnp.int32, (tr, tr), 1)
    earlier = (r_i > c_i).astype(BF16)
    slot = jnp.dot(earlier, m, preferred_element_type=F32) + carry[0:1, :] + ps_ref[...]
    e = e_ref[...]
    lane = lax.broadcasted_iota(jnp.int32, (tr, ne), 1)
    lane_o = lax.broadcasted_iota(jnp.int32, (tr, LANES), 1)
    dacc = jnp.zeros((tr, LANES), F32)
    for k in range(TOP_K):
        dk = jnp.sum(jnp.where(lane == e[:, k:k + 1], slot, 0.0), axis=-1, keepdims=True)
        dacc = jnp.where(lane_o == k, dk, dacc)
    d_ref[...] = dacc.astype(jnp.int32)
    carry[0:1, :] += jnp.sum(m.astype(F32), axis=0, keepdims=True)


def _rank(mask, eidx, pstart):
    t, ne = mask.shape
    tr = _row_tile(t, (512, 256))
    row = lambda i: (i, 0)
    return pl.pallas_call(
        _rank_kernel,
        grid=(t // tr,),
        in_specs=[pl.BlockSpec((tr, ne), row), pl.BlockSpec((tr, LANES), row),
                  pl.BlockSpec((1, ne), lambda i: (0, 0))],
        out_specs=pl.BlockSpec((tr, LANES), row),
        out_shape=jax.ShapeDtypeStruct((t, LANES), jnp.int32),
        scratch_shapes=[pltpu.VMEM((SUBLANES, ne), F32)],
        compiler_params=pltpu.CompilerParams(
            dimension_semantics=("arbitrary",), vmem_limit_bytes=VMEM_LIMIT),
        name="rank",
    )(mask, eidx, pstart)


def _dispatch_kernel(ps_ref, pl_ref, nu_ref, d_ref, h_hbm, xs_hbm, zbuf, sem, zsem, *, tt, nsub, bm, nb):
    i = pl.program_id(0)
    n = pl.num_programs(0)
    ne = ps_ref.shape[0]
    pieces = [1 << b for b in reversed(range(bm.bit_length() - 1))]

    def zero_fill(op):
        def pad(e, carry):
            start = ps_ref[e]
            length = pl_ref[e]
            for p in pieces:
                off = length - lax.rem(length, 2 * p)

                @pl.when(lax.rem(length, 2 * p) >= p)
                def _():
                    dst = xs_hbm.at[pl.ds(pl.multiple_of((start + off) * nsub, nsub), p * nsub)]
                    op(pltpu.make_async_copy(zbuf.at[pl.ds(0, p * nsub)], dst, zsem))
            return carry
        lax.fori_loop(0, ne, pad, 0)

        def tail(b, carry):
            dst = xs_hbm.at[pl.ds(pl.multiple_of(b * (bm * nsub), bm * nsub), bm * nsub)]
            op(pltpu.make_async_copy(zbuf, dst, zsem))
            return carry
        lax.fori_loop(nu_ref[0], nb, tail, 0)

    @pl.when(i == 0)
    def _():
        zbuf[...] = jnp.zeros_like(zbuf)
        zero_fill(lambda c: c.start())

    def body(r, carry):
        src = h_hbm.at[pl.ds(pl.multiple_of((i * tt + r) * nsub, nsub), nsub)]
        for k in range(TOP_K):
            dst = xs_hbm.at[pl.ds(pl.multiple_of(d_ref[0, 0, r * TOP_K + k] * nsub, nsub), nsub)]
            pltpu.make_async_copy(src, dst, sem.at[i % 2]).start()
        return carry

    lax.fori_loop(0, tt, body, 0)

    def wait_step(s):
        nrows = tt * TOP_K * nsub
        pltpu.make_async_copy(h_hbm.at[pl.ds(0, nrows)], xs_hbm.at[pl.ds(0, nrows)], sem.at[s]).wait()

    @pl.when(i > 0)
    def _():
        wait_step((i - 1) % 2)

    @pl.when(i == n - 1)
    def _():
        wait_step(i % 2)
        zero_fill(lambda c: c.wait())


def _dispatch(dest, h2_tiles, pad_start, pad_len, n_used, nb, nsub):
    t = dest.shape[0]
    tt = _row_tile(t, (256,))
    nt = t // tt
    bm = EXPERT_ROWS
    d3 = dest.reshape(nt, 1, tt * TOP_K)
    gs = pltpu.PrefetchScalarGridSpec(
        num_scalar_prefetch=3,
        grid=(nt,),
        in_specs=[pl.BlockSpec((1, 1, tt * TOP_K), lambda i, *_: (i, 0, 0), memory_space=pltpu.SMEM),
                  pl.BlockSpec(memory_space=pl.ANY)],
        out_specs=pl.BlockSpec(memory_space=pl.ANY),
        scratch_shapes=[pltpu.VMEM((bm * nsub, LANES), F32), pltpu.SemaphoreType.DMA((2,)),
                        pltpu.SemaphoreType.DMA],
    )
    return pl.pallas_call(
        functools.partial(_dispatch_kernel, tt=tt, nsub=nsub, bm=bm, nb=nb),
        grid_spec=gs,
        out_shape=jax.ShapeDtypeStruct((nb * bm * nsub, LANES), F32),
        compiler_params=pltpu.CompilerParams(dimension_semantics=("arbitrary",)),
        name="dispatch",
    )(pad_start, pad_len, n_used, d3, h2_tiles)


def _grouped_kernel(be_ref, nu_ref, x_ref, wg_ref, wu_ref, wd_ref, y_ref, wgb, wub, wdb):
    i = pl.program_id(0)

    @pl.when(i < nu_ref[0])
    def _():
        @pl.when(jnp.logical_or(i == 0, be_ref[i] != be_ref[jnp.maximum(i - 1, 0)]))
        def _():
            wgb[...] = wg_ref[0].astype(BF16)
            wub[...] = wu_ref[0].astype(BF16)
            wdb[...] = wd_ref[0].astype(BF16)

        d = wgb.shape[0]
        nsub = d // LANES
        x = _read_rows(x_ref, x_ref.shape[0] // nsub, nsub).astype(BF16)
        hg = jnp.dot(x, wgb[...], preferred_element_type=F32)
        hu = jnp.dot(x, wub[...], preferred_element_type=F32)
        y = jnp.dot((_silu(hg) * hu).astype(BF16), wdb[...], preferred_element_type=F32)
        _write_rows(y_ref, y)

    @pl.when(i >= nu_ref[0])
    def _():
        y_ref[...] = jnp.zeros_like(y_ref)


def _grouped(xs, block_e, n_used, wg, wu, wd):
    ne, d, ff = wg.shape
    nsub = d // LANES
    bm = EXPERT_ROWS
    nb = block_e.shape[0]
    used = lambda i, nu: jnp.minimum(i, nu[0] - 1)
    gs = pltpu.PrefetchScalarGridSpec(
        num_scalar_prefetch=2,
        grid=(nb,),
        in_specs=[pl.BlockSpec((bm * nsub, LANES), lambda i, be, nu: (used(i, nu), 0)),
                  pl.BlockSpec((1, d, ff), lambda i, be, nu: (be[used(i, nu)], 0, 0)),
                  pl.BlockSpec((1, d, ff), lambda i, be, nu: (be[used(i, nu)], 0, 0)),
                  pl.BlockSpec((1, ff, d), lambda i, be, nu: (be[used(i, nu)], 0, 0))],
        out_specs=pl.BlockSpec((bm * nsub, LANES), lambda i, be, nu: (i, 0)),
        scratch_shapes=[pltpu.VMEM((d, ff), BF16), pltpu.VMEM((d, ff), BF16), pltpu.VMEM((ff, d), BF16)],
    )
    return pl.pallas_call(
        _grouped_kernel,
        grid_spec=gs,
        out_shape=jax.ShapeDtypeStruct((nb * bm * nsub, LANES), F32),
        compiler_params=pltpu.CompilerParams(
            dimension_semantics=("arbitrary",), vmem_limit_bytes=VMEM_LIMIT),
        name="grouped",
    )(block_e, n_used, xs, wg, wu, wd)


def _combine_kernel(d_ref, dn_ref, y_hbm, w_ref, s_ref, x_ref, g2_ref, fg_ref, o_ref, buf, sem,
                    *, tt, nsub):
    i = pl.program_id(0)
    n = pl.num_programs(0)

    def gather(dref, slot):
        def body(r, carry):
            for k in range(TOP_K):
                src = y_hbm.at[pl.ds(pl.multiple_of(dref[0, 0, r * TOP_K + k] * nsub, nsub), nsub)]
                pltpu.make_async_copy(src, buf.at[slot, k, pl.ds(pl.multiple_of(r * nsub, nsub), nsub)],
                                      sem.at[slot]).start()
            return carry
        lax.fori_loop(0, tt, body, 0)

    @pl.when(i == 0)
    def _():
        gather(d_ref, 0)

    @pl.when(i + 1 < n)
    def _():
        gather(dn_ref, (i + 1) % 2)

    slot = i % 2
    for k in range(TOP_K):
        pltpu.make_async_copy(y_hbm.at[pl.ds(0, tt * nsub)], buf.at[slot, k], sem.at[slot]).wait()

    w = w_ref[...]
    routed = w[:, 0:1] * _read_rows(buf.at[slot, 0], tt, nsub)
    for k in range(1, TOP_K):
        routed = routed + w[:, k:k + 1] * _read_rows(buf.at[slot, k], tt, nsub)
    moe = routed + s_ref[...]
    d = moe.shape[1]
    m3 = moe.reshape(tt // MOD_ROWS, MOD_ROWS, d) * g2_ref[...][:, None, :]
    x2 = x_ref[...] + m3.reshape(tt, d)
    o_ref[...] = _rms(x2, fg_ref[...], RMS_EPS)


def _combine(dest, y_tiles, wts, shared, x1, g2_rows, final_g):
    t, d = x1.shape
    nsub = d // LANES
    tt = _row_tile(t, (256,))
    mr = tt // MOD_ROWS
    nt = t // tt
    d3 = dest.reshape(nt, 1, tt * TOP_K)
    kern = functools.partial(_combine_kernel, tt=tt, nsub=nsub)
    smem_blk = lambda f: pl.BlockSpec((1, 1, tt * TOP_K), f, memory_space=pltpu.SMEM)
    row = lambda i: (i, 0)
    return pl.pallas_call(
        kern,
        grid=(nt,),
        in_specs=[smem_blk(lambda i: (i, 0, 0)),
                  smem_blk(lambda i: (jnp.minimum(i + 1, nt - 1), 0, 0)),
                  pl.BlockSpec(memory_space=pl.ANY),
                  pl.BlockSpec((tt, LANES), row), pl.BlockSpec((tt, d), row), pl.BlockSpec((tt, d), row),
                  pl.BlockSpec((mr, d), row), pl.BlockSpec((1, d), lambda i: (0, 0))],
        out_specs=pl.BlockSpec((tt, d), row),
        out_shape=jax.ShapeDtypeStruct((t, d), F32),
        scratch_shapes=[pltpu.VMEM((2, TOP_K, tt * nsub, LANES), F32), pltpu.SemaphoreType.DMA((2,))],
        compiler_params=pltpu.CompilerParams(
            dimension_semantics=("arbitrary",), vmem_limit_bytes=VMEM_LIMIT),
        name="combine",
    )(d3, d3, y_tiles, wts, shared, x1, g2_rows, final_g.reshape(1, d))


def _plan(counts, n_assign):
    ne = counts.shape[0]
    bm = EXPERT_ROWS
    counts = counts.astype(jnp.int32)
    nblk = (counts + bm - 1) // bm
    bend = jnp.cumsum(nblk)
    pstart = ((bend - nblk) * bm).astype(F32).reshape(1, ne)
    nb = -(-(n_assign + ne * (bm - 1)) // bm)
    block_e = jnp.minimum(jnp.searchsorted(bend, jnp.arange(nb), side='right'), ne - 1).astype(jnp.int32)
    n_used = bend[-1:].astype(jnp.int32)
    pad_start = ((bend - nblk) * bm + counts).astype(jnp.int32)
    pad_len = (nblk * bm - counts).astype(jnp.int32)
    return pstart, block_e, n_used, pad_start, pad_len, nb


def per_g_pow2(ne):
    per_g = ne // N_EXPERT_GROUPS
    return per_g * N_EXPERT_GROUPS == ne and per_g & (per_g - 1) == 0


def _mod_rows(m, nbp, lp):
    return jnp.concatenate([jnp.repeat(m[:nbp], lp // MOD_ROWS, axis=0), m[nbp:]], axis=0)


def kernel(x_prompt, x_sample, state_ssm, state_conv, state_pool, c_prompt, c_sample, ln1_g, ln2_g, w_ada, b_ada, w_in, conv_w, conv_b, dt_bias, a_log, d_skip, ssd_norm_g, w_ssd_out, pool_w, pool_scale, w_out, w_router, router_bias, moe_w_gate, moe_w_up, moe_w_down, shared_w_gate, shared_w_up, shared_w_down, final_g):
    bp, lp, dm = x_prompt.shape
    bs, ls, _ = x_sample.shape
    depth = ln1_g.shape[0]
    heads, hd, ns = state_ssm.shape[2], state_ssm.shape[3], state_ssm.shape[4]
    di = heads * hd
    gn = SSD_GROUPS * ns
    cch = di + 2 * gn
    assert depth == 1 and ls == MOD_ROWS and lp % MOD_ROWS == 0 and heads <= LANES
    assert per_g_pow2(w_router.shape[2])
    tp, ts = bp * lp, bs * ls

    x_all = jnp.concatenate([x_prompt.reshape(tp, dm), x_sample.reshape(ts, dm)], axis=0)
    c_all = jnp.concatenate([c_prompt, c_sample], axis=0)

    o1, o2, o3, o4 = di, di + cch, di + cch + heads, di + cch + heads + dm
    cols = (0, di, 2 * di, di + cch, di + cch + dm, di + cch + 3 * dm)

    ssm_p, conv_p, pool_p, ssm_s, conv_s, pool_s = [], [], [], [], [], []
    for l in range(depth):
        wi = w_in[l]
        wcat = jnp.concatenate(
            [wi[:, :o1], wi[:, o1:o2], wi[:, o3:o4], wi[:, o4:],
             jnp.pad(wi[:, o2:o3], ((0, 0), (0, LANES - heads)))], axis=1).astype(BF16)
        pad_h = lambda v: jnp.pad(v.reshape(1, heads), ((0, 0), (0, LANES - heads)))
        seq_w = (conv_w[l], conv_b[l].reshape(1, cch), pad_h(dt_bias[l]), pad_h(a_log[l]),
                 jnp.repeat(d_skip[l], hd).reshape(1, di), ssd_norm_g[l].reshape(1, di),
                 w_ssd_out[l].astype(BF16), pool_w[l].astype(BF16), pool_scale[l].reshape(1, dm),
                 w_out[l].astype(BF16))

        mod = _ada(c_all, w_ada[l], b_ada[l])
        sh1, sc1, g1, sh2, sc2, g2 = jnp.split(mod, 6, axis=-1)

        proj = _inproj(x_all, _mod_rows(sc1, bp, lp), _mod_rows(sh1, bp, lp), ln1_g[l], wcat)

        zc = jnp.zeros((bp,) + state_conv.shape[2:], F32)
        zs = jnp.zeros((bp, heads, hd, ns), F32)
        zp = jnp.zeros((bp,) + state_pool.shape[2:], F32)
        xp1, ns_p, nc_p, np_p = _seq(proj, x_all, g1[:bp], zc, zs, zp, seq_w,
                                     row0=0, nb=bp, seqlen=lp, pos0=0, cols=cols)
        xs1, ns_s, nc_s, np_s = _seq(proj, x_all, g1[bp:], state_conv[l], state_ssm[l], state_pool[l],
                                     seq_w, row0=tp, nb=bs, seqlen=ls, pos0=PAST_LEN, cols=cols)
        x1 = jnp.concatenate([xp1, xs1], axis=0)
        conv_p.append(nc_p)
        conv_s.append(nc_s)
        pool_p.append(np_p)
        pool_s.append(np_s)
        ssm_p.append(ns_p)
        ssm_s.append(ns_s)

        h2_tiles, shared, eidx_l, wts, mask, counts = _moe_pre(
            x1, _mod_rows(sc2, bp, lp), _mod_rows(sh2, bp, lp), ln2_g[l],
            w_router[l].astype(BF16), router_bias[l],
            shared_w_gate[l].astype(BF16), shared_w_up[l].astype(BF16), shared_w_down[l].astype(BF16))
        nsub = dm // LANES
        pstart, block_e, n_used, pad_start, pad_len, nb = _plan(counts[0], (tp + ts) * TOP_K)
        dest = _rank(mask, eidx_l, pstart)[:, :TOP_K]
        xs = _dispatch(dest, h2_tiles, pad_start, pad_len, n_used, nb, nsub)
        y_tiles = _grouped(xs, block_e, n_used, moe_w_gate[l], moe_w_up[l], moe_w_down[l])
        x_all = _combine(dest, y_tiles, wts, shared, x1, _mod_rows(g2, bp, lp), final_g)

    y_prompt = x_all[:tp].reshape(bp, lp, dm)
    y_sample = x_all[tp:].reshape(bs, ls, dm)
    return (y_prompt, y_sample, jnp.stack(ssm_p), jnp.stack(conv_p), jnp.stack(pool_p),
            jnp.stack(ssm_s), jnp.stack(conv_s), jnp.stack(pool_s))
```

```python
import functools

import jax
import jax.numpy as jnp
from jax import lax
from jax.experimental import pallas as pl
from jax.experimental.pallas import tpu as pltpu

F32 = jnp.float32
BF16 = jnp.bfloat16
HIGHEST = lax.Precision.HIGHEST

RMS_EPS = 1e-6
SSD_NORM_EPS = 1e-5
CHUNK = 64
SSD_GROUPS = 4
POOL_WINDOWS = (2, 4, 8, 16)
PAST_LEN = 1024
TOP_K = 8
N_EXPERT_GROUPS = 8
TOPK_GROUPS = 4
ROUTED_SCALE = 2.5

LANES = 128
SUBLANES = 8
MOD_ROWS = 32
VMEM_LIMIT = 56 * 1024 * 1024
EXPERT_ROWS = 256


def _silu(x):
    return x * jax.nn.sigmoid(x)


def _softplus(x):
    return jnp.maximum(x, 0.0) + jnp.log1p(jnp.exp(-jnp.abs(x)))


def _row_tile(n, prefs):
    for t in prefs:
        if n % t == 0:
            return t
    return n


def _modulate(y, sc_ref, sh_ref):
    rows, d = y.shape
    y3 = y.reshape(rows // MOD_ROWS, MOD_ROWS, d)
    y3 = y3 * (1.0 + sc_ref[...][:, None, :]) + sh_ref[...][:, None, :]
    return y3.reshape(rows, d)


def _rms(x, g, eps):
    return x * lax.rsqrt(jnp.mean(x * x, axis=-1, keepdims=True) + eps) * g


def _ada_kernel(c_ref, w_ref, b_ref, o_ref):
    s = _silu(c_ref[...])
    o_ref[...] = jnp.dot(s, w_ref[...], preferred_element_type=F32, precision=HIGHEST) + b_ref[...]


def _ada(c_all, w_ada, b_ada):
    n, d = c_all.shape
    dout = w_ada.shape[1]
    tn = _row_tile(dout, (1024, 512, 256, 128))
    return pl.pallas_call(
        _ada_kernel,
        grid=(dout // tn,),
        in_specs=[pl.BlockSpec((n, d), lambda j: (0, 0)),
                  pl.BlockSpec((d, tn), lambda j: (0, j)),
                  pl.BlockSpec((1, tn), lambda j: (0, j))],
        out_specs=pl.BlockSpec((n, tn), lambda j: (0, j)),
        out_shape=jax.ShapeDtypeStruct((n, dout), F32),
        name="ada",
    )(c_all, w_ada, b_ada.reshape(1, dout))


def _inproj_kernel(x_ref, sc_ref, sh_ref, g_ref, w_ref, o_ref, h_scr):
    @pl.when(pl.program_id(1) == 0)
    def _():
        y = _rms(x_ref[...], g_ref[...], RMS_EPS)
        h_scr[...] = _modulate(y, sc_ref, sh_ref).astype(BF16)

    o_ref[...] = jnp.dot(h_scr[...], w_ref[...], preferred_element_type=F32)


def _inproj(x_all, sc_rows, sh_rows, g, wcat):
    t, d = x_all.shape
    n = wcat.shape[1]
    tm = _row_tile(t, (512, 256))
    tn = _row_tile(n, (1664, 1280, 1024, 640, 512, 256, 128))
    mr = tm // MOD_ROWS
    return pl.pallas_call(
        _inproj_kernel,
        grid=(t // tm, n // tn),
        in_specs=[pl.BlockSpec((tm, d), lambda i, j: (i, 0)),
                  pl.BlockSpec((mr, d), lambda i, j: (i, 0)),
                  pl.BlockSpec((mr, d), lambda i, j: (i, 0)),
                  pl.BlockSpec((1, d), lambda i, j: (0, 0)),
                  pl.BlockSpec((d, tn), lambda i, j: (0, j))],
        out_specs=pl.BlockSpec((tm, tn), lambda i, j: (i, j)),
        out_shape=jax.ShapeDtypeStruct((t, n), F32),
        scratch_shapes=[pltpu.VMEM((tm, d), BF16)],
        compiler_params=pltpu.CompilerParams(
            dimension_semantics=("parallel", "arbitrary"), vmem_limit_bytes=VMEM_LIMIT),
        name="inproj",
    )(x_all, sc_rows, sh_rows, g.reshape(1, d), wcat)


def _seq_kernel(z_ref, xp_ref, bc_ref, u_ref, gt_ref, dt_ref, x_ref, g1_ref,
                cst_ref, sst_ref, pst_ref, cw_ref, cb_ref, dtb_ref, alog_ref, dx_ref, ng_ref,
                wssd_ref, pw_ref, ps_ref, wout_ref,
                xo_ref, so_ref, co_ref, po_ref,
                cbuf, pbuf, s_scr, y_scr, xbc_scr, dt_scr,
                *, tq, q, pos0, heads, hd, ns, dm):
    j = pl.program_id(1)
    di = heads * hd
    gn = SSD_GROUPS * ns
    hpg = heads // SSD_GROUPS
    cw = cbuf.shape[0] - tq
    kw = cw_ref.shape[0]
    ph = pbuf.shape[0] - tq

    @pl.when(j == 0)
    def _():
        cbuf[cw - (kw - 1):cw, :] = cst_ref[0]
        pbuf[1:ph, :] = pst_ref[0]
        s_scr[...] = sst_ref[0]

    cbuf[cw:cw + tq, 0:di] = xp_ref[...]
    cbuf[cw:cw + tq, di:di + 2 * gn] = bc_ref[...]
    acc = cb_ref[...] + cbuf[cw:cw + tq, :] * cw_ref[kw - 1:kw, :]
    for k in range(kw - 1):
        off = cw - (kw - 1) + k
        acc = acc + cbuf[off:off + tq, :] * cw_ref[k:k + 1, :]
    xbc_scr[...] = _silu(acc)
    cbuf[cw - (kw - 1):cw, :] = cbuf[cw + tq - (kw - 1):cw + tq, :]

    dt_scr[...] = _softplus(dt_ref[...] + dtb_ref[...])
    a_row = -jnp.exp(alog_ref[...])

    r_i = lax.broadcasted_iota(jnp.int32, (q, q), 0)
    c_i = lax.broadcasted_iota(jnp.int32, (q, q), 1)
    causal = r_i >= c_i
    tril = causal.astype(F32)

    def chunk(c, carry):
        r0 = pl.multiple_of(c * q, q)
        dt_c = dt_scr[pl.ds(r0, q), :]
        cs = jnp.dot(tril, dt_c * a_row, preferred_element_type=F32, precision=HIGHEST)
        cs_end = cs[q - 1:q, :]
        wv = dt_c * jnp.exp(cs_end - cs)
        ecs = jnp.exp(cs)
        dec_end = jnp.exp(cs_end)
        cs_t = cs.T
        dt_t = dt_c.T
        for g in range(SSD_GROUPS):
            bg = xbc_scr[pl.ds(r0, q), di + g * ns:di + (g + 1) * ns]
            cg = xbc_scr[pl.ds(r0, q), di + gn + g * ns:di + gn + (g + 1) * ns]
            bgb = bg.astype(BF16)
            cb = lax.dot_general(cg.astype(BF16), bgb, (((1,), (1,)), ((), ())),
                                 preferred_element_type=F32)
            for r in range(hpg):
                h = g * hpg + r
                seg = cs[:, h:h + 1] - cs_t[h:h + 1, :]
                lm = jnp.exp(jnp.where(causal, seg, -jnp.inf))
                m = cb * lm * dt_t[h:h + 1, :]
                xh = xbc_scr[pl.ds(r0, q), h * hd:(h + 1) * hd]
                sh = s_scr[h * hd:(h + 1) * hd, :]
                yd = jnp.dot(m.astype(BF16), xh.astype(BF16), preferred_element_type=F32)
                yo = lax.dot_general((cg * ecs[:, h:h + 1]).astype(BF16), sh.astype(BF16),
                                     (((1,), (1,)), ((), ())), preferred_element_type=F32)
                y_scr[pl.ds(r0, q), h * hd:(h + 1) * hd] = yd + yo
                xw = (xh * wv[:, h:h + 1]).astype(BF16)
                upd = lax.dot_general(xw, bgb, (((0,), (0,)), ((), ())),
                                      preferred_element_type=F32)
                s_scr[h * hd:(h + 1) * hd, :] = sh * dec_end[:, h:h + 1] + upd
        return carry

    lax.fori_loop(0, tq // q, chunk, 0)

    @pl.when(j == pl.num_programs(1) - 1)
    def _():
        so_ref[0] = s_scr[...]
        co_ref[0] = cbuf[cw - (kw - 1):cw, :]

    y = y_scr[...] + xbc_scr[:, 0:di] * dx_ref[...]
    v = y * _silu(z_ref[...])
    gw = di // SSD_GROUPS
    parts = []
    for g in range(SSD_GROUPS):
        vg = v[:, g * gw:(g + 1) * gw]
        parts.append(vg * lax.rsqrt(jnp.mean(vg * vg, axis=-1, keepdims=True) + SSD_NORM_EPS))
    yn = (jnp.concatenate(parts, axis=-1) * ng_ref[...]).astype(BF16)
    a_br = jnp.dot(yn, wssd_ref[...], preferred_element_type=F32)

    pbuf[ph:ph + tq, :] = u_ref[...]
    pos = pos0 + j * tq + lax.broadcasted_iota(jnp.int32, (tq, 1), 0)
    pgd = dm // len(POOL_WINDOWS)
    b_parts = []
    for gi, w in enumerate(POOL_WINDOWS):
        lo = gi * pgd
        s = pbuf[ph:ph + tq, lo:lo + pgd]
        for i in range(1, w):
            s = s + pbuf[ph - i:ph - i + tq, lo:lo + pgd]
        cnt = jnp.minimum(w, pos + 1).astype(F32)
        pooled = s / cnt - pbuf[ph:ph + tq, lo:lo + pgd]
        b_parts.append(jnp.dot(pooled.astype(BF16), pw_ref[gi], preferred_element_type=F32))
    b_br = jnp.concatenate(b_parts, axis=-1) * ps_ref[...]
    pbuf[1:ph, :] = pbuf[tq + 1:tq + ph, :]

    @pl.when(j == pl.num_programs(1) - 1)
    def _():
        po_ref[0] = pbuf[1:ph, :]

    gates = jax.nn.sigmoid(gt_ref[...])
    merged = gates[:, 0:dm] * a_br + gates[:, dm:2 * dm] * b_br
    mix = jnp.dot(merged.astype(BF16), wout_ref[...], preferred_element_type=F32)
    xo_ref[...] = x_ref[...] + g1_ref[0] * mix


def _seq(proj, x_all, g1, conv_state, ssm_state, pool_state, wts, *, row0, nb, seqlen, pos0, cols):
    dm = x_all.shape[1]
    heads, hd, ns = ssm_state.shape[1], ssm_state.shape[2], ssm_state.shape[3]
    di = heads * hd
    gn = SSD_GROUPS * ns
    q = min(CHUNK, seqlen)
    tq = _row_tile(seqlen, (128, 64, 32))
    nj = seqlen // tq
    rb0 = row0 // tq
    oz, ox, obc, ou, og, odt = cols

    def rows(b, j):
        return rb0 + b * nj + j

    full = lambda a: pl.BlockSpec(a.shape, lambda b, j: (0,) * a.ndim)
    kern = functools.partial(_seq_kernel, tq=tq, q=q, pos0=pos0, heads=heads, hd=hd, ns=ns, dm=dm)
    kw1 = conv_state.shape[1]
    pst = pool_state.shape[1]
    xo, so, co, po = pl.pallas_call(
        kern,
        grid=(nb, nj),
        in_specs=[
            pl.BlockSpec((tq, di), lambda b, j: (rows(b, j), oz // di)),
            pl.BlockSpec((tq, di), lambda b, j: (rows(b, j), ox // di)),
            pl.BlockSpec((tq, 2 * gn), lambda b, j: (rows(b, j), obc // (2 * gn))),
            pl.BlockSpec((tq, dm), lambda b, j: (rows(b, j), ou // dm)),
            pl.BlockSpec((tq, 2 * dm), lambda b, j: (rows(b, j), og // (2 * dm))),
            pl.BlockSpec((tq, LANES), lambda b, j: (rows(b, j), odt // LANES)),
            pl.BlockSpec((tq, dm), lambda b, j: (rows(b, j), 0)),
            pl.BlockSpec((1, 1, dm), lambda b, j: (b, 0, 0)),
            pl.BlockSpec((1, kw1, di + 2 * gn), lambda b, j: (b, 0, 0)),
            pl.BlockSpec((1, di, ns), lambda b, j: (b, 0, 0)),
            pl.BlockSpec((1, pst, dm), lambda b, j: (b, 0, 0)),
        ] + [full(w) for w in wts],
        out_specs=[pl.BlockSpec((tq, dm), lambda b, j: (b * nj + j, 0)),
                   pl.BlockSpec((1, di, ns), lambda b, j: (b, 0, 0)),
                   pl.BlockSpec((1, kw1, di + 2 * gn), lambda b, j: (b, 0, 0)),
                   pl.BlockSpec((1, pst, dm), lambda b, j: (b, 0, 0))],
        out_shape=[jax.ShapeDtypeStruct((nb * seqlen, dm), F32),
                   jax.ShapeDtypeStruct((nb, di, ns), F32),
                   jax.ShapeDtypeStruct((nb, kw1, di + 2 * gn), F32),
                   jax.ShapeDtypeStruct((nb, pst, dm), F32)],
        scratch_shapes=[pltpu.VMEM((tq + SUBLANES, di + 2 * gn), F32),
                        pltpu.VMEM((tq + 2 * SUBLANES, dm), F32),
                        pltpu.VMEM((di, ns), F32),
                        pltpu.VMEM((tq, di), F32),
                        pltpu.VMEM((tq, di + 2 * gn), F32),
                        pltpu.VMEM((tq, LANES), F32)],
        compiler_params=pltpu.CompilerParams(
            dimension_semantics=("parallel", "arbitrary"), vmem_limit_bytes=VMEM_LIMIT),
        name="seq",
    )(proj, proj, proj, proj, proj, proj, x_all, g1.reshape(nb, 1, dm),
      conv_state, ssm_state.reshape(nb, di, ns), pool_state, *wts)
    return xo, so.reshape(nb, heads, hd, ns), co, po


def _moe_pre_kernel(x_ref, sc_ref, sh_ref, g_ref, wr_ref, rb_ref, swg_ref, swu_ref, swd_ref,
                    h_ref, s_ref, e_ref, w_ref, m_ref, cnt_ref):
    y = _rms(x_ref[...], g_ref[...], RMS_EPS)
    h = _modulate(y, sc_ref, sh_ref)
    _write_rows(h_ref, h)
    hb = h.astype(BF16)

    sg = jnp.dot(hb, swg_ref[...], preferred_element_type=F32)
    su = jnp.dot(hb, swu_ref[...], preferred_element_type=F32)
    s_ref[...] = jnp.dot((_silu(sg) * su).astype(BF16), swd_ref[...], preferred_element_type=F32)

    scores = jax.nn.sigmoid(jnp.dot(hb, wr_ref[...], preferred_element_type=F32))
    biased = scores + rb_ref[...]
    tm, ne = scores.shape
    per_g = ne // N_EXPERT_GROUPS
    neg = -jnp.inf
    lane_i = lax.broadcasted_iota(jnp.int32, (tm, ne), 1)
    lane = lane_i.astype(F32)
    lgroup = lax.shift_right_logical(lane_i, per_g.bit_length() - 1).astype(F32)
    lane_o = lax.broadcasted_iota(jnp.int32, (tm, LANES), 1).astype(F32)

    def first_argmax(v, idx, big):
        top = jnp.max(v, axis=-1, keepdims=True)
        return top, jnp.min(jnp.where(v == top, idx, big), axis=-1, keepdims=True)

    gs = jnp.full((tm, LANES), neg, F32)
    for g in range(N_EXPERT_GROUPS):
        mg = jnp.where(lgroup == g, biased, neg)
        t1, i1 = first_argmax(mg, lane, float(ne))
        t2 = jnp.max(jnp.where(lane == i1, neg, mg), axis=-1, keepdims=True)
        gs = jnp.where(lane_o == g, t1 + t2, gs)
    allowed = jnp.zeros((tm, ne), F32)
    for _ in range(TOPK_GROUPS):
        _, gi = first_argmax(gs, lane_o, float(LANES))
        allowed = jnp.where(lgroup == gi, 1.0, allowed)
        gs = jnp.where(lane_o == gi, neg, gs)
    mb = jnp.where(allowed > 0.0, biased, neg)
    eacc = jnp.zeros((tm, LANES), F32)
    wacc = jnp.zeros((tm, LANES), F32)
    chosen = jnp.zeros((tm, ne), F32)
    for k in range(TOP_K):
        _, ik = first_argmax(mb, lane, float(ne))
        sel = lane == ik
        wk = jnp.sum(jnp.where(sel, scores, 0.0), axis=-1, keepdims=True)
        eacc = jnp.where(lane_o == k, ik, eacc)
        wacc = jnp.where(lane_o == k, wk, wacc)
        chosen = jnp.where(sel, 1.0, chosen)
        mb = jnp.where(sel, neg, mb)
    wsum = jnp.sum(wacc, axis=-1, keepdims=True)
    e_ref[...] = eacc.astype(jnp.int32)
    w_ref[...] = wacc / (wsum + 1e-20) * ROUTED_SCALE
    m_ref[...] = chosen.astype(BF16)

    @pl.when(pl.program_id(0) == 0)
    def _():
        cnt_ref[...] = jnp.zeros_like(cnt_ref)

    cnt_ref[0:1, :] += jnp.sum(chosen, axis=0, keepdims=True)


def _moe_pre(x1, sc_rows, sh_rows, g, wr, rb, swg, swu, swd):
    t, d = x1.shape
    ne = wr.shape[1]
    ff = swg.shape[1]
    tm = _row_tile(t, (256,))
    mr = tm // MOD_ROWS
    nsub = d // LANES
    row = lambda i: (i, 0)
    const = lambda i: (0, 0)
    return pl.pallas_call(
        _moe_pre_kernel,
        grid=(t // tm,),
        in_specs=[pl.BlockSpec((tm, d), row), pl.BlockSpec((mr, d), row), pl.BlockSpec((mr, d), row),
                  pl.BlockSpec((1, d), const), pl.BlockSpec((d, ne), const), pl.BlockSpec((1, ne), const),
                  pl.BlockSpec((d, ff), const), pl.BlockSpec((d, ff), const), pl.BlockSpec((ff, d), const)],
        out_specs=[pl.BlockSpec((tm * nsub, LANES), row), pl.BlockSpec((tm, d), row),
                   pl.BlockSpec((tm, LANES), row), pl.BlockSpec((tm, LANES), row),
                   pl.BlockSpec((tm, ne), row), pl.BlockSpec((SUBLANES, ne), const)],
        out_shape=[jax.ShapeDtypeStruct((t * nsub, LANES), F32), jax.ShapeDtypeStruct((t, d), F32),
                   jax.ShapeDtypeStruct((t, LANES), jnp.int32), jax.ShapeDtypeStruct((t, LANES), F32),
                   jax.ShapeDtypeStruct((t, ne), BF16), jax.ShapeDtypeStruct((SUBLANES, ne), F32)],
        compiler_params=pltpu.CompilerParams(
            dimension_semantics=("arbitrary",), vmem_limit_bytes=VMEM_LIMIT),
        name="moe_pre",
    )(x1, sc_rows, sh_rows, g.reshape(1, d), wr, rb.reshape(1, ne), swg, swu, swd)


def _read_rows(buf, nrows, nsub):
    return jnp.concatenate([buf[pl.ds(c, nrows, stride=nsub), :] for c in range(nsub)], axis=-1)


def _write_rows(ref, v):
    nrows, width = v.shape
    nsub = width // LANES
    for c in range(nsub):
        ref[pl.ds(c, nrows, stride=nsub), :] = v[:, c * LANES:(c + 1) * LANES]


def _rank_kernel(m_ref, e_ref, ps_ref, d_ref, carry):
    i = pl.program_id(0)
    ne = m_ref.shape[1]

    @pl.when(i == 0)
    def _():
        carry[...] = jnp.zeros_like(carry)

    m = m_ref[...]
    tr = m.shape[0]
    r_i = lax.broadcasted_iota(jnp.int32, (tr, tr), 0)
    c_i = lax.broadcasted_iota(jnp.int32, (tr, tr), 1)
    earlier = (r_i > c_i).astype(BF16)
    slot = jnp.dot(earlier, m, preferred_element_type=F32) + carry[0:1, :] + ps_ref[...]
    e = e_ref[...]
    lane = lax.broadcasted_iota(jnp.int32, (tr, ne), 1)
    lane_o = lax.broadcasted_iota(jnp.int32, (tr, LANES), 1)
    dacc = jnp.zeros((tr, LANES), F32)
    for k in range(TOP_K):
        dk = jnp.sum(jnp.where(lane == e[:, k:k + 1], slot, 0.0), axis=-1, keepdims=True)
        dacc = jnp.where(lane_o == k, dk, dacc)
    d_ref[...] = dacc.astype(jnp.int32)
    carry[0:1, :] += jnp.sum(m.astype(F32), axis=0, keepdims=True)


def _rank(mask, eidx, pstart):
    t, ne = mask.shape
    tr = _row_tile(t, (512, 256))
    row = lambda i: (i, 0)
    return pl.pallas_call(
        _rank_kernel,
        grid=(t // tr,),
        in_specs=[pl.BlockSpec((tr, ne), row), pl.BlockSpec((tr, LANES), row),
                  pl.BlockSpec((1, ne), lambda i: (0, 0))],
        out_specs=pl.BlockSpec((tr, LANES), row),
        out_shape=jax.ShapeDtypeStruct((t, LANES), jnp.int32),
        scratch_shapes=[pltpu.VMEM((SUBLANES, ne), F32)],
        compiler_params=pltpu.CompilerParams(
            dimension_semantics=("arbitrary",), vmem_limit_bytes=VMEM_LIMIT),
        name="rank",
    )(mask, eidx, pstart)


def _dispatch_kernel(ps_ref, pl_ref, nu_ref, d_ref, h_ref, xs_hbm, zbuf, sem, zsem, *, tt, nsub, bm, nb):
    i = pl.program_id(0)
    n = pl.num_programs(0)
    ne = ps_ref.shape[0]
    pieces = [1 << b for b in reversed(range(bm.bit_length() - 1))]

    def zero_fill(op):
        def pad(e, carry):
            start = ps_ref[e]
            length = pl_ref[e]
            for p in pieces:
                off = length - lax.rem(length, 2 * p)

                @pl.when(lax.rem(length, 2 * p) >= p)
                def _():
                    dst = xs_hbm.at[pl.ds(pl.multiple_of((start + off) * nsub, nsub), p * nsub)]
                    op(pltpu.make_async_copy(zbuf.at[pl.ds(0, p * nsub)], dst, zsem))
            return carry
        lax.fori_loop(0, ne, pad, 0)

        def tail(b, carry):
            dst = xs_hbm.at[pl.ds(pl.multiple_of(b * (bm * nsub), bm * nsub), bm * nsub)]
            op(pltpu.make_async_copy(zbuf, dst, zsem))
            return carry
        lax.fori_loop(nu_ref[0], nb, tail, 0)

    @pl.when(i == 0)
    def _():
        zbuf[...] = jnp.zeros_like(zbuf)
        zero_fill(lambda c: c.start())

    def body(r, carry):
        src = h_ref.at[pl.ds(pl.multiple_of(r * nsub, nsub), nsub)]
        for k in range(TOP_K):
            dst = xs_hbm.at[pl.ds(pl.multiple_of(d_ref[0, 0, r * TOP_K + k] * nsub, nsub), nsub)]
            pltpu.make_async_copy(src, dst, sem).start(priority=k % 2)
        return carry

    lax.fori_loop(0, tt, body, 0)

    for _ in range(TOP_K):
        pltpu.make_async_copy(h_ref, xs_hbm.at[pl.ds(0, tt * nsub)], sem).wait()

    @pl.when(i == n - 1)
    def _():
        zero_fill(lambda c: c.wait())


def _dispatch(dest, h2_tiles, pad_start, pad_len, n_used, nb, nsub):
    t = dest.shape[0]
    tt = _row_tile(t, (256,))
    nt = t // tt
    bm = EXPERT_ROWS
    d3 = dest.reshape(nt, 1, tt * TOP_K)
    gs = pltpu.PrefetchScalarGridSpec(
        num_scalar_prefetch=3,
        grid=(nt,),
        in_specs=[pl.BlockSpec((1, 1, tt * TOP_K), lambda i, *_: (i, 0, 0), memory_space=pltpu.SMEM),
                  pl.BlockSpec((tt * nsub, LANES), lambda i, *_: (i, 0))],
        out_specs=pl.BlockSpec(memory_space=pl.ANY),
        scratch_shapes=[pltpu.VMEM((bm * nsub, LANES), F32), pltpu.SemaphoreType.DMA,
                        pltpu.SemaphoreType.DMA],
    )
    return pl.pallas_call(
        functools.partial(_dispatch_kernel, tt=tt, nsub=nsub, bm=bm, nb=nb),
        grid_spec=gs,
        out_shape=jax.ShapeDtypeStruct((nb * bm * nsub, LANES), F32),
        compiler_params=pltpu.CompilerParams(dimension_semantics=("arbitrary",)),
        name="dispatch",
    )(pad_start, pad_len, n_used, d3, h2_tiles)


def _grouped_kernel(be_ref, nu_ref, x_ref, wg_ref, wu_ref, wd_ref, y_ref, wgb, wub, wdb):
    i = pl.program_id(0)

    @pl.when(i < nu_ref[0])
    def _():
        @pl.when(jnp.logical_or(i == 0, be_ref[i] != be_ref[jnp.maximum(i - 1, 0)]))
        def _():
            wgb[...] = wg_ref[0].astype(BF16)
            wub[...] = wu_ref[0].astype(BF16)
            wdb[...] = wd_ref[0].astype(BF16)

        d = wgb.shape[0]
        nsub = d // LANES
        x = _read_rows(x_ref, x_ref.shape[0] // nsub, nsub).astype(BF16)
        hg = jnp.dot(x, wgb[...], preferred_element_type=F32)
        hu = jnp.dot(x, wub[...], preferred_element_type=F32)
        y = jnp.dot((_silu(hg) * hu).astype(BF16), wdb[...], preferred_element_type=F32)
        _write_rows(y_ref, y)

    @pl.when(i >= nu_ref[0])
    def _():
        y_ref[...] = jnp.zeros_like(y_ref)


def _grouped(xs, block_e, n_used, wg, wu, wd):
    ne, d, ff = wg.shape
    nsub = d // LANES
    bm = EXPERT_ROWS
    nb = block_e.shape[0]
    used = lambda i, nu: jnp.minimum(i, nu[0] - 1)
    gs = pltpu.PrefetchScalarGridSpec(
        num_scalar_prefetch=2,
        grid=(nb,),
        in_specs=[pl.BlockSpec((bm * nsub, LANES), lambda i, be, nu: (used(i, nu), 0)),
                  pl.BlockSpec((1, d, ff), lambda i, be, nu: (be[used(i, nu)], 0, 0)),
                  pl.BlockSpec((1, d, ff), lambda i, be, nu: (be[used(i, nu)], 0, 0)),
                  pl.BlockSpec((1, ff, d), lambda i, be, nu: (be[used(i, nu)], 0, 0))],
        out_specs=pl.BlockSpec((bm * nsub, LANES), lambda i, be, nu: (i, 0)),
        scratch_shapes=[pltpu.VMEM((d, ff), BF16), pltpu.VMEM((d, ff), BF16), pltpu.VMEM((ff, d), BF16)],
    )
    return pl.pallas_call(
        _grouped_kernel,
        grid_spec=gs,
        out_shape=jax.ShapeDtypeStruct((nb * bm * nsub, LANES), F32),
        compiler_params=pltpu.CompilerParams(
            dimension_semantics=("arbitrary",), vmem_limit_bytes=VMEM_LIMIT),
        name="grouped",
    )(block_e, n_used, xs, wg, wu, wd)


def _combine_kernel(d_ref, dn_ref, y_hbm, w_ref, s_ref, x_ref, g2_ref, fg_ref, o_ref, buf, sem,
                    *, tt, nsub):
    i = pl.program_id(0)
    n = pl.num_programs(0)

    def gather(dref, slot):
        def body(r, carry):
            for k in range(TOP_K):
                src = y_hbm.at[pl.ds(pl.multiple_of(dref[0, 0, r * TOP_K + k] * nsub, nsub), nsub)]
                pltpu.make_async_copy(src, buf.at[slot, k, pl.ds(pl.multiple_of(r * nsub, nsub), nsub)],
                                      sem.at[slot]).start(priority=k % 2)
            return carry
        lax.fori_loop(0, tt, body, 0)

    @pl.when(i == 0)
    def _():
        gather(d_ref, 0)

    @pl.when(i + 1 < n)
    def _():
        gather(dn_ref, (i + 1) % 2)

    slot = i % 2
    for k in range(TOP_K):
        pltpu.make_async_copy(y_hbm.at[pl.ds(0, tt * nsub)], buf.at[slot, k], sem.at[slot]).wait()

    w = w_ref[...]
    routed = w[:, 0:1] * _read_rows(buf.at[slot, 0], tt, nsub)
    for k in range(1, TOP_K):
        routed = routed + w[:, k:k + 1] * _read_rows(buf.at[slot, k], tt, nsub)
    moe = routed + s_ref[...]
    d = moe.shape[1]
    m3 = moe.reshape(tt // MOD_ROWS, MOD_ROWS, d) * g2_ref[...][:, None, :]
    x2 = x_ref[...] + m3.reshape(tt, d)
    o_ref[...] = _rms(x2, fg_ref[...], RMS_EPS)


def _combine(dest, y_tiles, wts, shared, x1, g2_rows, final_g):
    t, d = x1.shape
    nsub = d // LANES
    tt = _row_tile(t, (256,))
    mr = tt // MOD_ROWS
    nt = t // tt
    d3 = dest.reshape(nt, 1, tt * TOP_K)
    kern = functools.partial(_combine_kernel, tt=tt, nsub=nsub)
    smem_blk = lambda f: pl.BlockSpec((1, 1, tt * TOP_K), f, memory_space=pltpu.SMEM)
    row = lambda i: (i, 0)
    return pl.pallas_call(
        kern,
        grid=(nt,),
        in_specs=[smem_blk(lambda i: (i, 0, 0)),
                  smem_blk(lambda i: (jnp.minimum(i + 1, nt - 1), 0, 0)),
                  pl.BlockSpec(memory_space=pl.ANY),
                  pl.BlockSpec((tt, LANES), row), pl.BlockSpec((tt, d), row), pl.BlockSpec((tt, d), row),
                  pl.BlockSpec((mr, d), row), pl.BlockSpec((1, d), lambda i: (0, 0))],
        out_specs=pl.BlockSpec((tt, d), row),
        out_shape=jax.ShapeDtypeStruct((t, d), F32),
        scratch_shapes=[pltpu.VMEM((2, TOP_K, tt * nsub, LANES), F32), pltpu.SemaphoreType.DMA((2,))],
        compiler_params=pltpu.CompilerParams(
            dimension_semantics=("arbitrary",), vmem_limit_bytes=VMEM_LIMIT),
        name="combine",
    )(d3, d3, y_tiles, wts, shared, x1, g2_rows, final_g.reshape(1, d))


def _plan(counts, n_assign):
    ne = counts.shape[0]
    bm = EXPERT_ROWS
    counts = counts.astype(jnp.int32)
    nblk = (counts + bm - 1) // bm
    bend = jnp.cumsum(nblk)
    pstart = ((bend - nblk) * bm).astype(F32).reshape(1, ne)
    nb = -(-(n_assign + ne * (bm - 1)) // bm)
    block_e = jnp.minimum(jnp.searchsorted(bend, jnp.arange(nb), side='right'), ne - 1).astype(jnp.int32)
    n_used = bend[-1:].astype(jnp.int32)
    pad_start = ((bend - nblk) * bm + counts).astype(jnp.int32)
    pad_len = (nblk * bm - counts).astype(jnp.int32)
    return pstart, block_e, n_used, pad_start, pad_len, nb


def per_g_pow2(ne):
    per_g = ne // N_EXPERT_GROUPS
    return per_g * N_EXPERT_GROUPS == ne and per_g & (per_g - 1) == 0


def _mod_rows(m, nbp, lp):
    return jnp.concatenate([jnp.repeat(m[:nbp], lp // MOD_ROWS, axis=0), m[nbp:]], axis=0)


def kernel(x_prompt, x_sample, state_ssm, state_conv, state_pool, c_prompt, c_sample, ln1_g, ln2_g, w_ada, b_ada, w_in, conv_w, conv_b, dt_bias, a_log, d_skip, ssd_norm_g, w_ssd_out, pool_w, pool_scale, w_out, w_router, router_bias, moe_w_gate, moe_w_up, moe_w_down, shared_w_gate, shared_w_up, shared_w_down, final_g):
    bp, lp, dm = x_prompt.shape
    bs, ls, _ = x_sample.shape
    depth = ln1_g.shape[0]
    heads, hd, ns = state_ssm.shape[2], state_ssm.shape[3], state_ssm.shape[4]
    di = heads * hd
    gn = SSD_GROUPS * ns
    cch = di + 2 * gn
    assert depth == 1 and ls == MOD_ROWS and lp % MOD_ROWS == 0 and heads <= LANES
    assert per_g_pow2(w_router.shape[2])
    tp, ts = bp * lp, bs * ls

    x_all = jnp.concatenate([x_prompt.reshape(tp, dm), x_sample.reshape(ts, dm)], axis=0)
    c_all = jnp.concatenate([c_prompt, c_sample], axis=0)

    o1, o2, o3, o4 = di, di + cch, di + cch + heads, di + cch + heads + dm
    cols = (0, di, 2 * di, di + cch, di + cch + dm, di + cch + 3 * dm)

    ssm_p, conv_p, pool_p, ssm_s, conv_s, pool_s = [], [], [], [], [], []
    for l in range(depth):
        wi = w_in[l]
        wcat = jnp.concatenate(
            [wi[:, :o1], wi[:, o1:o2], wi[:, o3:o4], wi[:, o4:],
             jnp.pad(wi[:, o2:o3], ((0, 0), (0, LANES - heads)))], axis=1).astype(BF16)
        pad_h = lambda v: jnp.pad(v.reshape(1, heads), ((0, 0), (0, LANES - heads)))
        seq_w = (conv_w[l], conv_b[l].reshape(1, cch), pad_h(dt_bias[l]), pad_h(a_log[l]),
                 jnp.repeat(d_skip[l], hd).reshape(1, di), ssd_norm_g[l].reshape(1, di),
                 w_ssd_out[l].astype(BF16), pool_w[l].astype(BF16), pool_scale[l].reshape(1, dm),
                 w_out[l].astype(BF16))

        mod = _ada(c_all, w_ada[l], b_ada[l])
        sh1, sc1, g1, sh2, sc2, g2 = jnp.split(mod, 6, axis=-1)

        proj = _inproj(x_all, _mod_rows(sc1, bp, lp), _mod_rows(sh1, bp, lp), ln1_g[l], wcat)

        zc = jnp.zeros((bp,) + state_conv.shape[2:], F32)
        zs = jnp.zeros((bp, heads, hd, ns), F32)
        zp = jnp.zeros((bp,) + state_pool.shape[2:], F32)
        xp1, ns_p, nc_p, np_p = _seq(proj, x_all, g1[:bp], zc, zs, zp, seq_w,
                                     row0=0, nb=bp, seqlen=lp, pos0=0, cols=cols)
        xs1, ns_s, nc_s, np_s = _seq(proj, x_all, g1[bp:], state_conv[l], state_ssm[l], state_pool[l],
                                     seq_w, row0=tp, nb=bs, seqlen=ls, pos0=PAST_LEN, cols=cols)
        x1 = jnp.concatenate([xp1, xs1], axis=0)
        conv_p.append(nc_p)
        conv_s.append(nc_s)
        pool_p.append(np_p)
        pool_s.append(np_s)
        ssm_p.append(ns_p)
        ssm_s.append(ns_s)

        h2_tiles, shared, eidx_l, wts, mask, counts = _moe_pre(
            x1, _mod_rows(sc2, bp, lp), _mod_rows(sh2, bp, lp), ln2_g[l],
            w_router[l].astype(BF16), router_bias[l],
            shared_w_gate[l].astype(BF16), shared_w_up[l].astype(BF16), shared_w_down[l].astype(BF16))
        nsub = dm // LANES
        pstart, block_e, n_used, pad_start, pad_len, nb = _plan(counts[0], (tp + ts) * TOP_K)
        dest = _rank(mask, eidx_l, pstart)[:, :TOP_K]
        xs = _dispatch(dest, h2_tiles, pad_start, pad_len, n_used, nb, nsub)
        y_tiles = _grouped(xs, block_e, n_used, moe_w_gate[l], moe_w_up[l], moe_w_down[l])
        x_all = _combine(dest, y_tiles, wts, shared, x1, _mod_rows(g2, bp, lp), final_g)

    y_prompt = x_all[:tp].reshape(bp, lp, dm)
    y_sample = x_all[tp:].reshape(bs, ls, dm)
    return (y_prompt, y_sample, jnp.stack(ssm_p), jnp.stack(conv_p), jnp.stack(pool_p),
            jnp.stack(ssm_s), jnp.stack(conv_s), jnp.stack(pool_s))
```

```python
import functools

import jax
import jax.numpy as jnp
from jax import lax
from jax.experimental import pallas as pl
from jax.experimental.pallas import tpu as pltpu

F32 = jnp.float32
BF16 = jnp.bfloat16
HIGHEST = lax.Precision.HIGHEST

RMS_EPS = 1e-6
SSD_NORM_EPS = 1e-5
CHUNK = 64
SSD_GROUPS = 4
POOL_WINDOWS = (2, 4, 8, 16)
PAST_LEN = 1024
TOP_K = 8
N_EXPERT_GROUPS = 8
TOPK_GROUPS = 4
ROUTED_SCALE = 2.5

LANES = 128
SUBLANES = 8
MOD_ROWS = 32
VMEM_LIMIT = 56 * 1024 * 1024
EXPERT_ROWS = 256


def _silu(x):
    return x * jax.nn.sigmoid(x)


def _softplus(x):
    return jnp.maximum(x, 0.0) + jnp.log1p(jnp.exp(-jnp.abs(x)))


def _row_tile(n, prefs):
    for t in prefs:
        if n % t == 0:
            return t
    return n


def _modulate(y, sc_ref, sh_ref):
    rows, d = y.shape
    y3 = y.reshape(rows // MOD_ROWS, MOD_ROWS, d)
    y3 = y3 * (1.0 + sc_ref[...][:, None, :]) + sh_ref[...][:, None, :]
    return y3.reshape(rows, d)


def _rms(x, g, eps):
    return x * lax.rsqrt(jnp.mean(x * x, axis=-1, keepdims=True) + eps) * g


def _ada_kernel(c_ref, w_ref, b_ref, o_ref):
    s = _silu(c_ref[...])
    o_ref[...] = jnp.dot(s, w_ref[...], preferred_element_type=F32, precision=HIGHEST) + b_ref[...]


def _ada(c_all, w_ada, b_ada):
    n, d = c_all.shape
    dout = w_ada.shape[1]
    tn = _row_tile(dout, (1024, 512, 256, 128))
    return pl.pallas_call(
        _ada_kernel,
        grid=(dout // tn,),
        in_specs=[pl.BlockSpec((n, d), lambda j: (0, 0)),
                  pl.BlockSpec((d, tn), lambda j: (0, j)),
                  pl.BlockSpec((1, tn), lambda j: (0, j))],
        out_specs=pl.BlockSpec((n, tn), lambda j: (0, j)),
        out_shape=jax.ShapeDtypeStruct((n, dout), F32),
        name="ada",
    )(c_all, w_ada, b_ada.reshape(1, dout))


def _inproj_kernel(x_ref, sc_ref, sh_ref, g_ref, w_ref, o_ref, h_scr):
    @pl.when(pl.program_id(1) == 0)
    def _():
        y = _rms(x_ref[...], g_ref[...], RMS_EPS)
        h_scr[...] = _modulate(y, sc_ref, sh_ref).astype(BF16)

    o_ref[...] = jnp.dot(h_scr[...], w_ref[...], preferred_element_type=F32)


def _inproj(x_all, sc_rows, sh_rows, g, wcat):
    t, d = x_all.shape
    n = wcat.shape[1]
    tm = _row_tile(t, (512, 256))
    tn = _row_tile(n, (1664, 1280, 1024, 640, 512, 256, 128))
    mr = tm // MOD_ROWS
    return pl.pallas_call(
        _inproj_kernel,
        grid=(t // tm, n // tn),
        in_specs=[pl.BlockSpec((tm, d), lambda i, j: (i, 0)),
                  pl.BlockSpec((mr, d), lambda i, j: (i, 0)),
                  pl.BlockSpec((mr, d), lambda i, j: (i, 0)),
                  pl.BlockSpec((1, d), lambda i, j: (0, 0)),
                  pl.BlockSpec((d, tn), lambda i, j: (0, j))],
        out_specs=pl.BlockSpec((tm, tn), lambda i, j: (i, j)),
        out_shape=jax.ShapeDtypeStruct((t, n), F32),
        scratch_shapes=[pltpu.VMEM((tm, d), BF16)],
        compiler_params=pltpu.CompilerParams(
            dimension_semantics=("parallel", "arbitrary"), vmem_limit_bytes=VMEM_LIMIT),
        name="inproj",
    )(x_all, sc_rows, sh_rows, g.reshape(1, d), wcat)


def _seq_kernel(z_ref, xp_ref, bc_ref, u_ref, gt_ref, dt_ref, x_ref, g1_ref,
                cst_ref, sst_ref, pst_ref, cw_ref, cb_ref, dtb_ref, alog_ref, dx_ref, ng_ref,
                wssd_ref, pw_ref, ps_ref, wout_ref, e3_ref,
                xo_ref, so_ref, co_ref, po_ref,
                cbuf, pbuf, st_scr, y_scr, xbc_scr, csx_scr, wvx_scr,
                *, tq, q, pos0, heads, hd, ns, dm):
    j = pl.program_id(1)
    di = heads * hd
    gn = SSD_GROUPS * ns
    hpg = heads // SSD_GROUPS
    cw = cbuf.shape[0] - tq
    kw = cw_ref.shape[0]
    ph = pbuf.shape[0] - tq

    @pl.when(j == 0)
    def _():
        cbuf[cw - (kw - 1):cw, :] = cst_ref[0]
        pbuf[1:ph, :] = pst_ref[0]
        st_scr[...] = sst_ref[0].T

    cbuf[cw:cw + tq, 0:di] = xp_ref[...]
    cbuf[cw:cw + tq, di:di + 2 * gn] = bc_ref[...]
    acc = cb_ref[...] + cbuf[cw:cw + tq, :] * cw_ref[kw - 1:kw, :]
    for k in range(kw - 1):
        off = cw - (kw - 1) + k
        acc = acc + cbuf[off:off + tq, :] * cw_ref[k:k + 1, :]
    xbc_scr[...] = _silu(acc)
    cbuf[cw - (kw - 1):cw, :] = cbuf[cw + tq - (kw - 1):cw + tq, :]

    dt = _softplus(dt_ref[...] + dtb_ref[...])
    dta = dt * (-jnp.exp(alog_ref[...]))
    lq = q.bit_length() - 1
    r_i = lax.broadcasted_iota(jnp.int32, (tq, tq), 0)
    c_i = lax.broadcasted_iota(jnp.int32, (tq, tq), 1)
    same = lax.shift_right_logical(r_i, lq) == lax.shift_right_logical(c_i, lq)
    tril = jnp.where(same, (r_i >= c_i).astype(F32), 0.0)
    cs = jnp.dot(tril, dta, preferred_element_type=F32, precision=HIGHEST)
    cs_end = jnp.dot(same.astype(F32), dta, preferred_element_type=F32, precision=HIGHEST)
    wv = dt * jnp.exp(cs_end - cs)

    both = jnp.concatenate([cs, wv], axis=0)
    hi = both.astype(BF16)
    r1 = both - hi.astype(F32)
    mid = r1.astype(BF16)
    lo = (r1 - mid.astype(F32)).astype(BF16)
    ex = jnp.dot(jnp.concatenate([hi, mid, lo], axis=1), e3_ref[...], preferred_element_type=F32)
    csx_scr[...] = ex[0:tq]
    wvx_scr[...] = ex[tq:2 * tq]
    cs_t = cs.T
    dt_t = dt.T

    lane = lax.broadcasted_iota(jnp.int32, (q, 2 * hd), 1)
    kpos = jnp.bitwise_and(lane, hd - 1)
    causal2 = jnp.logical_and(lax.broadcasted_iota(jnp.int32, (q, 2 * hd), 0) >= kpos, kpos < q)
    first = lane < hd
    zrow = jnp.zeros((1, hd - q), F32)
    zblk = jnp.zeros((hd - q, 2 * hd), BF16)

    def pair_row(t, p, r0):
        parts = []
        for h in (2 * p, 2 * p + 1):
            parts.append(t[h:h + 1, r0:r0 + q])
            if q < hd:
                parts.append(zrow)
        return jnp.concatenate(parts, axis=1)

    def pair_rows(a, b):
        blocks = [a, zblk, b, zblk] if q < hd else [a, b]
        return jnp.concatenate(blocks, axis=0)

    gw = di // SSD_GROUPS
    for c in range(tq // q):
        r0 = c * q
        for g in range(SSD_GROUPS):
            bgb = xbc_scr[r0:r0 + q, di + g * ns:di + (g + 1) * ns].astype(BF16)
            cgb = xbc_scr[r0:r0 + q, di + gn + g * ns:di + gn + (g + 1) * ns].astype(BF16)
            cb2 = lax.dot_general(cgb, pair_rows(bgb, bgb), (((1,), (1,)), ((), ())),
                                  preferred_element_type=F32)
            stg = st_scr[:, g * gw:(g + 1) * gw]
            ecs = jnp.exp(csx_scr[r0:r0 + q, g * gw:(g + 1) * gw])
            yo = jnp.dot(cgb, stg.astype(BF16), preferred_element_type=F32) * ecs
            for pp in range(hpg // 2):
                p = g * (hpg // 2) + pp
                lo_l = p * 2 * hd
                seg = csx_scr[r0:r0 + q, lo_l:lo_l + 2 * hd] - pair_row(cs_t, p, r0)
                lm = jnp.exp(jnp.where(causal2, seg, -jnp.inf))
                m2 = (cb2 * lm * pair_row(dt_t, p, r0)).astype(BF16)
                xpair = xbc_scr[r0:r0 + q, lo_l:lo_l + 2 * hd]
                rhs = pair_rows(jnp.where(first, xpair, 0.0).astype(BF16),
                                jnp.where(first, 0.0, xpair).astype(BF16))
                yd = jnp.dot(m2, rhs, preferred_element_type=F32)
                y_scr[r0:r0 + q, lo_l:lo_l + 2 * hd] = yd + yo[:, pp * 2 * hd:(pp + 1) * 2 * hd]
            xw = (xbc_scr[r0:r0 + q, g * gw:(g + 1) * gw]
                  * wvx_scr[r0:r0 + q, g * gw:(g + 1) * gw]).astype(BF16)
            upd = lax.dot_general(bgb, xw, (((0,), (0,)), ((), ())), preferred_element_type=F32)
            st_scr[:, g * gw:(g + 1) * gw] = stg * ecs[q - 1:q, :] + upd

    @pl.when(j == pl.num_programs(1) - 1)
    def _():
        so_ref[0] = st_scr[...].T
        co_ref[0] = cbuf[cw - (kw - 1):cw, :]

    y = y_scr[...] + xbc_scr[:, 0:di] * dx_ref[...]
    v = y * _silu(z_ref[...])
    gw = di // SSD_GROUPS
    parts = []
    for g in range(SSD_GROUPS):
        vg = v[:, g * gw:(g + 1) * gw]
        parts.append(vg * lax.rsqrt(jnp.mean(vg * vg, axis=-1, keepdims=True) + SSD_NORM_EPS))
    yn = (jnp.concatenate(parts, axis=-1) * ng_ref[...]).astype(BF16)
    a_br = jnp.dot(yn, wssd_ref[...], preferred_element_type=F32)

    pbuf[ph:ph + tq, :] = u_ref[...]
    pos = pos0 + j * tq + lax.broadcasted_iota(jnp.int32, (tq, 1), 0)
    pgd = dm // len(POOL_WINDOWS)
    b_parts = []
    for gi, w in enumerate(POOL_WINDOWS):
        lo = gi * pgd
        s = pbuf[ph:ph + tq, lo:lo + pgd]
        for i in range(1, w):
            s = s + pbuf[ph - i:ph - i + tq, lo:lo + pgd]
        cnt = jnp.minimum(w, pos + 1).astype(F32)
        pooled = s / cnt - pbuf[ph:ph + tq, lo:lo + pgd]
        b_parts.append(jnp.dot(pooled.astype(BF16), pw_ref[gi], preferred_element_type=F32))
    b_br = jnp.concatenate(b_parts, axis=-1) * ps_ref[...]
    pbuf[1:ph, :] = pbuf[tq + 1:tq + ph, :]

    @pl.when(j == pl.num_programs(1) - 1)
    def _():
        po_ref[0] = pbuf[1:ph, :]

    gates = jax.nn.sigmoid(gt_ref[...])
    merged = gates[:, 0:dm] * a_br + gates[:, dm:2 * dm] * b_br
    mix = jnp.dot(merged.astype(BF16), wout_ref[...], preferred_element_type=F32)
    xo_ref[...] = x_ref[...] + g1_ref[0] * mix


def _seq(proj, x_all, g1, conv_state, ssm_state, pool_state, wts, *, row0, nb, seqlen, pos0, cols):
    dm = x_all.shape[1]
    heads, hd, ns = ssm_state.shape[1], ssm_state.shape[2], ssm_state.shape[3]
    di = heads * hd
    gn = SSD_GROUPS * ns
    q = min(CHUNK, seqlen)
    tq = _row_tile(seqlen, (128, 64, 32))
    nj = seqlen // tq
    rb0 = row0 // tq
    oz, ox, obc, ou, og, odt = cols

    def rows(b, j):
        return rb0 + b * nj + j

    full = lambda a: pl.BlockSpec(a.shape, lambda b, j: (0,) * a.ndim)
    kern = functools.partial(_seq_kernel, tq=tq, q=q, pos0=pos0, heads=heads, hd=hd, ns=ns, dm=dm)
    kw1 = conv_state.shape[1]
    pst = pool_state.shape[1]
    xo, so, co, po = pl.pallas_call(
        kern,
        grid=(nb, nj),
        in_specs=[
            pl.BlockSpec((tq, di), lambda b, j: (rows(b, j), oz // di)),
            pl.BlockSpec((tq, di), lambda b, j: (rows(b, j), ox // di)),
            pl.BlockSpec((tq, 2 * gn), lambda b, j: (rows(b, j), obc // (2 * gn))),
            pl.BlockSpec((tq, dm), lambda b, j: (rows(b, j), ou // dm)),
            pl.BlockSpec((tq, 2 * dm), lambda b, j: (rows(b, j), og // (2 * dm))),
            pl.BlockSpec((tq, LANES), lambda b, j: (rows(b, j), odt // LANES)),
            pl.BlockSpec((tq, dm), lambda b, j: (rows(b, j), 0)),
            pl.BlockSpec((1, 1, dm), lambda b, j: (b, 0, 0)),
            pl.BlockSpec((1, kw1, di + 2 * gn), lambda b, j: (b, 0, 0)),
            pl.BlockSpec((1, di, ns), lambda b, j: (b, 0, 0)),
            pl.BlockSpec((1, pst, dm), lambda b, j: (b, 0, 0)),
        ] + [full(w) for w in wts],
        out_specs=[pl.BlockSpec((tq, dm), lambda b, j: (b * nj + j, 0)),
                   pl.BlockSpec((1, di, ns), lambda b, j: (b, 0, 0)),
                   pl.BlockSpec((1, kw1, di + 2 * gn), lambda b, j: (b, 0, 0)),
                   pl.BlockSpec((1, pst, dm), lambda b, j: (b, 0, 0))],
        out_shape=[jax.ShapeDtypeStruct((nb * seqlen, dm), F32),
                   jax.ShapeDtypeStruct((nb, di, ns), F32),
                   jax.ShapeDtypeStruct((nb, kw1, di + 2 * gn), F32),
                   jax.ShapeDtypeStruct((nb, pst, dm), F32)],
        scratch_shapes=[pltpu.VMEM((tq + SUBLANES, di + 2 * gn), F32),
                        pltpu.VMEM((tq + 2 * SUBLANES, dm), F32),
                        pltpu.VMEM((ns, di), F32),
                        pltpu.VMEM((tq, di), F32),
                        pltpu.VMEM((tq, di + 2 * gn), F32),
                        pltpu.VMEM((tq, di), F32),
                        pltpu.VMEM((tq, di), F32)],
        compiler_params=pltpu.CompilerParams(
            dimension_semantics=("parallel", "arbitrary"), vmem_limit_bytes=VMEM_LIMIT),
        name="seq",
    )(proj, proj, proj, proj, proj, proj, x_all, g1.reshape(nb, 1, dm),
      conv_state, ssm_state.reshape(nb, di, ns), pool_state, *wts)
    return xo, so.reshape(nb, heads, hd, ns), co, po


def _moe_pre_kernel(x_ref, sc_ref, sh_ref, g_ref, wr_ref, rb_ref, swg_ref, swu_ref, swd_ref,
                    h_ref, s_ref, e_ref, w_ref, m_ref, cnt_ref):
    y = _rms(x_ref[...], g_ref[...], RMS_EPS)
    h = _modulate(y, sc_ref, sh_ref)
    _write_rows(h_ref, h)
    hb = h.astype(BF16)

    sg = jnp.dot(hb, swg_ref[...], preferred_element_type=F32)
    su = jnp.dot(hb, swu_ref[...], preferred_element_type=F32)
    s_ref[...] = jnp.dot((_silu(sg) * su).astype(BF16), swd_ref[...], preferred_element_type=F32)

    scores = jax.nn.sigmoid(jnp.dot(hb, wr_ref[...], preferred_element_type=F32))
    biased = scores + rb_ref[...]
    tm, ne = scores.shape
    per_g = ne // N_EXPERT_GROUPS
    neg = -jnp.inf
    lane_i = lax.broadcasted_iota(jnp.int32, (tm, ne), 1)
    lane = lane_i.astype(F32)
    lgroup = lax.shift_right_logical(lane_i, per_g.bit_length() - 1).astype(F32)
    lane_o = lax.broadcasted_iota(jnp.int32, (tm, LANES), 1).astype(F32)

    def first_argmax(v, idx, big):
        top = jnp.max(v, axis=-1, keepdims=True)
        return top, jnp.min(jnp.where(v == top, idx, big), axis=-1, keepdims=True)

    gs = jnp.full((tm, LANES), neg, F32)
    for g in range(N_EXPERT_GROUPS):
        mg = jnp.where(lgroup == g, biased, neg)
        t1, i1 = first_argmax(mg, lane, float(ne))
        t2 = jnp.max(jnp.where(lane == i1, neg, mg), axis=-1, keepdims=True)
        gs = jnp.where(lane_o == g, t1 + t2, gs)
    allowed = jnp.zeros((tm, ne), F32)
    for _ in range(TOPK_GROUPS):
        _, gi = first_argmax(gs, lane_o, float(LANES))
        allowed = jnp.where(lgroup == gi, 1.0, allowed)
        gs = jnp.where(lane_o == gi, neg, gs)
    mb = jnp.where(allowed > 0.0, biased, neg)
    eacc = jnp.zeros((tm, LANES), F32)
    wacc = jnp.zeros((tm, LANES), F32)
    chosen = jnp.zeros((tm, ne), F32)
    for k in range(TOP_K):
        _, ik = first_argmax(mb, lane, float(ne))
        sel = lane == ik
        wk = jnp.sum(jnp.where(sel, scores, 0.0), axis=-1, keepdims=True)
        eacc = jnp.where(lane_o == k, ik, eacc)
        wacc = jnp.where(lane_o == k, wk, wacc)
        chosen = jnp.where(sel, 1.0, chosen)
        mb = jnp.where(sel, neg, mb)
    wsum = jnp.sum(wacc, axis=-1, keepdims=True)
    e_ref[...] = eacc.astype(jnp.int32)
    w_ref[...] = wacc / (wsum + 1e-20) * ROUTED_SCALE
    m_ref[...] = chosen.astype(BF16)

    @pl.when(pl.program_id(0) == 0)
    def _():
        cnt_ref[...] = jnp.zeros_like(cnt_ref)

    cnt_ref[0:1, :] += jnp.sum(chosen, axis=0, keepdims=True)


def _moe_pre(x1, sc_rows, sh_rows, g, wr, rb, swg, swu, swd):
    t, d = x1.shape
    ne = wr.shape[1]
    ff = swg.shape[1]
    tm = _row_tile(t, (256,))
    mr = tm // MOD_ROWS
    nsub = d // LANES
    row = lambda i: (i, 0)
    const = lambda i: (0, 0)
    return pl.pallas_call(
        _moe_pre_kernel,
        grid=(t // tm,),
        in_specs=[pl.BlockSpec((tm, d), row), pl.BlockSpec((mr, d), row), pl.BlockSpec((mr, d), row),
                  pl.BlockSpec((1, d), const), pl.BlockSpec((d, ne), const), pl.BlockSpec((1, ne), const),
                  pl.BlockSpec((d, ff), const), pl.BlockSpec((d, ff), const), pl.BlockSpec((ff, d), const)],
        out_specs=[pl.BlockSpec((tm * nsub, LANES), row), pl.BlockSpec((tm, d), row),
                   pl.BlockSpec((tm, LANES), row), pl.BlockSpec((tm, LANES), row),
                   pl.BlockSpec((tm, ne), row), pl.BlockSpec((SUBLANES, ne), const)],
        out_shape=[jax.ShapeDtypeStruct((t * nsub, LANES), F32), jax.ShapeDtypeStruct((t, d), F32),
                   jax.ShapeDtypeStruct((t, LANES), jnp.int32), jax.ShapeDtypeStruct((t, LANES), F32),
                   jax.ShapeDtypeStruct((t, ne), BF16), jax.ShapeDtypeStruct((SUBLANES, ne), F32)],
        compiler_params=pltpu.CompilerParams(
            dimension_semantics=("arbitrary",), vmem_limit_bytes=VMEM_LIMIT),
        name="moe_pre",
    )(x1, sc_rows, sh_rows, g.reshape(1, d), wr, rb.reshape(1, ne), swg, swu, swd)


def _read_rows(buf, nrows, nsub):
    return jnp.concatenate([buf[pl.ds(c, nrows, stride=nsub), :] for c in range(nsub)], axis=-1)


def _write_rows(ref, v):
    nrows, width = v.shape
    nsub = width // LANES
    for c in range(nsub):
        ref[pl.ds(c, nrows, stride=nsub), :] = v[:, c * LANES:(c + 1) * LANES]


def _rank_kernel(m_ref, e_ref, ps_ref, d_ref, carry):
    i = pl.program_id(0)
    ne = m_ref.shape[1]

    @pl.when(i == 0)
    def _():
        carry[...] = jnp.zeros_like(carry)

    m = m_ref[...]
    tr = m.shape[0]
    r_i = lax.broadcasted_iota(jnp.int32, (tr, tr), 0)
    c_i = lax.broadcasted_iota(jnp.int32, (tr, tr), 1)
    earlier = (r_i > c_i).astype(BF16)
    slot = jnp.dot(earlier, m, preferred_element_type=F32) + carry[0:1, :] + ps_ref[...]
    e = e_ref[...]
    lane = lax.broadcasted_iota(jnp.int32, (tr, ne), 1)
    lane_o = lax.broadcasted_iota(jnp.int32, (tr, LANES), 1)
    dacc = jnp.zeros((tr, LANES), F32)
    for k in range(TOP_K):
        dk = jnp.sum(jnp.where(lane == e[:, k:k + 1], slot, 0.0), axis=-1, keepdims=True)
        dacc = jnp.where(lane_o == k, dk, dacc)
    d_ref[...] = dacc.astype(jnp.int32)
    carry[0:1, :] += jnp.sum(m.astype(F32), axis=0, keepdims=True)


def _rank(mask, eidx, pstart):
    t, ne = mask.shape
    tr = _row_tile(t, (512, 256))
    row = lambda i: (i, 0)
    return pl.pallas_call(
        _rank_kernel,
        grid=(t // tr,),
        in_specs=[pl.BlockSpec((tr, ne), row), pl.BlockSpec((tr, LANES), row),
                  pl.BlockSpec((1, ne), lambda i: (0, 0))],
        out_specs=pl.BlockSpec((tr, LANES), row),
        out_shape=jax.ShapeDtypeStruct((t, LANES), jnp.int32),
        scratch_shapes=[pltpu.VMEM((SUBLANES, ne), F32)],
        compiler_params=pltpu.CompilerParams(
            dimension_semantics=("arbitrary",), vmem_limit_bytes=VMEM_LIMIT),
        name="rank",
    )(mask, eidx, pstart)


def _dispatch_kernel(ps_ref, pl_ref, nu_ref, d_ref, h_ref, xs_hbm, zbuf, sem, zsem, *, tt, nsub, bm, nb):
    i = pl.program_id(0)
    n = pl.num_programs(0)
    ne = ps_ref.shape[0]
    pieces = [1 << b for b in reversed(range(bm.bit_length() - 1))]

    def zero_fill(op):
        def pad(e, carry):
            start = ps_ref[e]
            length = pl_ref[e]
            for p in pieces:
                off = length - lax.rem(length, 2 * p)

                @pl.when(lax.rem(length, 2 * p) >= p)
                def _():
                    dst = xs_hbm.at[pl.ds(pl.multiple_of((start + off) * nsub, nsub), p * nsub)]
                    op(pltpu.make_async_copy(zbuf.at[pl.ds(0, p * nsub)], dst, zsem))
            return carry
        lax.fori_loop(0, ne, pad, 0)

        def tail(b, carry):
            dst = xs_hbm.at[pl.ds(pl.multiple_of(b * (bm * nsub), bm * nsub), bm * nsub)]
            op(pltpu.make_async_copy(zbuf, dst, zsem))
            return carry
        lax.fori_loop(nu_ref[0], nb, tail, 0)

    @pl.when(i == 0)
    def _():
        zbuf[...] = jnp.zeros_like(zbuf)
        zero_fill(lambda c: c.start())

    def body(r, carry):
        src = h_ref.at[pl.ds(pl.multiple_of(r * nsub, nsub), nsub)]
        for k in range(TOP_K):
            dst = xs_hbm.at[pl.ds(pl.multiple_of(d_ref[0, 0, r * TOP_K + k] * nsub, nsub), nsub)]
            pltpu.make_async_copy(src, dst, sem).start(priority=k % 2)
        return carry

    lax.fori_loop(0, tt, body, 0)

    for _ in range(TOP_K):
        pltpu.make_async_copy(h_ref, xs_hbm.at[pl.ds(0, tt * nsub)], sem).wait()

    @pl.when(i == n - 1)
    def _():
        zero_fill(lambda c: c.wait())


def _dispatch(dest, h2_tiles, pad_start, pad_len, n_used, nb, nsub):
    t = dest.shape[0]
    tt = _row_tile(t, (256,))
    nt = t // tt
    bm = EXPERT_ROWS
    d3 = dest.reshape(nt, 1, tt * TOP_K)
    gs = pltpu.PrefetchScalarGridSpec(
        num_scalar_prefetch=3,
        grid=(nt,),
        in_specs=[pl.BlockSpec((1, 1, tt * TOP_K), lambda i, *_: (i, 0, 0), memory_space=pltpu.SMEM),
                  pl.BlockSpec((tt * nsub, LANES), lambda i, *_: (i, 0))],
        out_specs=pl.BlockSpec(memory_space=pl.ANY),
        scratch_shapes=[pltpu.VMEM((bm * nsub, LANES), F32), pltpu.SemaphoreType.DMA,
                        pltpu.SemaphoreType.DMA],
    )
    return pl.pallas_call(
        functools.partial(_dispatch_kernel, tt=tt, nsub=nsub, bm=bm, nb=nb),
        grid_spec=gs,
        out_shape=jax.ShapeDtypeStruct((nb * bm * nsub, LANES), F32),
        compiler_params=pltpu.CompilerParams(dimension_semantics=("arbitrary",)),
        name="dispatch",
    )(pad_start, pad_len, n_used, d3, h2_tiles)


def _grouped_kernel(be_ref, nu_ref, x_ref, wg_ref, wu_ref, wd_ref, y_ref, wgb, wub, wdb):
    i = pl.program_id(0)

    @pl.when(i < nu_ref[0])
    def _():
        @pl.when(jnp.logical_or(i == 0, be_ref[i] != be_ref[jnp.maximum(i - 1, 0)]))
        def _():
            wgb[...] = wg_ref[0].astype(BF16)
            wub[...] = wu_ref[0].astype(BF16)
            wdb[...] = wd_ref[0].astype(BF16)

        d = wgb.shape[0]
        nsub = d // LANES
        x = _read_rows(x_ref, x_ref.shape[0] // nsub, nsub).astype(BF16)
        hg = jnp.dot(x, wgb[...], preferred_element_type=F32)
        hu = jnp.dot(x, wub[...], preferred_element_type=F32)
        y = jnp.dot((_silu(hg) * hu).astype(BF16), wdb[...], preferred_element_type=F32)
        _write_rows(y_ref, y)

    @pl.when(i >= nu_ref[0])
    def _():
        y_ref[...] = jnp.zeros_like(y_ref)


def _grouped(xs, block_e, n_used, wg, wu, wd):
    ne, d, ff = wg.shape
    nsub = d // LANES
    bm = EXPERT_ROWS
    nb = block_e.shape[0]
    used = lambda i, nu: jnp.minimum(i, nu[0] - 1)
    gs = pltpu.PrefetchScalarGridSpec(
        num_scalar_prefetch=2,
        grid=(nb,),
        in_specs=[pl.BlockSpec((bm * nsub, LANES), lambda i, be, nu: (used(i, nu), 0)),
                  pl.BlockSpec((1, d, ff), lambda i, be, nu: (be[used(i, nu)], 0, 0)),
                  pl.BlockSpec((1, d, ff), lambda i, be, nu: (be[used(i, nu)], 0, 0)),
                  pl.BlockSpec((1, ff, d), lambda i, be, nu: (be[used(i, nu)], 0, 0))],
        out_specs=pl.BlockSpec((bm * nsub, LANES), lambda i, be, nu: (i, 0)),
        scratch_shapes=[pltpu.VMEM((d, ff), BF16), pltpu.VMEM((d, ff), BF16), pltpu.VMEM((ff, d), BF16)],
    )
    return pl.pallas_call(
        _grouped_kernel,
        grid_spec=gs,
        out_shape=jax.ShapeDtypeStruct((nb * bm * nsub, LANES), F32),
        compiler_params=pltpu.CompilerParams(
            dimension_semantics=("arbitrary",), vmem_limit_bytes=VMEM_LIMIT),
        name="grouped",
    )(block_e, n_used, xs, wg, wu, wd)


def _combine_kernel(d_ref, dn_ref, y_hbm, w_ref, s_ref, x_ref, g2_ref, fg_ref, o_ref, buf, sem,
                    *, tt, nsub):
    i = pl.program_id(0)
    n = pl.num_programs(0)

    def gather(dref, slot):
        def body(r, carry):
            for k in range(TOP_K):
                src = y_hbm.at[pl.ds(pl.multiple_of(dref[0, 0, r * TOP_K + k] * nsub, nsub), nsub)]
                pltpu.make_async_copy(src, buf.at[slot, k, pl.ds(pl.multiple_of(r * nsub, nsub), nsub)],
                                      sem.at[slot]).start(priority=k % 2)
            return carry
        lax.fori_loop(0, tt, body, 0)

    @pl.when(i == 0)
    def _():
        gather(d_ref, 0)

    @pl.when(i + 1 < n)
    def _():
        gather(dn_ref, (i + 1) % 2)

    slot = i % 2
    for k in range(TOP_K):
        pltpu.make_async_copy(y_hbm.at[pl.ds(0, tt * nsub)], buf.at[slot, k], sem.at[slot]).wait()

    w = w_ref[...]
    routed = w[:, 0:1] * _read_rows(buf.at[slot, 0], tt, nsub)
    for k in range(1, TOP_K):
        routed = routed + w[:, k:k + 1] * _read_rows(buf.at[slot, k], tt, nsub)
    moe = routed + s_ref[...]
    d = moe.shape[1]
    m3 = moe.reshape(tt // MOD_ROWS, MOD_ROWS, d) * g2_ref[...][:, None, :]
    x2 = x_ref[...] + m3.reshape(tt, d)
    o_ref[...] = _rms(x2, fg_ref[...], RMS_EPS)


def _combine(dest, y_tiles, wts, shared, x1, g2_rows, final_g):
    t, d = x1.shape
    nsub = d // LANES
    tt = _row_tile(t, (256,))
    mr = tt // MOD_ROWS
    nt = t // tt
    d3 = dest.reshape(nt, 1, tt * TOP_K)
    kern = functools.partial(_combine_kernel, tt=tt, nsub=nsub)
    smem_blk = lambda f: pl.BlockSpec((1, 1, tt * TOP_K), f, memory_space=pltpu.SMEM)
    row = lambda i: (i, 0)
    return pl.pallas_call(
        kern,
        grid=(nt,),
        in_specs=[smem_blk(lambda i: (i, 0, 0)),
                  smem_blk(lambda i: (jnp.minimum(i + 1, nt - 1), 0, 0)),
                  pl.BlockSpec(memory_space=pl.ANY),
                  pl.BlockSpec((tt, LANES), row), pl.BlockSpec((tt, d), row), pl.BlockSpec((tt, d), row),
                  pl.BlockSpec((mr, d), row), pl.BlockSpec((1, d), lambda i: (0, 0))],
        out_specs=pl.BlockSpec((tt, d), row),
        out_shape=jax.ShapeDtypeStruct((t, d), F32),
        scratch_shapes=[pltpu.VMEM((2, TOP_K, tt * nsub, LANES), F32), pltpu.SemaphoreType.DMA((2,))],
        compiler_params=pltpu.CompilerParams(
            dimension_semantics=("arbitrary",), vmem_limit_bytes=VMEM_LIMIT),
        name="combine",
    )(d3, d3, y_tiles, wts, shared, x1, g2_rows, final_g.reshape(1, d))


def _plan(counts, n_assign):
    ne = counts.shape[0]
    bm = EXPERT_ROWS
    counts = counts.astype(jnp.int32)
    nblk = (counts + bm - 1) // bm
    bend = jnp.cumsum(nblk)
    pstart = ((bend - nblk) * bm).astype(F32).reshape(1, ne)
    nb = -(-(n_assign + ne * (bm - 1)) // bm)
    block_e = jnp.minimum(jnp.searchsorted(bend, jnp.arange(nb), side='right'), ne - 1).astype(jnp.int32)
    n_used = bend[-1:].astype(jnp.int32)
    pad_start = ((bend - nblk) * bm + counts).astype(jnp.int32)
    pad_len = (nblk * bm - counts).astype(jnp.int32)
    return pstart, block_e, n_used, pad_start, pad_len, nb


def per_g_pow2(ne):
    per_g = ne // N_EXPERT_GROUPS
    return per_g * N_EXPERT_GROUPS == ne and per_g & (per_g - 1) == 0


def _mod_rows(m, nbp, lp):
    return jnp.concatenate([jnp.repeat(m[:nbp], lp // MOD_ROWS, axis=0), m[nbp:]], axis=0)


def kernel(x_prompt, x_sample, state_ssm, state_conv, state_pool, c_prompt, c_sample, ln1_g, ln2_g, w_ada, b_ada, w_in, conv_w, conv_b, dt_bias, a_log, d_skip, ssd_norm_g, w_ssd_out, pool_w, pool_scale, w_out, w_router, router_bias, moe_w_gate, moe_w_up, moe_w_down, shared_w_gate, shared_w_up, shared_w_down, final_g):
    bp, lp, dm = x_prompt.shape
    bs, ls, _ = x_sample.shape
    depth = ln1_g.shape[0]
    heads, hd, ns = state_ssm.shape[2], state_ssm.shape[3], state_ssm.shape[4]
    di = heads * hd
    gn = SSD_GROUPS * ns
    cch = di + 2 * gn
    assert depth == 1 and ls == MOD_ROWS and lp % MOD_ROWS == 0 and heads <= LANES
    assert per_g_pow2(w_router.shape[2])
    assert ns == 2 * hd and hd & (hd - 1) == 0 and (heads // SSD_GROUPS) % 2 == 0
    tp, ts = bp * lp, bs * ls
    expand3 = (jnp.arange(3 * LANES)[:, None] % LANES == jnp.arange(di)[None, :] // hd).astype(BF16)

    x_all = jnp.concatenate([x_prompt.reshape(tp, dm), x_sample.reshape(ts, dm)], axis=0)
    c_all = jnp.concatenate([c_prompt, c_sample], axis=0)

    o1, o2, o3, o4 = di, di + cch, di + cch + heads, di + cch + heads + dm
    cols = (0, di, 2 * di, di + cch, di + cch + dm, di + cch + 3 * dm)

    ssm_p, conv_p, pool_p, ssm_s, conv_s, pool_s = [], [], [], [], [], []
    for l in range(depth):
        wi = w_in[l]
        wcat = jnp.concatenate(
            [wi[:, :o1], wi[:, o1:o2], wi[:, o3:o4], wi[:, o4:],
             jnp.pad(wi[:, o2:o3], ((0, 0), (0, LANES - heads)))], axis=1).astype(BF16)
        pad_h = lambda v: jnp.pad(v.reshape(1, heads), ((0, 0), (0, LANES - heads)))
        seq_w = (conv_w[l], conv_b[l].reshape(1, cch), pad_h(dt_bias[l]), pad_h(a_log[l]),
                 jnp.repeat(d_skip[l], hd).reshape(1, di), ssd_norm_g[l].reshape(1, di),
                 w_ssd_out[l].astype(BF16), pool_w[l].astype(BF16), pool_scale[l].reshape(1, dm),
                 w_out[l].astype(BF16), expand3)

        mod = _ada(c_all, w_ada[l], b_ada[l])
        sh1, sc1, g1, sh2, sc2, g2 = jnp.split(mod, 6, axis=-1)

        proj = _inproj(x_all, _mod_rows(sc1, bp, lp), _mod_rows(sh1, bp, lp), ln1_g[l], wcat)

        zc = jnp.zeros((bp,) + state_conv.shape[2:], F32)
        zs = jnp.zeros((bp, heads, hd, ns), F32)
        zp = jnp.zeros((bp,) + state_pool.shape[2:], F32)
        xp1, ns_p, nc_p, np_p = _seq(proj, x_all, g1[:bp], zc, zs, zp, seq_w,
                                     row0=0, nb=bp, seqlen=lp, pos0=0, cols=cols)
        xs1, ns_s, nc_s, np_s = _seq(proj, x_all, g1[bp:], state_conv[l], state_ssm[l], state_pool[l],
                                     seq_w, row0=tp, nb=bs, seqlen=ls, pos0=PAST_LEN, cols=cols)
        x1 = jnp.concatenate([xp1, xs1], axis=0)
        conv_p.append(nc_p)
        conv_s.append(nc_s)
        pool_p.append(np_p)
        pool_s.append(np_s)
        ssm_p.append(ns_p)
        ssm_s.append(ns_s)

        h2_tiles, shared, eidx_l, wts, mask, counts = _moe_pre(
            x1, _mod_rows(sc2, bp, lp), _mod_rows(sh2, bp, lp), ln2_g[l],
            w_router[l].astype(BF16), router_bias[l],
            shared_w_gate[l].astype(BF16), shared_w_up[l].astype(BF16), shared_w_down[l].astype(BF16))
        nsub = dm // LANES
        pstart, block_e, n_used, pad_start, pad_len, nb = _plan(counts[0], (tp + ts) * TOP_K)
        dest = _rank(mask, eidx_l, pstart)[:, :TOP_K]
        xs = _dispatch(dest, h2_tiles, pad_start, pad_len, n_used, nb, nsub)
        y_tiles = _grouped(xs, block_e, n_used, moe_w_gate[l], moe_w_up[l], moe_w_down[l])
        x_all = _combine(dest, y_tiles, wts, shared, x1, _mod_rows(g2, bp, lp), final_g)

    y_prompt = x_all[:tp].reshape(bp, lp, dm)
    y_sample = x_all[tp:].reshape(bs, ls, dm)
    return (y_prompt, y_sample, jnp.stack(ssm_p), jnp.stack(conv_p), jnp.stack(pool_p),
            jnp.stack(ssm_s), jnp.stack(conv_s), jnp.stack(pool_s))
```

```python
import functools

import jax
import jax.numpy as jnp
from jax import lax
from jax.experimental import pallas as pl
from jax.experimental.pallas import tpu as pltpu

F32 = jnp.float32
BF16 = jnp.bfloat16
HIGHEST = lax.Precision.HIGHEST

RMS_EPS = 1e-6
SSD_NORM_EPS = 1e-5
CHUNK = 64
SSD_GROUPS = 4
POOL_WINDOWS = (2, 4, 8, 16)
PAST_LEN = 1024
TOP_K = 8
N_EXPERT_GROUPS = 8
TOPK_GROUPS = 4
ROUTED_SCALE = 2.5

LANES = 128
SUBLANES = 8
MOD_ROWS = 32
VMEM_LIMIT = 56 * 1024 * 1024
EXPERT_ROWS = 256
SEQ_TILE = 256


def _silu(x):
    return x * jax.nn.sigmoid(x)


def _softplus(x):
    return jnp.maximum(x, 0.0) + jnp.log1p(jnp.exp(-jnp.abs(x)))


def _row_tile(n, prefs):
    for t in prefs:
        if n % t == 0:
            return t
    return n


def _modulate(y, sc_ref, sh_ref):
    rows, d = y.shape
    y3 = y.reshape(rows // MOD_ROWS, MOD_ROWS, d)
    y3 = y3 * (1.0 + sc_ref[...][:, None, :]) + sh_ref[...][:, None, :]
    return y3.reshape(rows, d)


def _rms(x, g, eps):
    return x * lax.rsqrt(jnp.mean(x * x, axis=-1, keepdims=True) + eps) * g


def _ada_kernel(c_ref, w_ref, b_ref, o_ref):
    s = _silu(c_ref[...])
    o_ref[...] = jnp.dot(s, w_ref[...], preferred_element_type=F32, precision=HIGHEST) + b_ref[...]


def _ada(c_all, w_ada, b_ada):
    n, d = c_all.shape
    dout = w_ada.shape[1]
    tn = _row_tile(dout, (1024, 512, 256, 128))
    return pl.pallas_call(
        _ada_kernel,
        grid=(dout // tn,),
        in_specs=[pl.BlockSpec((n, d), lambda j: (0, 0)),
                  pl.BlockSpec((d, tn), lambda j: (0, j)),
                  pl.BlockSpec((1, tn), lambda j: (0, j))],
        out_specs=pl.BlockSpec((n, tn), lambda j: (0, j)),
        out_shape=jax.ShapeDtypeStruct((n, dout), F32),
        name="ada",
    )(c_all, w_ada, b_ada.reshape(1, dout))


def _two_part_specs(tm, d, ntp):
    return [pl.BlockSpec((tm, d), lambda i, *_: (jnp.minimum(i, ntp - 1), 0)),
            pl.BlockSpec((tm, d), lambda i, *_: (jnp.maximum(i - ntp, 0), 0))]


def _inproj_kernel(xp_ref, xs_ref, sc_ref, sh_ref, g_ref, w_ref, o_ref, dt_ref, *, ntp, tn):
    x = jnp.where(pl.program_id(0) < ntp, xp_ref[...], xs_ref[...])
    h = _modulate(_rms(x, g_ref[...], RMS_EPS), sc_ref, sh_ref).astype(BF16)
    n_main = o_ref.shape[1]
    for c0 in range(0, n_main, tn):
        o_ref[:, c0:c0 + tn] = jnp.dot(h, w_ref[:, c0:c0 + tn], preferred_element_type=F32).astype(BF16)
    dt_ref[...] = jnp.dot(h, w_ref[:, n_main:], preferred_element_type=F32)


def _inproj(xp, xs, sc_rows, sh_rows, g, wcat):
    tp, d = xp.shape
    ts = xs.shape[0]
    n = wcat.shape[1]
    n_main = n - LANES
    tm = _row_tile(ts, (512, 256))
    assert tp % tm == 0 and tm % (SUBLANES * MOD_ROWS) == 0
    ntp = tp // tm
    nt = ntp + ts // tm
    mr = tm // MOD_ROWS
    tn = _row_tile(n_main, (2048, 1024, 512, 256, 128))
    row = lambda i: (i, 0)
    return pl.pallas_call(
        functools.partial(_inproj_kernel, ntp=ntp, tn=tn),
        grid=(nt,),
        in_specs=_two_part_specs(tm, d, ntp) + [
            pl.BlockSpec((mr, d), row), pl.BlockSpec((mr, d), row),
            pl.BlockSpec((1, d), lambda i: (0, 0)),
            pl.BlockSpec((d, n), lambda i: (0, 0), pipeline_mode=pl.Buffered(1))],
        out_specs=[pl.BlockSpec((tm, n_main), row), pl.BlockSpec((tm, LANES), row)],
        out_shape=[jax.ShapeDtypeStruct((tp + ts, n_main), BF16),
                   jax.ShapeDtypeStruct((tp + ts, LANES), F32)],
        compiler_params=pltpu.CompilerParams(
            dimension_semantics=("parallel",), vmem_limit_bytes=VMEM_LIMIT),
        name="inproj",
    )(xp, xs, sc_rows, sh_rows, g.reshape(1, d), wcat)


def _seq_kernel(z_ref, xp_ref, bc_ref, u_ref, gt_ref, dt_ref, x_ref, g1_ref,
                cst_ref, sst_ref, pst_ref, cw_ref, cb_ref, dtb_ref, alog_ref, dx_ref, ng_ref,
                wssd_ref, pw_ref, ps_ref, wout_ref, e3_ref,
                xo_ref, so_ref, co_ref, po_ref,
                cbuf, pbuf, st_scr, y_scr, xbc_scr, csx_scr, wvx_scr,
                *, tq, q, pos0, heads, hd, ns, dm):
    j = pl.program_id(1)
    di = heads * hd
    gn = SSD_GROUPS * ns
    hpg = heads // SSD_GROUPS
    cw = cbuf.shape[0] - tq
    kw = cw_ref.shape[0]
    ph = pbuf.shape[0] - tq

    @pl.when(j == 0)
    def _():
        cbuf[cw - (kw - 1):cw, :] = cst_ref[0]
        pbuf[1:ph, :] = pst_ref[0]
        st_scr[...] = sst_ref[0].T

    cbuf[cw:cw + tq, 0:di] = xp_ref[...].astype(F32)
    cbuf[cw:cw + tq, di:di + 2 * gn] = bc_ref[...].astype(F32)
    acc = cb_ref[...] + cbuf[cw:cw + tq, :] * cw_ref[kw - 1:kw, :]
    for k in range(kw - 1):
        off = cw - (kw - 1) + k
        acc = acc + cbuf[off:off + tq, :] * cw_ref[k:k + 1, :]
    xbc_scr[...] = _silu(acc)
    cbuf[cw - (kw - 1):cw, :] = cbuf[cw + tq - (kw - 1):cw + tq, :]

    dt = _softplus(dt_ref[...] + dtb_ref[...])
    dta = dt * (-jnp.exp(alog_ref[...]))
    lq = q.bit_length() - 1
    r_i = lax.broadcasted_iota(jnp.int32, (tq, tq), 0)
    c_i = lax.broadcasted_iota(jnp.int32, (tq, tq), 1)
    same = lax.shift_right_logical(r_i, lq) == lax.shift_right_logical(c_i, lq)
    tril = jnp.where(same, (r_i >= c_i).astype(F32), 0.0)
    cs = jnp.dot(tril, dta, preferred_element_type=F32, precision=HIGHEST)
    cs_end = jnp.dot(same.astype(F32), dta, preferred_element_type=F32, precision=HIGHEST)
    wv = dt * jnp.exp(cs_end - cs)

    both = jnp.concatenate([cs, wv], axis=0)
    hi = both.astype(BF16)
    r1 = both - hi.astype(F32)
    mid = r1.astype(BF16)
    lo = (r1 - mid.astype(F32)).astype(BF16)
    ex = jnp.dot(jnp.concatenate([hi, mid, lo], axis=1), e3_ref[...], preferred_element_type=F32)
    csx_scr[...] = ex[0:tq]
    wvx_scr[...] = ex[tq:2 * tq]
    cs_t = cs.T
    dt_t = dt.T

    lane = lax.broadcasted_iota(jnp.int32, (q, 2 * hd), 1)
    kpos = jnp.bitwise_and(lane, hd - 1)
    causal2 = jnp.logical_and(lax.broadcasted_iota(jnp.int32, (q, 2 * hd), 0) >= kpos, kpos < q)
    first = lane < hd
    zrow = jnp.zeros((1, hd - q), F32)
    zblk = jnp.zeros((hd - q, 2 * hd), BF16)

    def pair_row(t, p, r0):
        parts = []
        for h in (2 * p, 2 * p + 1):
            parts.append(t[h:h + 1, r0:r0 + q])
            if q < hd:
                parts.append(zrow)
        return jnp.concatenate(parts, axis=1)

    def pair_rows(a, b):
        blocks = [a, zblk, b, zblk] if q < hd else [a, b]
        return jnp.concatenate(blocks, axis=0)

    gw = di // SSD_GROUPS
    for c in range(tq // q):
        r0 = c * q
        for g in range(SSD_GROUPS):
            bgb = xbc_scr[r0:r0 + q, di + g * ns:di + (g + 1) * ns].astype(BF16)
            cgb = xbc_scr[r0:r0 + q, di + gn + g * ns:di + gn + (g + 1) * ns].astype(BF16)
            cb2 = lax.dot_general(cgb, pair_rows(bgb, bgb), (((1,), (1,)), ((), ())),
                                  preferred_element_type=F32)
            stg = st_scr[:, g * gw:(g + 1) * gw]
            ecs = jnp.exp(csx_scr[r0:r0 + q, g * gw:(g + 1) * gw])
            yo = jnp.dot(cgb, stg.astype(BF16), preferred_element_type=F32) * ecs
            for pp in range(hpg // 2):
                p = g * (hpg // 2) + pp
                lo_l = p * 2 * hd
                seg = csx_scr[r0:r0 + q, lo_l:lo_l + 2 * hd] - pair_row(cs_t, p, r0)
                lm = jnp.exp(jnp.where(causal2, seg, -jnp.inf))
                m2 = (cb2 * lm * pair_row(dt_t, p, r0)).astype(BF16)
                xpair = xbc_scr[r0:r0 + q, lo_l:lo_l + 2 * hd]
                rhs = pair_rows(jnp.where(first, xpair, 0.0).astype(BF16),
                                jnp.where(first, 0.0, xpair).astype(BF16))
                yd = jnp.dot(m2, rhs, preferred_element_type=F32)
                y_scr[r0:r0 + q, lo_l:lo_l + 2 * hd] = yd + yo[:, pp * 2 * hd:(pp + 1) * 2 * hd]
            xw = (xbc_scr[r0:r0 + q, g * gw:(g + 1) * gw]
                  * wvx_scr[r0:r0 + q, g * gw:(g + 1) * gw]).astype(BF16)
            upd = lax.dot_general(bgb, xw, (((0,), (0,)), ((), ())), preferred_element_type=F32)
            st_scr[:, g * gw:(g + 1) * gw] = stg * ecs[q - 1:q, :] + upd

    @pl.when(j == pl.num_programs(1) - 1)
    def _():
        so_ref[0] = st_scr[...].T
        co_ref[0] = cbuf[cw - (kw - 1):cw, :]

    y = y_scr[...] + xbc_scr[:, 0:di] * dx_ref[...]
    v = y * _silu(z_ref[...].astype(F32))
    gw = di // SSD_GROUPS
    parts = []
    for g in range(SSD_GROUPS):
        vg = v[:, g * gw:(g + 1) * gw]
        parts.append(vg * lax.rsqrt(jnp.mean(vg * vg, axis=-1, keepdims=True) + SSD_NORM_EPS))
    yn = (jnp.concatenate(parts, axis=-1) * ng_ref[...]).astype(BF16)
    a_br = jnp.dot(yn, wssd_ref[...], preferred_element_type=F32)

    pbuf[ph:ph + tq, :] = u_ref[...].astype(F32)
    pos = pos0 + j * tq + lax.broadcasted_iota(jnp.int32, (tq, 1), 0)
    pgd = dm // len(POOL_WINDOWS)
    b_parts = []
    for gi, w in enumerate(POOL_WINDOWS):
        lo = gi * pgd
        s = pbuf[ph:ph + tq, lo:lo + pgd]
        for i in range(1, w):
            s = s + pbuf[ph - i:ph - i + tq, lo:lo + pgd]
        cnt = jnp.minimum(w, pos + 1).astype(F32)
        pooled = s / cnt - pbuf[ph:ph + tq, lo:lo + pgd]
        b_parts.append(jnp.dot(pooled.astype(BF16), pw_ref[gi], preferred_element_type=F32))
    b_br = jnp.concatenate(b_parts, axis=-1) * ps_ref[...]
    pbuf[1:ph, :] = pbuf[tq + 1:tq + ph, :]

    @pl.when(j == pl.num_programs(1) - 1)
    def _():
        po_ref[0] = pbuf[1:ph, :]

    gates = jax.nn.sigmoid(gt_ref[...].astype(F32))
    merged = gates[:, 0:dm] * a_br + gates[:, dm:2 * dm] * b_br
    mix = jnp.dot(merged.astype(BF16), wout_ref[...], preferred_element_type=F32)
    xo_ref[...] = x_ref[...] + g1_ref[0] * mix


def _seq(proj, dtp, x, g1, conv_state, ssm_state, pool_state, wts, *, row0, nb, seqlen, pos0, cols):
    dm = x.shape[1]
    heads, hd, ns = ssm_state.shape[1], ssm_state.shape[2], ssm_state.shape[3]
    di = heads * hd
    gn = SSD_GROUPS * ns
    q = min(CHUNK, seqlen)
    tq = _row_tile(seqlen, (SEQ_TILE, 128, 64, 32))
    nj = seqlen // tq
    rb0 = row0 // tq
    oz, ox, obc, ou, og = cols

    def rows(b, j):
        return rb0 + b * nj + j

    full = lambda a: pl.BlockSpec(a.shape, lambda b, j: (0,) * a.ndim, pipeline_mode=pl.Buffered(1))
    kern = functools.partial(_seq_kernel, tq=tq, q=q, pos0=pos0, heads=heads, hd=hd, ns=ns, dm=dm)
    kw1 = conv_state.shape[1]
    pst = pool_state.shape[1]
    xo, so, co, po = pl.pallas_call(
        kern,
        grid=(nb, nj),
        in_specs=[
            pl.BlockSpec((tq, di), lambda b, j: (rows(b, j), oz // di)),
            pl.BlockSpec((tq, di), lambda b, j: (rows(b, j), ox // di)),
            pl.BlockSpec((tq, 2 * gn), lambda b, j: (rows(b, j), obc // (2 * gn))),
            pl.BlockSpec((tq, dm), lambda b, j: (rows(b, j), ou // dm)),
            pl.BlockSpec((tq, 2 * dm), lambda b, j: (rows(b, j), og // (2 * dm))),
            pl.BlockSpec((tq, LANES), lambda b, j: (rows(b, j), 0)),
            pl.BlockSpec((tq, dm), lambda b, j: (b * nj + j, 0)),
            pl.BlockSpec((1, 1, dm), lambda b, j: (b, 0, 0)),
            pl.BlockSpec((1, kw1, di + 2 * gn), lambda b, j: (b, 0, 0)),
            pl.BlockSpec((1, di, ns), lambda b, j: (b, 0, 0)),
            pl.BlockSpec((1, pst, dm), lambda b, j: (b, 0, 0)),
        ] + [full(w) for w in wts],
        out_specs=[pl.BlockSpec((tq, dm), lambda b, j: (b * nj + j, 0)),
                   pl.BlockSpec((1, di, ns), lambda b, j: (b, 0, 0)),
                   pl.BlockSpec((1, kw1, di + 2 * gn), lambda b, j: (b, 0, 0)),
                   pl.BlockSpec((1, pst, dm), lambda b, j: (b, 0, 0))],
        out_shape=[jax.ShapeDtypeStruct((nb * seqlen, dm), F32),
                   jax.ShapeDtypeStruct((nb, di, ns), F32),
                   jax.ShapeDtypeStruct((nb, kw1, di + 2 * gn), F32),
                   jax.ShapeDtypeStruct((nb, pst, dm), F32)],
        scratch_shapes=[pltpu.VMEM((tq + SUBLANES, di + 2 * gn), F32),
                        pltpu.VMEM((tq + 2 * SUBLANES, dm), F32),
                        pltpu.VMEM((ns, di), F32),
                        pltpu.VMEM((tq, di), F32),
                        pltpu.VMEM((tq, di + 2 * gn), F32),
                        pltpu.VMEM((tq, di), F32),
                        pltpu.VMEM((tq, di), F32)],
        compiler_params=pltpu.CompilerParams(
            dimension_semantics=("parallel", "arbitrary"), vmem_limit_bytes=VMEM_LIMIT),
        name="seq",
    )(proj, proj, proj, proj, proj, dtp, x, g1.reshape(nb, 1, dm),
      conv_state, ssm_state.reshape(nb, di, ns), pool_state, *wts)
    return xo, so.reshape(nb, heads, hd, ns), co, po


def _moe_pre_kernel(xp_ref, xs_ref, sc_ref, sh_ref, g_ref, wr_ref, rb_ref, swg_ref, swu_ref, swd_ref,
                    h_ref, s_ref, e_ref, w_ref, m_ref, cnt_ref, *, ntp):
    x = jnp.where(pl.program_id(0) < ntp, xp_ref[...], xs_ref[...])
    y = _rms(x, g_ref[...], RMS_EPS)
    h = _modulate(y, sc_ref, sh_ref)
    _write_rows(h_ref, h)
    hb = h.astype(BF16)

    sg = jnp.dot(hb, swg_ref[...], preferred_element_type=F32)
    su = jnp.dot(hb, swu_ref[...], preferred_element_type=F32)
    s_ref[...] = jnp.dot((_silu(sg) * su).astype(BF16), swd_ref[...], preferred_element_type=F32)

    scores = jax.nn.sigmoid(jnp.dot(hb, wr_ref[...], preferred_element_type=F32))
    biased = scores + rb_ref[...]
    tm, ne = scores.shape
    per_g = ne // N_EXPERT_GROUPS
    neg = -jnp.inf
    lane_i = lax.broadcasted_iota(jnp.int32, (tm, ne), 1)
    lane = lane_i.astype(F32)
    lgroup = lax.shift_right_logical(lane_i, per_g.bit_length() - 1).astype(F32)
    lane_o = lax.broadcasted_iota(jnp.int32, (tm, LANES), 1).astype(F32)

    def first_argmax(v, idx, big):
        top = jnp.max(v, axis=-1, keepdims=True)
        return top, jnp.min(jnp.where(v == top, idx, big), axis=-1, keepdims=True)

    gs = jnp.full((tm, LANES), neg, F32)
    for g in range(N_EXPERT_GROUPS):
        mg = jnp.where(lgroup == g, biased, neg)
        t1, i1 = first_argmax(mg, lane, float(ne))
        t2 = jnp.max(jnp.where(lane == i1, neg, mg), axis=-1, keepdims=True)
        gs = jnp.where(lane_o == g, t1 + t2, gs)
    allowed = jnp.zeros((tm, ne), F32)
    for _ in range(TOPK_GROUPS):
        _, gi = first_argmax(gs, lane_o, float(LANES))
        allowed = jnp.where(lgroup == gi, 1.0, allowed)
        gs = jnp.where(lane_o == gi, neg, gs)
    mb = jnp.where(allowed > 0.0, biased, neg)
    eacc = jnp.zeros((tm, LANES), F32)
    wacc = jnp.zeros((tm, LANES), F32)
    chosen = jnp.zeros((tm, ne), F32)
    for k in range(TOP_K):
        _, ik = first_argmax(mb, lane, float(ne))
        sel = lane == ik
        wk = jnp.sum(jnp.where(sel, scores, 0.0), axis=-1, keepdims=True)
        eacc = jnp.where(lane_o == k, ik, eacc)
        wacc = jnp.where(lane_o == k, wk, wacc)
        chosen = jnp.where(sel, 1.0, chosen)
        mb = jnp.where(sel, neg, mb)
    wsum = jnp.sum(wacc, axis=-1, keepdims=True)
    e_ref[...] = eacc.astype(jnp.int32)
    w_ref[...] = wacc / (wsum + 1e-20) * ROUTED_SCALE
    m_ref[...] = chosen.astype(BF16)

    @pl.when(pl.program_id(0) == 0)
    def _():
        cnt_ref[...] = jnp.zeros_like(cnt_ref)

    cnt_ref[0:1, :] += jnp.sum(chosen, axis=0, keepdims=True)


def _moe_pre(xp1, xs1, sc_rows, sh_rows, g, wr, rb, swg, swu, swd):
    tp, d = xp1.shape
    ts = xs1.shape[0]
    t = tp + ts
    ne = wr.shape[1]
    ff = swg.shape[1]
    tm = _row_tile(ts, (256,))
    assert tp % tm == 0 and tm % (SUBLANES * MOD_ROWS) == 0
    ntp = tp // tm
    mr = tm // MOD_ROWS
    nsub = d // LANES
    row = lambda i: (i, 0)
    const = lambda i: (0, 0)
    return pl.pallas_call(
        functools.partial(_moe_pre_kernel, ntp=ntp),
        grid=(t // tm,),
        in_specs=_two_part_specs(tm, d, ntp) + [
                  pl.BlockSpec((mr, d), row), pl.BlockSpec((mr, d), row),
                  pl.BlockSpec((1, d), const), pl.BlockSpec((d, ne), const), pl.BlockSpec((1, ne), const),
                  pl.BlockSpec((d, ff), const), pl.BlockSpec((d, ff), const), pl.BlockSpec((ff, d), const)],
        out_specs=[pl.BlockSpec((tm * nsub, LANES), row), pl.BlockSpec((tm, d), row),
                   pl.BlockSpec((tm, LANES), row), pl.BlockSpec((tm, LANES), row),
                   pl.BlockSpec((tm, ne), row), pl.BlockSpec((SUBLANES, ne), const)],
        out_shape=[jax.ShapeDtypeStruct((t * nsub, LANES), F32), jax.ShapeDtypeStruct((t, d), F32),
                   jax.ShapeDtypeStruct((t, LANES), jnp.int32), jax.ShapeDtypeStruct((t, LANES), F32),
                   jax.ShapeDtypeStruct((t, ne), BF16), jax.ShapeDtypeStruct((SUBLANES, ne), F32)],
        compiler_params=pltpu.CompilerParams(
            dimension_semantics=("arbitrary",), vmem_limit_bytes=VMEM_LIMIT),
        name="moe_pre",
    )(xp1, xs1, sc_rows, sh_rows, g.reshape(1, d), wr, rb.reshape(1, ne), swg, swu, swd)


def _read_rows(buf, nrows, nsub):
    return jnp.concatenate([buf[pl.ds(c, nrows, stride=nsub), :] for c in range(nsub)], axis=-1)


def _write_rows(ref, v):
    nrows, width = v.shape
    nsub = width // LANES
    for c in range(nsub):
        ref[pl.ds(c, nrows, stride=nsub), :] = v[:, c * LANES:(c + 1) * LANES]


def _rank_kernel(m_ref, e_ref, ps_ref, d_ref, carry):
    i = pl.program_id(0)
    ne = m_ref.shape[1]

    @pl.when(i == 0)
    def _():
        carry[...] = jnp.zeros_like(carry)

    m = m_ref[...]
    tr = m.shape[0]
    r_i = lax.broadcasted_iota(jnp.int32, (tr, tr), 0)
    c_i = lax.broadcasted_iota(jnp.int32, (tr, tr), 1)
    earlier = (r_i > c_i).astype(BF16)
    slot = jnp.dot(earlier, m, preferred_element_type=F32) + carry[0:1, :] + ps_ref[...]
    e = e_ref[...]
    lane = lax.broadcasted_iota(jnp.int32, (tr, ne), 1)
    lane_o = lax.broadcasted_iota(jnp.int32, (tr, LANES), 1)
    dacc = jnp.zeros((tr, LANES), F32)
    for k in range(TOP_K):
        dk = jnp.sum(jnp.where(lane == e[:, k:k + 1], slot, 0.0), axis=-1, keepdims=True)
        dacc = jnp.where(lane_o == k, dk, dacc)
    d_ref[...] = dacc.astype(jnp.int32)
    carry[0:1, :] += jnp.sum(m.astype(F32), axis=0, keepdims=True)


def _rank(mask, eidx, pstart):
    t, ne = mask.shape
    tr = _row_tile(t, (512, 256))
    row = lambda i: (i, 0)
    return pl.pallas_call(
        _rank_kernel,
        grid=(t // tr,),
        in_specs=[pl.BlockSpec((tr, ne), row), pl.BlockSpec((tr, LANES), row),
                  pl.BlockSpec((1, ne), lambda i: (0, 0))],
        out_specs=pl.BlockSpec((tr, LANES), row),
        out_shape=jax.ShapeDtypeStruct((t, LANES), jnp.int32),
        scratch_shapes=[pltpu.VMEM((SUBLANES, ne), F32)],
        compiler_params=pltpu.CompilerParams(
            dimension_semantics=("arbitrary",), vmem_limit_bytes=VMEM_LIMIT),
        name="rank",
    )(mask, eidx, pstart)


def _dispatch_kernel(ps_ref, pl_ref, nu_ref, d_ref, h_ref, xs_hbm, zbuf, sem, zsem, *, tt, nsub, bm, nb):
    i = pl.program_id(0)
    n = pl.num_programs(0)
    ne = ps_ref.shape[0]
    pieces = [1 << b for b in reversed(range(bm.bit_length() - 1))]

    def zero_fill(op):
        def pad(e, carry):
            start = ps_ref[e]
            length = pl_ref[e]
            for p in pieces:
                off = length - lax.rem(length, 2 * p)

                @pl.when(lax.rem(length, 2 * p) >= p)
                def _():
                    dst = xs_hbm.at[pl.ds(pl.multiple_of((start + off) * nsub, nsub), p * nsub)]
                    op(pltpu.make_async_copy(zbuf.at[pl.ds(0, p * nsub)], dst, zsem))
            return carry
        lax.fori_loop(0, ne, pad, 0)

        def tail(b, carry):
            dst = xs_hbm.at[pl.ds(pl.multiple_of(b * (bm * nsub), bm * nsub), bm * nsub)]
            op(pltpu.make_async_copy(zbuf, dst, zsem))
            return carry
        lax.fori_loop(nu_ref[0], nb, tail, 0)

    @pl.when(i == 0)
    def _():
        zbuf[...] = jnp.zeros_like(zbuf)
        zero_fill(lambda c: c.start())

    def body(r, carry):
        src = h_ref.at[pl.ds(pl.multiple_of(r * nsub, nsub), nsub)]
        for k in range(TOP_K):
            dst = xs_hbm.at[pl.ds(pl.multiple_of(d_ref[0, 0, r * TOP_K + k] * nsub, nsub), nsub)]
            pltpu.make_async_copy(src, dst, sem).start(priority=k % 2)
        return carry

    lax.fori_loop(0, tt, body, 0)

    for _ in range(TOP_K):
        pltpu.make_async_copy(h_ref, xs_hbm.at[pl.ds(0, tt * nsub)], sem).wait()

    @pl.when(i == n - 1)
    def _():
        zero_fill(lambda c: c.wait())


def _dispatch(dest, h2_tiles, pad_start, pad_len, n_used, nb, nsub):
    t = dest.shape[0]
    tt = _row_tile(t, (256,))
    nt = t // tt
    bm = EXPERT_ROWS
    d3 = dest.reshape(nt, 1, tt * TOP_K)
    gs = pltpu.PrefetchScalarGridSpec(
        num_scalar_prefetch=3,
        grid=(nt,),
        in_specs=[pl.BlockSpec((1, 1, tt * TOP_K), lambda i, *_: (i, 0, 0), memory_space=pltpu.SMEM),
                  pl.BlockSpec((tt * nsub, LANES), lambda i, *_: (i, 0))],
        out_specs=pl.BlockSpec(memory_space=pl.ANY),
        scratch_shapes=[pltpu.VMEM((bm * nsub, LANES), F32), pltpu.SemaphoreType.DMA,
                        pltpu.SemaphoreType.DMA],
    )
    return pl.pallas_call(
        functools.partial(_dispatch_kernel, tt=tt, nsub=nsub, bm=bm, nb=nb),
        grid_spec=gs,
        out_shape=jax.ShapeDtypeStruct((nb * bm * nsub, LANES), F32),
        compiler_params=pltpu.CompilerParams(dimension_semantics=("arbitrary",)),
        name="dispatch",
    )(pad_start, pad_len, n_used, d3, h2_tiles)


def _grouped_kernel(be_ref, nu_ref, x_ref, wg_ref, wu_ref, wd_ref, y_ref, wgb, wub, wdb):
    i = pl.program_id(0)

    @pl.when(i < nu_ref[0])
    def _():
        @pl.when(jnp.logical_or(i == 0, be_ref[i] != be_ref[jnp.maximum(i - 1, 0)]))
        def _():
            wgb[...] = wg_ref[0].astype(BF16)
            wub[...] = wu_ref[0].astype(BF16)
            wdb[...] = wd_ref[0].astype(BF16)

        d = wgb.shape[0]
        nsub = d // LANES
        x = _read_rows(x_ref, x_ref.shape[0] // nsub, nsub).astype(BF16)
        hg = jnp.dot(x, wgb[...], preferred_element_type=F32)
        hu = jnp.dot(x, wub[...], preferred_element_type=F32)
        y = jnp.dot((_silu(hg) * hu).astype(BF16), wdb[...], preferred_element_type=F32)
        _write_rows(y_ref, y)

    @pl.when(i >= nu_ref[0])
    def _():
        y_ref[...] = jnp.zeros_like(y_ref)


def _grouped(xs, block_e, n_used, wg, wu, wd):
    ne, d, ff = wg.shape
    nsub = d // LANES
    bm = EXPERT_ROWS
    nb = block_e.shape[0]
    used = lambda i, nu: jnp.minimum(i, nu[0] - 1)
    gs = pltpu.PrefetchScalarGridSpec(
        num_scalar_prefetch=2,
        grid=(nb,),
        in_specs=[pl.BlockSpec((bm * nsub, LANES), lambda i, be, nu: (used(i, nu), 0)),
                  pl.BlockSpec((1, d, ff), lambda i, be, nu: (be[used(i, nu)], 0, 0)),
                  pl.BlockSpec((1, d, ff), lambda i, be, nu: (be[used(i, nu)], 0, 0)),
                  pl.BlockSpec((1, ff, d), lambda i, be, nu: (be[used(i, nu)], 0, 0))],
        out_specs=pl.BlockSpec((bm * nsub, LANES), lambda i, be, nu: (i, 0)),
        scratch_shapes=[pltpu.VMEM((d, ff), BF16), pltpu.VMEM((d, ff), BF16), pltpu.VMEM((ff, d), BF16)],
    )
    return pl.pallas_call(
        _grouped_kernel,
        grid_spec=gs,
        out_shape=jax.ShapeDtypeStruct((nb * bm * nsub, LANES), F32),
        compiler_params=pltpu.CompilerParams(
            dimension_semantics=("arbitrary",), vmem_limit_bytes=VMEM_LIMIT),
        name="grouped",
    )(block_e, n_used, xs, wg, wu, wd)


def _combine_kernel(d_ref, dn_ref, y_hbm, w_ref, s_ref, xp_ref, xs_ref, g2_ref, fg_ref, op_ref, os_ref,
                    buf, sem, *, tt, nsub, ntp):
    i = pl.program_id(0)
    n = pl.num_programs(0)

    def gather(dref, slot):
        def body(r, carry):
            for k in range(TOP_K):
                src = y_hbm.at[pl.ds(pl.multiple_of(dref[0, 0, r * TOP_K + k] * nsub, nsub), nsub)]
                pltpu.make_async_copy(src, buf.at[slot, k, pl.ds(pl.multiple_of(r * nsub, nsub), nsub)],
                                      sem.at[slot]).start(priority=k % 2)
            return carry
        lax.fori_loop(0, tt, body, 0)

    @pl.when(i == 0)
    def _():
        gather(d_ref, 0)

    @pl.when(i + 1 < n)
    def _():
        gather(dn_ref, (i + 1) % 2)

    slot = i % 2
    for k in range(TOP_K):
        pltpu.make_async_copy(y_hbm.at[pl.ds(0, tt * nsub)], buf.at[slot, k], sem.at[slot]).wait()

    w = w_ref[...]
    routed = w[:, 0:1] * _read_rows(buf.at[slot, 0], tt, nsub)
    for k in range(1, TOP_K):
        routed = routed + w[:, k:k + 1] * _read_rows(buf.at[slot, k], tt, nsub)
    moe = routed + s_ref[...]
    d = moe.shape[1]
    m3 = moe.reshape(tt // MOD_ROWS, MOD_ROWS, d) * g2_ref[...][:, None, :]
    x1 = jnp.where(i < ntp, xp_ref[...], xs_ref[...])
    out = _rms(x1 + m3.reshape(tt, d), fg_ref[...], RMS_EPS)

    @pl.when(i < ntp)
    def _():
        op_ref[...] = out

    @pl.when(i >= ntp)
    def _():
        os_ref[...] = out


def _combine(dest, y_tiles, wts, shared, xp1, xs1, g2_rows, final_g):
    tp, d = xp1.shape
    ts = xs1.shape[0]
    nsub = d // LANES
    tt = _row_tile(ts, (256,))
    assert tp % tt == 0 and tt % (SUBLANES * MOD_ROWS) == 0
    ntp = tp // tt
    mr = tt // MOD_ROWS
    nt = (tp + ts) // tt
    d3 = dest.reshape(nt, 1, tt * TOP_K)
    kern = functools.partial(_combine_kernel, tt=tt, nsub=nsub, ntp=ntp)
    smem_blk = lambda f: pl.BlockSpec((1, 1, tt * TOP_K), f, memory_space=pltpu.SMEM)
    row = lambda i: (i, 0)
    two = _two_part_specs(tt, d, ntp)
    return pl.pallas_call(
        kern,
        grid=(nt,),
        in_specs=[smem_blk(lambda i: (i, 0, 0)),
                  smem_blk(lambda i: (jnp.minimum(i + 1, nt - 1), 0, 0)),
                  pl.BlockSpec(memory_space=pl.ANY),
                  pl.BlockSpec((tt, LANES), row), pl.BlockSpec((tt, d), row)] + two + [
                  pl.BlockSpec((mr, d), row), pl.BlockSpec((1, d), lambda i: (0, 0))],
        out_specs=two,
        out_shape=[jax.ShapeDtypeStruct((tp, d), F32), jax.ShapeDtypeStruct((ts, d), F32)],
        scratch_shapes=[pltpu.VMEM((2, TOP_K, tt * nsub, LANES), F32), pltpu.SemaphoreType.DMA((2,))],
        compiler_params=pltpu.CompilerParams(
            dimension_semantics=("arbitrary",), vmem_limit_bytes=VMEM_LIMIT),
        name="combine",
    )(d3, d3, y_tiles, wts, shared, xp1, xs1, g2_rows, final_g.reshape(1, d))


def _plan(counts, n_assign):
    ne = counts.shape[0]
    bm = EXPERT_ROWS
    counts = counts.astype(jnp.int32)
    nblk = (counts + bm - 1) // bm
    bend = jnp.cumsum(nblk)
    pstart = ((bend - nblk) * bm).astype(F32).reshape(1, ne)
    nb = -(-(n_assign + ne * (bm - 1)) // bm)
    block_e = jnp.minimum(jnp.sum(bend[None, :] <= jnp.arange(nb)[:, None], axis=1), ne - 1).astype(jnp.int32)
    n_used = bend[-1:].astype(jnp.int32)
    pad_start = ((bend - nblk) * bm + counts).astype(jnp.int32)
    pad_len = (nblk * bm - counts).astype(jnp.int32)
    return pstart, block_e, n_used, pad_start, pad_len, nb


def per_g_pow2(ne):
    per_g = ne // N_EXPERT_GROUPS
    return per_g * N_EXPERT_GROUPS == ne and per_g & (per_g - 1) == 0


def _mod_rows(m, nbp, lp):
    return jnp.concatenate([jnp.repeat(m[:nbp], lp // MOD_ROWS, axis=0), m[nbp:]], axis=0)


def kernel(x_prompt, x_sample, state_ssm, state_conv, state_pool, c_prompt, c_sample, ln1_g, ln2_g, w_ada, b_ada, w_in, conv_w, conv_b, dt_bias, a_log, d_skip, ssd_norm_g, w_ssd_out, pool_w, pool_scale, w_out, w_router, router_bias, moe_w_gate, moe_w_up, moe_w_down, shared_w_gate, shared_w_up, shared_w_down, final_g):
    bp, lp, dm = x_prompt.shape
    bs, ls, _ = x_sample.shape
    depth = ln1_g.shape[0]
    heads, hd, ns = state_ssm.shape[2], state_ssm.shape[3], state_ssm.shape[4]
    di = heads * hd
    gn = SSD_GROUPS * ns
    cch = di + 2 * gn
    assert depth == 1 and ls == MOD_ROWS and lp % MOD_ROWS == 0 and heads <= LANES
    assert per_g_pow2(w_router.shape[2])
    assert ns == 2 * hd and hd & (hd - 1) == 0 and (heads // SSD_GROUPS) % 2 == 0
    tp, ts = bp * lp, bs * ls
    expand3 = (jnp.arange(3 * LANES)[:, None] % LANES == jnp.arange(di)[None, :] // hd).astype(BF16)

    xp, xs = x_prompt.reshape(tp, dm), x_sample.reshape(ts, dm)
    c_all = jnp.concatenate([c_prompt, c_sample], axis=0)

    o1, o2, o3, o4 = di, di + cch, di + cch + heads, di + cch + heads + dm
    cols = (0, di, 2 * di, di + cch, di + cch + dm)

    ssm_p, conv_p, pool_p, ssm_s, conv_s, pool_s = [], [], [], [], [], []
    for l in range(depth):
        wi = w_in[l]
        wcat = jnp.concatenate(
            [wi[:, :o1], wi[:, o1:o2], wi[:, o3:o4], wi[:, o4:],
             jnp.pad(wi[:, o2:o3], ((0, 0), (0, LANES - heads)))], axis=1).astype(BF16)
        pad_h = lambda v: jnp.pad(v.reshape(1, heads), ((0, 0), (0, LANES - heads)))
        seq_w = (conv_w[l], conv_b[l].reshape(1, cch), pad_h(dt_bias[l]), pad_h(a_log[l]),
                 jnp.repeat(d_skip[l], hd).reshape(1, di), ssd_norm_g[l].reshape(1, di),
                 w_ssd_out[l].astype(BF16), pool_w[l].astype(BF16), pool_scale[l].reshape(1, dm),
                 w_out[l].astype(BF16), expand3)

        mod = _ada(c_all, w_ada[l], b_ada[l])
        sh1, sc1, g1, sh2, sc2, g2 = jnp.split(mod, 6, axis=-1)

        proj, dtp = _inproj(xp, xs, _mod_rows(sc1, bp, lp), _mod_rows(sh1, bp, lp), ln1_g[l], wcat)

        zc = jnp.zeros((bp,) + state_conv.shape[2:], F32)
        zs = jnp.zeros((bp, heads, hd, ns), F32)
        zp = jnp.zeros((bp,) + state_pool.shape[2:], F32)
        xp1, ns_p, nc_p, np_p = _seq(proj, dtp, xp, g1[:bp], zc, zs, zp, seq_w,
                                     row0=0, nb=bp, seqlen=lp, pos0=0, cols=cols)
        xs1, ns_s, nc_s, np_s = _seq(proj, dtp, xs, g1[bp:], state_conv[l], state_ssm[l], state_pool[l],
                                     seq_w, row0=tp, nb=bs, seqlen=ls, pos0=PAST_LEN, cols=cols)
        conv_p.append(nc_p)
        conv_s.append(nc_s)
        pool_p.append(np_p)
        pool_s.append(np_s)
        ssm_p.append(ns_p)
        ssm_s.append(ns_s)

        h2_tiles, shared, eidx_l, wts, mask, counts = _moe_pre(
            xp1, xs1, _mod_rows(sc2, bp, lp), _mod_rows(sh2, bp, lp), ln2_g[l],
            w_router[l].astype(BF16), router_bias[l],
            shared_w_gate[l].astype(BF16), shared_w_up[l].astype(BF16), shared_w_down[l].astype(BF16))
        nsub = dm // LANES
        pstart, block_e, n_used, pad_start, pad_len, nb = _plan(counts[0], (tp + ts) * TOP_K)
        dest = _rank(mask, eidx_l, pstart)[:, :TOP_K]
        x_sorted = _dispatch(dest, h2_tiles, pad_start, pad_len, n_used, nb, nsub)
        y_tiles = _grouped(x_sorted, block_e, n_used, moe_w_gate[l], moe_w_up[l], moe_w_down[l])
        xp, xs = _combine(dest, y_tiles, wts, shared, xp1, xs1, _mod_rows(g2, bp, lp), final_g)

    y_prompt = xp.reshape(bp, lp, dm)
    y_sample = xs.reshape(bs, ls, dm)
    return (y_prompt, y_sample, jnp.stack(ssm_p), jnp.stack(conv_p), jnp.stack(pool_p),
            jnp.stack(ssm_s), jnp.stack(conv_s), jnp.stack(pool_s))
```

```python
import functools

import jax
import jax.numpy as jnp
from jax import lax
from jax.experimental import pallas as pl
from jax.experimental.pallas import tpu as pltpu

F32 = jnp.float32
BF16 = jnp.bfloat16
HIGHEST = lax.Precision.HIGHEST

RMS_EPS = 1e-6
SSD_NORM_EPS = 1e-5
CHUNK = 64
SSD_GROUPS = 4
POOL_WINDOWS = (2, 4, 8, 16)
PAST_LEN = 1024
TOP_K = 8
N_EXPERT_GROUPS = 8
TOPK_GROUPS = 4
ROUTED_SCALE = 2.5

LANES = 128
SUBLANES = 8
MOD_ROWS = 32
VMEM_LIMIT = 56 * 1024 * 1024
EXPERT_ROWS = 128
SEQ_TILE = 256


def _silu(x):
    return x * jax.nn.sigmoid(x)


def _softplus(x):
    return jnp.maximum(x, 0.0) + jnp.log1p(jnp.exp(-jnp.abs(x)))


def _row_tile(n, prefs):
    for t in prefs:
        if n % t == 0:
            return t
    return n


def _modulate(y, sc_ref, sh_ref):
    rows, d = y.shape
    y3 = y.reshape(rows // MOD_ROWS, MOD_ROWS, d)
    y3 = y3 * (1.0 + sc_ref[...][:, None, :]) + sh_ref[...][:, None, :]
    return y3.reshape(rows, d)


def _rms(x, g, eps):
    return x * lax.rsqrt(jnp.mean(x * x, axis=-1, keepdims=True) + eps) * g


def _ada_kernel(c_ref, w_ref, b_ref, o_ref):
    s = _silu(c_ref[...])
    o_ref[...] = jnp.dot(s, w_ref[...], preferred_element_type=F32, precision=HIGHEST) + b_ref[...]


def _ada(c_all, w_ada, b_ada):
    n, d = c_all.shape
    dout = w_ada.shape[1]
    tn = _row_tile(dout, (1024, 512, 256, 128))
    return pl.pallas_call(
        _ada_kernel,
        grid=(dout // tn,),
        in_specs=[pl.BlockSpec((n, d), lambda j: (0, 0)),
                  pl.BlockSpec((d, tn), lambda j: (0, j)),
                  pl.BlockSpec((1, tn), lambda j: (0, j))],
        out_specs=pl.BlockSpec((n, tn), lambda j: (0, j)),
        out_shape=jax.ShapeDtypeStruct((n, dout), F32),
        name="ada",
    )(c_all, w_ada, b_ada.reshape(1, dout))


def _two_part_specs(tm, d, ntp):
    return [pl.BlockSpec((tm, d), lambda i, *_: (jnp.minimum(i, ntp - 1), 0)),
            pl.BlockSpec((tm, d), lambda i, *_: (jnp.maximum(i - ntp, 0), 0))]


def _inproj_kernel(xp_ref, xs_ref, sc_ref, sh_ref, g_ref, w_ref, o_ref, dt_ref, *, ntp, tn):
    x = jnp.where(pl.program_id(0) < ntp, xp_ref[...], xs_ref[...])
    h = _modulate(_rms(x, g_ref[...], RMS_EPS), sc_ref, sh_ref).astype(BF16)
    n_main = o_ref.shape[1]
    for c0 in range(0, n_main, tn):
        o_ref[:, c0:c0 + tn] = jnp.dot(h, w_ref[:, c0:c0 + tn], preferred_element_type=F32).astype(BF16)
    dt_ref[...] = jnp.dot(h, w_ref[:, n_main:], preferred_element_type=F32)


def _inproj(xp, xs, sc_rows, sh_rows, g, wcat):
    tp, d = xp.shape
    ts = xs.shape[0]
    n = wcat.shape[1]
    n_main = n - LANES
    tm = _row_tile(ts, (512, 256))
    assert tp % tm == 0 and tm % (SUBLANES * MOD_ROWS) == 0
    ntp = tp // tm
    nt = ntp + ts // tm
    mr = tm // MOD_ROWS
    tn = _row_tile(n_main, (2048, 1024, 512, 256, 128))
    row = lambda i: (i, 0)
    return pl.pallas_call(
        functools.partial(_inproj_kernel, ntp=ntp, tn=tn),
        grid=(nt,),
        in_specs=_two_part_specs(tm, d, ntp) + [
            pl.BlockSpec((mr, d), row), pl.BlockSpec((mr, d), row),
            pl.BlockSpec((1, d), lambda i: (0, 0)),
            pl.BlockSpec((d, n), lambda i: (0, 0), pipeline_mode=pl.Buffered(1))],
        out_specs=[pl.BlockSpec((tm, n_main), row), pl.BlockSpec((tm, LANES), row)],
        out_shape=[jax.ShapeDtypeStruct((tp + ts, n_main), BF16),
                   jax.ShapeDtypeStruct((tp + ts, LANES), F32)],
        compiler_params=pltpu.CompilerParams(
            dimension_semantics=("parallel",), vmem_limit_bytes=VMEM_LIMIT),
        name="inproj",
    )(xp, xs, sc_rows, sh_rows, g.reshape(1, d), wcat)


def _seq_kernel(z_ref, xp_ref, bc_ref, u_ref, gt_ref, dt_ref, x_ref, g1_ref,
                cst_ref, sst_ref, pst_ref, cw_ref, cb_ref, dtb_ref, alog_ref, dx_ref, ng_ref,
                wssd_ref, pw_ref, ps_ref, wout_ref, e3_ref,
                xo_ref, so_ref, co_ref, po_ref,
                cbuf, pbuf, st_scr, y_scr, xbc_scr, csx_scr, wvx_scr,
                *, tq, q, pos0, heads, hd, ns, dm):
    j = pl.program_id(1)
    di = heads * hd
    gn = SSD_GROUPS * ns
    hpg = heads // SSD_GROUPS
    cw = cbuf.shape[0] - tq
    kw = cw_ref.shape[0]
    ph = pbuf.shape[0] - tq

    @pl.when(j == 0)
    def _():
        cbuf[cw - (kw - 1):cw, :] = cst_ref[0]
        pbuf[1:ph, :] = pst_ref[0]
        st_scr[...] = sst_ref[0].T

    cbuf[cw:cw + tq, 0:di] = xp_ref[...].astype(F32)
    cbuf[cw:cw + tq, di:di + 2 * gn] = bc_ref[...].astype(F32)
    acc = cb_ref[...] + cbuf[cw:cw + tq, :] * cw_ref[kw - 1:kw, :]
    for k in range(kw - 1):
        off = cw - (kw - 1) + k
        acc = acc + cbuf[off:off + tq, :] * cw_ref[k:k + 1, :]
    xbc_scr[...] = _silu(acc)
    cbuf[cw - (kw - 1):cw, :] = cbuf[cw + tq - (kw - 1):cw + tq, :]

    dt = _softplus(dt_ref[...] + dtb_ref[...])
    dta = dt * (-jnp.exp(alog_ref[...]))
    lq = q.bit_length() - 1
    r_i = lax.broadcasted_iota(jnp.int32, (tq, tq), 0)
    c_i = lax.broadcasted_iota(jnp.int32, (tq, tq), 1)
    same = lax.shift_right_logical(r_i, lq) == lax.shift_right_logical(c_i, lq)
    tril = jnp.where(same, (r_i >= c_i).astype(F32), 0.0)
    cs = jnp.dot(tril, dta, preferred_element_type=F32, precision=HIGHEST)
    cs_end = jnp.dot(same.astype(F32), dta, preferred_element_type=F32, precision=HIGHEST)
    wv = dt * jnp.exp(cs_end - cs)

    both = jnp.concatenate([cs, wv], axis=0)
    hi = both.astype(BF16)
    r1 = both - hi.astype(F32)
    mid = r1.astype(BF16)
    lo = (r1 - mid.astype(F32)).astype(BF16)
    ex = jnp.dot(jnp.concatenate([hi, mid, lo], axis=1), e3_ref[...], preferred_element_type=F32)
    csx_scr[...] = ex[0:tq]
    wvx_scr[...] = ex[tq:2 * tq]
    cs_t = cs.T
    dt_t = dt.T

    lane = lax.broadcasted_iota(jnp.int32, (q, 2 * hd), 1)
    kpos = jnp.bitwise_and(lane, hd - 1)
    causal2 = jnp.logical_and(lax.broadcasted_iota(jnp.int32, (q, 2 * hd), 0) >= kpos, kpos < q)
    first = lane < hd
    zrow = jnp.zeros((1, hd - q), F32)
    zblk = jnp.zeros((hd - q, 2 * hd), BF16)

    def pair_row(t, p, r0):
        parts = []
        for h in (2 * p, 2 * p + 1):
            parts.append(t[h:h + 1, r0:r0 + q])
            if q < hd:
                parts.append(zrow)
        return jnp.concatenate(parts, axis=1)

    def pair_rows(a, b):
        blocks = [a, zblk, b, zblk] if q < hd else [a, b]
        return jnp.concatenate(blocks, axis=0)

    gw = di // SSD_GROUPS
    for c in range(tq // q):
        r0 = c * q
        for g in range(SSD_GROUPS):
            bgb = xbc_scr[r0:r0 + q, di + g * ns:di + (g + 1) * ns].astype(BF16)
            cgb = xbc_scr[r0:r0 + q, di + gn + g * ns:di + gn + (g + 1) * ns].astype(BF16)
            cb2 = lax.dot_general(cgb, pair_rows(bgb, bgb), (((1,), (1,)), ((), ())),
                                  preferred_element_type=F32)
            stg = st_scr[:, g * gw:(g + 1) * gw]
            ecs = jnp.exp(csx_scr[r0:r0 + q, g * gw:(g + 1) * gw])
            yo = jnp.dot(cgb, stg.astype(BF16), preferred_element_type=F32) * ecs
            for pp in range(hpg // 2):
                p = g * (hpg // 2) + pp
                lo_l = p * 2 * hd
                seg = csx_scr[r0:r0 + q, lo_l:lo_l + 2 * hd] - pair_row(cs_t, p, r0)
                lm = jnp.exp(jnp.where(causal2, seg, -jnp.inf))
                m2 = (cb2 * lm * pair_row(dt_t, p, r0)).astype(BF16)
                xpair = xbc_scr[r0:r0 + q, lo_l:lo_l + 2 * hd]
                rhs = pair_rows(jnp.where(first, xpair, 0.0).astype(BF16),
                                jnp.where(first, 0.0, xpair).astype(BF16))
                yd = jnp.dot(m2, rhs, preferred_element_type=F32)
                y_scr[r0:r0 + q, lo_l:lo_l + 2 * hd] = yd + yo[:, pp * 2 * hd:(pp + 1) * 2 * hd]
            xw = (xbc_scr[r0:r0 + q, g * gw:(g + 1) * gw]
                  * wvx_scr[r0:r0 + q, g * gw:(g + 1) * gw]).astype(BF16)
            upd = lax.dot_general(bgb, xw, (((0,), (0,)), ((), ())), preferred_element_type=F32)
            st_scr[:, g * gw:(g + 1) * gw] = stg * ecs[q - 1:q, :] + upd

    @pl.when(j == pl.num_programs(1) - 1)
    def _():
        so_ref[0] = st_scr[...].T
        co_ref[0] = cbuf[cw - (kw - 1):cw, :]

    y = y_scr[...] + xbc_scr[:, 0:di] * dx_ref[...]
    v = y * _silu(z_ref[...].astype(F32))
    gw = di // SSD_GROUPS
    parts = []
    for g in range(SSD_GROUPS):
        vg = v[:, g * gw:(g + 1) * gw]
        parts.append(vg * lax.rsqrt(jnp.mean(vg * vg, axis=-1, keepdims=True) + SSD_NORM_EPS))
    yn = (jnp.concatenate(parts, axis=-1) * ng_ref[...]).astype(BF16)
    a_br = jnp.dot(yn, wssd_ref[...], preferred_element_type=F32)

    pbuf[ph:ph + tq, :] = u_ref[...].astype(F32)
    pos = pos0 + j * tq + lax.broadcasted_iota(jnp.int32, (tq, 1), 0)
    pgd = dm // len(POOL_WINDOWS)
    b_parts = []
    for gi, w in enumerate(POOL_WINDOWS):
        lo = gi * pgd
        s = pbuf[ph:ph + tq, lo:lo + pgd]
        for i in range(1, w):
            s = s + pbuf[ph - i:ph - i + tq, lo:lo + pgd]
        cnt = jnp.minimum(w, pos + 1).astype(F32)
        pooled = s / cnt - pbuf[ph:ph + tq, lo:lo + pgd]
        b_parts.append(jnp.dot(pooled.astype(BF16), pw_ref[gi], preferred_element_type=F32))
    b_br = jnp.concatenate(b_parts, axis=-1) * ps_ref[...]
    pbuf[1:ph, :] = pbuf[tq + 1:tq + ph, :]

    @pl.when(j == pl.num_programs(1) - 1)
    def _():
        po_ref[0] = pbuf[1:ph, :]

    gates = jax.nn.sigmoid(gt_ref[...].astype(F32))
    merged = gates[:, 0:dm] * a_br + gates[:, dm:2 * dm] * b_br
    mix = jnp.dot(merged.astype(BF16), wout_ref[...], preferred_element_type=F32)
    xo_ref[...] = x_ref[...] + g1_ref[0] * mix


def _seq(proj, dtp, x, g1, conv_state, ssm_state, pool_state, wts, *, row0, nb, seqlen, pos0, cols):
    dm = x.shape[1]
    heads, hd, ns = ssm_state.shape[1], ssm_state.shape[2], ssm_state.shape[3]
    di = heads * hd
    gn = SSD_GROUPS * ns
    q = min(CHUNK, seqlen)
    tq = _row_tile(seqlen, (SEQ_TILE, 128, 64, 32))
    nj = seqlen // tq
    rb0 = row0 // tq
    oz, ox, obc, ou, og = cols

    def rows(b, j):
        return rb0 + b * nj + j

    full = lambda a: pl.BlockSpec(a.shape, lambda b, j: (0,) * a.ndim, pipeline_mode=pl.Buffered(1))
    kern = functools.partial(_seq_kernel, tq=tq, q=q, pos0=pos0, heads=heads, hd=hd, ns=ns, dm=dm)
    kw1 = conv_state.shape[1]
    pst = pool_state.shape[1]
    xo, so, co, po = pl.pallas_call(
        kern,
        grid=(nb, nj),
        in_specs=[
            pl.BlockSpec((tq, di), lambda b, j: (rows(b, j), oz // di)),
            pl.BlockSpec((tq, di), lambda b, j: (rows(b, j), ox // di)),
            pl.BlockSpec((tq, 2 * gn), lambda b, j: (rows(b, j), obc // (2 * gn))),
            pl.BlockSpec((tq, dm), lambda b, j: (rows(b, j), ou // dm)),
            pl.BlockSpec((tq, 2 * dm), lambda b, j: (rows(b, j), og // (2 * dm))),
            pl.BlockSpec((tq, LANES), lambda b, j: (rows(b, j), 0)),
            pl.BlockSpec((tq, dm), lambda b, j: (b * nj + j, 0)),
            pl.BlockSpec((1, 1, dm), lambda b, j: (b, 0, 0)),
            pl.BlockSpec((1, kw1, di + 2 * gn), lambda b, j: (b, 0, 0)),
            pl.BlockSpec((1, di, ns), lambda b, j: (b, 0, 0)),
            pl.BlockSpec((1, pst, dm), lambda b, j: (b, 0, 0)),
        ] + [full(w) for w in wts],
        out_specs=[pl.BlockSpec((tq, dm), lambda b, j: (b * nj + j, 0)),
                   pl.BlockSpec((1, di, ns), lambda b, j: (b, 0, 0)),
                   pl.BlockSpec((1, kw1, di + 2 * gn), lambda b, j: (b, 0, 0)),
                   pl.BlockSpec((1, pst, dm), lambda b, j: (b, 0, 0))],
        out_shape=[jax.ShapeDtypeStruct((nb * seqlen, dm), F32),
                   jax.ShapeDtypeStruct((nb, di, ns), F32),
                   jax.ShapeDtypeStruct((nb, kw1, di + 2 * gn), F32),
                   jax.ShapeDtypeStruct((nb, pst, dm), F32)],
        scratch_shapes=[pltpu.VMEM((tq + SUBLANES, di + 2 * gn), F32),
                        pltpu.VMEM((tq + 2 * SUBLANES, dm), F32),
                        pltpu.VMEM((ns, di), F32),
                        pltpu.VMEM((tq, di), F32),
                        pltpu.VMEM((tq, di + 2 * gn), F32),
                        pltpu.VMEM((tq, di), F32),
                        pltpu.VMEM((tq, di), F32)],
        compiler_params=pltpu.CompilerParams(
            dimension_semantics=("parallel", "arbitrary"), vmem_limit_bytes=VMEM_LIMIT),
        name="seq",
    )(proj, proj, proj, proj, proj, dtp, x, g1.reshape(nb, 1, dm),
      conv_state, ssm_state.reshape(nb, di, ns), pool_state, *wts)
    return xo, so.reshape(nb, heads, hd, ns), co, po


def _moe_pre_kernel(xp_ref, xs_ref, sc_ref, sh_ref, g_ref, wr_ref, rb_ref, swg_ref, swu_ref, swd_ref,
                    h_ref, s_ref, e_ref, w_ref, m_ref, cnt_ref, *, ntp):
    x = jnp.where(pl.program_id(0) < ntp, xp_ref[...], xs_ref[...])
    y = _rms(x, g_ref[...], RMS_EPS)
    h = _modulate(y, sc_ref, sh_ref)
    _write_rows(h_ref, h)
    hb = h.astype(BF16)

    sg = jnp.dot(hb, swg_ref[...], preferred_element_type=F32)
    su = jnp.dot(hb, swu_ref[...], preferred_element_type=F32)
    s_ref[...] = jnp.dot((_silu(sg) * su).astype(BF16), swd_ref[...], preferred_element_type=F32)

    scores = jax.nn.sigmoid(jnp.dot(hb, wr_ref[...], preferred_element_type=F32))
    biased = scores + rb_ref[...]
    tm, ne = scores.shape
    per_g = ne // N_EXPERT_GROUPS
    neg = -jnp.inf
    lane_i = lax.broadcasted_iota(jnp.int32, (tm, ne), 1)
    lane = lane_i.astype(F32)
    lgroup = lax.shift_right_logical(lane_i, per_g.bit_length() - 1).astype(F32)
    lane_o = lax.broadcasted_iota(jnp.int32, (tm, LANES), 1).astype(F32)

    def first_argmax(v, idx, big):
        top = jnp.max(v, axis=-1, keepdims=True)
        return top, jnp.min(jnp.where(v == top, idx, big), axis=-1, keepdims=True)

    gs = jnp.full((tm, LANES), neg, F32)
    for g in range(N_EXPERT_GROUPS):
        mg = jnp.where(lgroup == g, biased, neg)
        t1, i1 = first_argmax(mg, lane, float(ne))
        t2 = jnp.max(jnp.where(lane == i1, neg, mg), axis=-1, keepdims=True)
        gs = jnp.where(lane_o == g, t1 + t2, gs)
    allowed = jnp.zeros((tm, ne), F32)
    for _ in range(TOPK_GROUPS):
        _, gi = first_argmax(gs, lane_o, float(LANES))
        allowed = jnp.where(lgroup == gi, 1.0, allowed)
        gs = jnp.where(lane_o == gi, neg, gs)
    mb = jnp.where(allowed > 0.0, biased, neg)
    eacc = jnp.zeros((tm, LANES), F32)
    wacc = jnp.zeros((tm, LANES), F32)
    chosen = jnp.zeros((tm, ne), F32)
    for k in range(TOP_K):
        _, ik = first_argmax(mb, lane, float(ne))
        sel = lane == ik
        wk = jnp.sum(jnp.where(sel, scores, 0.0), axis=-1, keepdims=True)
        eacc = jnp.where(lane_o == k, ik, eacc)
        wacc = jnp.where(lane_o == k, wk, wacc)
        chosen = jnp.where(sel, 1.0, chosen)
        mb = jnp.where(sel, neg, mb)
    wsum = jnp.sum(wacc, axis=-1, keepdims=True)
    e_ref[...] = eacc.astype(jnp.int32)
    w_ref[...] = wacc / (wsum + 1e-20) * ROUTED_SCALE
    m_ref[...] = chosen.astype(BF16)

    @pl.when(pl.program_id(0) == 0)
    def _():
        cnt_ref[...] = jnp.zeros_like(cnt_ref)

    cnt_ref[0:1, :] += jnp.sum(chosen, axis=0, keepdims=True)


def _moe_pre(xp1, xs1, sc_rows, sh_rows, g, wr, rb, swg, swu, swd):
    tp, d = xp1.shape
    ts = xs1.shape[0]
    t = tp + ts
    ne = wr.shape[1]
    ff = swg.shape[1]
    tm = _row_tile(ts, (256,))
    assert tp % tm == 0 and tm % (SUBLANES * MOD_ROWS) == 0
    ntp = tp // tm
    mr = tm // MOD_ROWS
    nsub = d // LANES
    row = lambda i: (i, 0)
    const = lambda i: (0, 0)
    return pl.pallas_call(
        functools.partial(_moe_pre_kernel, ntp=ntp),
        grid=(t // tm,),
        in_specs=_two_part_specs(tm, d, ntp) + [
                  pl.BlockSpec((mr, d), row), pl.BlockSpec((mr, d), row),
                  pl.BlockSpec((1, d), const), pl.BlockSpec((d, ne), const), pl.BlockSpec((1, ne), const),
                  pl.BlockSpec((d, ff), const), pl.BlockSpec((d, ff), const), pl.BlockSpec((ff, d), const)],
        out_specs=[pl.BlockSpec((tm * nsub, LANES), row), pl.BlockSpec((tm, d), row),
                   pl.BlockSpec((tm, LANES), row), pl.BlockSpec((tm, LANES), row),
                   pl.BlockSpec((tm, ne), row), pl.BlockSpec((SUBLANES, ne), const)],
        out_shape=[jax.ShapeDtypeStruct((t * nsub, LANES), F32), jax.ShapeDtypeStruct((t, d), F32),
                   jax.ShapeDtypeStruct((t, LANES), jnp.int32), jax.ShapeDtypeStruct((t, LANES), F32),
                   jax.ShapeDtypeStruct((t, ne), BF16), jax.ShapeDtypeStruct((SUBLANES, ne), F32)],
        compiler_params=pltpu.CompilerParams(
            dimension_semantics=("arbitrary",), vmem_limit_bytes=VMEM_LIMIT),
        name="moe_pre",
    )(xp1, xs1, sc_rows, sh_rows, g.reshape(1, d), wr, rb.reshape(1, ne), swg, swu, swd)


def _read_rows(buf, nrows, nsub):
    return jnp.concatenate([buf[pl.ds(c, nrows, stride=nsub), :] for c in range(nsub)], axis=-1)


def _write_rows(ref, v):
    nrows, width = v.shape
    nsub = width // LANES
    for c in range(nsub):
        ref[pl.ds(c, nrows, stride=nsub), :] = v[:, c * LANES:(c + 1) * LANES]


def _rank_kernel(m_ref, e_ref, ps_ref, d_ref, carry):
    i = pl.program_id(0)
    ne = m_ref.shape[1]

    @pl.when(i == 0)
    def _():
        carry[...] = jnp.zeros_like(carry)

    m = m_ref[...]
    tr = m.shape[0]
    r_i = lax.broadcasted_iota(jnp.int32, (tr, tr), 0)
    c_i = lax.broadcasted_iota(jnp.int32, (tr, tr), 1)
    earlier = (r_i > c_i).astype(BF16)
    slot = jnp.dot(earlier, m, preferred_element_type=F32) + carry[0:1, :] + ps_ref[...]
    e = e_ref[...]
    lane = lax.broadcasted_iota(jnp.int32, (tr, ne), 1)
    lane_o = lax.broadcasted_iota(jnp.int32, (tr, LANES), 1)
    dacc = jnp.zeros((tr, LANES), F32)
    for k in range(TOP_K):
        dk = jnp.sum(jnp.where(lane == e[:, k:k + 1], slot, 0.0), axis=-1, keepdims=True)
        dacc = jnp.where(lane_o == k, dk, dacc)
    d_ref[...] = dacc.astype(jnp.int32)
    carry[0:1, :] += jnp.sum(m.astype(F32), axis=0, keepdims=True)


def _rank(mask, eidx, pstart):
    t, ne = mask.shape
    tr = _row_tile(t, (512, 256))
    row = lambda i: (i, 0)
    return pl.pallas_call(
        _rank_kernel,
        grid=(t // tr,),
        in_specs=[pl.BlockSpec((tr, ne), row), pl.BlockSpec((tr, LANES), row),
                  pl.BlockSpec((1, ne), lambda i: (0, 0))],
        out_specs=pl.BlockSpec((tr, LANES), row),
        out_shape=jax.ShapeDtypeStruct((t, LANES), jnp.int32),
        scratch_shapes=[pltpu.VMEM((SUBLANES, ne), F32)],
        compiler_params=pltpu.CompilerParams(
            dimension_semantics=("arbitrary",), vmem_limit_bytes=VMEM_LIMIT),
        name="rank",
    )(mask, eidx, pstart)


def _dispatch_kernel(ps_ref, pl_ref, nu_ref, d_ref, h_ref, xs_hbm, zbuf, sem, zsem, *, tt, nsub, bm, nb):
    i = pl.program_id(0)
    n = pl.num_programs(0)
    ne = ps_ref.shape[0]
    pieces = [1 << b for b in reversed(range(bm.bit_length() - 1))]

    def zero_fill(op):
        def pad(e, carry):
            start = ps_ref[e]
            length = pl_ref[e]
            for p in pieces:
                off = length - lax.rem(length, 2 * p)

                @pl.when(lax.rem(length, 2 * p) >= p)
                def _():
                    dst = xs_hbm.at[pl.ds(pl.multiple_of((start + off) * nsub, nsub), p * nsub)]
                    op(pltpu.make_async_copy(zbuf.at[pl.ds(0, p * nsub)], dst, zsem))
            return carry
        lax.fori_loop(0, ne, pad, 0)

        def tail(b, carry):
            dst = xs_hbm.at[pl.ds(pl.multiple_of(b * (bm * nsub), bm * nsub), bm * nsub)]
            op(pltpu.make_async_copy(zbuf, dst, zsem))
            return carry
        lax.fori_loop(nu_ref[0], nb, tail, 0)

    @pl.when(i == 0)
    def _():
        zbuf[...] = jnp.zeros_like(zbuf)
        zero_fill(lambda c: c.start())

    def body(r, carry):
        src = h_ref.at[pl.ds(pl.multiple_of(r * nsub, nsub), nsub)]
        for k in range(TOP_K):
            dst = xs_hbm.at[pl.ds(pl.multiple_of(d_ref[0, 0, r * TOP_K + k] * nsub, nsub), nsub)]
            pltpu.make_async_copy(src, dst, sem).start(priority=k % 2)
        return carry

    lax.fori_loop(0, tt, body, 0)

    for _ in range(TOP_K):
        pltpu.make_async_copy(h_ref, xs_hbm.at[pl.ds(0, tt * nsub)], sem).wait()

    @pl.when(i == n - 1)
    def _():
        zero_fill(lambda c: c.wait())


def _dispatch(dest, h2_tiles, pad_start, pad_len, n_used, nb, nsub):
    t = dest.shape[0]
    tt = _row_tile(t, (256,))
    nt = t // tt
    bm = EXPERT_ROWS
    d3 = dest.reshape(nt, 1, tt * TOP_K)
    gs = pltpu.PrefetchScalarGridSpec(
        num_scalar_prefetch=3,
        grid=(nt,),
        in_specs=[pl.BlockSpec((1, 1, tt * TOP_K), lambda i, *_: (i, 0, 0), memory_space=pltpu.SMEM),
                  pl.BlockSpec((tt * nsub, LANES), lambda i, *_: (i, 0))],
        out_specs=pl.BlockSpec(memory_space=pl.ANY),
        scratch_shapes=[pltpu.VMEM((bm * nsub, LANES), F32), pltpu.SemaphoreType.DMA,
                        pltpu.SemaphoreType.DMA],
    )
    return pl.pallas_call(
        functools.partial(_dispatch_kernel, tt=tt, nsub=nsub, bm=bm, nb=nb),
        grid_spec=gs,
        out_shape=jax.ShapeDtypeStruct((nb * bm * nsub, LANES), F32),
        compiler_params=pltpu.CompilerParams(dimension_semantics=("arbitrary",)),
        name="dispatch",
    )(pad_start, pad_len, n_used, d3, h2_tiles)


def _grouped_kernel(b0_ref, nk_ref, nu_ref, x_hbm, wg_ref, wu_ref, wd_ref, y_hbm,
                    xbuf, ybuf, xsem, ysem, wgb, wub, wdb, *, bm, nsub, nb):
    e = pl.program_id(0)
    rows = bm * nsub
    nu = nu_ref[0]

    def x_copy(gb, slot):
        src = x_hbm.at[pl.ds(pl.multiple_of(gb * rows, rows), rows)]
        return pltpu.make_async_copy(src, xbuf.at[slot], xsem.at[slot])

    def y_copy(gb, slot):
        dst = y_hbm.at[pl.ds(pl.multiple_of(gb * rows, rows), rows)]
        return pltpu.make_async_copy(ybuf.at[slot], dst, ysem.at[slot])

    @pl.when(e == 0)
    def _():
        x_copy(0, 0).start()

    @pl.when(nk_ref[e] > 0)
    def _():
        wgb[...] = wg_ref[0].astype(BF16)
        wub[...] = wu_ref[0].astype(BF16)
        wdb[...] = wd_ref[0].astype(BF16)

        def block(b, carry):
            gb = b0_ref[e] + b
            slot = lax.rem(gb, 2)

            @pl.when(gb + 1 < nu)
            def _():
                x_copy(gb + 1, 1 - slot).start()

            x_copy(gb, slot).wait()

            @pl.when(gb >= 2)
            def _():
                y_copy(gb - 2, slot).wait()

            x = _read_rows(xbuf.at[slot], bm, nsub).astype(BF16)
            hg = jnp.dot(x, wgb[...], preferred_element_type=F32)
            hu = jnp.dot(x, wub[...], preferred_element_type=F32)
            y = jnp.dot((_silu(hg) * hu).astype(BF16), wdb[...], preferred_element_type=F32)
            _write_rows(ybuf.at[slot], y)
            y_copy(gb, slot).start()
            return carry

        lax.fori_loop(0, nk_ref[e], block, 0)

    @pl.when(e == pl.num_programs(0) - 1)
    def _():
        @pl.when(nu >= 2)
        def _():
            y_copy(nu - 2, lax.rem(nu, 2)).wait()

        y_copy(nu - 1, lax.rem(nu - 1, 2)).wait()
        ybuf[0] = jnp.zeros(ybuf.shape[1:], ybuf.dtype)

        def tail(gb, carry):
            y_copy(gb, 0).start()
            return carry

        def tail_wait(gb, carry):
            y_copy(gb, 0).wait()
            return carry

        lax.fori_loop(nu, nb, tail, 0)
        lax.fori_loop(nu, nb, tail_wait, 0)


def _grouped(xs, blk_start, blk_count, n_used, wg, wu, wd):
    ne, d, ff = wg.shape
    nsub = d // LANES
    bm = EXPERT_ROWS
    nb = xs.shape[0] // (bm * nsub)
    wspec = lambda shape: pl.BlockSpec(shape, lambda e, *_: (e, 0, 0))
    gs = pltpu.PrefetchScalarGridSpec(
        num_scalar_prefetch=3,
        grid=(ne,),
        in_specs=[pl.BlockSpec(memory_space=pl.ANY),
                  wspec((1, d, ff)), wspec((1, d, ff)), wspec((1, ff, d))],
        out_specs=pl.BlockSpec(memory_space=pl.ANY),
        scratch_shapes=[pltpu.VMEM((2, bm * nsub, LANES), F32), pltpu.VMEM((2, bm * nsub, LANES), F32),
                        pltpu.SemaphoreType.DMA((2,)), pltpu.SemaphoreType.DMA((2,)),
                        pltpu.VMEM((d, ff), BF16), pltpu.VMEM((d, ff), BF16), pltpu.VMEM((ff, d), BF16)],
    )
    return pl.pallas_call(
        functools.partial(_grouped_kernel, bm=bm, nsub=nsub, nb=nb),
        grid_spec=gs,
        out_shape=jax.ShapeDtypeStruct(xs.shape, F32),
        compiler_params=pltpu.CompilerParams(
            dimension_semantics=("arbitrary",), vmem_limit_bytes=VMEM_LIMIT),
        name="grouped",
    )(blk_start, blk_count, n_used, xs, wg, wu, wd)


def _combine_kernel(d_ref, dn_ref, y_hbm, w_ref, s_ref, xp_ref, xs_ref, g2_ref, fg_ref, op_ref, os_ref,
                    buf, sem, *, tt, nsub, ntp):
    i = pl.program_id(0)
    n = pl.num_programs(0)

    def gather(dref, slot):
        def body(r, carry):
            for k in range(TOP_K):
                src = y_hbm.at[pl.ds(pl.multiple_of(dref[0, 0, r * TOP_K + k] * nsub, nsub), nsub)]
                pltpu.make_async_copy(src, buf.at[slot, k, pl.ds(pl.multiple_of(r * nsub, nsub), nsub)],
                                      sem.at[slot]).start(priority=k % 2)
            return carry
        lax.fori_loop(0, tt, body, 0)

    @pl.when(i == 0)
    def _():
        gather(d_ref, 0)

    @pl.when(i + 1 < n)
    def _():
        gather(dn_ref, (i + 1) % 2)

    slot = i % 2
    for k in range(TOP_K):
        pltpu.make_async_copy(y_hbm.at[pl.ds(0, tt * nsub)], buf.at[slot, k], sem.at[slot]).wait()

    w = w_ref[...]
    routed = w[:, 0:1] * _read_rows(buf.at[slot, 0], tt, nsub)
    for k in range(1, TOP_K):
        routed = routed + w[:, k:k + 1] * _read_rows(buf.at[slot, k], tt, nsub)
    moe = routed + s_ref[...]
    d = moe.shape[1]
    m3 = moe.reshape(tt // MOD_ROWS, MOD_ROWS, d) * g2_ref[...][:, None, :]
    x1 = jnp.where(i < ntp, xp_ref[...], xs_ref[...])
    out = _rms(x1 + m3.reshape(tt, d), fg_ref[...], RMS_EPS)

    @pl.when(i < ntp)
    def _():
        op_ref[...] = out

    @pl.when(i >= ntp)
    def _():
        os_ref[...] = out


def _combine(dest, y_tiles, wts, shared, xp1, xs1, g2_rows, final_g):
    tp, d = xp1.shape
    ts = xs1.shape[0]
    nsub = d // LANES
    tt = _row_tile(ts, (256,))
    assert tp % tt == 0 and tt % (SUBLANES * MOD_ROWS) == 0
    ntp = tp // tt
    mr = tt // MOD_ROWS
    nt = (tp + ts) // tt
    d3 = dest.reshape(nt, 1, tt * TOP_K)
    kern = functools.partial(_combine_kernel, tt=tt, nsub=nsub, ntp=ntp)
    smem_blk = lambda f: pl.BlockSpec((1, 1, tt * TOP_K), f, memory_space=pltpu.SMEM)
    row = lambda i: (i, 0)
    two = _two_part_specs(tt, d, ntp)
    return pl.pallas_call(
        kern,
        grid=(nt,),
        in_specs=[smem_blk(lambda i: (i, 0, 0)),
                  smem_blk(lambda i: (jnp.minimum(i + 1, nt - 1), 0, 0)),
                  pl.BlockSpec(memory_space=pl.ANY),
                  pl.BlockSpec((tt, LANES), row), pl.BlockSpec((tt, d), row)] + two + [
                  pl.BlockSpec((mr, d), row), pl.BlockSpec((1, d), lambda i: (0, 0))],
        out_specs=two,
        out_shape=[jax.ShapeDtypeStruct((tp, d), F32), jax.ShapeDtypeStruct((ts, d), F32)],
        scratch_shapes=[pltpu.VMEM((2, TOP_K, tt * nsub, LANES), F32), pltpu.SemaphoreType.DMA((2,))],
        compiler_params=pltpu.CompilerParams(
            dimension_semantics=("arbitrary",), vmem_limit_bytes=VMEM_LIMIT),
        name="combine",
    )(d3, d3, y_tiles, wts, shared, xp1, xs1, g2_rows, final_g.reshape(1, d))


def _plan(counts, n_assign):
    ne = counts.shape[0]
    bm = EXPERT_ROWS
    counts = counts.astype(jnp.int32)
    nblk = (counts + bm - 1) // bm
    bend = jnp.cumsum(nblk)
    pstart = ((bend - nblk) * bm).astype(F32).reshape(1, ne)
    nb = -(-(n_assign + ne * (bm - 1)) // bm)
    n_used = bend[-1:].astype(jnp.int32)
    pad_start = ((bend - nblk) * bm + counts).astype(jnp.int32)
    pad_len = (nblk * bm - counts).astype(jnp.int32)
    return pstart, (bend - nblk).astype(jnp.int32), nblk, n_used, pad_start, pad_len, nb


def per_g_pow2(ne):
    per_g = ne // N_EXPERT_GROUPS
    return per_g * N_EXPERT_GROUPS == ne and per_g & (per_g - 1) == 0


def _mod_rows(m, nbp, lp):
    return jnp.concatenate([jnp.repeat(m[:nbp], lp // MOD_ROWS, axis=0), m[nbp:]], axis=0)


def kernel(x_prompt, x_sample, state_ssm, state_conv, state_pool, c_prompt, c_sample, ln1_g, ln2_g, w_ada, b_ada, w_in, conv_w, conv_b, dt_bias, a_log, d_skip, ssd_norm_g, w_ssd_out, pool_w, pool_scale, w_out, w_router, router_bias, moe_w_gate, moe_w_up, moe_w_down, shared_w_gate, shared_w_up, shared_w_down, final_g):
    bp, lp, dm = x_prompt.shape
    bs, ls, _ = x_sample.shape
    depth = ln1_g.shape[0]
    heads, hd, ns = state_ssm.shape[2], state_ssm.shape[3], state_ssm.shape[4]
    di = heads * hd
    gn = SSD_GROUPS * ns
    cch = di + 2 * gn
    assert depth == 1 and ls == MOD_ROWS and lp % MOD_ROWS == 0 and heads <= LANES
    assert per_g_pow2(w_router.shape[2])
    assert ns == 2 * hd and hd & (hd - 1) == 0 and (heads // SSD_GROUPS) % 2 == 0
    tp, ts = bp * lp, bs * ls
    expand3 = (jnp.arange(3 * LANES)[:, None] % LANES == jnp.arange(di)[None, :] // hd).astype(BF16)

    xp, xs = x_prompt.reshape(tp, dm), x_sample.reshape(ts, dm)
    c_all = jnp.concatenate([c_prompt, c_sample], axis=0)

    o1, o2, o3, o4 = di, di + cch, di + cch + heads, di + cch + heads + dm
    cols = (0, di, 2 * di, di + cch, di + cch + dm)

    ssm_p, conv_p, pool_p, ssm_s, conv_s, pool_s = [], [], [], [], [], []
    for l in range(depth):
        wi = w_in[l]
        wcat = jnp.concatenate(
            [wi[:, :o1], wi[:, o1:o2], wi[:, o3:o4], wi[:, o4:],
             jnp.pad(wi[:, o2:o3], ((0, 0), (0, LANES - heads)))], axis=1).astype(BF16)
        pad_h = lambda v: jnp.pad(v.reshape(1, heads), ((0, 0), (0, LANES - heads)))
        seq_w = (conv_w[l], conv_b[l].reshape(1, cch), pad_h(dt_bias[l]), pad_h(a_log[l]),
                 jnp.repeat(d_skip[l], hd).reshape(1, di), ssd_norm_g[l].reshape(1, di),
                 w_ssd_out[l].astype(BF16), pool_w[l].astype(BF16), pool_scale[l].reshape(1, dm),
                 w_out[l].astype(BF16), expand3)

        mod = _ada(c_all, w_ada[l], b_ada[l])
        sh1, sc1, g1, sh2, sc2, g2 = jnp.split(mod, 6, axis=-1)

        proj, dtp = _inproj(xp, xs, _mod_rows(sc1, bp, lp), _mod_rows(sh1, bp, lp), ln1_g[l], wcat)

        zc = jnp.zeros((bp,) + state_conv.shape[2:], F32)
        zs = jnp.zeros((bp, heads, hd, ns), F32)
        zp = jnp.zeros((bp,) + state_pool.shape[2:], F32)
        xp1, ns_p, nc_p, np_p = _seq(proj, dtp, xp, g1[:bp], zc, zs, zp, seq_w,
                                     row0=0, nb=bp, seqlen=lp, pos0=0, cols=cols)
        xs1, ns_s, nc_s, np_s = _seq(proj, dtp, xs, g1[bp:], state_conv[l], state_ssm[l], state_pool[l],
                                     seq_w, row0=tp, nb=bs, seqlen=ls, pos0=PAST_LEN, cols=cols)
        conv_p.append(nc_p)
        conv_s.append(nc_s)
        pool_p.append(np_p)
        pool_s.append(np_s)
        ssm_p.append(ns_p)
        ssm_s.append(ns_s)

        h2_tiles, shared, eidx_l, wts, mask, counts = _moe_pre(
            xp1, xs1, _mod_rows(sc2, bp, lp), _mod_rows(sh2, bp, lp), ln2_g[l],
            w_router[l].astype(BF16), router_bias[l],
            shared_w_gate[l].astype(BF16), shared_w_up[l].astype(BF16), shared_w_down[l].astype(BF16))
        nsub = dm // LANES
        pstart, blk_start, blk_count, n_used, pad_start, pad_len, nb = _plan(counts[0], (tp + ts) * TOP_K)
        dest = _rank(mask, eidx_l, pstart)[:, :TOP_K]
        x_sorted = _dispatch(dest, h2_tiles, pad_start, pad_len, n_used, nb, nsub)
        y_tiles = _grouped(x_sorted, blk_start, blk_count, n_used,
                           moe_w_gate[l], moe_w_up[l], moe_w_down[l])
        xp, xs = _combine(dest, y_tiles, wts, shared, xp1, xs1, _mod_rows(g2, bp, lp), final_g)

    y_prompt = xp.reshape(bp, lp, dm)
    y_sample = xs.reshape(bs, ls, dm)
    return (y_prompt, y_sample, jnp.stack(ssm_p), jnp.stack(conv_p), jnp.stack(pool_p),
            jnp.stack(ssm_s), jnp.stack(conv_s), jnp.stack(pool_s))
```

```python
import functools

import jax
import jax.numpy as jnp
from jax import lax
from jax.experimental import pallas as pl
from jax.experimental.pallas import tpu as pltpu

F32 = jnp.float32
BF16 = jnp.bfloat16
HIGHEST = lax.Precision.HIGHEST

RMS_EPS = 1e-6
SSD_NORM_EPS = 1e-5
CHUNK = 64
SSD_GROUPS = 4
POOL_WINDOWS = (2, 4, 8, 16)
PAST_LEN = 1024
TOP_K = 8
N_EXPERT_GROUPS = 8
TOPK_GROUPS = 4
ROUTED_SCALE = 2.5

LANES = 128
SUBLANES = 8
MOD_ROWS = 32
VMEM_LIMIT = 56 * 1024 * 1024
EXPERT_ROWS = 128
SEQ_TILE = 256
GROUP_RING = 4


def _silu(x):
    return x * jax.nn.sigmoid(x)


def _softplus(x):
    return jnp.maximum(x, 0.0) + jnp.log1p(jnp.exp(-jnp.abs(x)))


def _row_tile(n, prefs):
    for t in prefs:
        if n % t == 0:
            return t
    return n


def _modulate(y, sc_ref, sh_ref):
    rows, d = y.shape
    y3 = y.reshape(rows // MOD_ROWS, MOD_ROWS, d)
    y3 = y3 * (1.0 + sc_ref[...][:, None, :]) + sh_ref[...][:, None, :]
    return y3.reshape(rows, d)


def _rms(x, g, eps):
    return x * lax.rsqrt(jnp.mean(x * x, axis=-1, keepdims=True) + eps) * g


def _ada_kernel(c_ref, w_ref, b_ref, o_ref):
    s = _silu(c_ref[...])
    o_ref[...] = jnp.dot(s, w_ref[...], preferred_element_type=F32, precision=HIGHEST) + b_ref[...]


def _ada(c_all, w_ada, b_ada):
    n, d = c_all.shape
    dout = w_ada.shape[1]
    tn = _row_tile(dout, (1024, 512, 256, 128))
    return pl.pallas_call(
        _ada_kernel,
        grid=(dout // tn,),
        in_specs=[pl.BlockSpec((n, d), lambda j: (0, 0)),
                  pl.BlockSpec((d, tn), lambda j: (0, j)),
                  pl.BlockSpec((1, tn), lambda j: (0, j))],
        out_specs=pl.BlockSpec((n, tn), lambda j: (0, j)),
        out_shape=jax.ShapeDtypeStruct((n, dout), F32),
        name="ada",
    )(c_all, w_ada, b_ada.reshape(1, dout))


def _two_part_specs(tm, d, ntp):
    return [pl.BlockSpec((tm, d), lambda i, *_: (jnp.minimum(i, ntp - 1), 0)),
            pl.BlockSpec((tm, d), lambda i, *_: (jnp.maximum(i - ntp, 0), 0))]


def _inproj_kernel(xp_ref, xs_ref, sc_ref, sh_ref, g_ref, w_ref, o_ref, dt_ref, *, ntp, tn):
    x = jnp.where(pl.program_id(0) < ntp, xp_ref[...], xs_ref[...])
    h = _modulate(_rms(x, g_ref[...], RMS_EPS), sc_ref, sh_ref).astype(BF16)
    n_main = o_ref.shape[1]
    for c0 in range(0, n_main, tn):
        o_ref[:, c0:c0 + tn] = jnp.dot(h, w_ref[:, c0:c0 + tn], preferred_element_type=F32).astype(BF16)
    dt_ref[...] = jnp.dot(h, w_ref[:, n_main:], preferred_element_type=F32)


def _inproj(xp, xs, sc_rows, sh_rows, g, wcat):
    tp, d = xp.shape
    ts = xs.shape[0]
    n = wcat.shape[1]
    n_main = n - LANES
    tm = _row_tile(ts, (512, 256))
    assert tp % tm == 0 and tm % (SUBLANES * MOD_ROWS) == 0
    ntp = tp // tm
    nt = ntp + ts // tm
    mr = tm // MOD_ROWS
    tn = _row_tile(n_main, (2048, 1024, 512, 256, 128))
    row = lambda i: (i, 0)
    return pl.pallas_call(
        functools.partial(_inproj_kernel, ntp=ntp, tn=tn),
        grid=(nt,),
        in_specs=_two_part_specs(tm, d, ntp) + [
            pl.BlockSpec((mr, d), row), pl.BlockSpec((mr, d), row),
            pl.BlockSpec((1, d), lambda i: (0, 0)),
            pl.BlockSpec((d, n), lambda i: (0, 0), pipeline_mode=pl.Buffered(1))],
        out_specs=[pl.BlockSpec((tm, n_main), row), pl.BlockSpec((tm, LANES), row)],
        out_shape=[jax.ShapeDtypeStruct((tp + ts, n_main), BF16),
                   jax.ShapeDtypeStruct((tp + ts, LANES), F32)],
        compiler_params=pltpu.CompilerParams(
            dimension_semantics=("parallel",), vmem_limit_bytes=VMEM_LIMIT),
        name="inproj",
    )(xp, xs, sc_rows, sh_rows, g.reshape(1, d), wcat)


def _seq_kernel(z_ref, xp_ref, bc_ref, u_ref, gt_ref, dt_ref, x_ref, g1_ref,
                cst_ref, sst_ref, pst_ref, cw_ref, cb_ref, dtb_ref, alog_ref, dx_ref, ng_ref,
                wssd_ref, pw_ref, ps_ref, wout_ref, e3_ref,
                xo_ref, so_ref, co_ref, po_ref,
                cbuf, pbuf, st_scr, y_scr, xbc_scr, csx_scr, wvx_scr,
                *, tq, q, pos0, heads, hd, ns, dm):
    j = pl.program_id(1)
    di = heads * hd
    gn = SSD_GROUPS * ns
    hpg = heads // SSD_GROUPS
    cw = cbuf.shape[0] - tq
    kw = cw_ref.shape[0]
    ph = pbuf.shape[0] - tq

    @pl.when(j == 0)
    def _():
        cbuf[cw - (kw - 1):cw, :] = cst_ref[0]
        pbuf[1:ph, :] = pst_ref[0]
        st_scr[...] = sst_ref[0].T

    cbuf[cw:cw + tq, 0:di] = xp_ref[...].astype(F32)
    cbuf[cw:cw + tq, di:di + 2 * gn] = bc_ref[...].astype(F32)
    acc = cb_ref[...] + cbuf[cw:cw + tq, :] * cw_ref[kw - 1:kw, :]
    for k in range(kw - 1):
        off = cw - (kw - 1) + k
        acc = acc + cbuf[off:off + tq, :] * cw_ref[k:k + 1, :]
    xbc_scr[...] = _silu(acc)
    cbuf[cw - (kw - 1):cw, :] = cbuf[cw + tq - (kw - 1):cw + tq, :]

    dt = _softplus(dt_ref[...] + dtb_ref[...])
    dta = dt * (-jnp.exp(alog_ref[...]))
    lq = q.bit_length() - 1
    r_i = lax.broadcasted_iota(jnp.int32, (tq, tq), 0)
    c_i = lax.broadcasted_iota(jnp.int32, (tq, tq), 1)
    same = lax.shift_right_logical(r_i, lq) == lax.shift_right_logical(c_i, lq)
    tril = jnp.where(same, (r_i >= c_i).astype(F32), 0.0)
    cs = jnp.dot(tril, dta, preferred_element_type=F32, precision=HIGHEST)
    cs_end = jnp.dot(same.astype(F32), dta, preferred_element_type=F32, precision=HIGHEST)
    wv = dt * jnp.exp(cs_end - cs)

    both = jnp.concatenate([cs, wv], axis=0)
    hi = both.astype(BF16)
    r1 = both - hi.astype(F32)
    mid = r1.astype(BF16)
    lo = (r1 - mid.astype(F32)).astype(BF16)
    ex = jnp.dot(jnp.concatenate([hi, mid, lo], axis=1), e3_ref[...], preferred_element_type=F32)
    csx_scr[...] = ex[0:tq]
    wvx_scr[...] = ex[tq:2 * tq]
    cs_t = cs.T
    dt_t = dt.T

    lane = lax.broadcasted_iota(jnp.int32, (q, 2 * hd), 1)
    kpos = jnp.bitwise_and(lane, hd - 1)
    causal2 = jnp.logical_and(lax.broadcasted_iota(jnp.int32, (q, 2 * hd), 0) >= kpos, kpos < q)
    first = lane < hd
    zrow = jnp.zeros((1, hd - q), F32)
    zblk = jnp.zeros((hd - q, 2 * hd), BF16)

    def pair_row(t, p, r0):
        parts = []
        for h in (2 * p, 2 * p + 1):
            parts.append(t[h:h + 1, r0:r0 + q])
            if q < hd:
                parts.append(zrow)
        return jnp.concatenate(parts, axis=1)

    def pair_rows(a, b):
        blocks = [a, zblk, b, zblk] if q < hd else [a, b]
        return jnp.concatenate(blocks, axis=0)

    gw = di // SSD_GROUPS
    for c in range(tq // q):
        r0 = c * q
        for g in range(SSD_GROUPS):
            bgb = xbc_scr[r0:r0 + q, di + g * ns:di + (g + 1) * ns].astype(BF16)
            cgb = xbc_scr[r0:r0 + q, di + gn + g * ns:di + gn + (g + 1) * ns].astype(BF16)
            cb2 = lax.dot_general(cgb, pair_rows(bgb, bgb), (((1,), (1,)), ((), ())),
                                  preferred_element_type=F32)
            stg = st_scr[:, g * gw:(g + 1) * gw]
            ecs = jnp.exp(csx_scr[r0:r0 + q, g * gw:(g + 1) * gw])
            yo = jnp.dot(cgb, stg.astype(BF16), preferred_element_type=F32) * ecs
            for pp in range(hpg // 2):
                p = g * (hpg // 2) + pp
                lo_l = p * 2 * hd
                seg = csx_scr[r0:r0 + q, lo_l:lo_l + 2 * hd] - pair_row(cs_t, p, r0)
                lm = jnp.exp(jnp.where(causal2, seg, -jnp.inf))
                m2 = (cb2 * lm * pair_row(dt_t, p, r0)).astype(BF16)
                xpair = xbc_scr[r0:r0 + q, lo_l:lo_l + 2 * hd]
                rhs = pair_rows(jnp.where(first, xpair, 0.0).astype(BF16),
                                jnp.where(first, 0.0, xpair).astype(BF16))
                yd = jnp.dot(m2, rhs, preferred_element_type=F32)
                y_scr[r0:r0 + q, lo_l:lo_l + 2 * hd] = yd + yo[:, pp * 2 * hd:(pp + 1) * 2 * hd]
            xw = (xbc_scr[r0:r0 + q, g * gw:(g + 1) * gw]
                  * wvx_scr[r0:r0 + q, g * gw:(g + 1) * gw]).astype(BF16)
            upd = lax.dot_general(bgb, xw, (((0,), (0,)), ((), ())), preferred_element_type=F32)
            st_scr[:, g * gw:(g + 1) * gw] = stg * ecs[q - 1:q, :] + upd

    @pl.when(j == pl.num_programs(1) - 1)
    def _():
        so_ref[0] = st_scr[...].T
        co_ref[0] = cbuf[cw - (kw - 1):cw, :]

    y = y_scr[...] + xbc_scr[:, 0:di] * dx_ref[...]
    v = y * _silu(z_ref[...].astype(F32))
    gw = di // SSD_GROUPS
    parts = []
    for g in range(SSD_GROUPS):
        vg = v[:, g * gw:(g + 1) * gw]
        parts.append(vg * lax.rsqrt(jnp.mean(vg * vg, axis=-1, keepdims=True) + SSD_NORM_EPS))
    yn = (jnp.concatenate(parts, axis=-1) * ng_ref[...]).astype(BF16)
    a_br = jnp.dot(yn, wssd_ref[...], preferred_element_type=F32)

    pbuf[ph:ph + tq, :] = u_ref[...].astype(F32)
    pos = pos0 + j * tq + lax.broadcasted_iota(jnp.int32, (tq, 1), 0)
    pgd = dm // len(POOL_WINDOWS)
    b_parts = []
    for gi, w in enumerate(POOL_WINDOWS):
        lo = gi * pgd
        s = pbuf[ph:ph + tq, lo:lo + pgd]
        for i in range(1, w):
            s = s + pbuf[ph - i:ph - i + tq, lo:lo + pgd]
        cnt = jnp.minimum(w, pos + 1).astype(F32)
        pooled = s / cnt - pbuf[ph:ph + tq, lo:lo + pgd]
        b_parts.append(jnp.dot(pooled.astype(BF16), pw_ref[gi], preferred_element_type=F32))
    b_br = jnp.concatenate(b_parts, axis=-1) * ps_ref[...]
    pbuf[1:ph, :] = pbuf[tq + 1:tq + ph, :]

    @pl.when(j == pl.num_programs(1) - 1)
    def _():
        po_ref[0] = pbuf[1:ph, :]

    gates = jax.nn.sigmoid(gt_ref[...].astype(F32))
    merged = gates[:, 0:dm] * a_br + gates[:, dm:2 * dm] * b_br
    mix = jnp.dot(merged.astype(BF16), wout_ref[...], preferred_element_type=F32)
    xo_ref[...] = x_ref[...] + g1_ref[0] * mix


def _seq(proj, dtp, x, g1, conv_state, ssm_state, pool_state, wts, *, row0, nb, seqlen, pos0, cols):
    dm = x.shape[1]
    heads, hd, ns = ssm_state.shape[1], ssm_state.shape[2], ssm_state.shape[3]
    di = heads * hd
    gn = SSD_GROUPS * ns
    q = min(CHUNK, seqlen)
    tq = _row_tile(seqlen, (SEQ_TILE, 128, 64, 32))
    nj = seqlen // tq
    rb0 = row0 // tq
    oz, ox, obc, ou, og = cols

    def rows(b, j):
        return rb0 + b * nj + j

    full = lambda a: pl.BlockSpec(a.shape, lambda b, j: (0,) * a.ndim, pipeline_mode=pl.Buffered(1))
    kern = functools.partial(_seq_kernel, tq=tq, q=q, pos0=pos0, heads=heads, hd=hd, ns=ns, dm=dm)
    kw1 = conv_state.shape[1]
    pst = pool_state.shape[1]
    xo, so, co, po = pl.pallas_call(
        kern,
        grid=(nb, nj),
        in_specs=[
            pl.BlockSpec((tq, di), lambda b, j: (rows(b, j), oz // di)),
            pl.BlockSpec((tq, di), lambda b, j: (rows(b, j), ox // di)),
            pl.BlockSpec((tq, 2 * gn), lambda b, j: (rows(b, j), obc // (2 * gn))),
            pl.BlockSpec((tq, dm), lambda b, j: (rows(b, j), ou // dm)),
            pl.BlockSpec((tq, 2 * dm), lambda b, j: (rows(b, j), og // (2 * dm))),
            pl.BlockSpec((tq, LANES), lambda b, j: (rows(b, j), 0)),
            pl.BlockSpec((tq, dm), lambda b, j: (b * nj + j, 0)),
            pl.BlockSpec((1, 1, dm), lambda b, j: (b, 0, 0)),
            pl.BlockSpec((1, kw1, di + 2 * gn), lambda b, j: (b, 0, 0)),
            pl.BlockSpec((1, di, ns), lambda b, j: (b, 0, 0)),
            pl.BlockSpec((1, pst, dm), lambda b, j: (b, 0, 0)),
        ] + [full(w) for w in wts],
        out_specs=[pl.BlockSpec((tq, dm), lambda b, j: (b * nj + j, 0)),
                   pl.BlockSpec((1, di, ns), lambda b, j: (b, 0, 0)),
                   pl.BlockSpec((1, kw1, di + 2 * gn), lambda b, j: (b, 0, 0)),
                   pl.BlockSpec((1, pst, dm), lambda b, j: (b, 0, 0))],
        out_shape=[jax.ShapeDtypeStruct((nb * seqlen, dm), F32),
                   jax.ShapeDtypeStruct((nb, di, ns), F32),
                   jax.ShapeDtypeStruct((nb, kw1, di + 2 * gn), F32),
                   jax.ShapeDtypeStruct((nb, pst, dm), F32)],
        scratch_shapes=[pltpu.VMEM((tq + SUBLANES, di + 2 * gn), F32),
                        pltpu.VMEM((tq + 2 * SUBLANES, dm), F32),
                        pltpu.VMEM((ns, di), F32),
                        pltpu.VMEM((tq, di), F32),
                        pltpu.VMEM((tq, di + 2 * gn), F32),
                        pltpu.VMEM((tq, di), F32),
                        pltpu.VMEM((tq, di), F32)],
        compiler_params=pltpu.CompilerParams(
            dimension_semantics=("parallel", "arbitrary"), vmem_limit_bytes=VMEM_LIMIT),
        name="seq",
    )(proj, proj, proj, proj, proj, dtp, x, g1.reshape(nb, 1, dm),
      conv_state, ssm_state.reshape(nb, di, ns), pool_state, *wts)
    return xo, so.reshape(nb, heads, hd, ns), co, po


def _moe_pre_kernel(xp_ref, xs_ref, sc_ref, sh_ref, g_ref, wr_ref, rb_ref, swg_ref, swu_ref, swd_ref,
                    h_ref, s_ref, e_ref, w_ref, m_ref, cnt_ref, *, ntp):
    x = jnp.where(pl.program_id(0) < ntp, xp_ref[...], xs_ref[...])
    y = _rms(x, g_ref[...], RMS_EPS)
    h = _modulate(y, sc_ref, sh_ref)
    _write_rows(h_ref, h)
    hb = h.astype(BF16)

    sg = jnp.dot(hb, swg_ref[...], preferred_element_type=F32)
    su = jnp.dot(hb, swu_ref[...], preferred_element_type=F32)
    s_ref[...] = jnp.dot((_silu(sg) * su).astype(BF16), swd_ref[...], preferred_element_type=F32)

    scores = jax.nn.sigmoid(lax.dot_general(wr_ref[...], hb, (((1,), (1,)), ((), ())),
                                            preferred_element_type=F32))
    biased = scores + rb_ref[:, 0:1]
    ne, tm = scores.shape
    per_g = ne // N_EXPERT_GROUPS
    neg = -jnp.inf
    row_e = lax.broadcasted_iota(jnp.int32, (ne, tm), 0).astype(F32)
    row_g = lax.broadcasted_iota(jnp.int32, (N_EXPERT_GROUPS, tm), 0).astype(F32)

    def first_argmax(v, idx, big):
        top = jnp.max(v, axis=0, keepdims=True)
        return top, jnp.min(jnp.where(v == top, idx, big), axis=0, keepdims=True)

    gs = jnp.full((N_EXPERT_GROUPS, tm), neg, F32)
    for g in range(N_EXPERT_GROUPS):
        mg = biased[g * per_g:(g + 1) * per_g, :]
        rg = (lax.broadcasted_iota(jnp.int32, (per_g, tm), 0) + g * per_g).astype(F32)
        t1, i1 = first_argmax(mg, rg, float(ne))
        t2 = jnp.max(jnp.where(rg == i1, neg, mg), axis=0, keepdims=True)
        gs = jnp.where(row_g == g, t1 + t2, gs)
    grp_e = lax.shift_right_logical(lax.broadcasted_iota(jnp.int32, (ne, tm), 0),
                                    per_g.bit_length() - 1).astype(F32)
    allowed = jnp.zeros((ne, tm), F32)
    for _ in range(TOPK_GROUPS):
        _, gi = first_argmax(gs, row_g, float(N_EXPERT_GROUPS))
        allowed = jnp.where(grp_e == gi, 1.0, allowed)
        gs = jnp.where(row_g == gi, neg, gs)
    mb = jnp.where(allowed > 0.0, biased, neg)
    row_k = lax.broadcasted_iota(jnp.int32, (TOP_K, tm), 0)
    eacc = jnp.zeros((TOP_K, tm), F32)
    wacc = jnp.zeros((TOP_K, tm), F32)
    chosen = jnp.zeros((ne, tm), F32)
    for k in range(TOP_K):
        _, ik = first_argmax(mb, row_e, float(ne))
        sel = row_e == ik
        wk = jnp.sum(jnp.where(sel, scores, 0.0), axis=0, keepdims=True)
        eacc = jnp.where(row_k == k, ik, eacc)
        wacc = jnp.where(row_k == k, wk, wacc)
        chosen = jnp.where(sel, 1.0, chosen)
        mb = jnp.where(sel, neg, mb)
    wsum = jnp.sum(wacc, axis=0, keepdims=True)
    e_ref[...] = eacc.astype(jnp.int32)
    w_ref[...] = wacc / (wsum + 1e-20) * ROUTED_SCALE
    chosen_b = chosen.astype(BF16)
    m_ref[...] = chosen_b

    @pl.when(pl.program_id(0) == 0)
    def _():
        cnt_ref[...] = jnp.zeros_like(cnt_ref)

    cnt_ref[...] += jnp.dot(chosen_b, jnp.ones((tm, LANES), BF16), preferred_element_type=F32)


def _moe_pre(xp1, xs1, sc_rows, sh_rows, g, wr, rb, swg, swu, swd):
    tp, d = xp1.shape
    ts = xs1.shape[0]
    t = tp + ts
    ne = wr.shape[1]
    ff = swg.shape[1]
    tm = _row_tile(ts, (256,))
    assert tp % tm == 0 and tm % (SUBLANES * MOD_ROWS) == 0
    ntp = tp // tm
    mr = tm // MOD_ROWS
    nsub = d // LANES
    row = lambda i: (i, 0)
    col = lambda i: (0, i)
    const = lambda i: (0, 0)
    rb_col = jnp.broadcast_to(rb.reshape(ne, 1), (ne, LANES))
    return pl.pallas_call(
        functools.partial(_moe_pre_kernel, ntp=ntp),
        grid=(t // tm,),
        in_specs=_two_part_specs(tm, d, ntp) + [
                  pl.BlockSpec((mr, d), row), pl.BlockSpec((mr, d), row),
                  pl.BlockSpec((1, d), const), pl.BlockSpec((ne, d), const), pl.BlockSpec((ne, LANES), const),
                  pl.BlockSpec((d, ff), const), pl.BlockSpec((d, ff), const), pl.BlockSpec((ff, d), const)],
        out_specs=[pl.BlockSpec((tm * nsub, LANES), row), pl.BlockSpec((tm, d), row),
                   pl.BlockSpec((TOP_K, tm), col), pl.BlockSpec((TOP_K, tm), col),
                   pl.BlockSpec((ne, tm), col), pl.BlockSpec((ne, LANES), const)],
        out_shape=[jax.ShapeDtypeStruct((t * nsub, LANES), F32), jax.ShapeDtypeStruct((t, d), F32),
                   jax.ShapeDtypeStruct((TOP_K, t), jnp.int32), jax.ShapeDtypeStruct((TOP_K, t), F32),
                   jax.ShapeDtypeStruct((ne, t), BF16), jax.ShapeDtypeStruct((ne, LANES), F32)],
        compiler_params=pltpu.CompilerParams(
            dimension_semantics=("arbitrary",), vmem_limit_bytes=VMEM_LIMIT),
        name="moe_pre",
    )(xp1, xs1, sc_rows, sh_rows, g.reshape(1, d), wr.T, rb_col, swg, swu, swd)


def _read_rows(buf, nrows, nsub):
    return jnp.concatenate([buf[pl.ds(c, nrows, stride=nsub), :] for c in range(nsub)], axis=-1)


def _write_rows(ref, v):
    nrows, width = v.shape
    nsub = width // LANES
    for c in range(nsub):
        ref[pl.ds(c, nrows, stride=nsub), :] = v[:, c * LANES:(c + 1) * LANES]


def _rank_kernel(m_ref, e_ref, ps_ref, d_ref, carry):
    i = pl.program_id(0)

    @pl.when(i == 0)
    def _():
        carry[...] = jnp.zeros_like(carry)

    m = m_ref[...]
    ne, tr = m.shape
    r_i = lax.broadcasted_iota(jnp.int32, (tr, tr), 0)
    c_i = lax.broadcasted_iota(jnp.int32, (tr, tr), 1)
    earlier = (r_i < c_i).astype(BF16)
    base = carry[...] + ps_ref[...]
    slot = jnp.dot(m, earlier, preferred_element_type=F32) + jnp.concatenate([base] * (tr // LANES), axis=1)
    row_e = lax.broadcasted_iota(jnp.int32, (ne, tr), 0)
    d_rows = [jnp.sum(jnp.where(row_e == e_ref[k:k + 1, :], slot, 0.0), axis=0, keepdims=True)
              for k in range(TOP_K)]
    d_ref[...] = jnp.concatenate(d_rows, axis=0).astype(jnp.int32)
    carry[...] += jnp.dot(m, jnp.ones((tr, LANES), BF16), preferred_element_type=F32)


def _rank(mask_t, eidx_t, pstart):
    ne, t = mask_t.shape
    tr = _row_tile(t, (512, 256))
    col = lambda i: (0, i)
    return pl.pallas_call(
        _rank_kernel,
        grid=(t // tr,),
        in_specs=[pl.BlockSpec((ne, tr), col), pl.BlockSpec((TOP_K, tr), col),
                  pl.BlockSpec((ne, LANES), lambda i: (0, 0))],
        out_specs=pl.BlockSpec((TOP_K, tr), col),
        out_shape=jax.ShapeDtypeStruct((TOP_K, t), jnp.int32),
        scratch_shapes=[pltpu.VMEM((ne, LANES), F32)],
        compiler_params=pltpu.CompilerParams(
            dimension_semantics=("arbitrary",), vmem_limit_bytes=VMEM_LIMIT),
        name="rank",
    )(mask_t, eidx_t, pstart)


def _dispatch_kernel(ps_ref, pl_ref, nu_ref, d_ref, h_ref, xs_hbm, zbuf, sem, zsem, *, tt, nsub, bm, nb):
    i = pl.program_id(0)
    n = pl.num_programs(0)
    ne = ps_ref.shape[0]
    pieces = [1 << b for b in reversed(range(bm.bit_length() - 1))]

    def zero_fill(op):
        def pad(e, carry):
            start = ps_ref[e]
            length = pl_ref[e]
            for p in pieces:
                off = length - lax.rem(length, 2 * p)

                @pl.when(lax.rem(length, 2 * p) >= p)
                def _():
                    dst = xs_hbm.at[pl.ds(pl.multiple_of((start + off) * nsub, nsub), p * nsub)]
                    op(pltpu.make_async_copy(zbuf.at[pl.ds(0, p * nsub)], dst, zsem))
            return carry
        lax.fori_loop(0, ne, pad, 0)

        def tail(b, carry):
            dst = xs_hbm.at[pl.ds(pl.multiple_of(b * (bm * nsub), bm * nsub), bm * nsub)]
            op(pltpu.make_async_copy(zbuf, dst, zsem))
            return carry
        lax.fori_loop(nu_ref[0], nb, tail, 0)

    @pl.when(i == 0)
    def _():
        zbuf[...] = jnp.zeros_like(zbuf)
        zero_fill(lambda c: c.start())

    def body(r, carry):
        src = h_ref.at[pl.ds(pl.multiple_of(r * nsub, nsub), nsub)]
        for k in range(TOP_K):
            dst = xs_hbm.at[pl.ds(pl.multiple_of(d_ref[0, 0, r * TOP_K + k] * nsub, nsub), nsub)]
            pltpu.make_async_copy(src, dst, sem).start(priority=k % 2)
        return carry

    lax.fori_loop(0, tt, body, 0)

    for _ in range(TOP_K):
        pltpu.make_async_copy(h_ref, xs_hbm.at[pl.ds(0, tt * nsub)], sem).wait()

    @pl.when(i == n - 1)
    def _():
        zero_fill(lambda c: c.wait())


def _dispatch(dest, h2_tiles, pad_start, pad_len, n_used, nb, nsub):
    t = dest.shape[0]
    tt = _row_tile(t, (256,))
    nt = t // tt
    bm = EXPERT_ROWS
    d3 = dest.reshape(nt, 1, tt * TOP_K)
    gs = pltpu.PrefetchScalarGridSpec(
        num_scalar_prefetch=3,
        grid=(nt,),
        in_specs=[pl.BlockSpec((1, 1, tt * TOP_K), lambda i, *_: (i, 0, 0), memory_space=pltpu.SMEM),
                  pl.BlockSpec((tt * nsub, LANES), lambda i, *_: (i, 0))],
        out_specs=pl.BlockSpec(memory_space=pl.ANY),
        scratch_shapes=[pltpu.VMEM((bm * nsub, LANES), F32), pltpu.SemaphoreType.DMA,
                        pltpu.SemaphoreType.DMA],
    )
    return pl.pallas_call(
        functools.partial(_dispatch_kernel, tt=tt, nsub=nsub, bm=bm, nb=nb),
        grid_spec=gs,
        out_shape=jax.ShapeDtypeStruct((nb * bm * nsub, LANES), F32),
        compiler_params=pltpu.CompilerParams(dimension_semantics=("arbitrary",)),
        name="dispatch",
    )(pad_start, pad_len, n_used, d3, h2_tiles)


def _grouped_kernel(b0_ref, nk_ref, nu_ref, x_hbm, wg_ref, wu_ref, wd_ref, y_hbm,
                    xbuf, ybuf, xsem, ysem, wgb, wub, wdb, *, bm, nsub, nb):
    e = pl.program_id(0)
    ring = xbuf.shape[0]
    rows = bm * nsub
    nu = nu_ref[0]

    def x_copy(gb, slot):
        src = x_hbm.at[pl.ds(pl.multiple_of(gb * rows, rows), rows)]
        return pltpu.make_async_copy(src, xbuf.at[slot], xsem.at[slot])

    def y_copy(gb, slot):
        dst = y_hbm.at[pl.ds(pl.multiple_of(gb * rows, rows), rows)]
        return pltpu.make_async_copy(ybuf.at[slot], dst, ysem.at[slot])

    @pl.when(e == 0)
    def _():
        for g0 in range(ring - 1):
            @pl.when(g0 < nu)
            def _():
                x_copy(g0, g0).start()

    @pl.when(nk_ref[e] > 0)
    def _():
        wgb[...] = wg_ref[0].astype(BF16)
        wub[...] = wu_ref[0].astype(BF16)
        wdb[...] = wd_ref[0].astype(BF16)

        def block(b, carry):
            gb = b0_ref[e] + b
            slot = lax.rem(gb, ring)

            @pl.when(gb + ring - 1 < nu)
            def _():
                x_copy(gb + ring - 1, lax.rem(gb + ring - 1, ring)).start()

            x_copy(gb, slot).wait()

            @pl.when(gb >= ring)
            def _():
                y_copy(gb - ring, slot).wait()

            x = _read_rows(xbuf.at[slot], bm, nsub).astype(BF16)
            hg = jnp.dot(x, wgb[...], preferred_element_type=F32)
            hu = jnp.dot(x, wub[...], preferred_element_type=F32)
            y = jnp.dot((_silu(hg) * hu).astype(BF16), wdb[...], preferred_element_type=F32)
            _write_rows(ybuf.at[slot], y)
            y_copy(gb, slot).start()
            return carry

        lax.fori_loop(0, nk_ref[e], block, 0)

    @pl.when(e == pl.num_programs(0) - 1)
    def _():
        for back in range(ring, 0, -1):
            @pl.when(nu >= back)
            def _():
                y_copy(nu - back, lax.rem(nu - back, ring)).wait()

        ybuf[0] = jnp.zeros(ybuf.shape[1:], ybuf.dtype)

        def tail(gb, carry):
            y_copy(gb, 0).start()
            return carry

        def tail_wait(gb, carry):
            y_copy(gb, 0).wait()
            return carry

        lax.fori_loop(nu, nb, tail, 0)
        lax.fori_loop(nu, nb, tail_wait, 0)


def _grouped(xs, blk_start, blk_count, n_used, wg, wu, wd):
    ne, d, ff = wg.shape
    nsub = d // LANES
    bm = EXPERT_ROWS
    nb = xs.shape[0] // (bm * nsub)
    wspec = lambda shape: pl.BlockSpec(shape, lambda e, *_: (e, 0, 0))
    gs = pltpu.PrefetchScalarGridSpec(
        num_scalar_prefetch=3,
        grid=(ne,),
        in_specs=[pl.BlockSpec(memory_space=pl.ANY),
                  wspec((1, d, ff)), wspec((1, d, ff)), wspec((1, ff, d))],
        out_specs=pl.BlockSpec(memory_space=pl.ANY),
        scratch_shapes=[pltpu.VMEM((GROUP_RING, bm * nsub, LANES), F32),
                        pltpu.VMEM((GROUP_RING, bm * nsub, LANES), F32),
                        pltpu.SemaphoreType.DMA((GROUP_RING,)), pltpu.SemaphoreType.DMA((GROUP_RING,)),
                        pltpu.VMEM((d, ff), BF16), pltpu.VMEM((d, ff), BF16), pltpu.VMEM((ff, d), BF16)],
    )
    return pl.pallas_call(
        functools.partial(_grouped_kernel, bm=bm, nsub=nsub, nb=nb),
        grid_spec=gs,
        out_shape=jax.ShapeDtypeStruct(xs.shape, F32),
        compiler_params=pltpu.CompilerParams(
            dimension_semantics=("arbitrary",), vmem_limit_bytes=VMEM_LIMIT),
        name="grouped",
    )(blk_start, blk_count, n_used, xs, wg, wu, wd)


def _combine_kernel(d_ref, dn_ref, y_hbm, w_ref, s_ref, xp_ref, xs_ref, g2_ref, fg_ref, op_ref, os_ref,
                    buf, sem, *, tt, nsub, ntp):
    i = pl.program_id(0)
    n = pl.num_programs(0)

    def gather(dref, slot):
        def body(r, carry):
            for k in range(TOP_K):
                src = y_hbm.at[pl.ds(pl.multiple_of(dref[0, 0, r * TOP_K + k] * nsub, nsub), nsub)]
                pltpu.make_async_copy(src, buf.at[slot, k, pl.ds(pl.multiple_of(r * nsub, nsub), nsub)],
                                      sem.at[slot]).start(priority=k % 2)
            return carry
        lax.fori_loop(0, tt, body, 0)

    @pl.when(i == 0)
    def _():
        gather(d_ref, 0)

    @pl.when(i + 1 < n)
    def _():
        gather(dn_ref, (i + 1) % 2)

    slot = i % 2
    for k in range(TOP_K):
        pltpu.make_async_copy(y_hbm.at[pl.ds(0, tt * nsub)], buf.at[slot, k], sem.at[slot]).wait()

    w = w_ref[...]
    routed = w[:, 0:1] * _read_rows(buf.at[slot, 0], tt, nsub)
    for k in range(1, TOP_K):
        routed = routed + w[:, k:k + 1] * _read_rows(buf.at[slot, k], tt, nsub)
    moe = routed + s_ref[...]
    d = moe.shape[1]
    m3 = moe.reshape(tt // MOD_ROWS, MOD_ROWS, d) * g2_ref[...][:, None, :]
    x1 = jnp.where(i < ntp, xp_ref[...], xs_ref[...])
    out = _rms(x1 + m3.reshape(tt, d), fg_ref[...], RMS_EPS)

    @pl.when(i < ntp)
    def _():
        op_ref[...] = out

    @pl.when(i >= ntp)
    def _():
        os_ref[...] = out


def _combine(dest, y_tiles, wts, shared, xp1, xs1, g2_rows, final_g):
    tp, d = xp1.shape
    ts = xs1.shape[0]
    nsub = d // LANES
    tt = _row_tile(ts, (256,))
    assert tp % tt == 0 and tt % (SUBLANES * MOD_ROWS) == 0
    ntp = tp // tt
    mr = tt // MOD_ROWS
    nt = (tp + ts) // tt
    d3 = dest.reshape(nt, 1, tt * TOP_K)
    kern = functools.partial(_combine_kernel, tt=tt, nsub=nsub, ntp=ntp)
    smem_blk = lambda f: pl.BlockSpec((1, 1, tt * TOP_K), f, memory_space=pltpu.SMEM)
    row = lambda i: (i, 0)
    two = _two_part_specs(tt, d, ntp)
    return pl.pallas_call(
        kern,
        grid=(nt,),
        in_specs=[smem_blk(lambda i: (i, 0, 0)),
                  smem_blk(lambda i: (jnp.minimum(i + 1, nt - 1), 0, 0)),
                  pl.BlockSpec(memory_space=pl.ANY),
                  pl.BlockSpec((tt, LANES), row), pl.BlockSpec((tt, d), row)] + two + [
                  pl.BlockSpec((mr, d), row), pl.BlockSpec((1, d), lambda i: (0, 0))],
        out_specs=two,
        out_shape=[jax.ShapeDtypeStruct((tp, d), F32), jax.ShapeDtypeStruct((ts, d), F32)],
        scratch_shapes=[pltpu.VMEM((2, TOP_K, tt * nsub, LANES), F32), pltpu.SemaphoreType.DMA((2,))],
        compiler_params=pltpu.CompilerParams(
            dimension_semantics=("arbitrary",), vmem_limit_bytes=VMEM_LIMIT),
        name="combine",
    )(d3, d3, y_tiles, wts, shared, xp1, xs1, g2_rows, final_g.reshape(1, d))


def _plan(counts, n_assign):
    ne = counts.shape[0]
    bm = EXPERT_ROWS
    counts = counts.astype(jnp.int32)
    nblk = (counts + bm - 1) // bm
    bend = jnp.cumsum(nblk)
    pstart = jnp.broadcast_to(((bend - nblk) * bm).astype(F32).reshape(ne, 1), (ne, LANES))
    nb = -(-(n_assign + ne * (bm - 1)) // bm)
    n_used = bend[-1:].astype(jnp.int32)
    pad_start = ((bend - nblk) * bm + counts).astype(jnp.int32)
    pad_len = (nblk * bm - counts).astype(jnp.int32)
    return pstart, (bend - nblk).astype(jnp.int32), nblk, n_used, pad_start, pad_len, nb


def per_g_pow2(ne):
    per_g = ne // N_EXPERT_GROUPS
    return per_g * N_EXPERT_GROUPS == ne and per_g & (per_g - 1) == 0


def _mod_rows(m, nbp, lp):
    return jnp.concatenate([jnp.repeat(m[:nbp], lp // MOD_ROWS, axis=0), m[nbp:]], axis=0)


def kernel(x_prompt, x_sample, state_ssm, state_conv, state_pool, c_prompt, c_sample, ln1_g, ln2_g, w_ada, b_ada, w_in, conv_w, conv_b, dt_bias, a_log, d_skip, ssd_norm_g, w_ssd_out, pool_w, pool_scale, w_out, w_router, router_bias, moe_w_gate, moe_w_up, moe_w_down, shared_w_gate, shared_w_up, shared_w_down, final_g):
    bp, lp, dm = x_prompt.shape
    bs, ls, _ = x_sample.shape
    depth = ln1_g.shape[0]
    heads, hd, ns = state_ssm.shape[2], state_ssm.shape[3], state_ssm.shape[4]
    di = heads * hd
    gn = SSD_GROUPS * ns
    cch = di + 2 * gn
    assert depth == 1 and ls == MOD_ROWS and lp % MOD_ROWS == 0 and heads <= LANES
    assert per_g_pow2(w_router.shape[2])
    assert ns == 2 * hd and hd & (hd - 1) == 0 and (heads // SSD_GROUPS) % 2 == 0
    tp, ts = bp * lp, bs * ls
    expand3 = (jnp.arange(3 * LANES)[:, None] % LANES == jnp.arange(di)[None, :] // hd).astype(BF16)

    xp, xs = x_prompt.reshape(tp, dm), x_sample.reshape(ts, dm)
    c_all = jnp.concatenate([c_prompt, c_sample], axis=0)

    o1, o2, o3, o4 = di, di + cch, di + cch + heads, di + cch + heads + dm
    cols = (0, di, 2 * di, di + cch, di + cch + dm)

    ssm_p, conv_p, pool_p, ssm_s, conv_s, pool_s = [], [], [], [], [], []
    for l in range(depth):
        wi = w_in[l]
        wcat = jnp.concatenate(
            [wi[:, :o1], wi[:, o1:o2], wi[:, o3:o4], wi[:, o4:],
             jnp.pad(wi[:, o2:o3], ((0, 0), (0, LANES - heads)))], axis=1).astype(BF16)
        pad_h = lambda v: jnp.pad(v.reshape(1, heads), ((0, 0), (0, LANES - heads)))
        seq_w = (conv_w[l], conv_b[l].reshape(1, cch), pad_h(dt_bias[l]), pad_h(a_log[l]),
                 jnp.repeat(d_skip[l], hd).reshape(1, di), ssd_norm_g[l].reshape(1, di),
                 w_ssd_out[l].astype(BF16), pool_w[l].astype(BF16), pool_scale[l].reshape(1, dm),
                 w_out[l].astype(BF16), expand3)

        mod = _ada(c_all, w_ada[l], b_ada[l])
        sh1, sc1, g1, sh2, sc2, g2 = jnp.split(mod, 6, axis=-1)

        proj, dtp = _inproj(xp, xs, _mod_rows(sc1, bp, lp), _mod_rows(sh1, bp, lp), ln1_g[l], wcat)

        zc = jnp.zeros((bp,) + state_conv.shape[2:], F32)
        zs = jnp.zeros((bp, heads, hd, ns), F32)
        zp = jnp.zeros((bp,) + state_pool.shape[2:], F32)
        xp1, ns_p, nc_p, np_p = _seq(proj, dtp, xp, g1[:bp], zc, zs, zp, seq_w,
                                     row0=0, nb=bp, seqlen=lp, pos0=0, cols=cols)
        xs1, ns_s, nc_s, np_s = _seq(proj, dtp, xs, g1[bp:], state_conv[l], state_ssm[l], state_pool[l],
                                     seq_w, row0=tp, nb=bs, seqlen=ls, pos0=PAST_LEN, cols=cols)
        conv_p.append(nc_p)
        conv_s.append(nc_s)
        pool_p.append(np_p)
        pool_s.append(np_s)
        ssm_p.append(ns_p)
        ssm_s.append(ns_s)

        h2_tiles, shared, eidx_t, wts_t, mask_t, counts = _moe_pre(
            xp1, xs1, _mod_rows(sc2, bp, lp), _mod_rows(sh2, bp, lp), ln2_g[l],
            w_router[l].astype(BF16), router_bias[l],
            shared_w_gate[l].astype(BF16), shared_w_up[l].astype(BF16), shared_w_down[l].astype(BF16))
        nsub = dm // LANES
        pstart, blk_start, blk_count, n_used, pad_start, pad_len, nb = _plan(counts[:, 0], (tp + ts) * TOP_K)
        dest = _rank(mask_t, eidx_t, pstart).T
        wts = jnp.pad(wts_t.T, ((0, 0), (0, LANES - TOP_K)))
        x_sorted = _dispatch(dest, h2_tiles, pad_start, pad_len, n_used, nb, nsub)
        y_tiles = _grouped(x_sorted, blk_start, blk_count, n_used,
                           moe_w_gate[l], moe_w_up[l], moe_w_down[l])
        xp, xs = _combine(dest, y_tiles, wts, shared, xp1, xs1, _mod_rows(g2, bp, lp), final_g)

    y_prompt = xp.reshape(bp, lp, dm)
    y_sample = xs.reshape(bs, ls, dm)
    return (y_prompt, y_sample, jnp.stack(ssm_p), jnp.stack(conv_p), jnp.stack(pool_p),
            jnp.stack(ssm_s), jnp.stack(conv_s), jnp.stack(pool_s))
```

```python
import functools

import jax
import jax.numpy as jnp
from jax import lax
from jax.experimental import pallas as pl
from jax.experimental.pallas import tpu as pltpu

F32 = jnp.float32
BF16 = jnp.bfloat16
HIGHEST = lax.Precision.HIGHEST

RMS_EPS = 1e-6
SSD_NORM_EPS = 1e-5
CHUNK = 64
SSD_GROUPS = 4
POOL_WINDOWS = (2, 4, 8, 16)
PAST_LEN = 1024
TOP_K = 8
N_EXPERT_GROUPS = 8
TOPK_GROUPS = 4
ROUTED_SCALE = 2.5

LANES = 128
SUBLANES = 8
MOD_ROWS = 32
VMEM_LIMIT = 56 * 1024 * 1024
EXPERT_ROWS = 128
SEQ_TILE = 256
GROUP_RING = 4


def _sigmoid(x):
    return 0.5 * jnp.tanh(0.5 * x) + 0.5


def _silu(x):
    h = 0.5 * x
    return h + h * jnp.tanh(h)


def _softplus(x):
    return jnp.maximum(x, 0.0) + jnp.log1p(jnp.exp(-jnp.abs(x)))


def _row_tile(n, prefs):
    for t in prefs:
        if n % t == 0:
            return t
    return n


def _modulate(y, sc_ref, sh_ref):
    rows, d = y.shape
    y3 = y.reshape(rows // MOD_ROWS, MOD_ROWS, d)
    y3 = y3 * (1.0 + sc_ref[...][:, None, :]) + sh_ref[...][:, None, :]
    return y3.reshape(rows, d)


def _rms(x, g, eps):
    return x * lax.rsqrt(jnp.mean(x * x, axis=-1, keepdims=True) + eps) * g


def _ada_kernel(c_ref, w_ref, b_ref, o_ref):
    s = _silu(c_ref[...])
    o_ref[...] = jnp.dot(s, w_ref[...], preferred_element_type=F32, precision=HIGHEST) + b_ref[...]


def _ada(c_all, w_ada, b_ada):
    n, d = c_all.shape
    dout = w_ada.shape[1]
    tn = _row_tile(dout, (1024, 512, 256, 128))
    return pl.pallas_call(
        _ada_kernel,
        grid=(dout // tn,),
        in_specs=[pl.BlockSpec((n, d), lambda j: (0, 0)),
                  pl.BlockSpec((d, tn), lambda j: (0, j)),
                  pl.BlockSpec((1, tn), lambda j: (0, j))],
        out_specs=pl.BlockSpec((n, tn), lambda j: (0, j)),
        out_shape=jax.ShapeDtypeStruct((n, dout), F32),
        name="ada",
    )(c_all, w_ada, b_ada.reshape(1, dout))


def _two_part_specs(tm, d, ntp):
    return [pl.BlockSpec((tm, d), lambda i, *_: (jnp.minimum(i, ntp - 1), 0)),
            pl.BlockSpec((tm, d), lambda i, *_: (jnp.maximum(i - ntp, 0), 0))]


def _inproj_kernel(xp_ref, xs_ref, sc_ref, sh_ref, g_ref, w_ref, o_ref, dt_ref, *, ntp, tn):
    x = jnp.where(pl.program_id(0) < ntp, xp_ref[...], xs_ref[...])
    h = _modulate(_rms(x, g_ref[...], RMS_EPS), sc_ref, sh_ref).astype(BF16)
    n_main = o_ref.shape[1]
    for c0 in range(0, n_main, tn):
        o_ref[:, c0:c0 + tn] = jnp.dot(h, w_ref[:, c0:c0 + tn], preferred_element_type=F32).astype(BF16)
    dt_ref[...] = jnp.dot(h, w_ref[:, n_main:], preferred_element_type=F32)


def _inproj(xp, xs, sc_rows, sh_rows, g, wcat):
    tp, d = xp.shape
    ts = xs.shape[0]
    n = wcat.shape[1]
    n_main = n - LANES
    tm = _row_tile(ts, (512, 256))
    assert tp % tm == 0 and tm % (SUBLANES * MOD_ROWS) == 0
    ntp = tp // tm
    nt = ntp + ts // tm
    mr = tm // MOD_ROWS
    tn = _row_tile(n_main, (2048, 1024, 512, 256, 128))
    row = lambda i: (i, 0)
    return pl.pallas_call(
        functools.partial(_inproj_kernel, ntp=ntp, tn=tn),
        grid=(nt,),
        in_specs=_two_part_specs(tm, d, ntp) + [
            pl.BlockSpec((mr, d), row), pl.BlockSpec((mr, d), row),
            pl.BlockSpec((1, d), lambda i: (0, 0)),
            pl.BlockSpec((d, n), lambda i: (0, 0), pipeline_mode=pl.Buffered(1))],
        out_specs=[pl.BlockSpec((tm, n_main), row), pl.BlockSpec((tm, LANES), row)],
        out_shape=[jax.ShapeDtypeStruct((tp + ts, n_main), BF16),
                   jax.ShapeDtypeStruct((tp + ts, LANES), F32)],
        compiler_params=pltpu.CompilerParams(
            dimension_semantics=("parallel",), vmem_limit_bytes=VMEM_LIMIT),
        name="inproj",
    )(xp, xs, sc_rows, sh_rows, g.reshape(1, d), wcat)


def _seq_kernel(z_ref, xp_ref, bc_ref, u_ref, gt_ref, dt_ref, x_ref, g1_ref,
                cst_ref, sst_ref, pst_ref, cw_ref, cb_ref, dtb_ref, alog_ref, dx_ref, ng_ref,
                wssd_ref, pw_ref, ps_ref, wout_ref, e3_ref,
                xo_ref, so_ref, co_ref, po_ref,
                cbuf, pbuf, st_scr, y_scr, xbc_scr, csx_scr, wvx_scr,
                *, tq, q, pos0, heads, hd, ns, dm):
    j = pl.program_id(1)
    di = heads * hd
    gn = SSD_GROUPS * ns
    hpg = heads // SSD_GROUPS
    cw = cbuf.shape[0] - tq
    kw = cw_ref.shape[0]
    ph = pbuf.shape[0] - tq

    @pl.when(j == 0)
    def _():
        cbuf[0:cw - (kw - 1), :] = jnp.zeros((cw - (kw - 1), cbuf.shape[1]), F32)
        cbuf[cw - (kw - 1):cw, :] = cst_ref[0]
        pbuf[0:1, :] = jnp.zeros((1, dm), F32)
        pbuf[1:ph, :] = pst_ref[0]
        st_scr[...] = sst_ref[0].T

    cbuf[cw:cw + tq, 0:di] = xp_ref[...].astype(F32)
    cbuf[cw:cw + tq, di:di + 2 * gn] = bc_ref[...].astype(F32)
    ext = cbuf[...]
    acc = cb_ref[...] + ext[cw:cw + tq, :] * cw_ref[kw - 1:kw, :]
    for k in range(kw - 1):
        acc = acc + pltpu.roll(ext, kw - 1 - k, axis=0)[cw:cw + tq, :] * cw_ref[k:k + 1, :]
    xbc_scr[...] = _silu(acc)
    cbuf[cw - (kw - 1):cw, :] = cbuf[cw + tq - (kw - 1):cw + tq, :]

    dt = _softplus(dt_ref[...] + dtb_ref[...])
    dta = dt * (-jnp.exp(alog_ref[...]))
    lq = q.bit_length() - 1
    r_i = lax.broadcasted_iota(jnp.int32, (tq, tq), 0)
    c_i = lax.broadcasted_iota(jnp.int32, (tq, tq), 1)
    same = lax.shift_right_logical(r_i, lq) == lax.shift_right_logical(c_i, lq)
    tril = jnp.where(same, (r_i >= c_i).astype(F32), 0.0)
    cs = jnp.dot(tril, dta, preferred_element_type=F32, precision=HIGHEST)
    cs_end = jnp.dot(same.astype(F32), dta, preferred_element_type=F32, precision=HIGHEST)
    wv = dt * jnp.exp(cs_end - cs)

    both = jnp.concatenate([cs, wv], axis=0)
    hi = both.astype(BF16)
    r1 = both - hi.astype(F32)
    mid = r1.astype(BF16)
    lo = (r1 - mid.astype(F32)).astype(BF16)
    ex = jnp.dot(jnp.concatenate([hi, mid, lo], axis=1), e3_ref[...], preferred_element_type=F32)
    csx_scr[...] = ex[0:tq]
    wvx_scr[...] = ex[tq:2 * tq]
    cs_t = cs.T
    dt_t = dt.T

    lane = lax.broadcasted_iota(jnp.int32, (q, 2 * hd), 1)
    kpos = jnp.bitwise_and(lane, hd - 1)
    causal2 = jnp.logical_and(lax.broadcasted_iota(jnp.int32, (q, 2 * hd), 0) >= kpos, kpos < q)
    first = lane < hd
    zrow = jnp.zeros((1, hd - q), F32)
    zblk = jnp.zeros((hd - q, 2 * hd), BF16)

    def pair_row(t, p, r0):
        parts = []
        for h in (2 * p, 2 * p + 1):
            parts.append(t[h:h + 1, r0:r0 + q])
            if q < hd:
                parts.append(zrow)
        return jnp.concatenate(parts, axis=1)

    def pair_rows(a, b):
        blocks = [a, zblk, b, zblk] if q < hd else [a, b]
        return jnp.concatenate(blocks, axis=0)

    gw = di // SSD_GROUPS
    for c in range(tq // q):
        r0 = c * q
        for g in range(SSD_GROUPS):
            bgb = xbc_scr[r0:r0 + q, di + g * ns:di + (g + 1) * ns].astype(BF16)
            cgb = xbc_scr[r0:r0 + q, di + gn + g * ns:di + gn + (g + 1) * ns].astype(BF16)
            cb2 = lax.dot_general(cgb, pair_rows(bgb, bgb), (((1,), (1,)), ((), ())),
                                  preferred_element_type=F32)
            stg = st_scr[:, g * gw:(g + 1) * gw]
            ecs = jnp.exp(csx_scr[r0:r0 + q, g * gw:(g + 1) * gw])
            yo = jnp.dot(cgb, stg.astype(BF16), preferred_element_type=F32) * ecs
            for pp in range(hpg // 2):
                p = g * (hpg // 2) + pp
                lo_l = p * 2 * hd
                seg = csx_scr[r0:r0 + q, lo_l:lo_l + 2 * hd] - pair_row(cs_t, p, r0)
                lm = jnp.exp(jnp.where(causal2, seg, -jnp.inf))
                m2 = (cb2 * lm * pair_row(dt_t, p, r0)).astype(BF16)
                xpair = xbc_scr[r0:r0 + q, lo_l:lo_l + 2 * hd]
                rhs = pair_rows(jnp.where(first, xpair, 0.0).astype(BF16),
                                jnp.where(first, 0.0, xpair).astype(BF16))
                yd = jnp.dot(m2, rhs, preferred_element_type=F32)
                y_scr[r0:r0 + q, lo_l:lo_l + 2 * hd] = yd + yo[:, pp * 2 * hd:(pp + 1) * 2 * hd]
            xw = (xbc_scr[r0:r0 + q, g * gw:(g + 1) * gw]
                  * wvx_scr[r0:r0 + q, g * gw:(g + 1) * gw]).astype(BF16)
            upd = lax.dot_general(bgb, xw, (((0,), (0,)), ((), ())), preferred_element_type=F32)
            st_scr[:, g * gw:(g + 1) * gw] = stg * ecs[q - 1:q, :] + upd

    @pl.when(j == pl.num_programs(1) - 1)
    def _():
        so_ref[0] = st_scr[...].T
        co_ref[0] = cbuf[cw - (kw - 1):cw, :]

    y = y_scr[...] + xbc_scr[:, 0:di] * dx_ref[...]
    v = y * _silu(z_ref[...].astype(F32))
    gw = di // SSD_GROUPS
    parts = []
    for g in range(SSD_GROUPS):
        vg = v[:, g * gw:(g + 1) * gw]
        parts.append(vg * lax.rsqrt(jnp.mean(vg * vg, axis=-1, keepdims=True) + SSD_NORM_EPS))
    yn = (jnp.concatenate(parts, axis=-1) * ng_ref[...]).astype(BF16)
    a_br = jnp.dot(yn, wssd_ref[...], preferred_element_type=F32)

    pbuf[ph:ph + tq, :] = u_ref[...].astype(F32)
    pos = pos0 + j * tq + lax.broadcasted_iota(jnp.int32, (tq, 1), 0)
    pgd = dm // len(POOL_WINDOWS)
    b_parts = []
    for gi, w in enumerate(POOL_WINDOWS):
        lo = gi * pgd
        s = pbuf[:, lo:lo + pgd]
        step = 1
        while step < w:
            s = s + pltpu.roll(s, step, axis=0)
            step *= 2
        cnt = jnp.minimum(w, pos + 1).astype(F32)
        pooled = s[ph:ph + tq, :] / cnt - pbuf[ph:ph + tq, lo:lo + pgd]
        b_parts.append(jnp.dot(pooled.astype(BF16), pw_ref[gi], preferred_element_type=F32))
    b_br = jnp.concatenate(b_parts, axis=-1) * ps_ref[...]
    pbuf[1:ph, :] = pbuf[tq + 1:tq + ph, :]

    @pl.when(j == pl.num_programs(1) - 1)
    def _():
        po_ref[0] = pbuf[1:ph, :]

    gates = _sigmoid(gt_ref[...].astype(F32))
    merged = gates[:, 0:dm] * a_br + gates[:, dm:2 * dm] * b_br
    mix = jnp.dot(merged.astype(BF16), wout_ref[...], preferred_element_type=F32)
    xo_ref[...] = x_ref[...] + g1_ref[0] * mix


def _seq(proj, dtp, x, g1, conv_state, ssm_state, pool_state, wts, *, row0, nb, seqlen, pos0, cols):
    dm = x.shape[1]
    heads, hd, ns = ssm_state.shape[1], ssm_state.shape[2], ssm_state.shape[3]
    di = heads * hd
    gn = SSD_GROUPS * ns
    q = min(CHUNK, seqlen)
    tq = _row_tile(seqlen, (SEQ_TILE, 128, 64, 32))
    nj = seqlen // tq
    rb0 = row0 // tq
    oz, ox, obc, ou, og = cols

    def rows(b, j):
        return rb0 + b * nj + j

    full = lambda a: pl.BlockSpec(a.shape, lambda b, j: (0,) * a.ndim, pipeline_mode=pl.Buffered(1))
    kern = functools.partial(_seq_kernel, tq=tq, q=q, pos0=pos0, heads=heads, hd=hd, ns=ns, dm=dm)
    kw1 = conv_state.shape[1]
    pst = pool_state.shape[1]
    xo, so, co, po = pl.pallas_call(
        kern,
        grid=(nb, nj),
        in_specs=[
            pl.BlockSpec((tq, di), lambda b, j: (rows(b, j), oz // di)),
            pl.BlockSpec((tq, di), lambda b, j: (rows(b, j), ox // di)),
            pl.BlockSpec((tq, 2 * gn), lambda b, j: (rows(b, j), obc // (2 * gn))),
            pl.BlockSpec((tq, dm), lambda b, j: (rows(b, j), ou // dm)),
            pl.BlockSpec((tq, 2 * dm), lambda b, j: (rows(b, j), og // (2 * dm))),
            pl.BlockSpec((tq, LANES), lambda b, j: (rows(b, j), 0)),
            pl.BlockSpec((tq, dm), lambda b, j: (b * nj + j, 0)),
            pl.BlockSpec((1, 1, dm), lambda b, j: (b, 0, 0)),
            pl.BlockSpec((1, kw1, di + 2 * gn), lambda b, j: (b, 0, 0)),
            pl.BlockSpec((1, di, ns), lambda b, j: (b, 0, 0)),
            pl.BlockSpec((1, pst, dm), lambda b, j: (b, 0, 0)),
        ] + [full(w) for w in wts],
        out_specs=[pl.BlockSpec((tq, dm), lambda b, j: (b * nj + j, 0)),
                   pl.BlockSpec((1, di, ns), lambda b, j: (b, 0, 0)),
                   pl.BlockSpec((1, kw1, di + 2 * gn), lambda b, j: (b, 0, 0)),
                   pl.BlockSpec((1, pst, dm), lambda b, j: (b, 0, 0))],
        out_shape=[jax.ShapeDtypeStruct((nb * seqlen, dm), F32),
                   jax.ShapeDtypeStruct((nb, di, ns), F32),
                   jax.ShapeDtypeStruct((nb, kw1, di + 2 * gn), F32),
                   jax.ShapeDtypeStruct((nb, pst, dm), F32)],
        scratch_shapes=[pltpu.VMEM((tq + SUBLANES, di + 2 * gn), F32),
                        pltpu.VMEM((tq + 2 * SUBLANES, dm), F32),
                        pltpu.VMEM((ns, di), F32),
                        pltpu.VMEM((tq, di), F32),
                        pltpu.VMEM((tq, di + 2 * gn), F32),
                        pltpu.VMEM((tq, di), F32),
                        pltpu.VMEM((tq, di), F32)],
        compiler_params=pltpu.CompilerParams(
            dimension_semantics=("parallel", "arbitrary"), vmem_limit_bytes=VMEM_LIMIT),
        name="seq",
    )(proj, proj, proj, proj, proj, dtp, x, g1.reshape(nb, 1, dm),
      conv_state, ssm_state.reshape(nb, di, ns), pool_state, *wts)
    return xo, so.reshape(nb, heads, hd, ns), co, po


def _moe_pre_kernel(xp_ref, xs_ref, sc_ref, sh_ref, g_ref, wr_ref, rb_ref, swg_ref, swu_ref, swd_ref,
                    h_ref, s_ref, e_ref, w_ref, m_ref, cnt_ref, *, ntp):
    x = jnp.where(pl.program_id(0) < ntp, xp_ref[...], xs_ref[...])
    y = _rms(x, g_ref[...], RMS_EPS)
    h = _modulate(y, sc_ref, sh_ref)
    _write_rows(h_ref, h)
    hb = h.astype(BF16)

    sg = jnp.dot(hb, swg_ref[...], preferred_element_type=F32)
    su = jnp.dot(hb, swu_ref[...], preferred_element_type=F32)
    s_ref[...] = jnp.dot((_silu(sg) * su).astype(BF16), swd_ref[...], preferred_element_type=F32)

    scores = jax.nn.sigmoid(lax.dot_general(wr_ref[...], hb, (((1,), (1,)), ((), ())),
                                            preferred_element_type=F32))
    biased = scores + rb_ref[:, 0:1]
    ne, tm = scores.shape
    per_g = ne // N_EXPERT_GROUPS
    neg = -jnp.inf
    row_e = lax.broadcasted_iota(jnp.int32, (ne, tm), 0).astype(F32)
    row_g = lax.broadcasted_iota(jnp.int32, (N_EXPERT_GROUPS, tm), 0).astype(F32)

    def first_argmax(v, idx, big):
        top = jnp.max(v, axis=0, keepdims=True)
        return top, jnp.min(jnp.where(v == top, idx, big), axis=0, keepdims=True)

    gs = jnp.full((N_EXPERT_GROUPS, tm), neg, F32)
    for g in range(N_EXPERT_GROUPS):
        mg = biased[g * per_g:(g + 1) * per_g, :]
        rg = (lax.broadcasted_iota(jnp.int32, (per_g, tm), 0) + g * per_g).astype(F32)
        t1, i1 = first_argmax(mg, rg, float(ne))
        t2 = jnp.max(jnp.where(rg == i1, neg, mg), axis=0, keepdims=True)
        gs = jnp.where(row_g == g, t1 + t2, gs)
    grp_e = lax.shift_right_logical(lax.broadcasted_iota(jnp.int32, (ne, tm), 0),
                                    per_g.bit_length() - 1).astype(F32)
    allowed = jnp.zeros((ne, tm), F32)
    for _ in range(TOPK_GROUPS):
        _, gi = first_argmax(gs, row_g, float(N_EXPERT_GROUPS))
        allowed = jnp.where(grp_e == gi, 1.0, allowed)
        gs = jnp.where(row_g == gi, neg, gs)
    mb = jnp.where(allowed > 0.0, biased, neg)
    row_k = lax.broadcasted_iota(jnp.int32, (TOP_K, tm), 0)
    eacc = jnp.zeros((TOP_K, tm), F32)
    wacc = jnp.zeros((TOP_K, tm), F32)
    chosen = jnp.zeros((ne, tm), F32)
    for k in range(TOP_K):
        _, ik = first_argmax(mb, row_e, float(ne))
        sel = row_e == ik
        wk = jnp.sum(jnp.where(sel, scores, 0.0), axis=0, keepdims=True)
        eacc = jnp.where(row_k == k, ik, eacc)
        wacc = jnp.where(row_k == k, wk, wacc)
        chosen = jnp.where(sel, 1.0, chosen)
        mb = jnp.where(sel, neg, mb)
    wsum = jnp.sum(wacc, axis=0, keepdims=True)
    e_ref[...] = eacc.astype(jnp.int32)
    w_ref[...] = wacc / (wsum + 1e-20) * ROUTED_SCALE
    chosen_b = chosen.astype(BF16)
    m_ref[...] = chosen_b

    @pl.when(pl.program_id(0) == 0)
    def _():
        cnt_ref[...] = jnp.zeros_like(cnt_ref)

    cnt_ref[...] += jnp.dot(chosen_b, jnp.ones((tm, LANES), BF16), preferred_element_type=F32)


def _moe_pre(xp1, xs1, sc_rows, sh_rows, g, wr, rb, swg, swu, swd):
    tp, d = xp1.shape
    ts = xs1.shape[0]
    t = tp + ts
    ne = wr.shape[1]
    ff = swg.shape[1]
    tm = _row_tile(ts, (256,))
    assert tp % tm == 0 and tm % (SUBLANES * MOD_ROWS) == 0
    ntp = tp // tm
    mr = tm // MOD_ROWS
    nsub = _token_rows(d)
    row = lambda i: (i, 0)
    col = lambda i: (0, i)
    const = lambda i: (0, 0)
    rb_col = jnp.broadcast_to(rb.reshape(ne, 1), (ne, LANES))
    return pl.pallas_call(
        functools.partial(_moe_pre_kernel, ntp=ntp),
        grid=(t // tm,),
        in_specs=_two_part_specs(tm, d, ntp) + [
                  pl.BlockSpec((mr, d), row), pl.BlockSpec((mr, d), row),
                  pl.BlockSpec((1, d), const), pl.BlockSpec((ne, d), const), pl.BlockSpec((ne, LANES), const),
                  pl.BlockSpec((d, ff), const), pl.BlockSpec((d, ff), const), pl.BlockSpec((ff, d), const)],
        out_specs=[pl.BlockSpec((tm * nsub, LANES), row), pl.BlockSpec((tm, d), row),
                   pl.BlockSpec((TOP_K, tm), col), pl.BlockSpec((TOP_K, tm), col),
                   pl.BlockSpec((ne, tm), col), pl.BlockSpec((ne, LANES), const)],
        out_shape=[jax.ShapeDtypeStruct((t * nsub, LANES), ROW_DTYPE), jax.ShapeDtypeStruct((t, d), F32),
                   jax.ShapeDtypeStruct((TOP_K, t), jnp.int32), jax.ShapeDtypeStruct((TOP_K, t), F32),
                   jax.ShapeDtypeStruct((ne, t), BF16), jax.ShapeDtypeStruct((ne, LANES), F32)],
        compiler_params=pltpu.CompilerParams(
            dimension_semantics=("arbitrary",), vmem_limit_bytes=VMEM_LIMIT),
        name="moe_pre",
    )(xp1, xs1, sc_rows, sh_rows, g.reshape(1, d), wr.T, rb_col, swg, swu, swd)


ROW_DTYPE = F32


def _token_rows(d):
    return d // LANES


def _read_rows(buf, nrows, nsub):
    return jnp.concatenate([buf[pl.ds(c, nrows, stride=nsub), :] for c in range(nsub)], axis=-1)


def _write_rows(ref, v):
    nrows, width = v.shape
    nsub = _token_rows(width)
    for c in range(nsub):
        ref[pl.ds(c, nrows, stride=nsub), :] = v[:, c * LANES:(c + 1) * LANES]


def _rank_kernel(m_ref, e_ref, ps_ref, d_ref, carry):
    i = pl.program_id(0)

    @pl.when(i == 0)
    def _():
        carry[...] = jnp.zeros_like(carry)

    m = m_ref[...]
    ne, tr = m.shape
    r_i = lax.broadcasted_iota(jnp.int32, (tr, tr), 0)
    c_i = lax.broadcasted_iota(jnp.int32, (tr, tr), 1)
    earlier = (r_i < c_i).astype(BF16)
    base = carry[...] + ps_ref[...]
    slot = jnp.dot(m, earlier, preferred_element_type=F32) + jnp.concatenate([base] * (tr // LANES), axis=1)
    row_e = lax.broadcasted_iota(jnp.int32, (ne, tr), 0)
    d_rows = [jnp.sum(jnp.where(row_e == e_ref[k:k + 1, :], slot, 0.0), axis=0, keepdims=True)
              for k in range(TOP_K)]
    d_ref[...] = jnp.concatenate(d_rows, axis=0).astype(jnp.int32)
    carry[...] += jnp.dot(m, jnp.ones((tr, LANES), BF16), preferred_element_type=F32)


def _rank(mask_t, eidx_t, pstart):
    ne, t = mask_t.shape
    tr = _row_tile(t, (512, 256))
    col = lambda i: (0, i)
    return pl.pallas_call(
        _rank_kernel,
        grid=(t // tr,),
        in_specs=[pl.BlockSpec((ne, tr), col), pl.BlockSpec((TOP_K, tr), col),
                  pl.BlockSpec((ne, LANES), lambda i: (0, 0))],
        out_specs=pl.BlockSpec((TOP_K, tr), col),
        out_shape=jax.ShapeDtypeStruct((TOP_K, t), jnp.int32),
        scratch_shapes=[pltpu.VMEM((ne, LANES), F32)],
        compiler_params=pltpu.CompilerParams(
            dimension_semantics=("arbitrary",), vmem_limit_bytes=VMEM_LIMIT),
        name="rank",
    )(mask_t, eidx_t, pstart)


def _dispatch_kernel(ps_ref, pl_ref, nu_ref, d_ref, h_ref, xs_hbm, zbuf, sem, zsem, *, tt, nsub, bm, nb):
    i = pl.program_id(0)
    n = pl.num_programs(0)
    ne = ps_ref.shape[0]
    pieces = [1 << b for b in reversed(range(bm.bit_length() - 1))]

    def zero_fill(op):
        def pad(e, carry):
            start = ps_ref[e]
            length = pl_ref[e]
            for p in pieces:
                off = length - lax.rem(length, 2 * p)

                @pl.when(lax.rem(length, 2 * p) >= p)
                def _():
                    dst = xs_hbm.at[pl.ds(pl.multiple_of((start + off) * nsub, nsub), p * nsub)]
                    op(pltpu.make_async_copy(zbuf.at[pl.ds(0, p * nsub)], dst, zsem))
            return carry
        lax.fori_loop(0, ne, pad, 0)

        def tail(b, carry):
            dst = xs_hbm.at[pl.ds(pl.multiple_of(b * (bm * nsub), bm * nsub), bm * nsub)]
            op(pltpu.make_async_copy(zbuf, dst, zsem))
            return carry
        lax.fori_loop(nu_ref[0], nb, tail, 0)

    @pl.when(i == 0)
    def _():
        zbuf[...] = jnp.zeros_like(zbuf)
        zero_fill(lambda c: c.start())

    def body(r, carry):
        src = h_ref.at[pl.ds(pl.multiple_of(r * nsub, nsub), nsub)]
        for k in range(TOP_K):
            dst = xs_hbm.at[pl.ds(pl.multiple_of(d_ref[0, 0, r * TOP_K + k] * nsub, nsub), nsub)]
            pltpu.make_async_copy(src, dst, sem).start(priority=k % 2)
        return carry

    lax.fori_loop(0, tt, body, 0)

    for _ in range(TOP_K):
        pltpu.make_async_copy(h_ref, xs_hbm.at[pl.ds(0, tt * nsub)], sem).wait()

    @pl.when(i == n - 1)
    def _():
        zero_fill(lambda c: c.wait())


def _dispatch(dest, h2_tiles, pad_start, pad_len, n_used, nb, nsub):
    t = dest.shape[0]
    tt = _row_tile(t, (256,))
    nt = t // tt
    bm = EXPERT_ROWS
    d3 = dest.reshape(nt, 1, tt * TOP_K)
    gs = pltpu.PrefetchScalarGridSpec(
        num_scalar_prefetch=3,
        grid=(nt,),
        in_specs=[pl.BlockSpec((1, 1, tt * TOP_K), lambda i, *_: (i, 0, 0), memory_space=pltpu.SMEM),
                  pl.BlockSpec((tt * nsub, LANES), lambda i, *_: (i, 0))],
        out_specs=pl.BlockSpec(memory_space=pl.ANY),
        scratch_shapes=[pltpu.VMEM((bm * nsub, LANES), ROW_DTYPE), pltpu.SemaphoreType.DMA,
                        pltpu.SemaphoreType.DMA],
    )
    return pl.pallas_call(
        functools.partial(_dispatch_kernel, tt=tt, nsub=nsub, bm=bm, nb=nb),
        grid_spec=gs,
        out_shape=jax.ShapeDtypeStruct((nb * bm * nsub, LANES), ROW_DTYPE),
        compiler_params=pltpu.CompilerParams(dimension_semantics=("arbitrary",)),
        name="dispatch",
    )(pad_start, pad_len, n_used, d3, h2_tiles)


def _grouped_kernel(b0_ref, nk_ref, nu_ref, x_hbm, wg_ref, wu_ref, wd_ref, y_hbm,
                    xbuf, ybuf, xsem, ysem, wgb, wub, wdb, *, bm, nsub, nb):
    e = pl.program_id(0)
    ring = xbuf.shape[0]
    rows = bm * nsub
    nu = nu_ref[0]

    def x_copy(gb, slot):
        src = x_hbm.at[pl.ds(pl.multiple_of(gb * rows, rows), rows)]
        return pltpu.make_async_copy(src, xbuf.at[slot], xsem.at[slot])

    def y_copy(gb, slot):
        dst = y_hbm.at[pl.ds(pl.multiple_of(gb * rows, rows), rows)]
        return pltpu.make_async_copy(ybuf.at[slot], dst, ysem.at[slot])

    @pl.when(e == 0)
    def _():
        for g0 in range(ring - 1):
            @pl.when(g0 < nu)
            def _():
                x_copy(g0, g0).start()

    @pl.when(nk_ref[e] > 0)
    def _():
        wgb[...] = wg_ref[0].astype(BF16)
        wub[...] = wu_ref[0].astype(BF16)
        wdb[...] = wd_ref[0].astype(BF16)

        def block(b, carry):
            gb = b0_ref[e] + b
            slot = lax.rem(gb, ring)

            @pl.when(gb + ring - 1 < nu)
            def _():
                x_copy(gb + ring - 1, lax.rem(gb + ring - 1, ring)).start()

            x_copy(gb, slot).wait()

            @pl.when(gb >= ring)
            def _():
                y_copy(gb - ring, slot).wait()

            x = _read_rows(xbuf.at[slot], bm, nsub).astype(BF16)
            hg = jnp.dot(x, wgb[...], preferred_element_type=F32)
            hu = jnp.dot(x, wub[...], preferred_element_type=F32)
            y = jnp.dot((_silu(hg) * hu).astype(BF16), wdb[...], preferred_element_type=F32)
            _write_rows(ybuf.at[slot], y)
            y_copy(gb, slot).start()
            return carry

        lax.fori_loop(0, nk_ref[e], block, 0)

    @pl.when(e == pl.num_programs(0) - 1)
    def _():
        for back in range(ring, 0, -1):
            @pl.when(nu >= back)
            def _():
                y_copy(nu - back, lax.rem(nu - back, ring)).wait()

        ybuf[0] = jnp.zeros(ybuf.shape[1:], ybuf.dtype)

        def tail(gb, carry):
            y_copy(gb, 0).start()
            return carry

        def tail_wait(gb, carry):
            y_copy(gb, 0).wait()
            return carry

        lax.fori_loop(nu, nb, tail, 0)
        lax.fori_loop(nu, nb, tail_wait, 0)


def _grouped(xs, blk_start, blk_count, n_used, wg, wu, wd):
    ne, d, ff = wg.shape
    nsub = _token_rows(d)
    bm = EXPERT_ROWS
    nb = xs.shape[0] // (bm * nsub)
    wspec = lambda shape: pl.BlockSpec(shape, lambda e, *_: (e, 0, 0))
    gs = pltpu.PrefetchScalarGridSpec(
        num_scalar_prefetch=3,
        grid=(ne,),
        in_specs=[pl.BlockSpec(memory_space=pl.ANY),
                  wspec((1, d, ff)), wspec((1, d, ff)), wspec((1, ff, d))],
        out_specs=pl.BlockSpec(memory_space=pl.ANY),
        scratch_shapes=[pltpu.VMEM((GROUP_RING, bm * nsub, LANES), ROW_DTYPE),
                        pltpu.VMEM((GROUP_RING, bm * nsub, LANES), ROW_DTYPE),
                        pltpu.SemaphoreType.DMA((GROUP_RING,)), pltpu.SemaphoreType.DMA((GROUP_RING,)),
                        pltpu.VMEM((d, ff), BF16), pltpu.VMEM((d, ff), BF16), pltpu.VMEM((ff, d), BF16)],
    )
    return pl.pallas_call(
        functools.partial(_grouped_kernel, bm=bm, nsub=nsub, nb=nb),
        grid_spec=gs,
        out_shape=jax.ShapeDtypeStruct(xs.shape, xs.dtype),
        compiler_params=pltpu.CompilerParams(
            dimension_semantics=("arbitrary",), vmem_limit_bytes=VMEM_LIMIT),
        name="grouped",
    )(blk_start, blk_count, n_used, xs, wg, wu, wd)


def _combine_kernel(d_ref, dn_ref, y_hbm, w_ref, s_ref, xp_ref, xs_ref, g2_ref, fg_ref, op_ref, os_ref,
                    buf, sem, *, tt, nsub, ntp):
    i = pl.program_id(0)
    n = pl.num_programs(0)

    def gather(dref, slot):
        def body(r, carry):
            for k in range(TOP_K):
                src = y_hbm.at[pl.ds(pl.multiple_of(dref[0, 0, r * TOP_K + k] * nsub, nsub), nsub)]
                pltpu.make_async_copy(src, buf.at[slot, k, pl.ds(pl.multiple_of(r * nsub, nsub), nsub)],
                                      sem.at[slot]).start(priority=k % 2)
            return carry
        lax.fori_loop(0, tt, body, 0)

    @pl.when(i == 0)
    def _():
        gather(d_ref, 0)

    @pl.when(i + 1 < n)
    def _():
        gather(dn_ref, (i + 1) % 2)

    slot = i % 2
    for k in range(TOP_K):
        pltpu.make_async_copy(y_hbm.at[pl.ds(0, tt * nsub)], buf.at[slot, k], sem.at[slot]).wait()

    w = w_ref[...]
    routed = w[:, 0:1] * _read_rows(buf.at[slot, 0], tt, nsub)
    for k in range(1, TOP_K):
        routed = routed + w[:, k:k + 1] * _read_rows(buf.at[slot, k], tt, nsub)
    moe = routed + s_ref[...]
    d = moe.shape[1]
    m3 = moe.reshape(tt // MOD_ROWS, MOD_ROWS, d) * g2_ref[...][:, None, :]
    x1 = jnp.where(i < ntp, xp_ref[...], xs_ref[...])
    out = _rms(x1 + m3.reshape(tt, d), fg_ref[...], RMS_EPS)

    @pl.when(i < ntp)
    def _():
        op_ref[...] = out

    @pl.when(i >= ntp)
    def _():
        os_ref[...] = out


def _combine(dest, y_tiles, wts, shared, xp1, xs1, g2_rows, final_g):
    tp, d = xp1.shape
    ts = xs1.shape[0]
    nsub = _token_rows(d)
    tt = _row_tile(ts, (256,))
    assert tp % tt == 0 and tt % (SUBLANES * MOD_ROWS) == 0
    ntp = tp // tt
    mr = tt // MOD_ROWS
    nt = (tp + ts) // tt
    d3 = dest.reshape(nt, 1, tt * TOP_K)
    kern = functools.partial(_combine_kernel, tt=tt, nsub=nsub, ntp=ntp)
    smem_blk = lambda f: pl.BlockSpec((1, 1, tt * TOP_K), f, memory_space=pltpu.SMEM)
    row = lambda i: (i, 0)
    two = _two_part_specs(tt, d, ntp)
    return pl.pallas_call(
        kern,
        grid=(nt,),
        in_specs=[smem_blk(lambda i: (i, 0, 0)),
                  smem_blk(lambda i: (jnp.minimum(i + 1, nt - 1), 0, 0)),
                  pl.BlockSpec(memory_space=pl.ANY),
                  pl.BlockSpec((tt, LANES), row), pl.BlockSpec((tt, d), row)] + two + [
                  pl.BlockSpec((mr, d), row), pl.BlockSpec((1, d), lambda i: (0, 0))],
        out_specs=two,
        out_shape=[jax.ShapeDtypeStruct((tp, d), F32), jax.ShapeDtypeStruct((ts, d), F32)],
        scratch_shapes=[pltpu.VMEM((2, TOP_K, tt * nsub, LANES), ROW_DTYPE),
                        pltpu.SemaphoreType.DMA((2,))],
        compiler_params=pltpu.CompilerParams(
            dimension_semantics=("arbitrary",), vmem_limit_bytes=VMEM_LIMIT),
        name="combine",
    )(d3, d3, y_tiles, wts, shared, xp1, xs1, g2_rows, final_g.reshape(1, d))


def _plan(counts, n_assign):
    ne = counts.shape[0]
    bm = EXPERT_ROWS
    counts = counts.astype(jnp.int32)
    nblk = (counts + bm - 1) // bm
    bend = jnp.cumsum(nblk)
    pstart = jnp.broadcast_to(((bend - nblk) * bm).astype(F32).reshape(ne, 1), (ne, LANES))
    nb = -(-(n_assign + ne * (bm - 1)) // bm)
    n_used = bend[-1:].astype(jnp.int32)
    pad_start = ((bend - nblk) * bm + counts).astype(jnp.int32)
    pad_len = (nblk * bm - counts).astype(jnp.int32)
    return pstart, (bend - nblk).astype(jnp.int32), nblk, n_used, pad_start, pad_len, nb


def per_g_pow2(ne):
    per_g = ne // N_EXPERT_GROUPS
    return per_g * N_EXPERT_GROUPS == ne and per_g & (per_g - 1) == 0


def _mod_rows(m, nbp, lp):
    return jnp.concatenate([jnp.repeat(m[:nbp], lp // MOD_ROWS, axis=0), m[nbp:]], axis=0)


def kernel(x_prompt, x_sample, state_ssm, state_conv, state_pool, c_prompt, c_sample, ln1_g, ln2_g, w_ada, b_ada, w_in, conv_w, conv_b, dt_bias, a_log, d_skip, ssd_norm_g, w_ssd_out, pool_w, pool_scale, w_out, w_router, router_bias, moe_w_gate, moe_w_up, moe_w_down, shared_w_gate, shared_w_up, shared_w_down, final_g):
    bp, lp, dm = x_prompt.shape
    bs, ls, _ = x_sample.shape
    depth = ln1_g.shape[0]
    heads, hd, ns = state_ssm.shape[2], state_ssm.shape[3], state_ssm.shape[4]
    di = heads * hd
    gn = SSD_GROUPS * ns
    cch = di + 2 * gn
    assert depth == 1 and ls == MOD_ROWS and lp % MOD_ROWS == 0 and heads <= LANES
    assert per_g_pow2(w_router.shape[2])
    assert ns == 2 * hd and hd & (hd - 1) == 0 and (heads // SSD_GROUPS) % 2 == 0
    assert all(w & (w - 1) == 0 for w in POOL_WINDOWS) and state_pool.shape[2] == max(POOL_WINDOWS) - 1
    tp, ts = bp * lp, bs * ls
    expand3 = (jnp.arange(3 * LANES)[:, None] % LANES == jnp.arange(di)[None, :] // hd).astype(BF16)

    xp, xs = x_prompt.reshape(tp, dm), x_sample.reshape(ts, dm)
    c_all = jnp.concatenate([c_prompt, c_sample], axis=0)

    o1, o2, o3, o4 = di, di + cch, di + cch + heads, di + cch + heads + dm
    cols = (0, di, 2 * di, di + cch, di + cch + dm)

    ssm_p, conv_p, pool_p, ssm_s, conv_s, pool_s = [], [], [], [], [], []
    for l in range(depth):
        wi = w_in[l]
        wcat = jnp.concatenate(
            [wi[:, :o1], wi[:, o1:o2], wi[:, o3:o4], wi[:, o4:],
             jnp.pad(wi[:, o2:o3], ((0, 0), (0, LANES - heads)))], axis=1).astype(BF16)
        pad_h = lambda v: jnp.pad(v.reshape(1, heads), ((0, 0), (0, LANES - heads)))
        seq_w = (conv_w[l], conv_b[l].reshape(1, cch), pad_h(dt_bias[l]), pad_h(a_log[l]),
                 jnp.repeat(d_skip[l], hd).reshape(1, di), ssd_norm_g[l].reshape(1, di),
                 w_ssd_out[l].astype(BF16), pool_w[l].astype(BF16), pool_scale[l].reshape(1, dm),
                 w_out[l].astype(BF16), expand3)

        mod = _ada(c_all, w_ada[l], b_ada[l])
        sh1, sc1, g1, sh2, sc2, g2 = jnp.split(mod, 6, axis=-1)

        proj, dtp = _inproj(xp, xs, _mod_rows(sc1, bp, lp), _mod_rows(sh1, bp, lp), ln1_g[l], wcat)

        zc = jnp.zeros((bp,) + state_conv.shape[2:], F32)
        zs = jnp.zeros((bp, heads, hd, ns), F32)
        zp = jnp.zeros((bp,) + state_pool.shape[2:], F32)
        xp1, ns_p, nc_p, np_p = _seq(proj, dtp, xp, g1[:bp], zc, zs, zp, seq_w,
                                     row0=0, nb=bp, seqlen=lp, pos0=0, cols=cols)
        xs1, ns_s, nc_s, np_s = _seq(proj, dtp, xs, g1[bp:], state_conv[l], state_ssm[l], state_pool[l],
                                     seq_w, row0=tp, nb=bs, seqlen=ls, pos0=PAST_LEN, cols=cols)
        conv_p.append(nc_p)
        conv_s.append(nc_s)
        pool_p.append(np_p)
        pool_s.append(np_s)
        ssm_p.append(ns_p)
        ssm_s.append(ns_s)

        h2_tiles, shared, eidx_t, wts_t, mask_t, counts = _moe_pre(
            xp1, xs1, _mod_rows(sc2, bp, lp), _mod_rows(sh2, bp, lp), ln2_g[l],
            w_router[l].astype(BF16), router_bias[l],
            shared_w_gate[l].astype(BF16), shared_w_up[l].astype(BF16), shared_w_down[l].astype(BF16))
        nsub = _token_rows(dm)
        pstart, blk_start, blk_count, n_used, pad_start, pad_len, nb = _plan(counts[:, 0], (tp + ts) * TOP_K)
        dest = _rank(mask_t, eidx_t, pstart).T
        wts = jnp.pad(wts_t.T, ((0, 0), (0, LANES - TOP_K)))
        x_sorted = _dispatch(dest, h2_tiles, pad_start, pad_len, n_used, nb, nsub)
        y_tiles = _grouped(x_sorted, blk_start, blk_count, n_used,
                           moe_w_gate[l], moe_w_up[l], moe_w_down[l])
        xp, xs = _combine(dest, y_tiles, wts, shared, xp1, xs1, _mod_rows(g2, bp, lp), final_g)

    y_prompt = xp.reshape(bp, lp, dm)
    y_sample = xs.reshape(bs, ls, dm)
    return (y_prompt, y_sample, jnp.stack(ssm_p), jnp.stack(conv_p), jnp.stack(pool_p),
            jnp.stack(ssm_s), jnp.stack(conv_s), jnp.stack(pool_s))
```

```python
import functools

import jax
import jax.numpy as jnp
from jax import lax
from jax.experimental import pallas as pl
from jax.experimental.pallas import tpu as pltpu

F32 = jnp.float32
BF16 = jnp.bfloat16
HIGHEST = lax.Precision.HIGHEST

RMS_EPS = 1e-6
SSD_NORM_EPS = 1e-5
CHUNK = 64
SSD_GROUPS = 4
POOL_WINDOWS = (2, 4, 8, 16)
PAST_LEN = 1024
TOP_K = 8
N_EXPERT_GROUPS = 8
TOPK_GROUPS = 4
ROUTED_SCALE = 2.5

LANES = 128
SUBLANES = 8
MOD_ROWS = 32
VMEM_LIMIT = 56 * 1024 * 1024
EXPERT_ROWS = 128
SEQ_TILE = 256
GROUP_RING = 4
ROW_UNROLL = 2


def _sigmoid(x):
    return 0.5 * jnp.tanh(0.5 * x) + 0.5


def _silu(x):
    h = 0.5 * x
    return h + h * jnp.tanh(h)


def _softplus(x):
    return jnp.maximum(x, 0.0) + jnp.log1p(jnp.exp(-jnp.abs(x)))


def _row_tile(n, prefs):
    for t in prefs:
        if n % t == 0:
            return t
    return n


def _modulate(y, sc_ref, sh_ref):
    rows, d = y.shape
    y3 = y.reshape(rows // MOD_ROWS, MOD_ROWS, d)
    y3 = y3 * (1.0 + sc_ref[...][:, None, :]) + sh_ref[...][:, None, :]
    return y3.reshape(rows, d)


def _rms(x, g, eps):
    return x * lax.rsqrt(jnp.mean(x * x, axis=-1, keepdims=True) + eps) * g


def _ada_kernel(c_ref, w_ref, b_ref, o_ref):
    s = _silu(c_ref[...])
    o_ref[...] = jnp.dot(s, w_ref[...], preferred_element_type=F32, precision=HIGHEST) + b_ref[...]


def _ada(c_all, w_ada, b_ada):
    n, d = c_all.shape
    dout = w_ada.shape[1]
    tn = _row_tile(dout, (1024, 512, 256, 128))
    return pl.pallas_call(
        _ada_kernel,
        grid=(dout // tn,),
        in_specs=[pl.BlockSpec((n, d), lambda j: (0, 0)),
                  pl.BlockSpec((d, tn), lambda j: (0, j)),
                  pl.BlockSpec((1, tn), lambda j: (0, j))],
        out_specs=pl.BlockSpec((n, tn), lambda j: (0, j)),
        out_shape=jax.ShapeDtypeStruct((n, dout), F32),
        name="ada",
    )(c_all, w_ada, b_ada.reshape(1, dout))


def _two_part_specs(tm, d, ntp):
    return [pl.BlockSpec((tm, d), lambda i, *_: (jnp.minimum(i, ntp - 1), 0)),
            pl.BlockSpec((tm, d), lambda i, *_: (jnp.maximum(i - ntp, 0), 0))]


def _inproj_kernel(xp_ref, xs_ref, sc_ref, sh_ref, g_ref, w_ref, o_ref, dt_ref, *, ntp, tn):
    x = jnp.where(pl.program_id(0) < ntp, xp_ref[...], xs_ref[...])
    h = _modulate(_rms(x, g_ref[...], RMS_EPS), sc_ref, sh_ref).astype(BF16)
    n_main = o_ref.shape[1]
    for c0 in range(0, n_main, tn):
        o_ref[:, c0:c0 + tn] = jnp.dot(h, w_ref[:, c0:c0 + tn], preferred_element_type=F32).astype(BF16)
    dt_ref[...] = jnp.dot(h, w_ref[:, n_main:], preferred_element_type=F32)


def _inproj(xp, xs, sc_rows, sh_rows, g, wcat):
    tp, d = xp.shape
    ts = xs.shape[0]
    n = wcat.shape[1]
    n_main = n - LANES
    tm = _row_tile(ts, (512, 256))
    assert tp % tm == 0 and tm % (SUBLANES * MOD_ROWS) == 0
    ntp = tp // tm
    nt = ntp + ts // tm
    mr = tm // MOD_ROWS
    tn = _row_tile(n_main, (2048, 1024, 512, 256, 128))
    row = lambda i: (i, 0)
    return pl.pallas_call(
        functools.partial(_inproj_kernel, ntp=ntp, tn=tn),
        grid=(nt,),
        in_specs=_two_part_specs(tm, d, ntp) + [
            pl.BlockSpec((mr, d), row), pl.BlockSpec((mr, d), row),
            pl.BlockSpec((1, d), lambda i: (0, 0)),
            pl.BlockSpec((d, n), lambda i: (0, 0), pipeline_mode=pl.Buffered(1))],
        out_specs=[pl.BlockSpec((tm, n_main), row), pl.BlockSpec((tm, LANES), row)],
        out_shape=[jax.ShapeDtypeStruct((tp + ts, n_main), BF16),
                   jax.ShapeDtypeStruct((tp + ts, LANES), F32)],
        compiler_params=pltpu.CompilerParams(
            dimension_semantics=("parallel",), vmem_limit_bytes=VMEM_LIMIT),
        name="inproj",
    )(xp, xs, sc_rows, sh_rows, g.reshape(1, d), wcat)


def _seq_kernel(z_ref, xp_ref, bc_ref, u_ref, gt_ref, dt_ref, x_ref, g1_ref,
                cst_ref, sst_ref, pst_ref, cw_ref, cb_ref, dtb_ref, alog_ref, dx_ref, ng_ref,
                wssd_ref, pw_ref, ps_ref, wout_ref, e3_ref,
                xo_ref, so_ref, co_ref, po_ref,
                cbuf, pbuf, st_scr, y_scr, xbc_scr, csx_scr, wvx_scr,
                *, tq, q, pos0, heads, hd, ns, dm):
    j = pl.program_id(1)
    di = heads * hd
    gn = SSD_GROUPS * ns
    hpg = heads // SSD_GROUPS
    cw = cbuf.shape[0] - tq
    kw = cw_ref.shape[0]
    ph = pbuf.shape[0] - tq

    @pl.when(j == 0)
    def _():
        cbuf[0:cw - (kw - 1), :] = jnp.zeros((cw - (kw - 1), cbuf.shape[1]), F32)
        cbuf[cw - (kw - 1):cw, :] = cst_ref[0]
        pbuf[0:1, :] = jnp.zeros((1, dm), F32)
        pbuf[1:ph, :] = pst_ref[0]
        st_scr[...] = sst_ref[0].T

    cbuf[cw:cw + tq, 0:di] = xp_ref[...].astype(F32)
    cbuf[cw:cw + tq, di:di + 2 * gn] = bc_ref[...].astype(F32)
    ext = cbuf[...]
    acc = cb_ref[...] + ext[cw:cw + tq, :] * cw_ref[kw - 1:kw, :]
    for k in range(kw - 1):
        acc = acc + pltpu.roll(ext, kw - 1 - k, axis=0)[cw:cw + tq, :] * cw_ref[k:k + 1, :]
    xbc_scr[...] = _silu(acc)
    cbuf[cw - (kw - 1):cw, :] = cbuf[cw + tq - (kw - 1):cw + tq, :]

    dt = _softplus(dt_ref[...] + dtb_ref[...])
    dta = dt * (-jnp.exp(alog_ref[...]))
    lq = q.bit_length() - 1
    r_i = lax.broadcasted_iota(jnp.int32, (tq, tq), 0)
    c_i = lax.broadcasted_iota(jnp.int32, (tq, tq), 1)
    same = lax.shift_right_logical(r_i, lq) == lax.shift_right_logical(c_i, lq)
    tril = jnp.where(same, (r_i >= c_i).astype(F32), 0.0)
    cs = jnp.dot(tril, dta, preferred_element_type=F32, precision=HIGHEST)
    cs_end = jnp.dot(same.astype(F32), dta, preferred_element_type=F32, precision=HIGHEST)
    wv = dt * jnp.exp(cs_end - cs)

    both = jnp.concatenate([cs, wv], axis=0)
    hi = both.astype(BF16)
    r1 = both - hi.astype(F32)
    mid = r1.astype(BF16)
    lo = (r1 - mid.astype(F32)).astype(BF16)
    ex = jnp.dot(jnp.concatenate([hi, mid, lo], axis=1), e3_ref[...], preferred_element_type=F32)
    csx_scr[...] = ex[0:tq]
    wvx_scr[...] = ex[tq:2 * tq]
    cs_t = cs.T
    dt_t = dt.T

    lane = lax.broadcasted_iota(jnp.int32, (q, 2 * hd), 1)
    kpos = jnp.bitwise_and(lane, hd - 1)
    causal2 = jnp.logical_and(lax.broadcasted_iota(jnp.int32, (q, 2 * hd), 0) >= kpos, kpos < q)
    first = lane < hd
    zrow = jnp.zeros((1, hd - q), F32)
    zblk = jnp.zeros((hd - q, 2 * hd), BF16)

    def pair_row(t, p, r0):
        parts = []
        for h in (2 * p, 2 * p + 1):
            parts.append(t[h:h + 1, r0:r0 + q])
            if q < hd:
                parts.append(zrow)
        return jnp.concatenate(parts, axis=1)

    def pair_rows(a, b):
        blocks = [a, zblk, b, zblk] if q < hd else [a, b]
        return jnp.concatenate(blocks, axis=0)

    gw = di // SSD_GROUPS
    for c in range(tq // q):
        r0 = c * q
        for g in range(SSD_GROUPS):
            bgb = xbc_scr[r0:r0 + q, di + g * ns:di + (g + 1) * ns].astype(BF16)
            cgb = xbc_scr[r0:r0 + q, di + gn + g * ns:di + gn + (g + 1) * ns].astype(BF16)
            cb2 = lax.dot_general(cgb, pair_rows(bgb, bgb), (((1,), (1,)), ((), ())),
                                  preferred_element_type=F32)
            stg = st_scr[:, g * gw:(g + 1) * gw]
            ecs = jnp.exp(csx_scr[r0:r0 + q, g * gw:(g + 1) * gw])
            yo = jnp.dot(cgb, stg.astype(BF16), preferred_element_type=F32) * ecs
            for pp in range(hpg // 2):
                p = g * (hpg // 2) + pp
                lo_l = p * 2 * hd
                seg = csx_scr[r0:r0 + q, lo_l:lo_l + 2 * hd] - pair_row(cs_t, p, r0)
                lm = jnp.exp(jnp.where(causal2, seg, -jnp.inf))
                m2 = (cb2 * lm * pair_row(dt_t, p, r0)).astype(BF16)
                xpair = xbc_scr[r0:r0 + q, lo_l:lo_l + 2 * hd]
                rhs = pair_rows(jnp.where(first, xpair, 0.0).astype(BF16),
                                jnp.where(first, 0.0, xpair).astype(BF16))
                yd = jnp.dot(m2, rhs, preferred_element_type=F32)
                y_scr[r0:r0 + q, lo_l:lo_l + 2 * hd] = yd + yo[:, pp * 2 * hd:(pp + 1) * 2 * hd]
            xw = (xbc_scr[r0:r0 + q, g * gw:(g + 1) * gw]
                  * wvx_scr[r0:r0 + q, g * gw:(g + 1) * gw]).astype(BF16)
            upd = lax.dot_general(bgb, xw, (((0,), (0,)), ((), ())), preferred_element_type=F32)
            st_scr[:, g * gw:(g + 1) * gw] = stg * ecs[q - 1:q, :] + upd

    @pl.when(j == pl.num_programs(1) - 1)
    def _():
        so_ref[0] = st_scr[...].T
        co_ref[0] = cbuf[cw - (kw - 1):cw, :]

    y = y_scr[...] + xbc_scr[:, 0:di] * dx_ref[...]
    v = y * _silu(z_ref[...].astype(F32))
    gw = di // SSD_GROUPS
    parts = []
    for g in range(SSD_GROUPS):
        vg = v[:, g * gw:(g + 1) * gw]
        parts.append(vg * lax.rsqrt(jnp.mean(vg * vg, axis=-1, keepdims=True) + SSD_NORM_EPS))
    yn = (jnp.concatenate(parts, axis=-1) * ng_ref[...]).astype(BF16)
    a_br = jnp.dot(yn, wssd_ref[...], preferred_element_type=F32)

    pbuf[ph:ph + tq, :] = u_ref[...].astype(F32)
    pos = pos0 + j * tq + lax.broadcasted_iota(jnp.int32, (tq, 1), 0)
    pgd = dm // len(POOL_WINDOWS)
    b_parts = []
    for gi, w in enumerate(POOL_WINDOWS):
        lo = gi * pgd
        s = pbuf[:, lo:lo + pgd]
        step = 1
        while step < w:
            s = s + pltpu.roll(s, step, axis=0)
            step *= 2
        cnt = jnp.minimum(w, pos + 1).astype(F32)
        pooled = s[ph:ph + tq, :] / cnt - pbuf[ph:ph + tq, lo:lo + pgd]
        b_parts.append(jnp.dot(pooled.astype(BF16), pw_ref[gi], preferred_element_type=F32))
    b_br = jnp.concatenate(b_parts, axis=-1) * ps_ref[...]
    pbuf[1:ph, :] = pbuf[tq + 1:tq + ph, :]

    @pl.when(j == pl.num_programs(1) - 1)
    def _():
        po_ref[0] = pbuf[1:ph, :]

    gates = _sigmoid(gt_ref[...].astype(F32))
    merged = gates[:, 0:dm] * a_br + gates[:, dm:2 * dm] * b_br
    mix = jnp.dot(merged.astype(BF16), wout_ref[...], preferred_element_type=F32)
    xo_ref[...] = x_ref[...] + g1_ref[0] * mix


def _seq(proj, dtp, x, g1, conv_state, ssm_state, pool_state, wts, *, row0, nb, seqlen, pos0, cols):
    dm = x.shape[1]
    heads, hd, ns = ssm_state.shape[1], ssm_state.shape[2], ssm_state.shape[3]
    di = heads * hd
    gn = SSD_GROUPS * ns
    q = min(CHUNK, seqlen)
    tq = _row_tile(seqlen, (SEQ_TILE, 128, 64, 32))
    nj = seqlen // tq
    rb0 = row0 // tq
    oz, ox, obc, ou, og = cols

    def rows(b, j):
        return rb0 + b * nj + j

    full = lambda a: pl.BlockSpec(a.shape, lambda b, j: (0,) * a.ndim, pipeline_mode=pl.Buffered(1))
    kern = functools.partial(_seq_kernel, tq=tq, q=q, pos0=pos0, heads=heads, hd=hd, ns=ns, dm=dm)
    kw1 = conv_state.shape[1]
    pst = pool_state.shape[1]
    xo, so, co, po = pl.pallas_call(
        kern,
        grid=(nb, nj),
        in_specs=[
            pl.BlockSpec((tq, di), lambda b, j: (rows(b, j), oz // di)),
            pl.BlockSpec((tq, di), lambda b, j: (rows(b, j), ox // di)),
            pl.BlockSpec((tq, 2 * gn), lambda b, j: (rows(b, j), obc // (2 * gn))),
            pl.BlockSpec((tq, dm), lambda b, j: (rows(b, j), ou // dm)),
            pl.BlockSpec((tq, 2 * dm), lambda b, j: (rows(b, j), og // (2 * dm))),
            pl.BlockSpec((tq, LANES), lambda b, j: (rows(b, j), 0)),
            pl.BlockSpec((tq, dm), lambda b, j: (b * nj + j, 0)),
            pl.BlockSpec((1, 1, dm), lambda b, j: (b, 0, 0)),
            pl.BlockSpec((1, kw1, di + 2 * gn), lambda b, j: (b, 0, 0)),
            pl.BlockSpec((1, di, ns), lambda b, j: (b, 0, 0)),
            pl.BlockSpec((1, pst, dm), lambda b, j: (b, 0, 0)),
        ] + [full(w) for w in wts],
        out_specs=[pl.BlockSpec((tq, dm), lambda b, j: (b * nj + j, 0)),
                   pl.BlockSpec((1, di, ns), lambda b, j: (b, 0, 0)),
                   pl.BlockSpec((1, kw1, di + 2 * gn), lambda b, j: (b, 0, 0)),
                   pl.BlockSpec((1, pst, dm), lambda b, j: (b, 0, 0))],
        out_shape=[jax.ShapeDtypeStruct((nb * seqlen, dm), F32),
                   jax.ShapeDtypeStruct((nb, di, ns), F32),
                   jax.ShapeDtypeStruct((nb, kw1, di + 2 * gn), F32),
                   jax.ShapeDtypeStruct((nb, pst, dm), F32)],
        scratch_shapes=[pltpu.VMEM((tq + SUBLANES, di + 2 * gn), F32),
                        pltpu.VMEM((tq + 2 * SUBLANES, dm), F32),
                        pltpu.VMEM((ns, di), F32),
                        pltpu.VMEM((tq, di), F32),
                        pltpu.VMEM((tq, di + 2 * gn), F32),
                        pltpu.VMEM((tq, di), F32),
                        pltpu.VMEM((tq, di), F32)],
        compiler_params=pltpu.CompilerParams(
            dimension_semantics=("parallel", "arbitrary"), vmem_limit_bytes=VMEM_LIMIT),
        name="seq",
    )(proj, proj, proj, proj, proj, dtp, x, g1.reshape(nb, 1, dm),
      conv_state, ssm_state.reshape(nb, di, ns), pool_state, *wts)
    return xo, so.reshape(nb, heads, hd, ns), co, po


def _moe_pre_kernel(xp_ref, xs_ref, sc_ref, sh_ref, g_ref, wr_ref, rb_ref, swg_ref, swu_ref, swd_ref,
                    h_ref, s_ref, e_ref, w_ref, m_ref, cnt_ref, *, ntp):
    x = jnp.where(pl.program_id(0) < ntp, xp_ref[...], xs_ref[...])
    y = _rms(x, g_ref[...], RMS_EPS)
    h = _modulate(y, sc_ref, sh_ref)
    _write_rows(h_ref, h)
    hb = h.astype(BF16)

    sg = jnp.dot(hb, swg_ref[...], preferred_element_type=F32)
    su = jnp.dot(hb, swu_ref[...], preferred_element_type=F32)
    s_ref[...] = jnp.dot((_silu(sg) * su).astype(BF16), swd_ref[...], preferred_element_type=F32)

    scores = jax.nn.sigmoid(lax.dot_general(wr_ref[...], hb, (((1,), (1,)), ((), ())),
                                            preferred_element_type=F32))
    biased = scores + rb_ref[:, 0:1]
    ne, tm = scores.shape
    per_g = ne // N_EXPERT_GROUPS
    neg = -jnp.inf
    row_e = lax.broadcasted_iota(jnp.int32, (ne, tm), 0).astype(F32)
    row_g = lax.broadcasted_iota(jnp.int32, (N_EXPERT_GROUPS, tm), 0).astype(F32)

    def first_argmax(v, idx, big):
        top = jnp.max(v, axis=0, keepdims=True)
        return top, jnp.min(jnp.where(v == top, idx, big), axis=0, keepdims=True)

    gs = jnp.full((N_EXPERT_GROUPS, tm), neg, F32)
    for g in range(N_EXPERT_GROUPS):
        mg = biased[g * per_g:(g + 1) * per_g, :]
        rg = (lax.broadcasted_iota(jnp.int32, (per_g, tm), 0) + g * per_g).astype(F32)
        t1, i1 = first_argmax(mg, rg, float(ne))
        t2 = jnp.max(jnp.where(rg == i1, neg, mg), axis=0, keepdims=True)
        gs = jnp.where(row_g == g, t1 + t2, gs)
    grp_e = lax.shift_right_logical(lax.broadcasted_iota(jnp.int32, (ne, tm), 0),
                                    per_g.bit_length() - 1).astype(F32)
    allowed = jnp.zeros((ne, tm), F32)
    for _ in range(TOPK_GROUPS):
        _, gi = first_argmax(gs, row_g, float(N_EXPERT_GROUPS))
        allowed = jnp.where(grp_e == gi, 1.0, allowed)
        gs = jnp.where(row_g == gi, neg, gs)
    mb = jnp.where(allowed > 0.0, biased, neg)
    row_k = lax.broadcasted_iota(jnp.int32, (TOP_K, tm), 0)
    eacc = jnp.zeros((TOP_K, tm), F32)
    wacc = jnp.zeros((TOP_K, tm), F32)
    chosen = jnp.zeros((ne, tm), F32)
    for k in range(TOP_K):
        _, ik = first_argmax(mb, row_e, float(ne))
        sel = row_e == ik
        wk = jnp.sum(jnp.where(sel, scores, 0.0), axis=0, keepdims=True)
        eacc = jnp.where(row_k == k, ik, eacc)
        wacc = jnp.where(row_k == k, wk, wacc)
        chosen = jnp.where(sel, 1.0, chosen)
        mb = jnp.where(sel, neg, mb)
    wsum = jnp.sum(wacc, axis=0, keepdims=True)
    e_ref[...] = eacc.astype(jnp.int32)
    w_ref[...] = wacc / (wsum + 1e-20) * ROUTED_SCALE
    chosen_b = chosen.astype(BF16)
    m_ref[...] = chosen_b

    @pl.when(pl.program_id(0) == 0)
    def _():
        cnt_ref[...] = jnp.zeros_like(cnt_ref)

    cnt_ref[...] += jnp.dot(chosen_b, jnp.ones((tm, LANES), BF16), preferred_element_type=F32)


def _moe_pre(xp1, xs1, sc_rows, sh_rows, g, wr, rb, swg, swu, swd):
    tp, d = xp1.shape
    ts = xs1.shape[0]
    t = tp + ts
    ne = wr.shape[1]
    ff = swg.shape[1]
    tm = _row_tile(ts, (256,))
    assert tp % tm == 0 and tm % (SUBLANES * MOD_ROWS) == 0
    ntp = tp // tm
    mr = tm // MOD_ROWS
    nsub = _token_rows(d)
    row = lambda i: (i, 0)
    col = lambda i: (0, i)
    const = lambda i: (0, 0)
    rb_col = jnp.broadcast_to(rb.reshape(ne, 1), (ne, LANES))
    return pl.pallas_call(
        functools.partial(_moe_pre_kernel, ntp=ntp),
        grid=(t // tm,),
        in_specs=_two_part_specs(tm, d, ntp) + [
                  pl.BlockSpec((mr, d), row), pl.BlockSpec((mr, d), row),
                  pl.BlockSpec((1, d), const), pl.BlockSpec((ne, d), const), pl.BlockSpec((ne, LANES), const),
                  pl.BlockSpec((d, ff), const), pl.BlockSpec((d, ff), const), pl.BlockSpec((ff, d), const)],
        out_specs=[pl.BlockSpec((tm * nsub, LANES), row), pl.BlockSpec((tm, d), row),
                   pl.BlockSpec((TOP_K, tm), col), pl.BlockSpec((TOP_K, tm), col),
                   pl.BlockSpec((ne, tm), col), pl.BlockSpec((ne, LANES), const)],
        out_shape=[jax.ShapeDtypeStruct((t * nsub, LANES), ROW_DTYPE), jax.ShapeDtypeStruct((t, d), F32),
                   jax.ShapeDtypeStruct((TOP_K, t), jnp.int32), jax.ShapeDtypeStruct((TOP_K, t), F32),
                   jax.ShapeDtypeStruct((ne, t), BF16), jax.ShapeDtypeStruct((ne, LANES), F32)],
        compiler_params=pltpu.CompilerParams(
            dimension_semantics=("arbitrary",), vmem_limit_bytes=VMEM_LIMIT),
        name="moe_pre",
    )(xp1, xs1, sc_rows, sh_rows, g.reshape(1, d), wr.T, rb_col, swg, swu, swd)


ROW_DTYPE = F32


def _token_rows(d):
    return d // LANES


def _read_rows(buf, nrows, nsub):
    return jnp.concatenate([buf[pl.ds(c, nrows, stride=nsub), :] for c in range(nsub)], axis=-1)


def _write_rows(ref, v):
    nrows, width = v.shape
    nsub = _token_rows(width)
    for c in range(nsub):
        ref[pl.ds(c, nrows, stride=nsub), :] = v[:, c * LANES:(c + 1) * LANES]


def _rank_kernel(m_ref, e_ref, ps_ref, d_ref, carry):
    i = pl.program_id(0)

    @pl.when(i == 0)
    def _():
        carry[...] = jnp.zeros_like(carry)

    m = m_ref[...]
    ne, tr = m.shape
    r_i = lax.broadcasted_iota(jnp.int32, (tr, tr), 0)
    c_i = lax.broadcasted_iota(jnp.int32, (tr, tr), 1)
    earlier = (r_i < c_i).astype(BF16)
    base = carry[...] + ps_ref[...]
    slot = jnp.dot(m, earlier, preferred_element_type=F32) + jnp.concatenate([base] * (tr // LANES), axis=1)
    row_e = lax.broadcasted_iota(jnp.int32, (ne, tr), 0)
    d_rows = [jnp.sum(jnp.where(row_e == e_ref[k:k + 1, :], slot, 0.0), axis=0, keepdims=True)
              for k in range(TOP_K)]
    d_ref[...] = jnp.concatenate(d_rows, axis=0).astype(jnp.int32)
    carry[...] += jnp.dot(m, jnp.ones((tr, LANES), BF16), preferred_element_type=F32)


def _rank(mask_t, eidx_t, pstart):
    ne, t = mask_t.shape
    tr = _row_tile(t, (512, 256))
    col = lambda i: (0, i)
    return pl.pallas_call(
        _rank_kernel,
        grid=(t // tr,),
        in_specs=[pl.BlockSpec((ne, tr), col), pl.BlockSpec((TOP_K, tr), col),
                  pl.BlockSpec((ne, LANES), lambda i: (0, 0))],
        out_specs=pl.BlockSpec((TOP_K, tr), col),
        out_shape=jax.ShapeDtypeStruct((TOP_K, t), jnp.int32),
        scratch_shapes=[pltpu.VMEM((ne, LANES), F32)],
        compiler_params=pltpu.CompilerParams(
            dimension_semantics=("arbitrary",), vmem_limit_bytes=VMEM_LIMIT),
        name="rank",
    )(mask_t, eidx_t, pstart)


def _dispatch_kernel(ps_ref, pl_ref, nu_ref, d_ref, h_hbm, xs_hbm, zbuf, stage, lsem, sem, zsem,
                     *, tt, nsub, bm, nb):
    i = pl.program_id(0)
    n = pl.num_programs(0)
    ne = ps_ref.shape[0]
    pieces = [1 << b for b in reversed(range(bm.bit_length() - 1))]

    def zero_fill(op):
        def pad(e, carry):
            start = ps_ref[e]
            length = pl_ref[e]
            for p in pieces:
                off = length - lax.rem(length, 2 * p)

                @pl.when(lax.rem(length, 2 * p) >= p)
                def _():
                    dst = xs_hbm.at[pl.ds(pl.multiple_of((start + off) * nsub, nsub), p * nsub)]
                    op(pltpu.make_async_copy(zbuf.at[pl.ds(0, p * nsub)], dst, zsem))
            return carry
        lax.fori_loop(0, ne, pad, 0)

        def tail(b, carry):
            dst = xs_hbm.at[pl.ds(pl.multiple_of(b * (bm * nsub), bm * nsub), bm * nsub)]
            op(pltpu.make_async_copy(zbuf, dst, zsem))
            return carry
        lax.fori_loop(nu_ref[0], nb, tail, 0)

    stages = stage.shape[0]
    rows = tt * nsub

    def load(tile):
        src = h_hbm.at[pl.ds(pl.multiple_of(tile * rows, rows), rows)]
        return pltpu.make_async_copy(src, stage.at[lax.rem(tile, stages)], lsem.at[lax.rem(tile, stages)])

    def wait_rows(tile):
        s = lax.rem(tile, stages)
        for _ in range(TOP_K):
            pltpu.make_async_copy(stage.at[s], xs_hbm.at[pl.ds(0, rows)], sem.at[s]).wait()

    @pl.when(i == 0)
    def _():
        zbuf[...] = jnp.zeros_like(zbuf)
        zero_fill(lambda c: c.start())
        load(0).start()

    @pl.when(i >= stages - 1)
    def _():
        wait_rows(i - (stages - 1))

    @pl.when(i + 1 < n)
    def _():
        load(i + 1).start()

    load(i).wait()
    s = lax.rem(i, stages)

    def body(r2, carry):
        for u in range(ROW_UNROLL):
            r = r2 * ROW_UNROLL + u
            src = stage.at[s, pl.ds(pl.multiple_of(r * nsub, nsub), nsub)]
            for k in range(TOP_K):
                dst = xs_hbm.at[pl.ds(pl.multiple_of(d_ref[0, 0, r * TOP_K + k] * nsub, nsub), nsub)]
                pltpu.make_async_copy(src, dst, sem.at[s]).start(priority=k % 2)
        return carry

    lax.fori_loop(0, tt // ROW_UNROLL, body, 0)

    @pl.when(i == n - 1)
    def _():
        for back in range(stages - 2, -1, -1):
            @pl.when(i >= back)
            def _():
                wait_rows(i - back)
        zero_fill(lambda c: c.wait())


def _dispatch(dest, h2_tiles, pad_start, pad_len, n_used, nb, nsub):
    t = dest.shape[0]
    tt = _row_tile(t, (256,))
    nt = t // tt
    bm = EXPERT_ROWS
    d3 = dest.reshape(nt, 1, tt * TOP_K)
    gs = pltpu.PrefetchScalarGridSpec(
        num_scalar_prefetch=3,
        grid=(nt,),
        in_specs=[pl.BlockSpec((1, 1, tt * TOP_K), lambda i, *_: (i, 0, 0), memory_space=pltpu.SMEM),
                  pl.BlockSpec(memory_space=pl.ANY)],
        out_specs=pl.BlockSpec(memory_space=pl.ANY),
        scratch_shapes=[pltpu.VMEM((bm * nsub, LANES), ROW_DTYPE),
                        pltpu.VMEM((3, tt * nsub, LANES), ROW_DTYPE),
                        pltpu.SemaphoreType.DMA((3,)), pltpu.SemaphoreType.DMA((3,)),
                        pltpu.SemaphoreType.DMA],
    )
    return pl.pallas_call(
        functools.partial(_dispatch_kernel, tt=tt, nsub=nsub, bm=bm, nb=nb),
        grid_spec=gs,
        out_shape=jax.ShapeDtypeStruct((nb * bm * nsub, LANES), ROW_DTYPE),
        compiler_params=pltpu.CompilerParams(dimension_semantics=("arbitrary",)),
        name="dispatch",
    )(pad_start, pad_len, n_used, d3, h2_tiles)


def _grouped_kernel(b0_ref, nk_ref, nu_ref, x_hbm, wg_ref, wu_ref, wd_ref, y_hbm,
                    xbuf, ybuf, xsem, ysem, wgb, wub, wdb, *, bm, nsub, nb):
    e = pl.program_id(0)
    ring = xbuf.shape[0]
    rows = bm * nsub
    nu = nu_ref[0]

    def x_copy(gb, slot):
        src = x_hbm.at[pl.ds(pl.multiple_of(gb * rows, rows), rows)]
        return pltpu.make_async_copy(src, xbuf.at[slot], xsem.at[slot])

    def y_copy(gb, slot):
        dst = y_hbm.at[pl.ds(pl.multiple_of(gb * rows, rows), rows)]
        return pltpu.make_async_copy(ybuf.at[slot], dst, ysem.at[slot])

    @pl.when(e == 0)
    def _():
        for g0 in range(ring - 1):
            @pl.when(g0 < nu)
            def _():
                x_copy(g0, g0).start()

    @pl.when(nk_ref[e] > 0)
    def _():
        wgb[...] = wg_ref[0].astype(BF16)
        wub[...] = wu_ref[0].astype(BF16)
        wdb[...] = wd_ref[0].astype(BF16)

        def block(b, carry):
            gb = b0_ref[e] + b
            slot = lax.rem(gb, ring)

            @pl.when(gb + ring - 1 < nu)
            def _():
                x_copy(gb + ring - 1, lax.rem(gb + ring - 1, ring)).start()

            x_copy(gb, slot).wait()

            @pl.when(gb >= ring)
            def _():
                y_copy(gb - ring, slot).wait()

            x = _read_rows(xbuf.at[slot], bm, nsub).astype(BF16)
            hg = jnp.dot(x, wgb[...], preferred_element_type=F32)
            hu = jnp.dot(x, wub[...], preferred_element_type=F32)
            y = jnp.dot((_silu(hg) * hu).astype(BF16), wdb[...], preferred_element_type=F32)
            _write_rows(ybuf.at[slot], y)
            y_copy(gb, slot).start()
            return carry

        lax.fori_loop(0, nk_ref[e], block, 0)

    @pl.when(e == pl.num_programs(0) - 1)
    def _():
        for back in range(ring, 0, -1):
            @pl.when(nu >= back)
            def _():
                y_copy(nu - back, lax.rem(nu - back, ring)).wait()

        ybuf[0] = jnp.zeros(ybuf.shape[1:], ybuf.dtype)

        def tail(gb, carry):
            y_copy(gb, 0).start()
            return carry

        def tail_wait(gb, carry):
            y_copy(gb, 0).wait()
            return carry

        lax.fori_loop(nu, nb, tail, 0)
        lax.fori_loop(nu, nb, tail_wait, 0)


def _grouped(xs, blk_start, blk_count, n_used, wg, wu, wd):
    ne, d, ff = wg.shape
    nsub = _token_rows(d)
    bm = EXPERT_ROWS
    nb = xs.shape[0] // (bm * nsub)
    wspec = lambda shape: pl.BlockSpec(shape, lambda e, *_: (e, 0, 0))
    gs = pltpu.PrefetchScalarGridSpec(
        num_scalar_prefetch=3,
        grid=(ne,),
        in_specs=[pl.BlockSpec(memory_space=pl.ANY),
                  wspec((1, d, ff)), wspec((1, d, ff)), wspec((1, ff, d))],
        out_specs=pl.BlockSpec(memory_space=pl.ANY),
        scratch_shapes=[pltpu.VMEM((GROUP_RING, bm * nsub, LANES), ROW_DTYPE),
                        pltpu.VMEM((GROUP_RING, bm * nsub, LANES), ROW_DTYPE),
                        pltpu.SemaphoreType.DMA((GROUP_RING,)), pltpu.SemaphoreType.DMA((GROUP_RING,)),
                        pltpu.VMEM((d, ff), BF16), pltpu.VMEM((d, ff), BF16), pltpu.VMEM((ff, d), BF16)],
    )
    return pl.pallas_call(
        functools.partial(_grouped_kernel, bm=bm, nsub=nsub, nb=nb),
        grid_spec=gs,
        out_shape=jax.ShapeDtypeStruct(xs.shape, xs.dtype),
        compiler_params=pltpu.CompilerParams(
            dimension_semantics=("arbitrary",), vmem_limit_bytes=VMEM_LIMIT),
        name="grouped",
    )(blk_start, blk_count, n_used, xs, wg, wu, wd)


def _combine_kernel(d_ref, dn_ref, y_hbm, w_ref, s_ref, xp_ref, xs_ref, g2_ref, fg_ref, op_ref, os_ref,
                    buf, obuf, sem, *, tt, nsub, ntp):
    i = pl.program_id(0)
    n = pl.num_programs(0)
    slot = lax.rem(i, 2)

    def issue(dref, to_slot, r):
        for k in range(TOP_K):
            src = y_hbm.at[pl.ds(pl.multiple_of(dref[0, 0, r * TOP_K + k] * nsub, nsub), nsub)]
            pltpu.make_async_copy(src, buf.at[to_slot, k, pl.ds(pl.multiple_of(r * nsub, nsub), nsub)],
                                  sem.at[to_slot]).start(priority=k % 2)

    def wait_tile(s):
        for k in range(TOP_K):
            pltpu.make_async_copy(y_hbm.at[pl.ds(0, tt * nsub)], buf.at[s, k], sem.at[s]).wait()

    @pl.when(i == 0)
    def _():
        def first(r, carry):
            issue(d_ref, 0, r)
            return carry
        lax.fori_loop(0, tt, first, 0)

    wait_tile(slot)
    is_prompt = i < ntp

    def chunk(c, carry):
        r0 = pl.multiple_of(c * SUBLANES, SUBLANES)
        for u in range(SUBLANES):
            issue(dn_ref, 1 - slot, r0 + u)
        acc = s_ref[pl.ds(r0, SUBLANES), :]
        w8 = w_ref[pl.ds(r0, SUBLANES), :]
        for k in range(TOP_K):
            rows = jnp.concatenate(
                [buf[slot, k, pl.ds(r0 * nsub + cc, SUBLANES, stride=nsub), :] for cc in range(nsub)], axis=-1)
            acc = acc + w8[:, k:k + 1] * rows
        g2 = g2_ref[pl.ds(c // (MOD_ROWS // SUBLANES), 1), :]
        x1 = jnp.where(is_prompt, xp_ref[pl.ds(r0, SUBLANES), :], xs_ref[pl.ds(r0, SUBLANES), :])
        obuf[pl.ds(r0, SUBLANES), :] = _rms(x1 + acc * g2, fg_ref[...], RMS_EPS)
        return carry

    lax.fori_loop(0, tt // SUBLANES, chunk, 0)

    @pl.when(is_prompt)
    def _():
        op_ref[...] = obuf[...]

    @pl.when(jnp.logical_not(is_prompt))
    def _():
        os_ref[...] = obuf[...]

    @pl.when(i == n - 1)
    def _():
        wait_tile(1 - slot)


def _combine(dest, y_tiles, wts, shared, xp1, xs1, g2_rows, final_g):
    tp, d = xp1.shape
    ts = xs1.shape[0]
    nsub = _token_rows(d)
    tt = _row_tile(ts, (256,))
    assert tp % tt == 0 and tt % (SUBLANES * MOD_ROWS) == 0
    ntp = tp // tt
    mr = tt // MOD_ROWS
    nt = (tp + ts) // tt
    d3 = dest.reshape(nt, 1, tt * TOP_K)
    kern = functools.partial(_combine_kernel, tt=tt, nsub=nsub, ntp=ntp)
    smem_blk = lambda f: pl.BlockSpec((1, 1, tt * TOP_K), f, memory_space=pltpu.SMEM)
    row = lambda i: (i, 0)
    two = _two_part_specs(tt, d, ntp)
    return pl.pallas_call(
        kern,
        grid=(nt,),
        in_specs=[smem_blk(lambda i: (i, 0, 0)),
                  smem_blk(lambda i: (jnp.minimum(i + 1, nt - 1), 0, 0)),
                  pl.BlockSpec(memory_space=pl.ANY),
                  pl.BlockSpec((tt, LANES), row), pl.BlockSpec((tt, d), row)] + two + [
                  pl.BlockSpec((mr, d), row), pl.BlockSpec((1, d), lambda i: (0, 0))],
        out_specs=two,
        out_shape=[jax.ShapeDtypeStruct((tp, d), F32), jax.ShapeDtypeStruct((ts, d), F32)],
        scratch_shapes=[pltpu.VMEM((2, TOP_K, tt * nsub, LANES), ROW_DTYPE), pltpu.VMEM((tt, d), F32),
                        pltpu.SemaphoreType.DMA((2,))],
        compiler_params=pltpu.CompilerParams(
            dimension_semantics=("arbitrary",), vmem_limit_bytes=VMEM_LIMIT),
        name="combine",
    )(d3, d3, y_tiles, wts, shared, xp1, xs1, g2_rows, final_g.reshape(1, d))


def _plan(counts, n_assign):
    ne = counts.shape[0]
    bm = EXPERT_ROWS
    counts = counts.astype(jnp.int32)
    nblk = (counts + bm - 1) // bm
    bend = jnp.cumsum(nblk)
    pstart = jnp.broadcast_to(((bend - nblk) * bm).astype(F32).reshape(ne, 1), (ne, LANES))
    nb = -(-(n_assign + ne * (bm - 1)) // bm)
    n_used = bend[-1:].astype(jnp.int32)
    pad_start = ((bend - nblk) * bm + counts).astype(jnp.int32)
    pad_len = (nblk * bm - counts).astype(jnp.int32)
    return pstart, (bend - nblk).astype(jnp.int32), nblk, n_used, pad_start, pad_len, nb


def per_g_pow2(ne):
    per_g = ne // N_EXPERT_GROUPS
    return per_g * N_EXPERT_GROUPS == ne and per_g & (per_g - 1) == 0


def _mod_rows(m, nbp, lp):
    return jnp.concatenate([jnp.repeat(m[:nbp], lp // MOD_ROWS, axis=0), m[nbp:]], axis=0)


def kernel(x_prompt, x_sample, state_ssm, state_conv, state_pool, c_prompt, c_sample, ln1_g, ln2_g, w_ada, b_ada, w_in, conv_w, conv_b, dt_bias, a_log, d_skip, ssd_norm_g, w_ssd_out, pool_w, pool_scale, w_out, w_router, router_bias, moe_w_gate, moe_w_up, moe_w_down, shared_w_gate, shared_w_up, shared_w_down, final_g):
    bp, lp, dm = x_prompt.shape
    bs, ls, _ = x_sample.shape
    depth = ln1_g.shape[0]
    heads, hd, ns = state_ssm.shape[2], state_ssm.shape[3], state_ssm.shape[4]
    di = heads * hd
    gn = SSD_GROUPS * ns
    cch = di + 2 * gn
    assert depth == 1 and ls == MOD_ROWS and lp % MOD_ROWS == 0 and heads <= LANES
    assert per_g_pow2(w_router.shape[2])
    assert ns == 2 * hd and hd & (hd - 1) == 0 and (heads // SSD_GROUPS) % 2 == 0
    assert all(w & (w - 1) == 0 for w in POOL_WINDOWS) and state_pool.shape[2] == max(POOL_WINDOWS) - 1
    tp, ts = bp * lp, bs * ls
    expand3 = (jnp.arange(3 * LANES)[:, None] % LANES == jnp.arange(di)[None, :] // hd).astype(BF16)

    xp, xs = x_prompt.reshape(tp, dm), x_sample.reshape(ts, dm)
    c_all = jnp.concatenate([c_prompt, c_sample], axis=0)

    o1, o2, o3, o4 = di, di + cch, di + cch + heads, di + cch + heads + dm
    cols = (0, di, 2 * di, di + cch, di + cch + dm)

    ssm_p, conv_p, pool_p, ssm_s, conv_s, pool_s = [], [], [], [], [], []
    for l in range(depth):
        wi = w_in[l]
        wcat = jnp.concatenate(
            [wi[:, :o1], wi[:, o1:o2], wi[:, o3:o4], wi[:, o4:],
             jnp.pad(wi[:, o2:o3], ((0, 0), (0, LANES - heads)))], axis=1).astype(BF16)
        pad_h = lambda v: jnp.pad(v.reshape(1, heads), ((0, 0), (0, LANES - heads)))
        seq_w = (conv_w[l], conv_b[l].reshape(1, cch), pad_h(dt_bias[l]), pad_h(a_log[l]),
                 jnp.repeat(d_skip[l], hd).reshape(1, di), ssd_norm_g[l].reshape(1, di),
                 w_ssd_out[l].astype(BF16), pool_w[l].astype(BF16), pool_scale[l].reshape(1, dm),
                 w_out[l].astype(BF16), expand3)

        mod = _ada(c_all, w_ada[l], b_ada[l])
        sh1, sc1, g1, sh2, sc2, g2 = jnp.split(mod, 6, axis=-1)

        proj, dtp = _inproj(xp, xs, _mod_rows(sc1, bp, lp), _mod_rows(sh1, bp, lp), ln1_g[l], wcat)

        zc = jnp.zeros((bp,) + state_conv.shape[2:], F32)
        zs = jnp.zeros((bp, heads, hd, ns), F32)
        zp = jnp.zeros((bp,) + state_pool.shape[2:], F32)
        xp1, ns_p, nc_p, np_p = _seq(proj, dtp, xp, g1[:bp], zc, zs, zp, seq_w,
                                     row0=0, nb=bp, seqlen=lp, pos0=0, cols=cols)
        xs1, ns_s, nc_s, np_s = _seq(proj, dtp, xs, g1[bp:], state_conv[l], state_ssm[l], state_pool[l],
                                     seq_w, row0=tp, nb=bs, seqlen=ls, pos0=PAST_LEN, cols=cols)
        conv_p.append(nc_p)
        conv_s.append(nc_s)
        pool_p.append(np_p)
        pool_s.append(np_s)
        ssm_p.append(ns_p)
        ssm_s.append(ns_s)

        h2_tiles, shared, eidx_t, wts_t, mask_t, counts = _moe_pre(
            xp1, xs1, _mod_rows(sc2, bp, lp), _mod_rows(sh2, bp, lp), ln2_g[l],
            w_router[l].astype(BF16), router_bias[l],
            shared_w_gate[l].astype(BF16), shared_w_up[l].astype(BF16), shared_w_down[l].astype(BF16))
        nsub = _token_rows(dm)
        pstart, blk_start, blk_count, n_used, pad_start, pad_len, nb = _plan(counts[:, 0], (tp + ts) * TOP_K)
        dest = _rank(mask_t, eidx_t, pstart).T
        wts = jnp.pad(wts_t.T, ((0, 0), (0, LANES - TOP_K)))
        x_sorted = _dispatch(dest, h2_tiles, pad_start, pad_len, n_used, nb, nsub)
        y_tiles = _grouped(x_sorted, blk_start, blk_count, n_used,
                           moe_w_gate[l], moe_w_up[l], moe_w_down[l])
        xp, xs = _combine(dest, y_tiles, wts, shared, xp1, xs1, _mod_rows(g2, bp, lp), final_g)

    y_prompt = xp.reshape(bp, lp, dm)
    y_sample = xs.reshape(bs, ls, dm)
    return (y_prompt, y_sample, jnp.stack(ssm_p), jnp.stack(conv_p), jnp.stack(pool_p),
            jnp.stack(ssm_s), jnp.stack(conv_s), jnp.stack(pool_s))
```

```python
import functools

import jax
import jax.numpy as jnp
from jax import lax
from jax.experimental import pallas as pl
from jax.experimental.pallas import tpu as pltpu

F32 = jnp.float32
BF16 = jnp.bfloat16
HIGHEST = lax.Precision.HIGHEST

RMS_EPS = 1e-6
SSD_NORM_EPS = 1e-5
CHUNK = 64
SSD_GROUPS = 4
POOL_WINDOWS = (2, 4, 8, 16)
PAST_LEN = 1024
TOP_K = 8
N_EXPERT_GROUPS = 8
TOPK_GROUPS = 4
ROUTED_SCALE = 2.5

LANES = 128
SUBLANES = 8
MOD_ROWS = 32
VMEM_LIMIT = 56 * 1024 * 1024
EXPERT_ROWS = 128
SEQ_TILE = 256
GROUP_RING = 4
ROW_UNROLL = 2


def _sigmoid(x):
    return 0.5 * jnp.tanh(0.5 * x) + 0.5


def _silu(x):
    h = 0.5 * x
    return h + h * jnp.tanh(h)


def _softplus(x):
    return jnp.maximum(x, 0.0) + jnp.log1p(jnp.exp(-jnp.abs(x)))


def _row_tile(n, prefs):
    for t in prefs:
        if n % t == 0:
            return t
    return n


def _modulate(y, sc_ref, sh_ref):
    rows, d = y.shape
    y3 = y.reshape(rows // MOD_ROWS, MOD_ROWS, d)
    y3 = y3 * (1.0 + sc_ref[...][:, None, :]) + sh_ref[...][:, None, :]
    return y3.reshape(rows, d)


def _rms(x, g, eps):
    return x * lax.rsqrt(jnp.mean(x * x, axis=-1, keepdims=True) + eps) * g


def _ada_kernel(c_ref, w_ref, b_ref, o_ref):
    s = _silu(c_ref[...])
    o_ref[...] = jnp.dot(s, w_ref[...], preferred_element_type=F32, precision=HIGHEST) + b_ref[...]


def _ada(c_all, w_ada, b_ada):
    n, d = c_all.shape
    dout = w_ada.shape[1]
    tn = _row_tile(dout, (1024, 512, 256, 128))
    return pl.pallas_call(
        _ada_kernel,
        grid=(dout // tn,),
        in_specs=[pl.BlockSpec((n, d), lambda j: (0, 0)),
                  pl.BlockSpec((d, tn), lambda j: (0, j)),
                  pl.BlockSpec((1, tn), lambda j: (0, j))],
        out_specs=pl.BlockSpec((n, tn), lambda j: (0, j)),
        out_shape=jax.ShapeDtypeStruct((n, dout), F32),
        name="ada",
    )(c_all, w_ada, b_ada.reshape(1, dout))


def _two_part_specs(tm, d, ntp):
    return [pl.BlockSpec((tm, d), lambda i, *_: (jnp.minimum(i, ntp - 1), 0)),
            pl.BlockSpec((tm, d), lambda i, *_: (jnp.maximum(i - ntp, 0), 0))]


def _inproj_kernel(xp_ref, xs_ref, sc_ref, sh_ref, g_ref, w_ref, o_ref, dt_ref, *, ntp, tn):
    x = jnp.where(pl.program_id(0) < ntp, xp_ref[...], xs_ref[...])
    h = _modulate(_rms(x, g_ref[...], RMS_EPS), sc_ref, sh_ref).astype(BF16)
    n_main = o_ref.shape[1]
    for c0 in range(0, n_main, tn):
        o_ref[:, c0:c0 + tn] = jnp.dot(h, w_ref[:, c0:c0 + tn], preferred_element_type=F32).astype(BF16)
    dt_ref[...] = jnp.dot(h, w_ref[:, n_main:], preferred_element_type=F32)


def _inproj(xp, xs, sc_rows, sh_rows, g, wcat):
    tp, d = xp.shape
    ts = xs.shape[0]
    n = wcat.shape[1]
    n_main = n - LANES
    tm = _row_tile(ts, (512, 256))
    assert tp % tm == 0 and tm % (SUBLANES * MOD_ROWS) == 0
    ntp = tp // tm
    nt = ntp + ts // tm
    mr = tm // MOD_ROWS
    tn = _row_tile(n_main, (2048, 1024, 512, 256, 128))
    row = lambda i: (i, 0)
    return pl.pallas_call(
        functools.partial(_inproj_kernel, ntp=ntp, tn=tn),
        grid=(nt,),
        in_specs=_two_part_specs(tm, d, ntp) + [
            pl.BlockSpec((mr, d), row), pl.BlockSpec((mr, d), row),
            pl.BlockSpec((1, d), lambda i: (0, 0)),
            pl.BlockSpec((d, n), lambda i: (0, 0), pipeline_mode=pl.Buffered(1))],
        out_specs=[pl.BlockSpec((tm, n_main), row), pl.BlockSpec((tm, LANES), row)],
        out_shape=[jax.ShapeDtypeStruct((tp + ts, n_main), BF16),
                   jax.ShapeDtypeStruct((tp + ts, LANES), F32)],
        compiler_params=pltpu.CompilerParams(
            dimension_semantics=("parallel",), vmem_limit_bytes=VMEM_LIMIT),
        name="inproj",
    )(xp, xs, sc_rows, sh_rows, g.reshape(1, d), wcat)


def _seq_kernel(z_ref, xp_ref, bc_ref, u_ref, gt_ref, dt_ref, x_ref, g1_ref,
                cst_ref, sst_ref, pst_ref, cw_ref, cb_ref, dtb_ref, alog_ref, dx_ref, ng_ref,
                wssd_ref, pw_ref, ps_ref, wout_ref, e3_ref,
                xo_ref, so_ref, co_ref, po_ref,
                cbuf, pbuf, st_scr, y_scr, xbc_scr, csx_scr, wvx_scr,
                *, tq, q, pos0, heads, hd, ns, dm):
    j = pl.program_id(1)
    di = heads * hd
    gn = SSD_GROUPS * ns
    hpg = heads // SSD_GROUPS
    cw = cbuf.shape[0] - tq
    kw = cw_ref.shape[0]
    ph = pbuf.shape[0] - tq

    @pl.when(j == 0)
    def _():
        cbuf[0:cw - (kw - 1), :] = jnp.zeros((cw - (kw - 1), cbuf.shape[1]), F32)
        cbuf[cw - (kw - 1):cw, :] = cst_ref[0]
        pbuf[0:1, :] = jnp.zeros((1, dm), F32)
        pbuf[1:ph, :] = pst_ref[0]
        st_scr[...] = sst_ref[0].T

    cbuf[cw:cw + tq, 0:di] = xp_ref[...].astype(F32)
    cbuf[cw:cw + tq, di:di + 2 * gn] = bc_ref[...].astype(F32)
    ext = cbuf[...]
    acc = cb_ref[...] + ext[cw:cw + tq, :] * cw_ref[kw - 1:kw, :]
    for k in range(kw - 1):
        acc = acc + pltpu.roll(ext, kw - 1 - k, axis=0)[cw:cw + tq, :] * cw_ref[k:k + 1, :]
    xbc_scr[...] = _silu(acc)
    cbuf[cw - (kw - 1):cw, :] = cbuf[cw + tq - (kw - 1):cw + tq, :]

    dt = _softplus(dt_ref[...] + dtb_ref[...])
    dta = dt * (-jnp.exp(alog_ref[...]))
    lq = q.bit_length() - 1
    r_i = lax.broadcasted_iota(jnp.int32, (tq, tq), 0)
    c_i = lax.broadcasted_iota(jnp.int32, (tq, tq), 1)
    same = lax.shift_right_logical(r_i, lq) == lax.shift_right_logical(c_i, lq)
    tril = jnp.where(same, (r_i >= c_i).astype(F32), 0.0)
    cs = jnp.dot(tril, dta, preferred_element_type=F32, precision=HIGHEST)
    cs_end = jnp.dot(same.astype(F32), dta, preferred_element_type=F32, precision=HIGHEST)
    wv = dt * jnp.exp(cs_end - cs)

    both = jnp.concatenate([cs, wv], axis=0)
    hi = both.astype(BF16)
    r1 = both - hi.astype(F32)
    mid = r1.astype(BF16)
    lo = (r1 - mid.astype(F32)).astype(BF16)
    ex = jnp.dot(jnp.concatenate([hi, mid, lo], axis=1), e3_ref[...], preferred_element_type=F32)
    csx_scr[...] = ex[0:tq]
    wvx_scr[...] = ex[tq:2 * tq]
    cs_t = cs.T
    dt_t = dt.T

    lane = lax.broadcasted_iota(jnp.int32, (q, 2 * hd), 1)
    kpos = jnp.bitwise_and(lane, hd - 1)
    causal2 = jnp.logical_and(lax.broadcasted_iota(jnp.int32, (q, 2 * hd), 0) >= kpos, kpos < q)
    first = lane < hd
    zrow = jnp.zeros((1, hd - q), F32)
    zblk = jnp.zeros((hd - q, 2 * hd), BF16)

    def pair_row(t, p, r0):
        parts = []
        for h in (2 * p, 2 * p + 1):
            parts.append(t[h:h + 1, r0:r0 + q])
            if q < hd:
                parts.append(zrow)
        return jnp.concatenate(parts, axis=1)

    def pair_rows(a, b):
        blocks = [a, zblk, b, zblk] if q < hd else [a, b]
        return jnp.concatenate(blocks, axis=0)

    gw = di // SSD_GROUPS
    for c in range(tq // q):
        r0 = c * q
        for g in range(SSD_GROUPS):
            bgb = xbc_scr[r0:r0 + q, di + g * ns:di + (g + 1) * ns].astype(BF16)
            cgb = xbc_scr[r0:r0 + q, di + gn + g * ns:di + gn + (g + 1) * ns].astype(BF16)
            cb2 = lax.dot_general(cgb, pair_rows(bgb, bgb), (((1,), (1,)), ((), ())),
                                  preferred_element_type=F32)
            stg = st_scr[:, g * gw:(g + 1) * gw]
            ecs = jnp.exp(csx_scr[r0:r0 + q, g * gw:(g + 1) * gw])
            yo = jnp.dot(cgb, stg.astype(BF16), preferred_element_type=F32) * ecs
            for pp in range(hpg // 2):
                p = g * (hpg // 2) + pp
                lo_l = p * 2 * hd
                seg = csx_scr[r0:r0 + q, lo_l:lo_l + 2 * hd] - pair_row(cs_t, p, r0)
                lm = jnp.exp(jnp.where(causal2, seg, -jnp.inf))
                m2 = (cb2 * lm * pair_row(dt_t, p, r0)).astype(BF16)
                xpair = xbc_scr[r0:r0 + q, lo_l:lo_l + 2 * hd]
                rhs = pair_rows(jnp.where(first, xpair, 0.0).astype(BF16),
                                jnp.where(first, 0.0, xpair).astype(BF16))
                yd = jnp.dot(m2, rhs, preferred_element_type=F32)
                y_scr[r0:r0 + q, lo_l:lo_l + 2 * hd] = yd + yo[:, pp * 2 * hd:(pp + 1) * 2 * hd]
            xw = (xbc_scr[r0:r0 + q, g * gw:(g + 1) * gw]
                  * wvx_scr[r0:r0 + q, g * gw:(g + 1) * gw]).astype(BF16)
            upd = lax.dot_general(bgb, xw, (((0,), (0,)), ((), ())), preferred_element_type=F32)
            st_scr[:, g * gw:(g + 1) * gw] = stg * ecs[q - 1:q, :] + upd

    @pl.when(j == pl.num_programs(1) - 1)
    def _():
        so_ref[0] = st_scr[...].T
        co_ref[0] = cbuf[cw - (kw - 1):cw, :]

    y = y_scr[...] + xbc_scr[:, 0:di] * dx_ref[...]
    v = y * _silu(z_ref[...].astype(F32))
    gw = di // SSD_GROUPS
    parts = []
    for g in range(SSD_GROUPS):
        vg = v[:, g * gw:(g + 1) * gw]
        parts.append(vg * lax.rsqrt(jnp.mean(vg * vg, axis=-1, keepdims=True) + SSD_NORM_EPS))
    yn = (jnp.concatenate(parts, axis=-1) * ng_ref[...]).astype(BF16)
    a_br = jnp.dot(yn, wssd_ref[...], preferred_element_type=F32)

    pbuf[ph:ph + tq, :] = u_ref[...].astype(F32)
    pos = pos0 + j * tq + lax.broadcasted_iota(jnp.int32, (tq, 1), 0)
    pgd = dm // len(POOL_WINDOWS)
    b_parts = []
    for gi, w in enumerate(POOL_WINDOWS):
        lo = gi * pgd
        s = pbuf[:, lo:lo + pgd]
        step = 1
        while step < w:
            s = s + pltpu.roll(s, step, axis=0)
            step *= 2
        cnt = jnp.minimum(w, pos + 1).astype(F32)
        pooled = s[ph:ph + tq, :] / cnt - pbuf[ph:ph + tq, lo:lo + pgd]
        b_parts.append(jnp.dot(pooled.astype(BF16), pw_ref[gi], preferred_element_type=F32))
    b_br = jnp.concatenate(b_parts, axis=-1) * ps_ref[...]
    pbuf[1:ph, :] = pbuf[tq + 1:tq + ph, :]

    @pl.when(j == pl.num_programs(1) - 1)
    def _():
        po_ref[0] = pbuf[1:ph, :]

    gates = _sigmoid(gt_ref[...].astype(F32))
    merged = gates[:, 0:dm] * a_br + gates[:, dm:2 * dm] * b_br
    mix = jnp.dot(merged.astype(BF16), wout_ref[...], preferred_element_type=F32)
    xo_ref[...] = x_ref[...] + g1_ref[0] * mix


def _seq(proj, dtp, x, g1, conv_state, ssm_state, pool_state, wts, *, row0, nb, seqlen, pos0, cols):
    dm = x.shape[1]
    heads, hd, ns = ssm_state.shape[1], ssm_state.shape[2], ssm_state.shape[3]
    di = heads * hd
    gn = SSD_GROUPS * ns
    q = min(CHUNK, seqlen)
    tq = _row_tile(seqlen, (SEQ_TILE, 128, 64, 32))
    nj = seqlen // tq
    rb0 = row0 // tq
    oz, ox, obc, ou, og = cols

    def rows(b, j):
        return rb0 + b * nj + j

    full = lambda a: pl.BlockSpec(a.shape, lambda b, j: (0,) * a.ndim, pipeline_mode=pl.Buffered(1))
    kern = functools.partial(_seq_kernel, tq=tq, q=q, pos0=pos0, heads=heads, hd=hd, ns=ns, dm=dm)
    kw1 = conv_state.shape[1]
    pst = pool_state.shape[1]
    xo, so, co, po = pl.pallas_call(
        kern,
        grid=(nb, nj),
        in_specs=[
            pl.BlockSpec((tq, di), lambda b, j: (rows(b, j), oz // di)),
            pl.BlockSpec((tq, di), lambda b, j: (rows(b, j), ox // di)),
            pl.BlockSpec((tq, 2 * gn), lambda b, j: (rows(b, j), obc // (2 * gn))),
            pl.BlockSpec((tq, dm), lambda b, j: (rows(b, j), ou // dm)),
            pl.BlockSpec((tq, 2 * dm), lambda b, j: (rows(b, j), og // (2 * dm))),
            pl.BlockSpec((tq, LANES), lambda b, j: (rows(b, j), 0)),
            pl.BlockSpec((tq, dm), lambda b, j: (b * nj + j, 0)),
            pl.BlockSpec((1, 1, dm), lambda b, j: (b, 0, 0)),
            pl.BlockSpec((1, kw1, di + 2 * gn), lambda b, j: (b, 0, 0)),
            pl.BlockSpec((1, di, ns), lambda b, j: (b, 0, 0)),
            pl.BlockSpec((1, pst, dm), lambda b, j: (b, 0, 0)),
        ] + [full(w) for w in wts],
        out_specs=[pl.BlockSpec((tq, dm), lambda b, j: (b * nj + j, 0)),
                   pl.BlockSpec((1, di, ns), lambda b, j: (b, 0, 0)),
                   pl.BlockSpec((1, kw1, di + 2 * gn), lambda b, j: (b, 0, 0)),
                   pl.BlockSpec((1, pst, dm), lambda b, j: (b, 0, 0))],
        out_shape=[jax.ShapeDtypeStruct((nb * seqlen, dm), F32),
                   jax.ShapeDtypeStruct((nb, di, ns), F32),
                   jax.ShapeDtypeStruct((nb, kw1, di + 2 * gn), F32),
                   jax.ShapeDtypeStruct((nb, pst, dm), F32)],
        scratch_shapes=[pltpu.VMEM((tq + SUBLANES, di + 2 * gn), F32),
                        pltpu.VMEM((tq + 2 * SUBLANES, dm), F32),
                        pltpu.VMEM((ns, di), F32),
                        pltpu.VMEM((tq, di), F32),
                        pltpu.VMEM((tq, di + 2 * gn), F32),
                        pltpu.VMEM((tq, di), F32),
                        pltpu.VMEM((tq, di), F32)],
        compiler_params=pltpu.CompilerParams(
            dimension_semantics=("parallel", "arbitrary"), vmem_limit_bytes=VMEM_LIMIT),
        name="seq",
    )(proj, proj, proj, proj, proj, dtp, x, g1.reshape(nb, 1, dm),
      conv_state, ssm_state.reshape(nb, di, ns), pool_state, *wts)
    return xo, so.reshape(nb, heads, hd, ns), co, po


def _moe_pre_kernel(xp_ref, xs_ref, sc_ref, sh_ref, g_ref, wr_ref, rb_ref, swg_ref, swu_ref, swd_ref,
                    h_ref, s_ref, e_ref, w_ref, m_ref, cnt_ref, *, ntp):
    x = jnp.where(pl.program_id(0) < ntp, xp_ref[...], xs_ref[...])
    y = _rms(x, g_ref[...], RMS_EPS)
    h = _modulate(y, sc_ref, sh_ref)
    _write_rows(h_ref, h)
    hb = h.astype(BF16)

    sg = jnp.dot(hb, swg_ref[...], preferred_element_type=F32)
    su = jnp.dot(hb, swu_ref[...], preferred_element_type=F32)
    s_ref[...] = jnp.dot((_silu(sg) * su).astype(BF16), swd_ref[...], preferred_element_type=F32)

    scores = jax.nn.sigmoid(lax.dot_general(wr_ref[...], hb, (((1,), (1,)), ((), ())),
                                            preferred_element_type=F32))
    biased = scores + rb_ref[:, 0:1]
    ne, tm = scores.shape
    per_g = ne // N_EXPERT_GROUPS
    neg = -jnp.inf
    row_e = lax.broadcasted_iota(jnp.int32, (ne, tm), 0).astype(F32)
    row_g = lax.broadcasted_iota(jnp.int32, (N_EXPERT_GROUPS, tm), 0).astype(F32)

    def first_argmax(v, idx, big):
        top = jnp.max(v, axis=0, keepdims=True)
        return top, jnp.min(jnp.where(v == top, idx, big), axis=0, keepdims=True)

    gs = jnp.full((N_EXPERT_GROUPS, tm), neg, F32)
    for g in range(N_EXPERT_GROUPS):
        mg = biased[g * per_g:(g + 1) * per_g, :]
        rg = (lax.broadcasted_iota(jnp.int32, (per_g, tm), 0) + g * per_g).astype(F32)
        t1, i1 = first_argmax(mg, rg, float(ne))
        t2 = jnp.max(jnp.where(rg == i1, neg, mg), axis=0, keepdims=True)
        gs = jnp.where(row_g == g, t1 + t2, gs)
    grp_e = lax.shift_right_logical(lax.broadcasted_iota(jnp.int32, (ne, tm), 0),
                                    per_g.bit_length() - 1).astype(F32)
    allowed = jnp.zeros((ne, tm), F32)
    for _ in range(TOPK_GROUPS):
        _, gi = first_argmax(gs, row_g, float(N_EXPERT_GROUPS))
        allowed = jnp.where(grp_e == gi, 1.0, allowed)
        gs = jnp.where(row_g == gi, neg, gs)
    mb = jnp.where(allowed > 0.0, biased, neg)
    row_k = lax.broadcasted_iota(jnp.int32, (TOP_K, tm), 0)
    eacc = jnp.zeros((TOP_K, tm), F32)
    wacc = jnp.zeros((TOP_K, tm), F32)
    chosen = jnp.zeros((ne, tm), F32)
    for k in range(TOP_K):
        _, ik = first_argmax(mb, row_e, float(ne))
        sel = row_e == ik
        wk = jnp.sum(jnp.where(sel, scores, 0.0), axis=0, keepdims=True)
        eacc = jnp.where(row_k == k, ik, eacc)
        wacc = jnp.where(row_k == k, wk, wacc)
        chosen = jnp.where(sel, 1.0, chosen)
        mb = jnp.where(sel, neg, mb)
    wsum = jnp.sum(wacc, axis=0, keepdims=True)
    e_ref[...] = eacc.astype(jnp.int32)
    w_ref[...] = wacc / (wsum + 1e-20) * ROUTED_SCALE
    chosen_b = chosen.astype(BF16)
    m_ref[...] = chosen_b

    @pl.when(pl.program_id(0) == 0)
    def _():
        cnt_ref[...] = jnp.zeros_like(cnt_ref)

    cnt_ref[...] += jnp.dot(chosen_b, jnp.ones((tm, LANES), BF16), preferred_element_type=F32)


def _moe_pre(xp1, xs1, sc_rows, sh_rows, g, wr, rb, swg, swu, swd):
    tp, d = xp1.shape
    ts = xs1.shape[0]
    t = tp + ts
    ne = wr.shape[1]
    ff = swg.shape[1]
    tm = _row_tile(ts, (256,))
    assert tp % tm == 0 and tm % (SUBLANES * MOD_ROWS) == 0
    ntp = tp // tm
    mr = tm // MOD_ROWS
    nsub = _token_rows(d)
    row = lambda i: (i, 0)
    col = lambda i: (0, i)
    const = lambda i: (0, 0)
    rb_col = jnp.broadcast_to(rb.reshape(ne, 1), (ne, LANES))
    return pl.pallas_call(
        functools.partial(_moe_pre_kernel, ntp=ntp),
        grid=(t // tm,),
        in_specs=_two_part_specs(tm, d, ntp) + [
                  pl.BlockSpec((mr, d), row), pl.BlockSpec((mr, d), row),
                  pl.BlockSpec((1, d), const), pl.BlockSpec((ne, d), const), pl.BlockSpec((ne, LANES), const),
                  pl.BlockSpec((d, ff), const), pl.BlockSpec((d, ff), const), pl.BlockSpec((ff, d), const)],
        out_specs=[pl.BlockSpec((tm * nsub, LANES), row), pl.BlockSpec((tm, d), row),
                   pl.BlockSpec((TOP_K, tm), col), pl.BlockSpec((TOP_K, tm), col),
                   pl.BlockSpec((ne, tm), col), pl.BlockSpec((ne, LANES), const)],
        out_shape=[jax.ShapeDtypeStruct((t * nsub, LANES), ROW_DTYPE), jax.ShapeDtypeStruct((t, d), F32),
                   jax.ShapeDtypeStruct((TOP_K, t), jnp.int32), jax.ShapeDtypeStruct((TOP_K, t), F32),
                   jax.ShapeDtypeStruct((ne, t), BF16), jax.ShapeDtypeStruct((ne, LANES), F32)],
        compiler_params=pltpu.CompilerParams(
            dimension_semantics=("arbitrary",), vmem_limit_bytes=VMEM_LIMIT),
        name="moe_pre",
    )(xp1, xs1, sc_rows, sh_rows, g.reshape(1, d), wr.T, rb_col, swg, swu, swd)


ROW_DTYPE = F32


def _token_rows(d):
    return d // LANES


def _read_rows(buf, nrows, nsub):
    return jnp.concatenate([buf[pl.ds(c, nrows, stride=nsub), :] for c in range(nsub)], axis=-1)


def _write_rows(ref, v):
    nrows, width = v.shape
    nsub = _token_rows(width)
    for c in range(nsub):
        ref[pl.ds(c, nrows, stride=nsub), :] = v[:, c * LANES:(c + 1) * LANES]


def _rank_kernel(m_ref, e_ref, ps_ref, d_ref, carry):
    i = pl.program_id(0)

    @pl.when(i == 0)
    def _():
        carry[...] = jnp.zeros_like(carry)

    m = m_ref[...]
    ne, tr = m.shape
    r_i = lax.broadcasted_iota(jnp.int32, (tr, tr), 0)
    c_i = lax.broadcasted_iota(jnp.int32, (tr, tr), 1)
    earlier = (r_i < c_i).astype(BF16)
    base = carry[...] + ps_ref[...]
    slot = jnp.dot(m, earlier, preferred_element_type=F32) + jnp.concatenate([base] * (tr // LANES), axis=1)
    row_e = lax.broadcasted_iota(jnp.int32, (ne, tr), 0)
    d_rows = [jnp.sum(jnp.where(row_e == e_ref[k:k + 1, :], slot, 0.0), axis=0, keepdims=True)
              for k in range(TOP_K)]
    d_ref[...] = jnp.concatenate(d_rows, axis=0).astype(jnp.int32)
    carry[...] += jnp.dot(m, jnp.ones((tr, LANES), BF16), preferred_element_type=F32)


def _rank(mask_t, eidx_t, pstart):
    ne, t = mask_t.shape
    tr = _row_tile(t, (512, 256))
    col = lambda i: (0, i)
    return pl.pallas_call(
        _rank_kernel,
        grid=(t // tr,),
        in_specs=[pl.BlockSpec((ne, tr), col), pl.BlockSpec((TOP_K, tr), col),
                  pl.BlockSpec((ne, LANES), lambda i: (0, 0))],
        out_specs=pl.BlockSpec((TOP_K, tr), col),
        out_shape=jax.ShapeDtypeStruct((TOP_K, t), jnp.int32),
        scratch_shapes=[pltpu.VMEM((ne, LANES), F32)],
        compiler_params=pltpu.CompilerParams(
            dimension_semantics=("arbitrary",), vmem_limit_bytes=VMEM_LIMIT),
        name="rank",
    )(mask_t, eidx_t, pstart)


def _dispatch_kernel(ps_ref, pl_ref, nu_ref, d_ref, h_hbm, xs_hbm, zbuf, stage, lsem, sem, zsem,
                     *, tt, nsub, bm, nb):
    i = pl.program_id(0)
    n = pl.num_programs(0)
    ne = ps_ref.shape[0]
    pieces = [1 << b for b in reversed(range(bm.bit_length() - 1))]

    def zero_fill(op):
        def pad(e, carry):
            start = ps_ref[e]
            length = pl_ref[e]
            for p in pieces:
                off = length - lax.rem(length, 2 * p)

                @pl.when(lax.rem(length, 2 * p) >= p)
                def _():
                    dst = xs_hbm.at[pl.ds(pl.multiple_of((start + off) * nsub, nsub), p * nsub)]
                    op(pltpu.make_async_copy(zbuf.at[pl.ds(0, p * nsub)], dst, zsem))
            return carry
        lax.fori_loop(0, ne, pad, 0)

        def tail(b, carry):
            dst = xs_hbm.at[pl.ds(pl.multiple_of(b * (bm * nsub), bm * nsub), bm * nsub)]
            op(pltpu.make_async_copy(zbuf, dst, zsem))
            return carry
        lax.fori_loop(nu_ref[0], nb, tail, 0)

    stages = stage.shape[0]
    rows = tt * nsub

    def load(tile):
        src = h_hbm.at[pl.ds(pl.multiple_of(tile * rows, rows), rows)]
        return pltpu.make_async_copy(src, stage.at[lax.rem(tile, stages)], lsem.at[lax.rem(tile, stages)])

    def wait_rows(tile):
        s = lax.rem(tile, stages)
        for _ in range(TOP_K):
            pltpu.make_async_copy(stage.at[s], xs_hbm.at[pl.ds(0, rows)], sem.at[s]).wait()

    @pl.when(i == 0)
    def _():
        zbuf[...] = jnp.zeros_like(zbuf)
        zero_fill(lambda c: c.start())
        load(0).start()

    @pl.when(i >= stages - 1)
    def _():
        wait_rows(i - (stages - 1))

    @pl.when(i + 1 < n)
    def _():
        load(i + 1).start()

    load(i).wait()
    s = lax.rem(i, stages)

    def body(r2, carry):
        for u in range(ROW_UNROLL):
            r = r2 * ROW_UNROLL + u
            src = stage.at[s, pl.ds(pl.multiple_of(r * nsub, nsub), nsub)]
            for k in range(TOP_K):
                dst = xs_hbm.at[pl.ds(pl.multiple_of(d_ref[0, 0, r * TOP_K + k] * nsub, nsub), nsub)]
                pltpu.make_async_copy(src, dst, sem.at[s]).start(priority=k % 2)
        return carry

    lax.fori_loop(0, tt // ROW_UNROLL, body, 0)

    @pl.when(i == n - 1)
    def _():
        for back in range(stages - 2, -1, -1):
            @pl.when(i >= back)
            def _():
                wait_rows(i - back)
        zero_fill(lambda c: c.wait())


def _dispatch(dest, h2_tiles, pad_start, pad_len, n_used, nb, nsub):
    t = dest.shape[0]
    tt = _row_tile(t, (256,))
    nt = t // tt
    bm = EXPERT_ROWS
    d3 = dest.reshape(nt, 1, tt * TOP_K)
    gs = pltpu.PrefetchScalarGridSpec(
        num_scalar_prefetch=3,
        grid=(nt,),
        in_specs=[pl.BlockSpec((1, 1, tt * TOP_K), lambda i, *_: (i, 0, 0), memory_space=pltpu.SMEM),
                  pl.BlockSpec(memory_space=pl.ANY)],
        out_specs=pl.BlockSpec(memory_space=pl.ANY),
        scratch_shapes=[pltpu.VMEM((bm * nsub, LANES), ROW_DTYPE),
                        pltpu.VMEM((3, tt * nsub, LANES), ROW_DTYPE),
                        pltpu.SemaphoreType.DMA((3,)), pltpu.SemaphoreType.DMA((3,)),
                        pltpu.SemaphoreType.DMA],
    )
    return pl.pallas_call(
        functools.partial(_dispatch_kernel, tt=tt, nsub=nsub, bm=bm, nb=nb),
        grid_spec=gs,
        out_shape=jax.ShapeDtypeStruct((nb * bm * nsub, LANES), ROW_DTYPE),
        compiler_params=pltpu.CompilerParams(dimension_semantics=("arbitrary",)),
        name="dispatch",
    )(pad_start, pad_len, n_used, d3, h2_tiles)


def _grouped_kernel(b0_ref, nk_ref, nu_ref, x_hbm, wg_ref, wu_ref, wd_ref, y_hbm,
                    xbuf, ybuf, xsem, ysem, wgb, wub, wdb, *, bm, nsub, nb):
    e = pl.program_id(0)
    ring = xbuf.shape[0]
    rows = bm * nsub
    nu = nu_ref[0]

    def x_copy(gb, slot):
        src = x_hbm.at[pl.ds(pl.multiple_of(gb * rows, rows), rows)]
        return pltpu.make_async_copy(src, xbuf.at[slot], xsem.at[slot])

    def y_copy(gb, slot):
        dst = y_hbm.at[pl.ds(pl.multiple_of(gb * rows, rows), rows)]
        return pltpu.make_async_copy(ybuf.at[slot], dst, ysem.at[slot])

    @pl.when(e == 0)
    def _():
        for g0 in range(ring - 1):
            @pl.when(g0 < nu)
            def _():
                x_copy(g0, g0).start()

    @pl.when(nk_ref[e] > 0)
    def _():
        wgb[...] = wg_ref[0].astype(BF16)
        wub[...] = wu_ref[0].astype(BF16)
        wdb[...] = wd_ref[0].astype(BF16)

        def block(b, carry):
            gb = b0_ref[e] + b
            slot = lax.rem(gb, ring)

            @pl.when(gb + ring - 1 < nu)
            def _():
                x_copy(gb + ring - 1, lax.rem(gb + ring - 1, ring)).start()

            x_copy(gb, slot).wait()

            @pl.when(gb >= ring)
            def _():
                y_copy(gb - ring, slot).wait()

            x = _read_rows(xbuf.at[slot], bm, nsub).astype(BF16)
            hg = jnp.dot(x, wgb[...], preferred_element_type=F32)
            hu = jnp.dot(x, wub[...], preferred_element_type=F32)
            y = jnp.dot((_silu(hg) * hu).astype(BF16), wdb[...], preferred_element_type=F32)
            _write_rows(ybuf.at[slot], y)
            y_copy(gb, slot).start()
            return carry

        lax.fori_loop(0, nk_ref[e], block, 0)

    @pl.when(e == pl.num_programs(0) - 1)
    def _():
        for back in range(ring, 0, -1):
            @pl.when(nu >= back)
            def _():
                y_copy(nu - back, lax.rem(nu - back, ring)).wait()

        ybuf[0] = jnp.zeros(ybuf.shape[1:], ybuf.dtype)

        def tail(gb, carry):
            y_copy(gb, 0).start()
            return carry

        def tail_wait(gb, carry):
            y_copy(gb, 0).wait()
            return carry

        lax.fori_loop(nu, nb, tail, 0)
        lax.fori_loop(nu, nb, tail_wait, 0)


def _grouped(xs, blk_start, blk_count, n_used, wg, wu, wd):
    ne, d, ff = wg.shape
    nsub = _token_rows(d)
    bm = EXPERT_ROWS
    nb = xs.shape[0] // (bm * nsub)
    wspec = lambda shape: pl.BlockSpec(shape, lambda e, *_: (e, 0, 0))
    gs = pltpu.PrefetchScalarGridSpec(
        num_scalar_prefetch=3,
        grid=(ne,),
        in_specs=[pl.BlockSpec(memory_space=pl.ANY),
                  wspec((1, d, ff)), wspec((1, d, ff)), wspec((1, ff, d))],
        out_specs=pl.BlockSpec(memory_space=pl.ANY),
        scratch_shapes=[pltpu.VMEM((GROUP_RING, bm * nsub, LANES), ROW_DTYPE),
                        pltpu.VMEM((GROUP_RING, bm * nsub, LANES), ROW_DTYPE),
                        pltpu.SemaphoreType.DMA((GROUP_RING,)), pltpu.SemaphoreType.DMA((GROUP_RING,)),
                        pltpu.VMEM((d, ff), BF16), pltpu.VMEM((d, ff), BF16), pltpu.VMEM((ff, d), BF16)],
    )
    return pl.pallas_call(
        functools.partial(_grouped_kernel, bm=bm, nsub=nsub, nb=nb),
        grid_spec=gs,
        out_shape=jax.ShapeDtypeStruct(xs.shape, xs.dtype),
        compiler_params=pltpu.CompilerParams(
            dimension_semantics=("arbitrary",), vmem_limit_bytes=VMEM_LIMIT),
        name="grouped",
    )(blk_start, blk_count, n_used, xs, wg, wu, wd)


def _combine_kernel(d_ref, dn_ref, y_hbm, w_ref, s_ref, xp_ref, xs_ref, g2_ref, fg_ref, op_ref, os_ref,
                    buf_a, buf_b, obuf, sem_a, sem_b, *, tt, nsub, ntp):
    i = pl.program_id(0)
    n = pl.num_programs(0)

    def issue(dref, buf, sem, r):
        for k in range(TOP_K):
            src = y_hbm.at[pl.ds(pl.multiple_of(dref[0, 0, r * TOP_K + k] * nsub, nsub), nsub)]
            pltpu.make_async_copy(src, buf.at[k, pl.ds(pl.multiple_of(r * nsub, nsub), nsub)],
                                  sem).start(priority=k % 2)

    def wait_tile(buf, sem):
        for k in range(TOP_K):
            pltpu.make_async_copy(y_hbm.at[pl.ds(0, tt * nsub)], buf.at[k], sem).wait()

    @pl.when(i == 0)
    def _():
        def first(r, carry):
            issue(d_ref, buf_a, sem_a, r)
            return carry
        lax.fori_loop(0, tt, first, 0)

    is_prompt = i < ntp

    def step(cur, cur_sem, nxt, nxt_sem):
        wait_tile(cur, cur_sem)

        def chunk(c, carry):
            r0 = pl.multiple_of(c * SUBLANES, SUBLANES)
            for u in range(SUBLANES):
                issue(dn_ref, nxt, nxt_sem, r0 + u)
            acc = s_ref[pl.ds(r0, SUBLANES), :]
            w8 = w_ref[pl.ds(r0, SUBLANES), :]
            for k in range(TOP_K):
                rows = jnp.concatenate(
                    [cur[k, pl.ds(r0 * nsub + cc, SUBLANES, stride=nsub), :] for cc in range(nsub)], axis=-1)
                acc = acc + w8[:, k:k + 1] * rows
            g2 = g2_ref[pl.ds(c // (MOD_ROWS // SUBLANES), 1), :]
            x1 = jnp.where(is_prompt, xp_ref[pl.ds(r0, SUBLANES), :], xs_ref[pl.ds(r0, SUBLANES), :])
            obuf[pl.ds(r0, SUBLANES), :] = _rms(x1 + acc * g2, fg_ref[...], RMS_EPS)
            return carry

        lax.fori_loop(0, tt // SUBLANES, chunk, 0)

        @pl.when(i == n - 1)
        def _():
            wait_tile(nxt, nxt_sem)

    @pl.when(lax.rem(i, 2) == 0)
    def _():
        step(buf_a, sem_a, buf_b, sem_b)

    @pl.when(lax.rem(i, 2) == 1)
    def _():
        step(buf_b, sem_b, buf_a, sem_a)

    @pl.when(is_prompt)
    def _():
        op_ref[...] = obuf[...]

    @pl.when(jnp.logical_not(is_prompt))
    def _():
        os_ref[...] = obuf[...]


def _combine(dest, y_tiles, wts, shared, xp1, xs1, g2_rows, final_g):
    tp, d = xp1.shape
    ts = xs1.shape[0]
    nsub = _token_rows(d)
    tt = _row_tile(ts, (256,))
    assert tp % tt == 0 and tt % (SUBLANES * MOD_ROWS) == 0
    ntp = tp // tt
    mr = tt // MOD_ROWS
    nt = (tp + ts) // tt
    d3 = dest.reshape(nt, 1, tt * TOP_K)
    kern = functools.partial(_combine_kernel, tt=tt, nsub=nsub, ntp=ntp)
    smem_blk = lambda f: pl.BlockSpec((1, 1, tt * TOP_K), f, memory_space=pltpu.SMEM)
    row = lambda i: (i, 0)
    two = _two_part_specs(tt, d, ntp)
    return pl.pallas_call(
        kern,
        grid=(nt,),
        in_specs=[smem_blk(lambda i: (i, 0, 0)),
                  smem_blk(lambda i: (jnp.minimum(i + 1, nt - 1), 0, 0)),
                  pl.BlockSpec(memory_space=pl.ANY),
                  pl.BlockSpec((tt, LANES), row), pl.BlockSpec((tt, d), row)] + two + [
                  pl.BlockSpec((mr, d), row), pl.BlockSpec((1, d), lambda i: (0, 0))],
        out_specs=two,
        out_shape=[jax.ShapeDtypeStruct((tp, d), F32), jax.ShapeDtypeStruct((ts, d), F32)],
        scratch_shapes=[pltpu.VMEM((TOP_K, tt * nsub, LANES), ROW_DTYPE),
                        pltpu.VMEM((TOP_K, tt * nsub, LANES), ROW_DTYPE), pltpu.VMEM((tt, d), F32),
                        pltpu.SemaphoreType.DMA, pltpu.SemaphoreType.DMA],
        compiler_params=pltpu.CompilerParams(
            dimension_semantics=("arbitrary",), vmem_limit_bytes=VMEM_LIMIT),
        name="combine",
    )(d3, d3, y_tiles, wts, shared, xp1, xs1, g2_rows, final_g.reshape(1, d))


def _plan(counts, n_assign):
    ne = counts.shape[0]
    bm = EXPERT_ROWS
    counts = counts.astype(jnp.int32)
    nblk = (counts + bm - 1) // bm
    bend = jnp.cumsum(nblk)
    pstart = jnp.broadcast_to(((bend - nblk) * bm).astype(F32).reshape(ne, 1), (ne, LANES))
    nb = -(-(n_assign + ne * (bm - 1)) // bm)
    n_used = bend[-1:].astype(jnp.int32)
    pad_start = ((bend - nblk) * bm + counts).astype(jnp.int32)
    pad_len = (nblk * bm - counts).astype(jnp.int32)
    return pstart, (bend - nblk).astype(jnp.int32), nblk, n_used, pad_start, pad_len, nb


def per_g_pow2(ne):
    per_g = ne // N_EXPERT_GROUPS
    return per_g * N_EXPERT_GROUPS == ne and per_g & (per_g - 1) == 0


def _mod_rows(m, nbp, lp):
    return jnp.concatenate([jnp.repeat(m[:nbp], lp // MOD_ROWS, axis=0), m[nbp:]], axis=0)


def kernel(x_prompt, x_sample, state_ssm, state_conv, state_pool, c_prompt, c_sample, ln1_g, ln2_g, w_ada, b_ada, w_in, conv_w, conv_b, dt_bias, a_log, d_skip, ssd_norm_g, w_ssd_out, pool_w, pool_scale, w_out, w_router, router_bias, moe_w_gate, moe_w_up, moe_w_down, shared_w_gate, shared_w_up, shared_w_down, final_g):
    bp, lp, dm = x_prompt.shape
    bs, ls, _ = x_sample.shape
    depth = ln1_g.shape[0]
    heads, hd, ns = state_ssm.shape[2], state_ssm.shape[3], state_ssm.shape[4]
    di = heads * hd
    gn = SSD_GROUPS * ns
    cch = di + 2 * gn
    assert depth == 1 and ls == MOD_ROWS and lp % MOD_ROWS == 0 and heads <= LANES
    assert per_g_pow2(w_router.shape[2])
    assert ns == 2 * hd and hd & (hd - 1) == 0 and (heads // SSD_GROUPS) % 2 == 0
    assert all(w & (w - 1) == 0 for w in POOL_WINDOWS) and state_pool.shape[2] == max(POOL_WINDOWS) - 1
    tp, ts = bp * lp, bs * ls
    expand3 = (jnp.arange(3 * LANES)[:, None] % LANES == jnp.arange(di)[None, :] // hd).astype(BF16)

    xp, xs = x_prompt.reshape(tp, dm), x_sample.reshape(ts, dm)
    c_all = jnp.concatenate([c_prompt, c_sample], axis=0)

    o1, o2, o3, o4 = di, di + cch, di + cch + heads, di + cch + heads + dm
    cols = (0, di, 2 * di, di + cch, di + cch + dm)

    ssm_p, conv_p, pool_p, ssm_s, conv_s, pool_s = [], [], [], [], [], []
    for l in range(depth):
        wi = w_in[l]
        wcat = jnp.concatenate(
            [wi[:, :o1], wi[:, o1:o2], wi[:, o3:o4], wi[:, o4:],
             jnp.pad(wi[:, o2:o3], ((0, 0), (0, LANES - heads)))], axis=1).astype(BF16)
        pad_h = lambda v: jnp.pad(v.reshape(1, heads), ((0, 0), (0, LANES - heads)))
        seq_w = (conv_w[l], conv_b[l].reshape(1, cch), pad_h(dt_bias[l]), pad_h(a_log[l]),
                 jnp.repeat(d_skip[l], hd).reshape(1, di), ssd_norm_g[l].reshape(1, di),
                 w_ssd_out[l].astype(BF16), pool_w[l].astype(BF16), pool_scale[l].reshape(1, dm),
                 w_out[l].astype(BF16), expand3)

        mod = _ada(c_all, w_ada[l], b_ada[l])
        sh1, sc1, g1, sh2, sc2, g2 = jnp.split(mod, 6, axis=-1)

        proj, dtp = _inproj(xp, xs, _mod_rows(sc1, bp, lp), _mod_rows(sh1, bp, lp), ln1_g[l], wcat)

        zc = jnp.zeros((bp,) + state_conv.shape[2:], F32)
        zs = jnp.zeros((bp, heads, hd, ns), F32)
        zp = jnp.zeros((bp,) + state_pool.shape[2:], F32)
        xp1, ns_p, nc_p, np_p = _seq(proj, dtp, xp, g1[:bp], zc, zs, zp, seq_w,
                                     row0=0, nb=bp, seqlen=lp, pos0=0, cols=cols)
        xs1, ns_s, nc_s, np_s = _seq(proj, dtp, xs, g1[bp:], state_conv[l], state_ssm[l], state_pool[l],
                                     seq_w, row0=tp, nb=bs, seqlen=ls, pos0=PAST_LEN, cols=cols)
        conv_p.append(nc_p)
        conv_s.append(nc_s)
        pool_p.append(np_p)
        pool_s.append(np_s)
        ssm_p.append(ns_p)
        ssm_s.append(ns_s)

        h2_tiles, shared, eidx_t, wts_t, mask_t, counts = _moe_pre(
            xp1, xs1, _mod_rows(sc2, bp, lp), _mod_rows(sh2, bp, lp), ln2_g[l],
            w_router[l].astype(BF16), router_bias[l],
            shared_w_gate[l].astype(BF16), shared_w_up[l].astype(BF16), shared_w_down[l].astype(BF16))
        nsub = _token_rows(dm)
        pstart, blk_start, blk_count, n_used, pad_start, pad_len, nb = _plan(counts[:, 0], (tp + ts) * TOP_K)
        dest = _rank(mask_t, eidx_t, pstart).T
        wts = jnp.pad(wts_t.T, ((0, 0), (0, LANES - TOP_K)))
        x_sorted = _dispatch(dest, h2_tiles, pad_start, pad_len, n_used, nb, nsub)
        y_tiles = _grouped(x_sorted, blk_start, blk_count, n_used,
                           moe_w_gate[l], moe_w_up[l], moe_w_down[l])
        xp, xs = _combine(dest, y_tiles, wts, shared, xp1, xs1, _mod_rows(g2, bp, lp), final_g)

    y_prompt = xp.reshape(bp, lp, dm)
    y_sample = xs.reshape(bs, ls, dm)
    return (y_prompt, y_sample, jnp.stack(ssm_p), jnp.stack(conv_p), jnp.stack(pool_p),
            jnp.stack(ssm_s), jnp.stack(conv_s), jnp.stack(pool_s))
```

```python
import functools

import jax
import jax.numpy as jnp
from jax import lax
from jax.experimental import pallas as pl
from jax.experimental.pallas import tpu as pltpu

F32 = jnp.float32
BF16 = jnp.bfloat16
HIGHEST = lax.Precision.HIGHEST

RMS_EPS = 1e-6
SSD_NORM_EPS = 1e-5
CHUNK = 64
SSD_GROUPS = 4
POOL_WINDOWS = (2, 4, 8, 16)
PAST_LEN = 1024
TOP_K = 8
N_EXPERT_GROUPS = 8
TOPK_GROUPS = 4
ROUTED_SCALE = 2.5

LANES = 128
SUBLANES = 8
MOD_ROWS = 32
VMEM_LIMIT = 56 * 1024 * 1024
EXPERT_ROWS = 256
SEQ_TILE = 256
GROUP_RING = 3
ROW_UNROLL = 2
DISPATCH_STAGES = 3


def _sigmoid(x):
    return 0.5 * jnp.tanh(0.5 * x) + 0.5


def _silu(x):
    h = 0.5 * x
    return h + h * jnp.tanh(h)


def _softplus(x):
    return jnp.maximum(x, 0.0) + jnp.log1p(jnp.exp(-jnp.abs(x)))


def _row_tile(n, prefs):
    for t in prefs:
        if n % t == 0:
            return t
    return n


def _modulate(y, sc_ref, sh_ref):
    rows, d = y.shape
    y3 = y.reshape(rows // MOD_ROWS, MOD_ROWS, d)
    y3 = y3 * (1.0 + sc_ref[...][:, None, :]) + sh_ref[...][:, None, :]
    return y3.reshape(rows, d)


def _rms(x, g, eps):
    return x * lax.rsqrt(jnp.mean(x * x, axis=-1, keepdims=True) + eps) * g


def _ada_kernel(c_ref, w_ref, b_ref, o_ref):
    s = _silu(c_ref[...])
    o_ref[...] = jnp.dot(s, w_ref[...], preferred_element_type=F32, precision=HIGHEST) + b_ref[...]


def _ada(c_all, w_ada, b_ada):
    n, d = c_all.shape
    dout = w_ada.shape[1]
    tn = _row_tile(dout, (1024, 512, 256, 128))
    return pl.pallas_call(
        _ada_kernel,
        grid=(dout // tn,),
        in_specs=[pl.BlockSpec((n, d), lambda j: (0, 0)),
                  pl.BlockSpec((d, tn), lambda j: (0, j)),
                  pl.BlockSpec((1, tn), lambda j: (0, j))],
        out_specs=pl.BlockSpec((n, tn), lambda j: (0, j)),
        out_shape=jax.ShapeDtypeStruct((n, dout), F32),
        name="ada",
    )(c_all, w_ada, b_ada.reshape(1, dout))


def _two_part_specs(tm, d, ntp):
    return [pl.BlockSpec((tm, d), lambda i, *_: (jnp.minimum(i, ntp - 1), 0)),
            pl.BlockSpec((tm, d), lambda i, *_: (jnp.maximum(i - ntp, 0), 0))]


def _inproj_kernel(xp_ref, xs_ref, sc_ref, sh_ref, g_ref, w_ref, o_ref, dt_ref, *, ntp, tn):
    x = jnp.where(pl.program_id(0) < ntp, xp_ref[...], xs_ref[...])
    h = _modulate(_rms(x, g_ref[...], RMS_EPS), sc_ref, sh_ref).astype(BF16)
    n_main = o_ref.shape[1]
    for c0 in range(0, n_main, tn):
        o_ref[:, c0:c0 + tn] = jnp.dot(h, w_ref[:, c0:c0 + tn], preferred_element_type=F32).astype(BF16)
    dt_ref[...] = jnp.dot(h, w_ref[:, n_main:], preferred_element_type=F32)


def _inproj(xp, xs, sc_rows, sh_rows, g, wcat):
    tp, d = xp.shape
    ts = xs.shape[0]
    n = wcat.shape[1]
    n_main = n - LANES
    tm = _row_tile(ts, (512, 256))
    assert tp % tm == 0 and tm % (SUBLANES * MOD_ROWS) == 0
    ntp = tp // tm
    nt = ntp + ts // tm
    mr = tm // MOD_ROWS
    tn = _row_tile(n_main, (2048, 1024, 512, 256, 128))
    row = lambda i: (i, 0)
    return pl.pallas_call(
        functools.partial(_inproj_kernel, ntp=ntp, tn=tn),
        grid=(nt,),
        in_specs=_two_part_specs(tm, d, ntp) + [
            pl.BlockSpec((mr, d), row), pl.BlockSpec((mr, d), row),
            pl.BlockSpec((1, d), lambda i: (0, 0)),
            pl.BlockSpec((d, n), lambda i: (0, 0), pipeline_mode=pl.Buffered(1))],
        out_specs=[pl.BlockSpec((tm, n_main), row), pl.BlockSpec((tm, LANES), row)],
        out_shape=[jax.ShapeDtypeStruct((tp + ts, n_main), BF16),
                   jax.ShapeDtypeStruct((tp + ts, LANES), F32)],
        compiler_params=pltpu.CompilerParams(
            dimension_semantics=("parallel",), vmem_limit_bytes=VMEM_LIMIT),
        name="inproj",
    )(xp, xs, sc_rows, sh_rows, g.reshape(1, d), wcat)


def _seq_kernel(z_ref, xp_ref, bc_ref, u_ref, gt_ref, dt_ref, x_ref, g1_ref,
                cst_ref, sst_ref, pst_ref, cw_ref, cb_ref, dtb_ref, alog_ref, dx_ref, ng_ref,
                wssd_ref, pw_ref, ps_ref, wout_ref, e3_ref,
                xo_ref, so_ref, co_ref, po_ref,
                cbuf, pbuf, st_scr, y_scr, xbc_scr, csx_scr, wvx_scr,
                *, tq, q, pos0, heads, hd, ns, dm):
    j = pl.program_id(1)
    di = heads * hd
    gn = SSD_GROUPS * ns
    hpg = heads // SSD_GROUPS
    cw = cbuf.shape[0] - tq
    kw = cw_ref.shape[0]
    ph = pbuf.shape[0] - tq

    @pl.when(j == 0)
    def _():
        cbuf[0:cw - (kw - 1), :] = jnp.zeros((cw - (kw - 1), cbuf.shape[1]), F32)
        cbuf[cw - (kw - 1):cw, :] = cst_ref[0]
        pbuf[0:1, :] = jnp.zeros((1, dm), F32)
        pbuf[1:ph, :] = pst_ref[0]
        st_scr[...] = sst_ref[0].T

    cbuf[cw:cw + tq, 0:di] = xp_ref[...].astype(F32)
    cbuf[cw:cw + tq, di:di + 2 * gn] = bc_ref[...].astype(F32)
    ext = cbuf[...]
    acc = cb_ref[...] + ext[cw:cw + tq, :] * cw_ref[kw - 1:kw, :]
    for k in range(kw - 1):
        acc = acc + pltpu.roll(ext, kw - 1 - k, axis=0)[cw:cw + tq, :] * cw_ref[k:k + 1, :]
    xbc_scr[...] = _silu(acc)
    cbuf[cw - (kw - 1):cw, :] = cbuf[cw + tq - (kw - 1):cw + tq, :]

    dt = _softplus(dt_ref[...] + dtb_ref[...])
    dta = dt * (-jnp.exp(alog_ref[...]))
    lq = q.bit_length() - 1
    r_i = lax.broadcasted_iota(jnp.int32, (tq, tq), 0)
    c_i = lax.broadcasted_iota(jnp.int32, (tq, tq), 1)
    same = lax.shift_right_logical(r_i, lq) == lax.shift_right_logical(c_i, lq)
    tril = jnp.where(same, (r_i >= c_i).astype(F32), 0.0)
    cs = jnp.dot(tril, dta, preferred_element_type=F32, precision=HIGHEST)
    cs_end = jnp.dot(same.astype(F32), dta, preferred_element_type=F32, precision=HIGHEST)
    wv = dt * jnp.exp(cs_end - cs)

    both = jnp.concatenate([cs, wv], axis=0)
    hi = both.astype(BF16)
    r1 = both - hi.astype(F32)
    mid = r1.astype(BF16)
    lo = (r1 - mid.astype(F32)).astype(BF16)
    ex = jnp.dot(jnp.concatenate([hi, mid, lo], axis=1), e3_ref[...], preferred_element_type=F32)
    csx_scr[...] = ex[0:tq]
    wvx_scr[...] = ex[tq:2 * tq]
    cs_t = cs.T
    dt_t = dt.T

    lane = lax.broadcasted_iota(jnp.int32, (q, 2 * hd), 1)
    kpos = jnp.bitwise_and(lane, hd - 1)
    causal2 = jnp.logical_and(lax.broadcasted_iota(jnp.int32, (q, 2 * hd), 0) >= kpos, kpos < q)
    first = lane < hd
    zrow = jnp.zeros((1, hd - q), F32)
    zblk = jnp.zeros((hd - q, 2 * hd), BF16)

    def pair_row(t, p, r0):
        parts = []
        for h in (2 * p, 2 * p + 1):
            parts.append(t[h:h + 1, r0:r0 + q])
            if q < hd:
                parts.append(zrow)
        return jnp.concatenate(parts, axis=1)

    def pair_rows(a, b):
        blocks = [a, zblk, b, zblk] if q < hd else [a, b]
        return jnp.concatenate(blocks, axis=0)

    gw = di // SSD_GROUPS
    for c in range(tq // q):
        r0 = c * q
        for g in range(SSD_GROUPS):
            bgb = xbc_scr[r0:r0 + q, di + g * ns:di + (g + 1) * ns].astype(BF16)
            cgb = xbc_scr[r0:r0 + q, di + gn + g * ns:di + gn + (g + 1) * ns].astype(BF16)
            cb2 = lax.dot_general(cgb, pair_rows(bgb, bgb), (((1,), (1,)), ((), ())),
                                  preferred_element_type=F32)
            stg = st_scr[:, g * gw:(g + 1) * gw]
            ecs = jnp.exp(csx_scr[r0:r0 + q, g * gw:(g + 1) * gw])
            yo = jnp.dot(cgb, stg.astype(BF16), preferred_element_type=F32) * ecs
            for pp in range(hpg // 2):
                p = g * (hpg // 2) + pp
                lo_l = p * 2 * hd
                seg = csx_scr[r0:r0 + q, lo_l:lo_l + 2 * hd] - pair_row(cs_t, p, r0)
                lm = jnp.exp(jnp.where(causal2, seg, -jnp.inf))
                m2 = (cb2 * lm * pair_row(dt_t, p, r0)).astype(BF16)
                xpair = xbc_scr[r0:r0 + q, lo_l:lo_l + 2 * hd]
                rhs = pair_rows(jnp.where(first, xpair, 0.0).astype(BF16),
                                jnp.where(first, 0.0, xpair).astype(BF16))
                yd = jnp.dot(m2, rhs, preferred_element_type=F32)
                y_scr[r0:r0 + q, lo_l:lo_l + 2 * hd] = yd + yo[:, pp * 2 * hd:(pp + 1) * 2 * hd]
            xw = (xbc_scr[r0:r0 + q, g * gw:(g + 1) * gw]
                  * wvx_scr[r0:r0 + q, g * gw:(g + 1) * gw]).astype(BF16)
            upd = lax.dot_general(bgb, xw, (((0,), (0,)), ((), ())), preferred_element_type=F32)
            st_scr[:, g * gw:(g + 1) * gw] = stg * ecs[q - 1:q, :] + upd

    @pl.when(j == pl.num_programs(1) - 1)
    def _():
        so_ref[0] = st_scr[...].T
        co_ref[0] = cbuf[cw - (kw - 1):cw, :]

    y = y_scr[...] + xbc_scr[:, 0:di] * dx_ref[...]
    v = y * _silu(z_ref[...].astype(F32))
    gw = di // SSD_GROUPS
    parts = []
    for g in range(SSD_GROUPS):
        vg = v[:, g * gw:(g + 1) * gw]
        parts.append(vg * lax.rsqrt(jnp.mean(vg * vg, axis=-1, keepdims=True) + SSD_NORM_EPS))
    yn = (jnp.concatenate(parts, axis=-1) * ng_ref[...]).astype(BF16)
    a_br = jnp.dot(yn, wssd_ref[...], preferred_element_type=F32)

    pbuf[ph:ph + tq, :] = u_ref[...].astype(F32)
    pos = pos0 + j * tq + lax.broadcasted_iota(jnp.int32, (tq, 1), 0)
    pgd = dm // len(POOL_WINDOWS)
    b_parts = []
    for gi, w in enumerate(POOL_WINDOWS):
        lo = gi * pgd
        s = pbuf[:, lo:lo + pgd]
        step = 1
        while step < w:
            s = s + pltpu.roll(s, step, axis=0)
            step *= 2
        cnt = jnp.minimum(w, pos + 1).astype(F32)
        pooled = s[ph:ph + tq, :] / cnt - pbuf[ph:ph + tq, lo:lo + pgd]
        b_parts.append(jnp.dot(pooled.astype(BF16), pw_ref[gi], preferred_element_type=F32))
    b_br = jnp.concatenate(b_parts, axis=-1) * ps_ref[...]
    pbuf[1:ph, :] = pbuf[tq + 1:tq + ph, :]

    @pl.when(j == pl.num_programs(1) - 1)
    def _():
        po_ref[0] = pbuf[1:ph, :]

    gates = _sigmoid(gt_ref[...].astype(F32))
    merged = gates[:, 0:dm] * a_br + gates[:, dm:2 * dm] * b_br
    mix = jnp.dot(merged.astype(BF16), wout_ref[...], preferred_element_type=F32)
    xo_ref[...] = x_ref[...] + g1_ref[0] * mix


def _seq(proj, dtp, x, g1, conv_state, ssm_state, pool_state, wts, *, row0, nb, seqlen, pos0, cols):
    dm = x.shape[1]
    heads, hd, ns = ssm_state.shape[1], ssm_state.shape[2], ssm_state.shape[3]
    di = heads * hd
    gn = SSD_GROUPS * ns
    q = min(CHUNK, seqlen)
    tq = _row_tile(seqlen, (SEQ_TILE, 128, 64, 32))
    nj = seqlen // tq
    rb0 = row0 // tq
    oz, ox, obc, ou, og = cols

    def rows(b, j):
        return rb0 + b * nj + j

    full = lambda a: pl.BlockSpec(a.shape, lambda b, j: (0,) * a.ndim, pipeline_mode=pl.Buffered(1))
    kern = functools.partial(_seq_kernel, tq=tq, q=q, pos0=pos0, heads=heads, hd=hd, ns=ns, dm=dm)
    kw1 = conv_state.shape[1]
    pst = pool_state.shape[1]
    xo, so, co, po = pl.pallas_call(
        kern,
        grid=(nb, nj),
        in_specs=[
            pl.BlockSpec((tq, di), lambda b, j: (rows(b, j), oz // di)),
            pl.BlockSpec((tq, di), lambda b, j: (rows(b, j), ox // di)),
            pl.BlockSpec((tq, 2 * gn), lambda b, j: (rows(b, j), obc // (2 * gn))),
            pl.BlockSpec((tq, dm), lambda b, j: (rows(b, j), ou // dm)),
            pl.BlockSpec((tq, 2 * dm), lambda b, j: (rows(b, j), og // (2 * dm))),
            pl.BlockSpec((tq, LANES), lambda b, j: (rows(b, j), 0)),
            pl.BlockSpec((tq, dm), lambda b, j: (b * nj + j, 0)),
            pl.BlockSpec((1, 1, dm), lambda b, j: (b, 0, 0)),
            pl.BlockSpec((1, kw1, di + 2 * gn), lambda b, j: (b, 0, 0)),
            pl.BlockSpec((1, di, ns), lambda b, j: (b, 0, 0)),
            pl.BlockSpec((1, pst, dm), lambda b, j: (b, 0, 0)),
        ] + [full(w) for w in wts],
        out_specs=[pl.BlockSpec((tq, dm), lambda b, j: (b * nj + j, 0)),
                   pl.BlockSpec((1, di, ns), lambda b, j: (b, 0, 0)),
                   pl.BlockSpec((1, kw1, di + 2 * gn), lambda b, j: (b, 0, 0)),
                   pl.BlockSpec((1, pst, dm), lambda b, j: (b, 0, 0))],
        out_shape=[jax.ShapeDtypeStruct((nb * seqlen, dm), F32),
                   jax.ShapeDtypeStruct((nb, di, ns), F32),
                   jax.ShapeDtypeStruct((nb, kw1, di + 2 * gn), F32),
                   jax.ShapeDtypeStruct((nb, pst, dm), F32)],
        scratch_shapes=[pltpu.VMEM((tq + SUBLANES, di + 2 * gn), F32),
                        pltpu.VMEM((tq + 2 * SUBLANES, dm), F32),
                        pltpu.VMEM((ns, di), F32),
                        pltpu.VMEM((tq, di), F32),
                        pltpu.VMEM((tq, di + 2 * gn), F32),
                        pltpu.VMEM((tq, di), F32),
                        pltpu.VMEM((tq, di), F32)],
        compiler_params=pltpu.CompilerParams(
            dimension_semantics=("parallel", "arbitrary"), vmem_limit_bytes=VMEM_LIMIT),
        name="seq",
    )(proj, proj, proj, proj, proj, dtp, x, g1.reshape(nb, 1, dm),
      conv_state, ssm_state.reshape(nb, di, ns), pool_state, *wts)
    return xo, so.reshape(nb, heads, hd, ns), co, po


def _moe_pre_kernel(xp_ref, xs_ref, sc_ref, sh_ref, g_ref, wr_ref, rb_ref, swg_ref, swu_ref, swd_ref,
                    h_ref, s_ref, e_ref, w_ref, m_ref, cnt_ref, *, ntp):
    x = jnp.where(pl.program_id(0) < ntp, xp_ref[...], xs_ref[...])
    y = _rms(x, g_ref[...], RMS_EPS)
    h = _modulate(y, sc_ref, sh_ref)
    _write_rows(h_ref, h)
    hb = h.astype(BF16)

    sg = jnp.dot(hb, swg_ref[...], preferred_element_type=F32)
    su = jnp.dot(hb, swu_ref[...], preferred_element_type=F32)
    s_ref[...] = jnp.dot((_silu(sg) * su).astype(BF16), swd_ref[...], preferred_element_type=F32)

    scores = jax.nn.sigmoid(lax.dot_general(wr_ref[...], hb, (((1,), (1,)), ((), ())),
                                            preferred_element_type=F32))
    biased = scores + rb_ref[:, 0:1]
    ne, tm = scores.shape
    per_g = ne // N_EXPERT_GROUPS
    neg = -jnp.inf
    row_e = lax.broadcasted_iota(jnp.int32, (ne, tm), 0).astype(F32)
    row_g = lax.broadcasted_iota(jnp.int32, (N_EXPERT_GROUPS, tm), 0).astype(F32)

    def first_argmax(v, idx, big):
        top = jnp.max(v, axis=0, keepdims=True)
        return top, jnp.min(jnp.where(v == top, idx, big), axis=0, keepdims=True)

    gs = jnp.full((N_EXPERT_GROUPS, tm), neg, F32)
    for g in range(N_EXPERT_GROUPS):
        mg = biased[g * per_g:(g + 1) * per_g, :]
        rg = (lax.broadcasted_iota(jnp.int32, (per_g, tm), 0) + g * per_g).astype(F32)
        t1, i1 = first_argmax(mg, rg, float(ne))
        t2 = jnp.max(jnp.where(rg == i1, neg, mg), axis=0, keepdims=True)
        gs = jnp.where(row_g == g, t1 + t2, gs)
    grp_e = lax.shift_right_logical(lax.broadcasted_iota(jnp.int32, (ne, tm), 0),
                                    per_g.bit_length() - 1).astype(F32)
    allowed = jnp.zeros((ne, tm), F32)
    for _ in range(TOPK_GROUPS):
        _, gi = first_argmax(gs, row_g, float(N_EXPERT_GROUPS))
        allowed = jnp.where(grp_e == gi, 1.0, allowed)
        gs = jnp.where(row_g == gi, neg, gs)
    mb = jnp.where(allowed > 0.0, biased, neg)
    row_k = lax.broadcasted_iota(jnp.int32, (TOP_K, tm), 0)
    eacc = jnp.zeros((TOP_K, tm), F32)
    wacc = jnp.zeros((TOP_K, tm), F32)
    chosen = jnp.zeros((ne, tm), F32)
    for k in range(TOP_K):
        _, ik = first_argmax(mb, row_e, float(ne))
        sel = row_e == ik
        wk = jnp.sum(jnp.where(sel, scores, 0.0), axis=0, keepdims=True)
        eacc = jnp.where(row_k == k, ik, eacc)
        wacc = jnp.where(row_k == k, wk, wacc)
        chosen = jnp.where(sel, 1.0, chosen)
        mb = jnp.where(sel, neg, mb)
    wsum = jnp.sum(wacc, axis=0, keepdims=True)
    e_ref[...] = eacc.astype(jnp.int32)
    w_ref[...] = wacc / (wsum + 1e-20) * ROUTED_SCALE
    chosen_b = chosen.astype(BF16)
    m_ref[...] = chosen_b

    @pl.when(pl.program_id(0) == 0)
    def _():
        cnt_ref[...] = jnp.zeros_like(cnt_ref)

    cnt_ref[...] += jnp.dot(chosen_b, jnp.ones((tm, LANES), BF16), preferred_element_type=F32)


def _moe_pre(xp1, xs1, sc_rows, sh_rows, g, wr, rb, swg, swu, swd):
    tp, d = xp1.shape
    ts = xs1.shape[0]
    t = tp + ts
    ne = wr.shape[1]
    ff = swg.shape[1]
    tm = _row_tile(ts, (256,))
    assert tp % tm == 0 and tm % (SUBLANES * MOD_ROWS) == 0
    ntp = tp // tm
    mr = tm // MOD_ROWS
    nsub = _token_rows(d)
    row = lambda i: (i, 0)
    col = lambda i: (0, i)
    const = lambda i: (0, 0)
    rb_col = jnp.broadcast_to(rb.reshape(ne, 1), (ne, LANES))
    return pl.pallas_call(
        functools.partial(_moe_pre_kernel, ntp=ntp),
        grid=(t // tm,),
        in_specs=_two_part_specs(tm, d, ntp) + [
                  pl.BlockSpec((mr, d), row), pl.BlockSpec((mr, d), row),
                  pl.BlockSpec((1, d), const), pl.BlockSpec((ne, d), const), pl.BlockSpec((ne, LANES), const),
                  pl.BlockSpec((d, ff), const), pl.BlockSpec((d, ff), const), pl.BlockSpec((ff, d), const)],
        out_specs=[pl.BlockSpec((tm * nsub, LANES), row), pl.BlockSpec((tm, d), row),
                   pl.BlockSpec((TOP_K, tm), col), pl.BlockSpec((TOP_K, tm), col),
                   pl.BlockSpec((ne, tm), col), pl.BlockSpec((ne, LANES), const)],
        out_shape=[jax.ShapeDtypeStruct((t * nsub, LANES), ROW_DTYPE), jax.ShapeDtypeStruct((t, d), F32),
                   jax.ShapeDtypeStruct((TOP_K, t), jnp.int32), jax.ShapeDtypeStruct((TOP_K, t), F32),
                   jax.ShapeDtypeStruct((ne, t), BF16), jax.ShapeDtypeStruct((ne, LANES), F32)],
        compiler_params=pltpu.CompilerParams(
            dimension_semantics=("arbitrary",), vmem_limit_bytes=VMEM_LIMIT),
        name="moe_pre",
    )(xp1, xs1, sc_rows, sh_rows, g.reshape(1, d), wr.T, rb_col, swg, swu, swd)


ROW_DTYPE = F32


def _token_rows(d):
    return d // LANES


def _read_rows(buf, nrows, nsub):
    return jnp.concatenate([buf[pl.ds(c, nrows, stride=nsub), :] for c in range(nsub)], axis=-1)


def _write_rows(ref, v):
    nrows, width = v.shape
    nsub = _token_rows(width)
    for c in range(nsub):
        ref[pl.ds(c, nrows, stride=nsub), :] = v[:, c * LANES:(c + 1) * LANES]


def _rank_kernel(m_ref, e_ref, ps_ref, d_ref, carry):
    i = pl.program_id(0)

    @pl.when(i == 0)
    def _():
        carry[...] = jnp.zeros_like(carry)

    m = m_ref[...]
    ne, tr = m.shape
    r_i = lax.broadcasted_iota(jnp.int32, (tr, tr), 0)
    c_i = lax.broadcasted_iota(jnp.int32, (tr, tr), 1)
    earlier = (r_i < c_i).astype(BF16)
    base = carry[...] + ps_ref[...]
    slot = jnp.dot(m, earlier, preferred_element_type=F32) + jnp.concatenate([base] * (tr // LANES), axis=1)
    row_e = lax.broadcasted_iota(jnp.int32, (ne, tr), 0)
    d_rows = [jnp.sum(jnp.where(row_e == e_ref[k:k + 1, :], slot, 0.0), axis=0, keepdims=True)
              for k in range(TOP_K)]
    d_ref[...] = jnp.concatenate(d_rows, axis=0).astype(jnp.int32)
    carry[...] += jnp.dot(m, jnp.ones((tr, LANES), BF16), preferred_element_type=F32)


def _rank(mask_t, eidx_t, pstart):
    ne, t = mask_t.shape
    tr = _row_tile(t, (512, 256))
    col = lambda i: (0, i)
    return pl.pallas_call(
        _rank_kernel,
        grid=(t // tr,),
        in_specs=[pl.BlockSpec((ne, tr), col), pl.BlockSpec((TOP_K, tr), col),
                  pl.BlockSpec((ne, LANES), lambda i: (0, 0))],
        out_specs=pl.BlockSpec((TOP_K, tr), col),
        out_shape=jax.ShapeDtypeStruct((TOP_K, t), jnp.int32),
        scratch_shapes=[pltpu.VMEM((ne, LANES), F32)],
        compiler_params=pltpu.CompilerParams(
            dimension_semantics=("arbitrary",), vmem_limit_bytes=VMEM_LIMIT),
        name="rank",
    )(mask_t, eidx_t, pstart)


def _dispatch_kernel(ps_ref, pl_ref, nu_ref, d_ref, h_hbm, xs_hbm, zbuf, stage, lsem, sem, zsem,
                     *, tt, nsub, bm, nb):
    i = pl.program_id(0)
    n = pl.num_programs(0)
    ne = ps_ref.shape[0]
    pieces = [1 << b for b in reversed(range(bm.bit_length() - 1))]

    def zero_fill(op):
        def pad(e, carry):
            start = ps_ref[e]
            length = pl_ref[e]
            for p in pieces:
                off = length - lax.rem(length, 2 * p)

                @pl.when(lax.rem(length, 2 * p) >= p)
                def _():
                    dst = xs_hbm.at[pl.ds(pl.multiple_of((start + off) * nsub, nsub), p * nsub)]
                    op(pltpu.make_async_copy(zbuf.at[pl.ds(0, p * nsub)], dst, zsem))
            return carry
        lax.fori_loop(0, ne, pad, 0)

        def tail(b, carry):
            dst = xs_hbm.at[pl.ds(pl.multiple_of(b * (bm * nsub), bm * nsub), bm * nsub)]
            op(pltpu.make_async_copy(zbuf, dst, zsem))
            return carry
        lax.fori_loop(nu_ref[0], nb, tail, 0)

    stages = stage.shape[0]
    rows = tt * nsub

    def load(tile):
        src = h_hbm.at[pl.ds(pl.multiple_of(tile * rows, rows), rows)]
        return pltpu.make_async_copy(src, stage.at[lax.rem(tile, stages)], lsem.at[lax.rem(tile, stages)])

    def wait_rows(tile):
        s = lax.rem(tile, stages)
        for _ in range(TOP_K):
            pltpu.make_async_copy(stage.at[s], xs_hbm.at[pl.ds(0, rows)], sem.at[s]).wait()

    @pl.when(i == 0)
    def _():
        zbuf[...] = jnp.zeros_like(zbuf)
        zero_fill(lambda c: c.start())
        load(0).start()

    @pl.when(i >= stages - 1)
    def _():
        wait_rows(i - (stages - 1))

    @pl.when(i + 1 < n)
    def _():
        load(i + 1).start()

    load(i).wait()
    s = lax.rem(i, stages)

    def body(r2, carry):
        for u in range(ROW_UNROLL):
            r = r2 * ROW_UNROLL + u
            src = stage.at[s, pl.ds(pl.multiple_of(r * nsub, nsub), nsub)]
            for k in range(TOP_K):
                dst = xs_hbm.at[pl.ds(pl.multiple_of(d_ref[0, 0, r * TOP_K + k] * nsub, nsub), nsub)]
                pltpu.make_async_copy(src, dst, sem.at[s]).start(priority=k % 2)
        return carry

    lax.fori_loop(0, tt // ROW_UNROLL, body, 0)

    @pl.when(i == n - 1)
    def _():
        for back in range(stages - 2, -1, -1):
            @pl.when(i >= back)
            def _():
                wait_rows(i - back)
        zero_fill(lambda c: c.wait())


def _dispatch(dest, h2_tiles, pad_start, pad_len, n_used, nb, nsub):
    t = dest.shape[0]
    tt = _row_tile(t, (256,))
    nt = t // tt
    bm = EXPERT_ROWS
    d3 = dest.reshape(nt, 1, tt * TOP_K)
    gs = pltpu.PrefetchScalarGridSpec(
        num_scalar_prefetch=3,
        grid=(nt,),
        in_specs=[pl.BlockSpec((1, 1, tt * TOP_K), lambda i, *_: (i, 0, 0), memory_space=pltpu.SMEM),
                  pl.BlockSpec(memory_space=pl.ANY)],
        out_specs=pl.BlockSpec(memory_space=pl.ANY),
        scratch_shapes=[pltpu.VMEM((bm * nsub, LANES), ROW_DTYPE),
                        pltpu.VMEM((DISPATCH_STAGES, tt * nsub, LANES), ROW_DTYPE),
                        pltpu.SemaphoreType.DMA((DISPATCH_STAGES,)),
                        pltpu.SemaphoreType.DMA((DISPATCH_STAGES,)),
                        pltpu.SemaphoreType.DMA],
    )
    return pl.pallas_call(
        functools.partial(_dispatch_kernel, tt=tt, nsub=nsub, bm=bm, nb=nb),
        grid_spec=gs,
        out_shape=jax.ShapeDtypeStruct((nb * bm * nsub, LANES), ROW_DTYPE),
        compiler_params=pltpu.CompilerParams(dimension_semantics=("arbitrary",)),
        name="dispatch",
    )(pad_start, pad_len, n_used, d3, h2_tiles)


def _grouped_kernel(b0_ref, nk_ref, nu_ref, x_hbm, wg_ref, wu_ref, wd_ref, y_hbm,
                    xbuf, ybuf, xsem, ysem, wgb, wub, wdb, *, bm, nsub, nb):
    e = pl.program_id(0)
    ring = xbuf.shape[0]
    rows = bm * nsub
    nu = nu_ref[0]

    def x_copy(gb, slot):
        src = x_hbm.at[pl.ds(pl.multiple_of(gb * rows, rows), rows)]
        return pltpu.make_async_copy(src, xbuf.at[slot], xsem.at[slot])

    def y_copy(gb, slot):
        dst = y_hbm.at[pl.ds(pl.multiple_of(gb * rows, rows), rows)]
        return pltpu.make_async_copy(ybuf.at[slot], dst, ysem.at[slot])

    @pl.when(e == 0)
    def _():
        for g0 in range(ring - 1):
            @pl.when(g0 < nu)
            def _():
                x_copy(g0, g0).start()

    @pl.when(nk_ref[e] > 0)
    def _():
        wgb[...] = wg_ref[0].astype(BF16)
        wub[...] = wu_ref[0].astype(BF16)
        wdb[...] = wd_ref[0].astype(BF16)

        def block(b, carry):
            gb = b0_ref[e] + b
            slot = lax.rem(gb, ring)

            @pl.when(gb + ring - 1 < nu)
            def _():
                x_copy(gb + ring - 1, lax.rem(gb + ring - 1, ring)).start()

            x_copy(gb, slot).wait()

            @pl.when(gb >= ring)
            def _():
                y_copy(gb - ring, slot).wait()

            x = _read_rows(xbuf.at[slot], bm, nsub).astype(BF16)
            hg = jnp.dot(x, wgb[...], preferred_element_type=F32)
            hu = jnp.dot(x, wub[...], preferred_element_type=F32)
            y = jnp.dot((_silu(hg) * hu).astype(BF16), wdb[...], preferred_element_type=F32)
            _write_rows(ybuf.at[slot], y)
            y_copy(gb, slot).start()
            return carry

        lax.fori_loop(0, nk_ref[e], block, 0)

    @pl.when(e == pl.num_programs(0) - 1)
    def _():
        for back in range(ring, 0, -1):
            @pl.when(nu >= back)
            def _():
                y_copy(nu - back, lax.rem(nu - back, ring)).wait()

        ybuf[0] = jnp.zeros(ybuf.shape[1:], ybuf.dtype)

        def tail(gb, carry):
            y_copy(gb, 0).start()
            return carry

        def tail_wait(gb, carry):
            y_copy(gb, 0).wait()
            return carry

        lax.fori_loop(nu, nb, tail, 0)
        lax.fori_loop(nu, nb, tail_wait, 0)


def _grouped(xs, blk_start, blk_count, n_used, wg, wu, wd):
    ne, d, ff = wg.shape
    nsub = _token_rows(d)
    bm = EXPERT_ROWS
    nb = xs.shape[0] // (bm * nsub)
    wspec = lambda shape: pl.BlockSpec(shape, lambda e, *_: (e, 0, 0))
    gs = pltpu.PrefetchScalarGridSpec(
        num_scalar_prefetch=3,
        grid=(ne,),
        in_specs=[pl.BlockSpec(memory_space=pl.ANY),
                  wspec((1, d, ff)), wspec((1, d, ff)), wspec((1, ff, d))],
        out_specs=pl.BlockSpec(memory_space=pl.ANY),
        scratch_shapes=[pltpu.VMEM((GROUP_RING, bm * nsub, LANES), ROW_DTYPE),
                        pltpu.VMEM((GROUP_RING, bm * nsub, LANES), ROW_DTYPE),
                        pltpu.SemaphoreType.DMA((GROUP_RING,)), pltpu.SemaphoreType.DMA((GROUP_RING,)),
                        pltpu.VMEM((d, ff), BF16), pltpu.VMEM((d, ff), BF16), pltpu.VMEM((ff, d), BF16)],
    )
    return pl.pallas_call(
        functools.partial(_grouped_kernel, bm=bm, nsub=nsub, nb=nb),
        grid_spec=gs,
        out_shape=jax.ShapeDtypeStruct(xs.shape, xs.dtype),
        compiler_params=pltpu.CompilerParams(
            dimension_semantics=("arbitrary",), vmem_limit_bytes=VMEM_LIMIT),
        name="grouped",
    )(blk_start, blk_count, n_used, xs, wg, wu, wd)


def _combine_kernel(d_ref, dn_ref, y_hbm, w_ref, s_ref, xp_ref, xs_ref, g2_ref, fg_ref, op_ref, os_ref,
                    buf_a, buf_b, obuf, sem_a, sem_b, *, tt, nsub, ntp):
    i = pl.program_id(0)
    n = pl.num_programs(0)

    def issue(dref, buf, sem, r):
        for k in range(TOP_K):
            src = y_hbm.at[pl.ds(pl.multiple_of(dref[0, 0, r * TOP_K + k] * nsub, nsub), nsub)]
            pltpu.make_async_copy(src, buf.at[k, pl.ds(pl.multiple_of(r * nsub, nsub), nsub)],
                                  sem).start(priority=k % 2)

    def wait_tile(buf, sem):
        for k in range(TOP_K):
            pltpu.make_async_copy(y_hbm.at[pl.ds(0, tt * nsub)], buf.at[k], sem).wait()

    @pl.when(i == 0)
    def _():
        def first(r, carry):
            issue(d_ref, buf_a, sem_a, r)
            return carry
        lax.fori_loop(0, tt, first, 0)

    is_prompt = i < ntp

    def step(cur, cur_sem, nxt, nxt_sem):
        wait_tile(cur, cur_sem)

        def chunk(c, carry):
            r0 = pl.multiple_of(c * SUBLANES, SUBLANES)
            for u in range(SUBLANES):
                issue(dn_ref, nxt, nxt_sem, r0 + u)
            acc = s_ref[pl.ds(r0, SUBLANES), :]
            w8 = w_ref[pl.ds(r0, SUBLANES), :]
            for k in range(TOP_K):
                rows = jnp.concatenate(
                    [cur[k, pl.ds(r0 * nsub + cc, SUBLANES, stride=nsub), :] for cc in range(nsub)], axis=-1)
                acc = acc + w8[:, k:k + 1] * rows
            g2 = g2_ref[pl.ds(c // (MOD_ROWS // SUBLANES), 1), :]
            x1 = jnp.where(is_prompt, xp_ref[pl.ds(r0, SUBLANES), :], xs_ref[pl.ds(r0, SUBLANES), :])
            obuf[pl.ds(r0, SUBLANES), :] = _rms(x1 + acc * g2, fg_ref[...], RMS_EPS)
            return carry

        lax.fori_loop(0, tt // SUBLANES, chunk, 0)

        @pl.when(i == n - 1)
        def _():
            wait_tile(nxt, nxt_sem)

    @pl.when(lax.rem(i, 2) == 0)
    def _():
        step(buf_a, sem_a, buf_b, sem_b)

    @pl.when(lax.rem(i, 2) == 1)
    def _():
        step(buf_b, sem_b, buf_a, sem_a)

    @pl.when(is_prompt)
    def _():
        op_ref[...] = obuf[...]

    @pl.when(jnp.logical_not(is_prompt))
    def _():
        os_ref[...] = obuf[...]


def _combine(dest, y_tiles, wts, shared, xp1, xs1, g2_rows, final_g):
    tp, d = xp1.shape
    ts = xs1.shape[0]
    nsub = _token_rows(d)
    tt = _row_tile(ts, (256,))
    assert tp % tt == 0 and tt % (SUBLANES * MOD_ROWS) == 0
    ntp = tp // tt
    mr = tt // MOD_ROWS
    nt = (tp + ts) // tt
    d3 = dest.reshape(nt, 1, tt * TOP_K)
    kern = functools.partial(_combine_kernel, tt=tt, nsub=nsub, ntp=ntp)
    smem_blk = lambda f: pl.BlockSpec((1, 1, tt * TOP_K), f, memory_space=pltpu.SMEM)
    row = lambda i: (i, 0)
    two = _two_part_specs(tt, d, ntp)
    return pl.pallas_call(
        kern,
        grid=(nt,),
        in_specs=[smem_blk(lambda i: (i, 0, 0)),
                  smem_blk(lambda i: (jnp.minimum(i + 1, nt - 1), 0, 0)),
                  pl.BlockSpec(memory_space=pl.ANY),
                  pl.BlockSpec((tt, LANES), row), pl.BlockSpec((tt, d), row)] + two + [
                  pl.BlockSpec((mr, d), row), pl.BlockSpec((1, d), lambda i: (0, 0))],
        out_specs=two,
        out_shape=[jax.ShapeDtypeStruct((tp, d), F32), jax.ShapeDtypeStruct((ts, d), F32)],
        scratch_shapes=[pltpu.VMEM((TOP_K, tt * nsub, LANES), ROW_DTYPE),
                        pltpu.VMEM((TOP_K, tt * nsub, LANES), ROW_DTYPE), pltpu.VMEM((tt, d), F32),
                        pltpu.SemaphoreType.DMA, pltpu.SemaphoreType.DMA],
        compiler_params=pltpu.CompilerParams(
            dimension_semantics=("arbitrary",), vmem_limit_bytes=VMEM_LIMIT),
        name="combine",
    )(d3, d3, y_tiles, wts, shared, xp1, xs1, g2_rows, final_g.reshape(1, d))


def _plan(counts, n_assign):
    ne = counts.shape[0]
    bm = EXPERT_ROWS
    counts = counts.astype(jnp.int32)
    nblk = (counts + bm - 1) // bm
    bend = jnp.cumsum(nblk)
    pstart = jnp.broadcast_to(((bend - nblk) * bm).astype(F32).reshape(ne, 1), (ne, LANES))
    nb = -(-(n_assign + ne * (bm - 1)) // bm)
    n_used = bend[-1:].astype(jnp.int32)
    pad_start = ((bend - nblk) * bm + counts).astype(jnp.int32)
    pad_len = (nblk * bm - counts).astype(jnp.int32)
    return pstart, (bend - nblk).astype(jnp.int32), nblk, n_used, pad_start, pad_len, nb


def per_g_pow2(ne):
    per_g = ne // N_EXPERT_GROUPS
    return per_g * N_EXPERT_GROUPS == ne and per_g & (per_g - 1) == 0


def _mod_rows(m, nbp, lp):
    return jnp.concatenate([jnp.repeat(m[:nbp], lp // MOD_ROWS, axis=0), m[nbp:]], axis=0)


def kernel(x_prompt, x_sample, state_ssm, state_conv, state_pool, c_prompt, c_sample, ln1_g, ln2_g, w_ada, b_ada, w_in, conv_w, conv_b, dt_bias, a_log, d_skip, ssd_norm_g, w_ssd_out, pool_w, pool_scale, w_out, w_router, router_bias, moe_w_gate, moe_w_up, moe_w_down, shared_w_gate, shared_w_up, shared_w_down, final_g):
    bp, lp, dm = x_prompt.shape
    bs, ls, _ = x_sample.shape
    depth = ln1_g.shape[0]
    heads, hd, ns = state_ssm.shape[2], state_ssm.shape[3], state_ssm.shape[4]
    di = heads * hd
    gn = SSD_GROUPS * ns
    cch = di + 2 * gn
    assert depth == 1 and ls == MOD_ROWS and lp % MOD_ROWS == 0 and heads <= LANES
    assert per_g_pow2(w_router.shape[2])
    assert ns == 2 * hd and hd & (hd - 1) == 0 and (heads // SSD_GROUPS) % 2 == 0
    assert all(w & (w - 1) == 0 for w in POOL_WINDOWS) and state_pool.shape[2] == max(POOL_WINDOWS) - 1
    tp, ts = bp * lp, bs * ls
    expand3 = (jnp.arange(3 * LANES)[:, None] % LANES == jnp.arange(di)[None, :] // hd).astype(BF16)

    xp, xs = x_prompt.reshape(tp, dm), x_sample.reshape(ts, dm)
    c_all = jnp.concatenate([c_prompt, c_sample], axis=0)

    o1, o2, o3, o4 = di, di + cch, di + cch + heads, di + cch + heads + dm
    cols = (0, di, 2 * di, di + cch, di + cch + dm)

    ssm_p, conv_p, pool_p, ssm_s, conv_s, pool_s = [], [], [], [], [], []
    for l in range(depth):
        wi = w_in[l]
        wcat = jnp.concatenate(
            [wi[:, :o1], wi[:, o1:o2], wi[:, o3:o4], wi[:, o4:],
             jnp.pad(wi[:, o2:o3], ((0, 0), (0, LANES - heads)))], axis=1).astype(BF16)
        pad_h = lambda v: jnp.pad(v.reshape(1, heads), ((0, 0), (0, LANES - heads)))
        seq_w = (conv_w[l], conv_b[l].reshape(1, cch), pad_h(dt_bias[l]), pad_h(a_log[l]),
                 jnp.repeat(d_skip[l], hd).reshape(1, di), ssd_norm_g[l].reshape(1, di),
                 w_ssd_out[l].astype(BF16), pool_w[l].astype(BF16), pool_scale[l].reshape(1, dm),
                 w_out[l].astype(BF16), expand3)

        mod = _ada(c_all, w_ada[l], b_ada[l])
        sh1, sc1, g1, sh2, sc2, g2 = jnp.split(mod, 6, axis=-1)

        proj, dtp = _inproj(xp, xs, _mod_rows(sc1, bp, lp), _mod_rows(sh1, bp, lp), ln1_g[l], wcat)

        zc = jnp.zeros((bp,) + state_conv.shape[2:], F32)
        zs = jnp.zeros((bp, heads, hd, ns), F32)
        zp = jnp.zeros((bp,) + state_pool.shape[2:], F32)
        xp1, ns_p, nc_p, np_p = _seq(proj, dtp, xp, g1[:bp], zc, zs, zp, seq_w,
                                     row0=0, nb=bp, seqlen=lp, pos0=0, cols=cols)
        xs1, ns_s, nc_s, np_s = _seq(proj, dtp, xs, g1[bp:], state_conv[l], state_ssm[l], state_pool[l],
                                     seq_w, row0=tp, nb=bs, seqlen=ls, pos0=PAST_LEN, cols=cols)
        conv_p.append(nc_p)
        conv_s.append(nc_s)
        pool_p.append(np_p)
        pool_s.append(np_s)
        ssm_p.append(ns_p)
        ssm_s.append(ns_s)

        h2_tiles, shared, eidx_t, wts_t, mask_t, counts = _moe_pre(
            xp1, xs1, _mod_rows(sc2, bp, lp), _mod_rows(sh2, bp, lp), ln2_g[l],
            w_router[l].astype(BF16), router_bias[l],
            shared_w_gate[l].astype(BF16), shared_w_up[l].astype(BF16), shared_w_down[l].astype(BF16))
        nsub = _token_rows(dm)
        pstart, blk_start, blk_count, n_used, pad_start, pad_len, nb = _plan(counts[:, 0], (tp + ts) * TOP_K)
        dest = _rank(mask_t, eidx_t, pstart).T
        wts = jnp.pad(wts_t.T, ((0, 0), (0, LANES - TOP_K)))
        x_sorted = _dispatch(dest, h2_tiles, pad_start, pad_len, n_used, nb, nsub)
        y_tiles = _grouped(x_sorted, blk_start, blk_count, n_used,
                           moe_w_gate[l], moe_w_up[l], moe_w_down[l])
        xp, xs = _combine(dest, y_tiles, wts, shared, xp1, xs1, _mod_rows(g2, bp, lp), final_g)

    y_prompt = xp.reshape(bp, lp, dm)
    y_sample = xs.reshape(bs, ls, dm)
    return (y_prompt, y_sample, jnp.stack(ssm_p), jnp.stack(conv_p), jnp.stack(pool_p),
            jnp.stack(ssm_s), jnp.stack(conv_s), jnp.stack(pool_s))
```

```python
import functools

import jax
import jax.numpy as jnp
from jax import lax
from jax.experimental import pallas as pl
from jax.experimental.pallas import tpu as pltpu

F32 = jnp.float32
BF16 = jnp.bfloat16
HIGHEST = lax.Precision.HIGHEST

RMS_EPS = 1e-6
SSD_NORM_EPS = 1e-5
CHUNK = 64
SSD_GROUPS = 4
POOL_WINDOWS = (2, 4, 8, 16)
PAST_LEN = 1024
TOP_K = 8
N_EXPERT_GROUPS = 8
TOPK_GROUPS = 4
ROUTED_SCALE = 2.5

LANES = 128
SUBLANES = 8
MOD_ROWS = 32
VMEM_LIMIT = 56 * 1024 * 1024
EXPERT_ROWS = 256
SEQ_TILE = 256
GROUP_RING = 4
ROW_UNROLL = 4
DISPATCH_STAGES = 3


def _sigmoid(x):
    return 0.5 * jnp.tanh(0.5 * x) + 0.5


def _silu(x):
    h = 0.5 * x
    return h + h * jnp.tanh(h)


def _softplus(x):
    return jnp.maximum(x, 0.0) + jnp.log1p(jnp.exp(-jnp.abs(x)))


def _row_tile(n, prefs):
    for t in prefs:
        if n % t == 0:
            return t
    return n


def _modulate(y, sc_ref, sh_ref):
    rows, d = y.shape
    y3 = y.reshape(rows // MOD_ROWS, MOD_ROWS, d)
    y3 = y3 * (1.0 + sc_ref[...][:, None, :]) + sh_ref[...][:, None, :]
    return y3.reshape(rows, d)


def _rms(x, g, eps):
    return x * lax.rsqrt(jnp.mean(x * x, axis=-1, keepdims=True) + eps) * g


def _ada_kernel(c_ref, w_ref, b_ref, o_ref):
    s = _silu(c_ref[...])
    o_ref[...] = jnp.dot(s, w_ref[...], preferred_element_type=F32, precision=HIGHEST) + b_ref[...]


def _ada(c_all, w_ada, b_ada):
    n, d = c_all.shape
    dout = w_ada.shape[1]
    tn = _row_tile(dout, (1024, 512, 256, 128))
    return pl.pallas_call(
        _ada_kernel,
        grid=(dout // tn,),
        in_specs=[pl.BlockSpec((n, d), lambda j: (0, 0)),
                  pl.BlockSpec((d, tn), lambda j: (0, j)),
                  pl.BlockSpec((1, tn), lambda j: (0, j))],
        out_specs=pl.BlockSpec((n, tn), lambda j: (0, j)),
        out_shape=jax.ShapeDtypeStruct((n, dout), F32),
        name="ada",
    )(c_all, w_ada, b_ada.reshape(1, dout))


def _two_part_specs(tm, d, ntp):
    return [pl.BlockSpec((tm, d), lambda i, *_: (jnp.minimum(i, ntp - 1), 0)),
            pl.BlockSpec((tm, d), lambda i, *_: (jnp.maximum(i - ntp, 0), 0))]


def _inproj_kernel(xp_ref, xs_ref, sc_ref, sh_ref, g_ref, w_ref, o_ref, dt_ref, *, ntp, tn):
    x = jnp.where(pl.program_id(0) < ntp, xp_ref[...], xs_ref[...])
    h = _modulate(_rms(x, g_ref[...], RMS_EPS), sc_ref, sh_ref).astype(BF16)
    n_main = o_ref.shape[1]
    for c0 in range(0, n_main, tn):
        o_ref[:, c0:c0 + tn] = jnp.dot(h, w_ref[:, c0:c0 + tn], preferred_element_type=F32).astype(BF16)
    dt_ref[...] = jnp.dot(h, w_ref[:, n_main:], preferred_element_type=F32)


def _inproj(xp, xs, sc_rows, sh_rows, g, wcat):
    tp, d = xp.shape
    ts = xs.shape[0]
    n = wcat.shape[1]
    n_main = n - LANES
    tm = _row_tile(ts, (512, 256))
    assert tp % tm == 0 and tm % (SUBLANES * MOD_ROWS) == 0
    ntp = tp // tm
    nt = ntp + ts // tm
    mr = tm // MOD_ROWS
    tn = _row_tile(n_main, (2048, 1024, 512, 256, 128))
    row = lambda i: (i, 0)
    return pl.pallas_call(
        functools.partial(_inproj_kernel, ntp=ntp, tn=tn),
        grid=(nt,),
        in_specs=_two_part_specs(tm, d, ntp) + [
            pl.BlockSpec((mr, d), row), pl.BlockSpec((mr, d), row),
            pl.BlockSpec((1, d), lambda i: (0, 0)),
            pl.BlockSpec((d, n), lambda i: (0, 0), pipeline_mode=pl.Buffered(1))],
        out_specs=[pl.BlockSpec((tm, n_main), row), pl.BlockSpec((tm, LANES), row)],
        out_shape=[jax.ShapeDtypeStruct((tp + ts, n_main), BF16),
                   jax.ShapeDtypeStruct((tp + ts, LANES), F32)],
        compiler_params=pltpu.CompilerParams(
            dimension_semantics=("parallel",), vmem_limit_bytes=VMEM_LIMIT),
        name="inproj",
    )(xp, xs, sc_rows, sh_rows, g.reshape(1, d), wcat)


def _seq_kernel(z_ref, xp_ref, bc_ref, u_ref, gt_ref, dt_ref, x_ref, g1_ref,
                cst_ref, sst_ref, pst_ref, cw_ref, cb_ref, dtb_ref, alog_ref, dx_ref, ng_ref,
                wssd_ref, pw_ref, ps_ref, wout_ref, e3_ref,
                xo_ref, so_ref, co_ref, po_ref,
                cbuf, pbuf, st_scr, y_scr, xbc_scr, csx_scr, wvx_scr,
                *, tq, q, pos0, heads, hd, ns, dm):
    j = pl.program_id(1)
    di = heads * hd
    gn = SSD_GROUPS * ns
    hpg = heads // SSD_GROUPS
    cw = cbuf.shape[0] - tq
    kw = cw_ref.shape[0]
    ph = pbuf.shape[0] - tq

    @pl.when(j == 0)
    def _():
        cbuf[0:cw - (kw - 1), :] = jnp.zeros((cw - (kw - 1), cbuf.shape[1]), F32)
        cbuf[cw - (kw - 1):cw, :] = cst_ref[0]
        pbuf[0:1, :] = jnp.zeros((1, dm), F32)
        pbuf[1:ph, :] = pst_ref[0]
        st_scr[...] = sst_ref[0].T

    cbuf[cw:cw + tq, 0:di] = xp_ref[...].astype(F32)
    cbuf[cw:cw + tq, di:di + 2 * gn] = bc_ref[...].astype(F32)
    ext = cbuf[...]
    acc = cb_ref[...] + ext[cw:cw + tq, :] * cw_ref[kw - 1:kw, :]
    for k in range(kw - 1):
        acc = acc + pltpu.roll(ext, kw - 1 - k, axis=0)[cw:cw + tq, :] * cw_ref[k:k + 1, :]
    xbc_scr[...] = _silu(acc)
    cbuf[cw - (kw - 1):cw, :] = cbuf[cw + tq - (kw - 1):cw + tq, :]

    dt = _softplus(dt_ref[...] + dtb_ref[...])
    dta = dt * (-jnp.exp(alog_ref[...]))
    lq = q.bit_length() - 1
    r_i = lax.broadcasted_iota(jnp.int32, (tq, tq), 0)
    c_i = lax.broadcasted_iota(jnp.int32, (tq, tq), 1)
    same = lax.shift_right_logical(r_i, lq) == lax.shift_right_logical(c_i, lq)
    tril = jnp.where(same, (r_i >= c_i).astype(F32), 0.0)
    cs = jnp.dot(tril, dta, preferred_element_type=F32, precision=HIGHEST)
    cs_end = jnp.dot(same.astype(F32), dta, preferred_element_type=F32, precision=HIGHEST)
    wv = dt * jnp.exp(cs_end - cs)

    both = jnp.concatenate([cs, wv], axis=0)
    hi = both.astype(BF16)
    r1 = both - hi.astype(F32)
    mid = r1.astype(BF16)
    lo = (r1 - mid.astype(F32)).astype(BF16)
    ex = jnp.dot(jnp.concatenate([hi, mid, lo], axis=1), e3_ref[...], preferred_element_type=F32)
    csx_scr[...] = ex[0:tq]
    wvx_scr[...] = ex[tq:2 * tq]
    cs_t = cs.T
    dt_t = dt.T

    lane = lax.broadcasted_iota(jnp.int32, (q, 2 * hd), 1)
    kpos = jnp.bitwise_and(lane, hd - 1)
    causal2 = jnp.logical_and(lax.broadcasted_iota(jnp.int32, (q, 2 * hd), 0) >= kpos, kpos < q)
    first = lane < hd
    zrow = jnp.zeros((1, hd - q), F32)
    zblk = jnp.zeros((hd - q, 2 * hd), BF16)

    def pair_row(t, p, r0):
        parts = []
        for h in (2 * p, 2 * p + 1):
            parts.append(t[h:h + 1, r0:r0 + q])
            if q < hd:
                parts.append(zrow)
        return jnp.concatenate(parts, axis=1)

    def pair_rows(a, b):
        blocks = [a, zblk, b, zblk] if q < hd else [a, b]
        return jnp.concatenate(blocks, axis=0)

    gw = di // SSD_GROUPS
    for c in range(tq // q):
        r0 = c * q
        for g in range(SSD_GROUPS):
            bgb = xbc_scr[r0:r0 + q, di + g * ns:di + (g + 1) * ns].astype(BF16)
            cgb = xbc_scr[r0:r0 + q, di + gn + g * ns:di + gn + (g + 1) * ns].astype(BF16)
            cb2 = lax.dot_general(cgb, pair_rows(bgb, bgb), (((1,), (1,)), ((), ())),
                                  preferred_element_type=F32)
            stg = st_scr[:, g * gw:(g + 1) * gw]
            ecs = jnp.exp(csx_scr[r0:r0 + q, g * gw:(g + 1) * gw])
            yo = jnp.dot(cgb, stg.astype(BF16), preferred_element_type=F32) * ecs
            for pp in range(hpg // 2):
                p = g * (hpg // 2) + pp
                lo_l = p * 2 * hd
                seg = csx_scr[r0:r0 + q, lo_l:lo_l + 2 * hd] - pair_row(cs_t, p, r0)
                lm = jnp.exp(jnp.where(causal2, seg, -jnp.inf))
                m2 = (cb2 * lm * pair_row(dt_t, p, r0)).astype(BF16)
                xpair = xbc_scr[r0:r0 + q, lo_l:lo_l + 2 * hd]
                rhs = pair_rows(jnp.where(first, xpair, 0.0).astype(BF16),
                                jnp.where(first, 0.0, xpair).astype(BF16))
                yd = jnp.dot(m2, rhs, preferred_element_type=F32)
                y_scr[r0:r0 + q, lo_l:lo_l + 2 * hd] = yd + yo[:, pp * 2 * hd:(pp + 1) * 2 * hd]
            xw = (xbc_scr[r0:r0 + q, g * gw:(g + 1) * gw]
                  * wvx_scr[r0:r0 + q, g * gw:(g + 1) * gw]).astype(BF16)
            upd = lax.dot_general(bgb, xw, (((0,), (0,)), ((), ())), preferred_element_type=F32)
            st_scr[:, g * gw:(g + 1) * gw] = stg * ecs[q - 1:q, :] + upd

    @pl.when(j == pl.num_programs(1) - 1)
    def _():
        so_ref[0] = st_scr[...].T
        co_ref[0] = cbuf[cw - (kw - 1):cw, :]

    y = y_scr[...] + xbc_scr[:, 0:di] * dx_ref[...]
    v = y * _silu(z_ref[...].astype(F32))
    gw = di // SSD_GROUPS
    parts = []
    for g in range(SSD_GROUPS):
        vg = v[:, g * gw:(g + 1) * gw]
        parts.append(vg * lax.rsqrt(jnp.mean(vg * vg, axis=-1, keepdims=True) + SSD_NORM_EPS))
    yn = (jnp.concatenate(parts, axis=-1) * ng_ref[...]).astype(BF16)
    a_br = jnp.dot(yn, wssd_ref[...], preferred_element_type=F32)

    pbuf[ph:ph + tq, :] = u_ref[...].astype(F32)
    pos = pos0 + j * tq + lax.broadcasted_iota(jnp.int32, (tq, 1), 0)
    pgd = dm // len(POOL_WINDOWS)
    b_parts = []
    for gi, w in enumerate(POOL_WINDOWS):
        lo = gi * pgd
        s = pbuf[:, lo:lo + pgd]
        step = 1
        while step < w:
            s = s + pltpu.roll(s, step, axis=0)
            step *= 2
        cnt = jnp.minimum(w, pos + 1).astype(F32)
        pooled = s[ph:ph + tq, :] / cnt - pbuf[ph:ph + tq, lo:lo + pgd]
        b_parts.append(jnp.dot(pooled.astype(BF16), pw_ref[gi], preferred_element_type=F32))
    b_br = jnp.concatenate(b_parts, axis=-1) * ps_ref[...]
    pbuf[1:ph, :] = pbuf[tq + 1:tq + ph, :]

    @pl.when(j == pl.num_programs(1) - 1)
    def _():
        po_ref[0] = pbuf[1:ph, :]

    gates = _sigmoid(gt_ref[...].astype(F32))
    merged = gates[:, 0:dm] * a_br + gates[:, dm:2 * dm] * b_br
    mix = jnp.dot(merged.astype(BF16), wout_ref[...], preferred_element_type=F32)
    xo_ref[...] = x_ref[...] + g1_ref[0] * mix


def _seq(proj, dtp, x, g1, conv_state, ssm_state, pool_state, wts, *, row0, nb, seqlen, pos0, cols):
    dm = x.shape[1]
    heads, hd, ns = ssm_state.shape[1], ssm_state.shape[2], ssm_state.shape[3]
    di = heads * hd
    gn = SSD_GROUPS * ns
    q = min(CHUNK, seqlen)
    tq = _row_tile(seqlen, (SEQ_TILE, 128, 64, 32))
    nj = seqlen // tq
    rb0 = row0 // tq
    oz, ox, obc, ou, og = cols

    def rows(b, j):
        return rb0 + b * nj + j

    full = lambda a: pl.BlockSpec(a.shape, lambda b, j: (0,) * a.ndim, pipeline_mode=pl.Buffered(1))
    kern = functools.partial(_seq_kernel, tq=tq, q=q, pos0=pos0, heads=heads, hd=hd, ns=ns, dm=dm)
    kw1 = conv_state.shape[1]
    pst = pool_state.shape[1]
    xo, so, co, po = pl.pallas_call(
        kern,
        grid=(nb, nj),
        in_specs=[
            pl.BlockSpec((tq, di), lambda b, j: (rows(b, j), oz // di)),
            pl.BlockSpec((tq, di), lambda b, j: (rows(b, j), ox // di)),
            pl.BlockSpec((tq, 2 * gn), lambda b, j: (rows(b, j), obc // (2 * gn))),
            pl.BlockSpec((tq, dm), lambda b, j: (rows(b, j), ou // dm)),
            pl.BlockSpec((tq, 2 * dm), lambda b, j: (rows(b, j), og // (2 * dm))),
            pl.BlockSpec((tq, LANES), lambda b, j: (rows(b, j), 0)),
            pl.BlockSpec((tq, dm), lambda b, j: (b * nj + j, 0)),
            pl.BlockSpec((1, 1, dm), lambda b, j: (b, 0, 0)),
            pl.BlockSpec((1, kw1, di + 2 * gn), lambda b, j: (b, 0, 0)),
            pl.BlockSpec((1, di, ns), lambda b, j: (b, 0, 0)),
            pl.BlockSpec((1, pst, dm), lambda b, j: (b, 0, 0)),
        ] + [full(w) for w in wts],
        out_specs=[pl.BlockSpec((tq, dm), lambda b, j: (b * nj + j, 0)),
                   pl.BlockSpec((1, di, ns), lambda b, j: (b, 0, 0)),
                   pl.BlockSpec((1, kw1, di + 2 * gn), lambda b, j: (b, 0, 0)),
                   pl.BlockSpec((1, pst, dm), lambda b, j: (b, 0, 0))],
        out_shape=[jax.ShapeDtypeStruct((nb * seqlen, dm), F32),
                   jax.ShapeDtypeStruct((nb, di, ns), F32),
                   jax.ShapeDtypeStruct((nb, kw1, di + 2 * gn), F32),
                   jax.ShapeDtypeStruct((nb, pst, dm), F32)],
        scratch_shapes=[pltpu.VMEM((tq + SUBLANES, di + 2 * gn), F32),
                        pltpu.VMEM((tq + 2 * SUBLANES, dm), F32),
                        pltpu.VMEM((ns, di), F32),
                        pltpu.VMEM((tq, di), F32),
                        pltpu.VMEM((tq, di + 2 * gn), F32),
                        pltpu.VMEM((tq, di), F32),
                        pltpu.VMEM((tq, di), F32)],
        compiler_params=pltpu.CompilerParams(
            dimension_semantics=("parallel", "arbitrary"), vmem_limit_bytes=VMEM_LIMIT),
        name="seq",
    )(proj, proj, proj, proj, proj, dtp, x, g1.reshape(nb, 1, dm),
      conv_state, ssm_state.reshape(nb, di, ns), pool_state, *wts)
    return xo, so.reshape(nb, heads, hd, ns), co, po


def _moe_pre_kernel(xp_ref, xs_ref, sc_ref, sh_ref, g_ref, wr_ref, rb_ref, swg_ref, swu_ref, swd_ref,
                    h_ref, s_ref, e_ref, w_ref, m_ref, cnt_ref, *, ntp):
    x = jnp.where(pl.program_id(0) < ntp, xp_ref[...], xs_ref[...])
    y = _rms(x, g_ref[...], RMS_EPS)
    h = _modulate(y, sc_ref, sh_ref)
    _write_rows(h_ref, h)
    hb = h.astype(BF16)

    sg = jnp.dot(hb, swg_ref[...], preferred_element_type=F32)
    su = jnp.dot(hb, swu_ref[...], preferred_element_type=F32)
    s_ref[...] = jnp.dot((_silu(sg) * su).astype(BF16), swd_ref[...], preferred_element_type=F32)

    scores = jax.nn.sigmoid(lax.dot_general(wr_ref[...], hb, (((1,), (1,)), ((), ())),
                                            preferred_element_type=F32))
    biased = scores + rb_ref[:, 0:1]
    ne, tm = scores.shape
    per_g = ne // N_EXPERT_GROUPS
    neg = -jnp.inf
    row_e = lax.broadcasted_iota(jnp.int32, (ne, tm), 0).astype(F32)
    row_g = lax.broadcasted_iota(jnp.int32, (N_EXPERT_GROUPS, tm), 0).astype(F32)

    def first_argmax(v, idx, big):
        top = jnp.max(v, axis=0, keepdims=True)
        return top, jnp.min(jnp.where(v == top, idx, big), axis=0, keepdims=True)

    gs = jnp.full((N_EXPERT_GROUPS, tm), neg, F32)
    for g in range(N_EXPERT_GROUPS):
        mg = biased[g * per_g:(g + 1) * per_g, :]
        rg = (lax.broadcasted_iota(jnp.int32, (per_g, tm), 0) + g * per_g).astype(F32)
        t1, i1 = first_argmax(mg, rg, float(ne))
        t2 = jnp.max(jnp.where(rg == i1, neg, mg), axis=0, keepdims=True)
        gs = jnp.where(row_g == g, t1 + t2, gs)
    grp_e = lax.shift_right_logical(lax.broadcasted_iota(jnp.int32, (ne, tm), 0),
                                    per_g.bit_length() - 1).astype(F32)
    allowed = jnp.zeros((ne, tm), F32)
    for _ in range(TOPK_GROUPS):
        _, gi = first_argmax(gs, row_g, float(N_EXPERT_GROUPS))
        allowed = jnp.where(grp_e == gi, 1.0, allowed)
        gs = jnp.where(row_g == gi, neg, gs)
    mb = jnp.where(allowed > 0.0, biased, neg)
    row_k = lax.broadcasted_iota(jnp.int32, (TOP_K, tm), 0)
    eacc = jnp.zeros((TOP_K, tm), F32)
    wacc = jnp.zeros((TOP_K, tm), F32)
    chosen = jnp.zeros((ne, tm), F32)
    for k in range(TOP_K):
        _, ik = first_argmax(mb, row_e, float(ne))
        sel = row_e == ik
        wk = jnp.sum(jnp.where(sel, scores, 0.0), axis=0, keepdims=True)
        eacc = jnp.where(row_k == k, ik, eacc)
        wacc = jnp.where(row_k == k, wk, wacc)
        chosen = jnp.where(sel, 1.0, chosen)
        mb = jnp.where(sel, neg, mb)
    wsum = jnp.sum(wacc, axis=0, keepdims=True)
    e_ref[...] = eacc.astype(jnp.int32)
    w_ref[...] = wacc / (wsum + 1e-20) * ROUTED_SCALE
    chosen_b = chosen.astype(BF16)
    m_ref[...] = chosen_b

    @pl.when(pl.program_id(0) == 0)
    def _():
        cnt_ref[...] = jnp.zeros_like(cnt_ref)

    cnt_ref[...] += jnp.dot(chosen_b, jnp.ones((tm, LANES), BF16), preferred_element_type=F32)


def _moe_pre(xp1, xs1, sc_rows, sh_rows, g, wr, rb, swg, swu, swd):
    tp, d = xp1.shape
    ts = xs1.shape[0]
    t = tp + ts
    ne = wr.shape[1]
    ff = swg.shape[1]
    tm = _row_tile(ts, (256,))
    assert tp % tm == 0 and tm % (SUBLANES * MOD_ROWS) == 0
    ntp = tp // tm
    mr = tm // MOD_ROWS
    nsub = _token_rows(d)
    row = lambda i: (i, 0)
    col = lambda i: (0, i)
    const = lambda i: (0, 0)
    rb_col = jnp.broadcast_to(rb.reshape(ne, 1), (ne, LANES))
    return pl.pallas_call(
        functools.partial(_moe_pre_kernel, ntp=ntp),
        grid=(t // tm,),
        in_specs=_two_part_specs(tm, d, ntp) + [
                  pl.BlockSpec((mr, d), row), pl.BlockSpec((mr, d), row),
                  pl.BlockSpec((1, d), const), pl.BlockSpec((ne, d), const), pl.BlockSpec((ne, LANES), const),
                  pl.BlockSpec((d, ff), const), pl.BlockSpec((d, ff), const), pl.BlockSpec((ff, d), const)],
        out_specs=[pl.BlockSpec((tm * nsub, LANES), row), pl.BlockSpec((tm, d), row),
                   pl.BlockSpec((TOP_K, tm), col), pl.BlockSpec((TOP_K, tm), col),
                   pl.BlockSpec((ne, tm), col), pl.BlockSpec((ne, LANES), const)],
        out_shape=[jax.ShapeDtypeStruct((t * nsub, LANES), ROW_DTYPE), jax.ShapeDtypeStruct((t, d), F32),
                   jax.ShapeDtypeStruct((TOP_K, t), jnp.int32), jax.ShapeDtypeStruct((TOP_K, t), F32),
                   jax.ShapeDtypeStruct((ne, t), BF16), jax.ShapeDtypeStruct((ne, LANES), F32)],
        compiler_params=pltpu.CompilerParams(
            dimension_semantics=("arbitrary",), vmem_limit_bytes=VMEM_LIMIT),
        name="moe_pre",
    )(xp1, xs1, sc_rows, sh_rows, g.reshape(1, d), wr.T, rb_col, swg, swu, swd)


ROW_DTYPE = F32


def _token_rows(d):
    return d // LANES


def _read_rows(buf, nrows, nsub):
    return jnp.concatenate([buf[pl.ds(c, nrows, stride=nsub), :] for c in range(nsub)], axis=-1)


def _write_rows(ref, v):
    nrows, width = v.shape
    nsub = _token_rows(width)
    for c in range(nsub):
        ref[pl.ds(c, nrows, stride=nsub), :] = v[:, c * LANES:(c + 1) * LANES]


def _rank_kernel(m_ref, e_ref, ps_ref, d_ref, carry):
    i = pl.program_id(0)

    @pl.when(i == 0)
    def _():
        carry[...] = jnp.zeros_like(carry)

    m = m_ref[...]
    ne, tr = m.shape
    r_i = lax.broadcasted_iota(jnp.int32, (tr, tr), 0)
    c_i = lax.broadcasted_iota(jnp.int32, (tr, tr), 1)
    earlier = (r_i < c_i).astype(BF16)
    base = carry[...] + ps_ref[...]
    slot = jnp.dot(m, earlier, preferred_element_type=F32) + jnp.concatenate([base] * (tr // LANES), axis=1)
    row_e = lax.broadcasted_iota(jnp.int32, (ne, tr), 0)
    d_rows = [jnp.sum(jnp.where(row_e == e_ref[k:k + 1, :], slot, 0.0), axis=0, keepdims=True)
              for k in range(TOP_K)]
    d_ref[...] = jnp.concatenate(d_rows, axis=0).astype(jnp.int32)
    carry[...] += jnp.dot(m, jnp.ones((tr, LANES), BF16), preferred_element_type=F32)


def _rank(mask_t, eidx_t, pstart):
    ne, t = mask_t.shape
    tr = _row_tile(t, (512, 256))
    col = lambda i: (0, i)
    return pl.pallas_call(
        _rank_kernel,
        grid=(t // tr,),
        in_specs=[pl.BlockSpec((ne, tr), col), pl.BlockSpec((TOP_K, tr), col),
                  pl.BlockSpec((ne, LANES), lambda i: (0, 0))],
        out_specs=pl.BlockSpec((TOP_K, tr), col),
        out_shape=jax.ShapeDtypeStruct((TOP_K, t), jnp.int32),
        scratch_shapes=[pltpu.VMEM((ne, LANES), F32)],
        compiler_params=pltpu.CompilerParams(
            dimension_semantics=("arbitrary",), vmem_limit_bytes=VMEM_LIMIT),
        name="rank",
    )(mask_t, eidx_t, pstart)


def _dispatch_kernel(ps_ref, pl_ref, nu_ref, d_ref, h_hbm, xs_hbm, zbuf, stage, lsem, sem, zsem,
                     *, tt, nsub, bm, nb):
    i = pl.program_id(0)
    n = pl.num_programs(0)
    ne = ps_ref.shape[0]
    pieces = [1 << b for b in reversed(range(bm.bit_length() - 1))]

    def zero_fill(op):
        def pad(e, carry):
            start = ps_ref[e]
            length = pl_ref[e]
            for p in pieces:
                off = length - lax.rem(length, 2 * p)

                @pl.when(lax.rem(length, 2 * p) >= p)
                def _():
                    dst = xs_hbm.at[pl.ds(pl.multiple_of((start + off) * nsub, nsub), p * nsub)]
                    op(pltpu.make_async_copy(zbuf.at[pl.ds(0, p * nsub)], dst, zsem))
            return carry
        lax.fori_loop(0, ne, pad, 0)

        def tail(b, carry):
            dst = xs_hbm.at[pl.ds(pl.multiple_of(b * (bm * nsub), bm * nsub), bm * nsub)]
            op(pltpu.make_async_copy(zbuf, dst, zsem))
            return carry
        lax.fori_loop(nu_ref[0], nb, tail, 0)

    stages = stage.shape[0]
    rows = tt * nsub

    def load(tile):
        src = h_hbm.at[pl.ds(pl.multiple_of(tile * rows, rows), rows)]
        return pltpu.make_async_copy(src, stage.at[lax.rem(tile, stages)], lsem.at[lax.rem(tile, stages)])

    def wait_rows(tile):
        s = lax.rem(tile, stages)
        for _ in range(TOP_K):
            pltpu.make_async_copy(stage.at[s], xs_hbm.at[pl.ds(0, rows)], sem.at[s]).wait()

    @pl.when(i == 0)
    def _():
        zbuf[...] = jnp.zeros_like(zbuf)
        zero_fill(lambda c: c.start())
        load(0).start()

    @pl.when(i >= stages - 1)
    def _():
        wait_rows(i - (stages - 1))

    @pl.when(i + 1 < n)
    def _():
        load(i + 1).start()

    load(i).wait()
    s = lax.rem(i, stages)

    def body(r2, carry):
        for u in range(ROW_UNROLL):
            r = r2 * ROW_UNROLL + u
            src = stage.at[s, pl.ds(pl.multiple_of(r * nsub, nsub), nsub)]
            for k in range(TOP_K):
                dst = xs_hbm.at[pl.ds(pl.multiple_of(d_ref[0, 0, r * TOP_K + k] * nsub, nsub), nsub)]
                pltpu.make_async_copy(src, dst, sem.at[s]).start(priority=k % 2)
        return carry

    lax.fori_loop(0, tt // ROW_UNROLL, body, 0)

    @pl.when(i == n - 1)
    def _():
        for back in range(stages - 2, -1, -1):
            @pl.when(i >= back)
            def _():
                wait_rows(i - back)
        zero_fill(lambda c: c.wait())


def _dispatch(dest, h2_tiles, pad_start, pad_len, n_used, nb, nsub):
    t = dest.shape[0]
    tt = _row_tile(t, (256,))
    nt = t // tt
    bm = EXPERT_ROWS
    d3 = dest.reshape(nt, 1, tt * TOP_K)
    gs = pltpu.PrefetchScalarGridSpec(
        num_scalar_prefetch=3,
        grid=(nt,),
        in_specs=[pl.BlockSpec((1, 1, tt * TOP_K), lambda i, *_: (i, 0, 0), memory_space=pltpu.SMEM),
                  pl.BlockSpec(memory_space=pl.ANY)],
        out_specs=pl.BlockSpec(memory_space=pl.ANY),
        scratch_shapes=[pltpu.VMEM((bm * nsub, LANES), ROW_DTYPE),
                        pltpu.VMEM((DISPATCH_STAGES, tt * nsub, LANES), ROW_DTYPE),
                        pltpu.SemaphoreType.DMA((DISPATCH_STAGES,)),
                        pltpu.SemaphoreType.DMA((DISPATCH_STAGES,)),
                        pltpu.SemaphoreType.DMA],
    )
    return pl.pallas_call(
        functools.partial(_dispatch_kernel, tt=tt, nsub=nsub, bm=bm, nb=nb),
        grid_spec=gs,
        out_shape=jax.ShapeDtypeStruct((nb * bm * nsub, LANES), ROW_DTYPE),
        compiler_params=pltpu.CompilerParams(dimension_semantics=("arbitrary",)),
        name="dispatch",
    )(pad_start, pad_len, n_used, d3, h2_tiles)


def _grouped_kernel(b0_ref, nk_ref, nu_ref, x_hbm, wg_ref, wu_ref, wd_ref, y_hbm,
                    xbuf, ybuf, xsem, ysem, wgb, wub, wdb, *, bm, nsub, nb):
    e = pl.program_id(0)
    ring = xbuf.shape[0]
    rows = bm * nsub
    nu = nu_ref[0]

    def x_copy(gb, slot):
        src = x_hbm.at[pl.ds(pl.multiple_of(gb * rows, rows), rows)]
        return pltpu.make_async_copy(src, xbuf.at[slot], xsem.at[slot])

    def y_copy(gb, slot):
        dst = y_hbm.at[pl.ds(pl.multiple_of(gb * rows, rows), rows)]
        return pltpu.make_async_copy(ybuf.at[slot], dst, ysem.at[slot])

    @pl.when(e == 0)
    def _():
        for g0 in range(ring - 1):
            @pl.when(g0 < nu)
            def _():
                x_copy(g0, g0).start()

    @pl.when(nk_ref[e] > 0)
    def _():
        wgb[...] = wg_ref[0].astype(BF16)
        wub[...] = wu_ref[0].astype(BF16)
        wdb[...] = wd_ref[0].astype(BF16)

        def block(b, carry):
            gb = b0_ref[e] + b
            slot = lax.rem(gb, ring)

            @pl.when(gb + ring - 1 < nu)
            def _():
                x_copy(gb + ring - 1, lax.rem(gb + ring - 1, ring)).start()

            x_copy(gb, slot).wait()

            @pl.when(gb >= ring)
            def _():
                y_copy(gb - ring, slot).wait()

            x = _read_rows(xbuf.at[slot], bm, nsub).astype(BF16)
            hg = jnp.dot(x, wgb[...], preferred_element_type=F32)
            hu = jnp.dot(x, wub[...], preferred_element_type=F32)
            y = jnp.dot((_silu(hg) * hu).astype(BF16), wdb[...], preferred_element_type=F32)
            _write_rows(ybuf.at[slot], y)
            y_copy(gb, slot).start()
            return carry

        lax.fori_loop(0, nk_ref[e], block, 0)

    @pl.when(e == pl.num_programs(0) - 1)
    def _():
        for back in range(ring, 0, -1):
            @pl.when(nu >= back)
            def _():
                y_copy(nu - back, lax.rem(nu - back, ring)).wait()

        ybuf[0] = jnp.zeros(ybuf.shape[1:], ybuf.dtype)

        def tail(gb, carry):
            y_copy(gb, 0).start()
            return carry

        def tail_wait(gb, carry):
            y_copy(gb, 0).wait()
            return carry

        lax.fori_loop(nu, nb, tail, 0)
        lax.fori_loop(nu, nb, tail_wait, 0)


def _grouped(xs, blk_start, blk_count, n_used, wg, wu, wd):
    ne, d, ff = wg.shape
    nsub = _token_rows(d)
    bm = EXPERT_ROWS
    nb = xs.shape[0] // (bm * nsub)
    wspec = lambda shape: pl.BlockSpec(shape, lambda e, *_: (e, 0, 0))
    gs = pltpu.PrefetchScalarGridSpec(
        num_scalar_prefetch=3,
        grid=(ne,),
        in_specs=[pl.BlockSpec(memory_space=pl.ANY),
                  wspec((1, d, ff)), wspec((1, d, ff)), wspec((1, ff, d))],
        out_specs=pl.BlockSpec(memory_space=pl.ANY),
        scratch_shapes=[pltpu.VMEM((GROUP_RING, bm * nsub, LANES), ROW_DTYPE),
                        pltpu.VMEM((GROUP_RING, bm * nsub, LANES), ROW_DTYPE),
                        pltpu.SemaphoreType.DMA((GROUP_RING,)), pltpu.SemaphoreType.DMA((GROUP_RING,)),
                        pltpu.VMEM((d, ff), BF16), pltpu.VMEM((d, ff), BF16), pltpu.VMEM((ff, d), BF16)],
    )
    return pl.pallas_call(
        functools.partial(_grouped_kernel, bm=bm, nsub=nsub, nb=nb),
        grid_spec=gs,
        out_shape=jax.ShapeDtypeStruct(xs.shape, xs.dtype),
        compiler_params=pltpu.CompilerParams(
            dimension_semantics=("arbitrary",), vmem_limit_bytes=VMEM_LIMIT),
        name="grouped",
    )(blk_start, blk_count, n_used, xs, wg, wu, wd)


def _combine_kernel(d_ref, dn_ref, y_hbm, w_ref, s_ref, xp_ref, xs_ref, g2_ref, fg_ref, op_ref, os_ref,
                    buf_a, buf_b, obuf, sem_a, sem_b, *, tt, nsub, ntp):
    i = pl.program_id(0)
    n = pl.num_programs(0)

    def issue(dref, buf, sem, r):
        for k in range(TOP_K):
            src = y_hbm.at[pl.ds(pl.multiple_of(dref[0, 0, r * TOP_K + k] * nsub, nsub), nsub)]
            pltpu.make_async_copy(src, buf.at[k, pl.ds(pl.multiple_of(r * nsub, nsub), nsub)],
                                  sem).start(priority=k % 2)

    def wait_tile(buf, sem):
        for k in range(TOP_K):
            pltpu.make_async_copy(y_hbm.at[pl.ds(0, tt * nsub)], buf.at[k], sem).wait()

    @pl.when(i == 0)
    def _():
        def first(r, carry):
            issue(d_ref, buf_a, sem_a, r)
            return carry
        lax.fori_loop(0, tt, first, 0)

    is_prompt = i < ntp

    def step(cur, cur_sem, nxt, nxt_sem):
        wait_tile(cur, cur_sem)

        def chunk(c, carry):
            r0 = pl.multiple_of(c * SUBLANES, SUBLANES)
            for u in range(SUBLANES):
                issue(dn_ref, nxt, nxt_sem, r0 + u)
            acc = s_ref[pl.ds(r0, SUBLANES), :]
            w8 = w_ref[pl.ds(r0, SUBLANES), :]
            for k in range(TOP_K):
                rows = jnp.concatenate(
                    [cur[k, pl.ds(r0 * nsub + cc, SUBLANES, stride=nsub), :] for cc in range(nsub)], axis=-1)
                acc = acc + w8[:, k:k + 1] * rows
            g2 = g2_ref[pl.ds(c // (MOD_ROWS // SUBLANES), 1), :]
            x1 = jnp.where(is_prompt, xp_ref[pl.ds(r0, SUBLANES), :], xs_ref[pl.ds(r0, SUBLANES), :])
            obuf[pl.ds(r0, SUBLANES), :] = _rms(x1 + acc * g2, fg_ref[...], RMS_EPS)
            return carry

        lax.fori_loop(0, tt // SUBLANES, chunk, 0)

        @pl.when(i == n - 1)
        def _():
            wait_tile(nxt, nxt_sem)

    @pl.when(lax.rem(i, 2) == 0)
    def _():
        step(buf_a, sem_a, buf_b, sem_b)

    @pl.when(lax.rem(i, 2) == 1)
    def _():
        step(buf_b, sem_b, buf_a, sem_a)

    @pl.when(is_prompt)
    def _():
        op_ref[...] = obuf[...]

    @pl.when(jnp.logical_not(is_prompt))
    def _():
        os_ref[...] = obuf[...]


def _combine(dest, y_tiles, wts, shared, xp1, xs1, g2_rows, final_g):
    tp, d = xp1.shape
    ts = xs1.shape[0]
    nsub = _token_rows(d)
    tt = _row_tile(ts, (256,))
    assert tp % tt == 0 and tt % (SUBLANES * MOD_ROWS) == 0
    ntp = tp // tt
    mr = tt // MOD_ROWS
    nt = (tp + ts) // tt
    d3 = dest.reshape(nt, 1, tt * TOP_K)
    kern = functools.partial(_combine_kernel, tt=tt, nsub=nsub, ntp=ntp)
    smem_blk = lambda f: pl.BlockSpec((1, 1, tt * TOP_K), f, memory_space=pltpu.SMEM)
    row = lambda i: (i, 0)
    two = _two_part_specs(tt, d, ntp)
    return pl.pallas_call(
        kern,
        grid=(nt,),
        in_specs=[smem_blk(lambda i: (i, 0, 0)),
                  smem_blk(lambda i: (jnp.minimum(i + 1, nt - 1), 0, 0)),
                  pl.BlockSpec(memory_space=pl.ANY),
                  pl.BlockSpec((tt, LANES), row), pl.BlockSpec((tt, d), row)] + two + [
                  pl.BlockSpec((mr, d), row), pl.BlockSpec((1, d), lambda i: (0, 0))],
        out_specs=two,
        out_shape=[jax.ShapeDtypeStruct((tp, d), F32), jax.ShapeDtypeStruct((ts, d), F32)],
        scratch_shapes=[pltpu.VMEM((TOP_K, tt * nsub, LANES), ROW_DTYPE),
                        pltpu.VMEM((TOP_K, tt * nsub, LANES), ROW_DTYPE), pltpu.VMEM((tt, d), F32),
                        pltpu.SemaphoreType.DMA, pltpu.SemaphoreType.DMA],
        compiler_params=pltpu.CompilerParams(
            dimension_semantics=("arbitrary",), vmem_limit_bytes=VMEM_LIMIT),
        name="combine",
    )(d3, d3, y_tiles, wts, shared, xp1, xs1, g2_rows, final_g.reshape(1, d))


def _plan(counts, n_assign):
    ne = counts.shape[0]
    bm = EXPERT_ROWS
    counts = counts.astype(jnp.int32)
    nblk = (counts + bm - 1) // bm
    bend = jnp.cumsum(nblk)
    pstart = jnp.broadcast_to(((bend - nblk) * bm).astype(F32).reshape(ne, 1), (ne, LANES))
    nb = -(-(n_assign + ne * (bm - 1)) // bm)
    n_used = bend[-1:].astype(jnp.int32)
    pad_start = ((bend - nblk) * bm + counts).astype(jnp.int32)
    pad_len = (nblk * bm - counts).astype(jnp.int32)
    return pstart, (bend - nblk).astype(jnp.int32), nblk, n_used, pad_start, pad_len, nb


def per_g_pow2(ne):
    per_g = ne // N_EXPERT_GROUPS
    return per_g * N_EXPERT_GROUPS == ne and per_g & (per_g - 1) == 0


def _mod_rows(m, nbp, lp):
    return jnp.concatenate([jnp.repeat(m[:nbp], lp // MOD_ROWS, axis=0), m[nbp:]], axis=0)


def kernel(x_prompt, x_sample, state_ssm, state_conv, state_pool, c_prompt, c_sample, ln1_g, ln2_g, w_ada, b_ada, w_in, conv_w, conv_b, dt_bias, a_log, d_skip, ssd_norm_g, w_ssd_out, pool_w, pool_scale, w_out, w_router, router_bias, moe_w_gate, moe_w_up, moe_w_down, shared_w_gate, shared_w_up, shared_w_down, final_g):
    bp, lp, dm = x_prompt.shape
    bs, ls, _ = x_sample.shape
    depth = ln1_g.shape[0]
    heads, hd, ns = state_ssm.shape[2], state_ssm.shape[3], state_ssm.shape[4]
    di = heads * hd
    gn = SSD_GROUPS * ns
    cch = di + 2 * gn
    assert depth == 1 and ls == MOD_ROWS and lp % MOD_ROWS == 0 and heads <= LANES
    assert per_g_pow2(w_router.shape[2])
    assert ns == 2 * hd and hd & (hd - 1) == 0 and (heads // SSD_GROUPS) % 2 == 0
    assert all(w & (w - 1) == 0 for w in POOL_WINDOWS) and state_pool.shape[2] == max(POOL_WINDOWS) - 1
    tp, ts = bp * lp, bs * ls
    expand3 = (jnp.arange(3 * LANES)[:, None] % LANES == jnp.arange(di)[None, :] // hd).astype(BF16)

    xp, xs = x_prompt.reshape(tp, dm), x_sample.reshape(ts, dm)
    c_all = jnp.concatenate([c_prompt, c_sample], axis=0)

    o1, o2, o3, o4 = di, di + cch, di + cch + heads, di + cch + heads + dm
    cols = (0, di, 2 * di, di + cch, di + cch + dm)

    ssm_p, conv_p, pool_p, ssm_s, conv_s, pool_s = [], [], [], [], [], []
    for l in range(depth):
        wi = w_in[l]
        wcat = jnp.concatenate(
            [wi[:, :o1], wi[:, o1:o2], wi[:, o3:o4], wi[:, o4:],
             jnp.pad(wi[:, o2:o3], ((0, 0), (0, LANES - heads)))], axis=1).astype(BF16)
        pad_h = lambda v: jnp.pad(v.reshape(1, heads), ((0, 0), (0, LANES - heads)))
        seq_w = (conv_w[l], conv_b[l].reshape(1, cch), pad_h(dt_bias[l]), pad_h(a_log[l]),
                 jnp.repeat(d_skip[l], hd).reshape(1, di), ssd_norm_g[l].reshape(1, di),
                 w_ssd_out[l].astype(BF16), pool_w[l].astype(BF16), pool_scale[l].reshape(1, dm),
                 w_out[l].astype(BF16), expand3)

        mod = _ada(c_all, w_ada[l], b_ada[l])
        sh1, sc1, g1, sh2, sc2, g2 = jnp.split(mod, 6, axis=-1)

        proj, dtp = _inproj(xp, xs, _mod_rows(sc1, bp, lp), _mod_rows(sh1, bp, lp), ln1_g[l], wcat)

        zc = jnp.zeros((bp,) + state_conv.shape[2:], F32)
        zs = jnp.zeros((bp, heads, hd, ns), F32)
        zp = jnp.zeros((bp,) + state_pool.shape[2:], F32)
        xp1, ns_p, nc_p, np_p = _seq(proj, dtp, xp, g1[:bp], zc, zs, zp, seq_w,
                                     row0=0, nb=bp, seqlen=lp, pos0=0, cols=cols)
        xs1, ns_s, nc_s, np_s = _seq(proj, dtp, xs, g1[bp:], state_conv[l], state_ssm[l], state_pool[l],
                                     seq_w, row0=tp, nb=bs, seqlen=ls, pos0=PAST_LEN, cols=cols)
        conv_p.append(nc_p)
        conv_s.append(nc_s)
        pool_p.append(np_p)
        pool_s.append(np_s)
        ssm_p.append(ns_p)
        ssm_s.append(ns_s)

        h2_tiles, shared, eidx_t, wts_t, mask_t, counts = _moe_pre(
            xp1, xs1, _mod_rows(sc2, bp, lp), _mod_rows(sh2, bp, lp), ln2_g[l],
            w_router[l].astype(BF16), router_bias[l],
            shared_w_gate[l].astype(BF16), shared_w_up[l].astype(BF16), shared_w_down[l].astype(BF16))
        nsub = _token_rows(dm)
        pstart, blk_start, blk_count, n_used, pad_start, pad_len, nb = _plan(counts[:, 0], (tp + ts) * TOP_K)
        dest = _rank(mask_t, eidx_t, pstart).T
        wts = jnp.pad(wts_t.T, ((0, 0), (0, LANES - TOP_K)))
        x_sorted = _dispatch(dest, h2_tiles, pad_start, pad_len, n_used, nb, nsub)
        y_tiles = _grouped(x_sorted, blk_start, blk_count, n_used,
                           moe_w_gate[l], moe_w_up[l], moe_w_down[l])
        xp, xs = _combine(dest, y_tiles, wts, shared, xp1, xs1, _mod_rows(g2, bp, lp), final_g)

    y_prompt = xp.reshape(bp, lp, dm)
    y_sample = xs.reshape(bs, ls, dm)
    return (y_prompt, y_sample, jnp.stack(ssm_p), jnp.stack(conv_p), jnp.stack(pool_p),
            jnp.stack(ssm_s), jnp.stack(conv_s), jnp.stack(pool_s))
```

```python
import functools

import jax
import jax.numpy as jnp
from jax import lax
from jax.experimental import pallas as pl
from jax.experimental.pallas import tpu as pltpu

F32 = jnp.float32
BF16 = jnp.bfloat16
HIGHEST = lax.Precision.HIGHEST

RMS_EPS = 1e-6
SSD_NORM_EPS = 1e-5
CHUNK = 64
SSD_GROUPS = 4
POOL_WINDOWS = (2, 4, 8, 16)
PAST_LEN = 1024
TOP_K = 8
N_EXPERT_GROUPS = 8
TOPK_GROUPS = 4
ROUTED_SCALE = 2.5

LANES = 128
SUBLANES = 8
MOD_ROWS = 32
VMEM_LIMIT = 56 * 1024 * 1024
EXPERT_ROWS = 256
SEQ_TILE = 256
GROUP_RING = 6
ROW_UNROLL = 4
DISPATCH_STAGES = 3


def _sigmoid(x):
    return 0.5 * jnp.tanh(0.5 * x) + 0.5


def _silu(x):
    h = 0.5 * x
    return h + h * jnp.tanh(h)


def _softplus(x):
    return jnp.maximum(x, 0.0) + jnp.log1p(jnp.exp(-jnp.abs(x)))


def _row_tile(n, prefs):
    for t in prefs:
        if n % t == 0:
            return t
    return n


def _modulate(y, sc_ref, sh_ref):
    rows, d = y.shape
    y3 = y.reshape(rows // MOD_ROWS, MOD_ROWS, d)
    y3 = y3 * (1.0 + sc_ref[...][:, None, :]) + sh_ref[...][:, None, :]
    return y3.reshape(rows, d)


def _rms(x, g, eps):
    return x * lax.rsqrt(jnp.mean(x * x, axis=-1, keepdims=True) + eps) * g


def _ada_kernel(c_ref, w_ref, b_ref, o_ref):
    s = _silu(c_ref[...])
    o_ref[...] = jnp.dot(s, w_ref[...], preferred_element_type=F32, precision=HIGHEST) + b_ref[...]


def _ada(c_all, w_ada, b_ada):
    n, d = c_all.shape
    dout = w_ada.shape[1]
    tn = _row_tile(dout, (1024, 512, 256, 128))
    return pl.pallas_call(
        _ada_kernel,
        grid=(dout // tn,),
        in_specs=[pl.BlockSpec((n, d), lambda j: (0, 0)),
                  pl.BlockSpec((d, tn), lambda j: (0, j)),
                  pl.BlockSpec((1, tn), lambda j: (0, j))],
        out_specs=pl.BlockSpec((n, tn), lambda j: (0, j)),
        out_shape=jax.ShapeDtypeStruct((n, dout), F32),
        name="ada",
    )(c_all, w_ada, b_ada.reshape(1, dout))


def _two_part_specs(tm, d, ntp):
    return [pl.BlockSpec((tm, d), lambda i, *_: (jnp.minimum(i, ntp - 1), 0)),
            pl.BlockSpec((tm, d), lambda i, *_: (jnp.maximum(i - ntp, 0), 0))]


def _inproj_kernel(xp_ref, xs_ref, sc_ref, sh_ref, g_ref, w_ref, o_ref, dt_ref, *, ntp, tn):
    x = jnp.where(pl.program_id(0) < ntp, xp_ref[...], xs_ref[...])
    h = _modulate(_rms(x, g_ref[...], RMS_EPS), sc_ref, sh_ref).astype(BF16)
    n_main = o_ref.shape[1]
    for c0 in range(0, n_main, tn):
        o_ref[:, c0:c0 + tn] = jnp.dot(h, w_ref[:, c0:c0 + tn], preferred_element_type=F32).astype(BF16)
    dt_ref[...] = jnp.dot(h, w_ref[:, n_main:], preferred_element_type=F32)


def _inproj(xp, xs, sc_rows, sh_rows, g, wcat):
    tp, d = xp.shape
    ts = xs.shape[0]
    n = wcat.shape[1]
    n_main = n - LANES
    tm = _row_tile(ts, (512, 256))
    assert tp % tm == 0 and tm % (SUBLANES * MOD_ROWS) == 0
    ntp = tp // tm
    nt = ntp + ts // tm
    mr = tm // MOD_ROWS
    tn = _row_tile(n_main, (2048, 1024, 512, 256, 128))
    row = lambda i: (i, 0)
    return pl.pallas_call(
        functools.partial(_inproj_kernel, ntp=ntp, tn=tn),
        grid=(nt,),
        in_specs=_two_part_specs(tm, d, ntp) + [
            pl.BlockSpec((mr, d), row), pl.BlockSpec((mr, d), row),
            pl.BlockSpec((1, d), lambda i: (0, 0)),
            pl.BlockSpec((d, n), lambda i: (0, 0), pipeline_mode=pl.Buffered(1))],
        out_specs=[pl.BlockSpec((tm, n_main), row), pl.BlockSpec((tm, LANES), row)],
        out_shape=[jax.ShapeDtypeStruct((tp + ts, n_main), BF16),
                   jax.ShapeDtypeStruct((tp + ts, LANES), F32)],
        compiler_params=pltpu.CompilerParams(
            dimension_semantics=("parallel",), vmem_limit_bytes=VMEM_LIMIT),
        name="inproj",
    )(xp, xs, sc_rows, sh_rows, g.reshape(1, d), wcat)


def _seq_kernel(z_ref, xp_ref, bc_ref, u_ref, gt_ref, dt_ref, x_ref, g1_ref,
                cst_ref, sst_ref, pst_ref, cw_ref, cb_ref, dtb_ref, alog_ref, dx_ref, ng_ref,
                wssd_ref, pw_ref, ps_ref, wout_ref, e3_ref,
                xo_ref, so_ref, co_ref, po_ref,
                cbuf, pbuf, st_scr, y_scr, xbc_scr, csx_scr, wvx_scr,
                *, tq, q, pos0, heads, hd, ns, dm):
    j = pl.program_id(1)
    di = heads * hd
    gn = SSD_GROUPS * ns
    hpg = heads // SSD_GROUPS
    cw = cbuf.shape[0] - tq
    kw = cw_ref.shape[0]
    ph = pbuf.shape[0] - tq

    @pl.when(j == 0)
    def _():
        cbuf[0:cw - (kw - 1), :] = jnp.zeros((cw - (kw - 1), cbuf.shape[1]), F32)
        cbuf[cw - (kw - 1):cw, :] = cst_ref[0]
        pbuf[0:1, :] = jnp.zeros((1, dm), F32)
        pbuf[1:ph, :] = pst_ref[0]
        st_scr[...] = sst_ref[0].T

    cbuf[cw:cw + tq, 0:di] = xp_ref[...].astype(F32)
    cbuf[cw:cw + tq, di:di + 2 * gn] = bc_ref[...].astype(F32)
    ext = cbuf[...]
    acc = cb_ref[...] + ext[cw:cw + tq, :] * cw_ref[kw - 1:kw, :]
    for k in range(kw - 1):
        acc = acc + pltpu.roll(ext, kw - 1 - k, axis=0)[cw:cw + tq, :] * cw_ref[k:k + 1, :]
    xbc_scr[...] = _silu(acc)
    cbuf[cw - (kw - 1):cw, :] = cbuf[cw + tq - (kw - 1):cw + tq, :]

    dt = _softplus(dt_ref[...] + dtb_ref[...])
    dta = dt * (-jnp.exp(alog_ref[...]))
    lq = q.bit_length() - 1
    r_i = lax.broadcasted_iota(jnp.int32, (tq, tq), 0)
    c_i = lax.broadcasted_iota(jnp.int32, (tq, tq), 1)
    same = lax.shift_right_logical(r_i, lq) == lax.shift_right_logical(c_i, lq)
    tril = jnp.where(same, (r_i >= c_i).astype(F32), 0.0)
    cs = jnp.dot(tril, dta, preferred_element_type=F32, precision=HIGHEST)
    cs_end = jnp.dot(same.astype(F32), dta, preferred_element_type=F32, precision=HIGHEST)
    wv = dt * jnp.exp(cs_end - cs)

    both = jnp.concatenate([cs, wv], axis=0)
    hi = both.astype(BF16)
    r1 = both - hi.astype(F32)
    mid = r1.astype(BF16)
    lo = (r1 - mid.astype(F32)).astype(BF16)
    ex = jnp.dot(jnp.concatenate([hi, mid, lo], axis=1), e3_ref[...], preferred_element_type=F32)
    csx_scr[...] = ex[0:tq]
    wvx_scr[...] = ex[tq:2 * tq]
    cs_t = cs.T
    dt_t = dt.T

    lane = lax.broadcasted_iota(jnp.int32, (q, 2 * hd), 1)
    kpos = jnp.bitwise_and(lane, hd - 1)
    causal2 = jnp.logical_and(lax.broadcasted_iota(jnp.int32, (q, 2 * hd), 0) >= kpos, kpos < q)
    first = lane < hd
    zrow = jnp.zeros((1, hd - q), F32)
    zblk = jnp.zeros((hd - q, 2 * hd), BF16)

    def pair_row(t, p, r0):
        parts = []
        for h in (2 * p, 2 * p + 1):
            parts.append(t[h:h + 1, r0:r0 + q])
            if q < hd:
                parts.append(zrow)
        return jnp.concatenate(parts, axis=1)

    def pair_rows(a, b):
        blocks = [a, zblk, b, zblk] if q < hd else [a, b]
        return jnp.concatenate(blocks, axis=0)

    gw = di // SSD_GROUPS
    for c in range(tq // q):
        r0 = c * q
        for g in range(SSD_GROUPS):
            bgb = xbc_scr[r0:r0 + q, di + g * ns:di + (g + 1) * ns].astype(BF16)
            cgb = xbc_scr[r0:r0 + q, di + gn + g * ns:di + gn + (g + 1) * ns].astype(BF16)
            cb2 = lax.dot_general(cgb, pair_rows(bgb, bgb), (((1,), (1,)), ((), ())),
                                  preferred_element_type=F32)
            stg = st_scr[:, g * gw:(g + 1) * gw]
            ecs = jnp.exp(csx_scr[r0:r0 + q, g * gw:(g + 1) * gw])
            yo = jnp.dot(cgb, stg.astype(BF16), preferred_element_type=F32) * ecs
            for pp in range(hpg // 2):
                p = g * (hpg // 2) + pp
                lo_l = p * 2 * hd
                seg = csx_scr[r0:r0 + q, lo_l:lo_l + 2 * hd] - pair_row(cs_t, p, r0)
                lm = jnp.exp(jnp.where(causal2, seg, -jnp.inf))
                m2 = (cb2 * lm * pair_row(dt_t, p, r0)).astype(BF16)
                xpair = xbc_scr[r0:r0 + q, lo_l:lo_l + 2 * hd]
                rhs = pair_rows(jnp.where(first, xpair, 0.0).astype(BF16),
                                jnp.where(first, 0.0, xpair).astype(BF16))
                yd = jnp.dot(m2, rhs, preferred_element_type=F32)
                y_scr[r0:r0 + q, lo_l:lo_l + 2 * hd] = yd + yo[:, pp * 2 * hd:(pp + 1) * 2 * hd]
            xw = (xbc_scr[r0:r0 + q, g * gw:(g + 1) * gw]
                  * wvx_scr[r0:r0 + q, g * gw:(g + 1) * gw]).astype(BF16)
            upd = lax.dot_general(bgb, xw, (((0,), (0,)), ((), ())), preferred_element_type=F32)
            st_scr[:, g * gw:(g + 1) * gw] = stg * ecs[q - 1:q, :] + upd

    @pl.when(j == pl.num_programs(1) - 1)
    def _():
        so_ref[0] = st_scr[...].T
        co_ref[0] = cbuf[cw - (kw - 1):cw, :]

    y = y_scr[...] + xbc_scr[:, 0:di] * dx_ref[...]
    v = y * _silu(z_ref[...].astype(F32))
    gw = di // SSD_GROUPS
    parts = []
    for g in range(SSD_GROUPS):
        vg = v[:, g * gw:(g + 1) * gw]
        parts.append(vg * lax.rsqrt(jnp.mean(vg * vg, axis=-1, keepdims=True) + SSD_NORM_EPS))
    yn = (jnp.concatenate(parts, axis=-1) * ng_ref[...]).astype(BF16)
    a_br = jnp.dot(yn, wssd_ref[...], preferred_element_type=F32)

    pbuf[ph:ph + tq, :] = u_ref[...].astype(F32)
    pos = pos0 + j * tq + lax.broadcasted_iota(jnp.int32, (tq, 1), 0)
    pgd = dm // len(POOL_WINDOWS)
    b_parts = []
    for gi, w in enumerate(POOL_WINDOWS):
        lo = gi * pgd
        s = pbuf[:, lo:lo + pgd]
        step = 1
        while step < w:
            s = s + pltpu.roll(s, step, axis=0)
            step *= 2
        cnt = jnp.minimum(w, pos + 1).astype(F32)
        pooled = s[ph:ph + tq, :] / cnt - pbuf[ph:ph + tq, lo:lo + pgd]
        b_parts.append(jnp.dot(pooled.astype(BF16), pw_ref[gi], preferred_element_type=F32))
    b_br = jnp.concatenate(b_parts, axis=-1) * ps_ref[...]
    pbuf[1:ph, :] = pbuf[tq + 1:tq + ph, :]

    @pl.when(j == pl.num_programs(1) - 1)
    def _():
        po_ref[0] = pbuf[1:ph, :]

    gates = _sigmoid(gt_ref[...].astype(F32))
    merged = gates[:, 0:dm] * a_br + gates[:, dm:2 * dm] * b_br
    mix = jnp.dot(merged.astype(BF16), wout_ref[...], preferred_element_type=F32)
    xo_ref[...] = x_ref[...] + g1_ref[0] * mix


def _seq(proj, dtp, x, g1, conv_state, ssm_state, pool_state, wts, *, row0, nb, seqlen, pos0, cols):
    dm = x.shape[1]
    heads, hd, ns = ssm_state.shape[1], ssm_state.shape[2], ssm_state.shape[3]
    di = heads * hd
    gn = SSD_GROUPS * ns
    q = min(CHUNK, seqlen)
    tq = _row_tile(seqlen, (SEQ_TILE, 128, 64, 32))
    nj = seqlen // tq
    rb0 = row0 // tq
    oz, ox, obc, ou, og = cols

    def rows(b, j):
        return rb0 + b * nj + j

    full = lambda a: pl.BlockSpec(a.shape, lambda b, j: (0,) * a.ndim, pipeline_mode=pl.Buffered(1))
    kern = functools.partial(_seq_kernel, tq=tq, q=q, pos0=pos0, heads=heads, hd=hd, ns=ns, dm=dm)
    kw1 = conv_state.shape[1]
    pst = pool_state.shape[1]
    xo, so, co, po = pl.pallas_call(
        kern,
        grid=(nb, nj),
        in_specs=[
            pl.BlockSpec((tq, di), lambda b, j: (rows(b, j), oz // di)),
            pl.BlockSpec((tq, di), lambda b, j: (rows(b, j), ox // di)),
            pl.BlockSpec((tq, 2 * gn), lambda b, j: (rows(b, j), obc // (2 * gn))),
            pl.BlockSpec((tq, dm), lambda b, j: (rows(b, j), ou // dm)),
            pl.BlockSpec((tq, 2 * dm), lambda b, j: (rows(b, j), og // (2 * dm))),
            pl.BlockSpec((tq, LANES), lambda b, j: (rows(b, j), 0)),
            pl.BlockSpec((tq, dm), lambda b, j: (b * nj + j, 0)),
            pl.BlockSpec((1, 1, dm), lambda b, j: (b, 0, 0)),
            pl.BlockSpec((1, kw1, di + 2 * gn), lambda b, j: (b, 0, 0)),
            pl.BlockSpec((1, di, ns), lambda b, j: (b, 0, 0)),
            pl.BlockSpec((1, pst, dm), lambda b, j: (b, 0, 0)),
        ] + [full(w) for w in wts],
        out_specs=[pl.BlockSpec((tq, dm), lambda b, j: (b * nj + j, 0)),
                   pl.BlockSpec((1, di, ns), lambda b, j: (b, 0, 0)),
                   pl.BlockSpec((1, kw1, di + 2 * gn), lambda b, j: (b, 0, 0)),
                   pl.BlockSpec((1, pst, dm), lambda b, j: (b, 0, 0))],
        out_shape=[jax.ShapeDtypeStruct((nb * seqlen, dm), F32),
                   jax.ShapeDtypeStruct((nb, di, ns), F32),
                   jax.ShapeDtypeStruct((nb, kw1, di + 2 * gn), F32),
                   jax.ShapeDtypeStruct((nb, pst, dm), F32)],
        scratch_shapes=[pltpu.VMEM((tq + SUBLANES, di + 2 * gn), F32),
                        pltpu.VMEM((tq + 2 * SUBLANES, dm), F32),
                        pltpu.VMEM((ns, di), F32),
                        pltpu.VMEM((tq, di), F32),
                        pltpu.VMEM((tq, di + 2 * gn), F32),
                        pltpu.VMEM((tq, di), F32),
                        pltpu.VMEM((tq, di), F32)],
        compiler_params=pltpu.CompilerParams(
            dimension_semantics=("parallel", "arbitrary"), vmem_limit_bytes=VMEM_LIMIT),
        name="seq",
    )(proj, proj, proj, proj, proj, dtp, x, g1.reshape(nb, 1, dm),
      conv_state, ssm_state.reshape(nb, di, ns), pool_state, *wts)
    return xo, so.reshape(nb, heads, hd, ns), co, po


def _moe_pre_kernel(xp_ref, xs_ref, sc_ref, sh_ref, g_ref, wr_ref, rb_ref, swg_ref, swu_ref, swd_ref,
                    h_ref, s_ref, e_ref, w_ref, m_ref, cnt_ref, *, ntp):
    x = jnp.where(pl.program_id(0) < ntp, xp_ref[...], xs_ref[...])
    y = _rms(x, g_ref[...], RMS_EPS)
    h = _modulate(y, sc_ref, sh_ref)
    _write_rows(h_ref, h)
    hb = h.astype(BF16)

    sg = jnp.dot(hb, swg_ref[...], preferred_element_type=F32)
    su = jnp.dot(hb, swu_ref[...], preferred_element_type=F32)
    s_ref[...] = jnp.dot((_silu(sg) * su).astype(BF16), swd_ref[...], preferred_element_type=F32)

    scores = jax.nn.sigmoid(lax.dot_general(wr_ref[...], hb, (((1,), (1,)), ((), ())),
                                            preferred_element_type=F32))
    biased = scores + rb_ref[:, 0:1]
    ne, tm = scores.shape
    per_g = ne // N_EXPERT_GROUPS
    neg = -jnp.inf
    row_e = lax.broadcasted_iota(jnp.int32, (ne, tm), 0).astype(F32)
    row_g = lax.broadcasted_iota(jnp.int32, (N_EXPERT_GROUPS, tm), 0).astype(F32)

    def first_argmax(v, idx, big):
        top = jnp.max(v, axis=0, keepdims=True)
        return top, jnp.min(jnp.where(v == top, idx, big), axis=0, keepdims=True)

    gs = jnp.full((N_EXPERT_GROUPS, tm), neg, F32)
    for g in range(N_EXPERT_GROUPS):
        mg = biased[g * per_g:(g + 1) * per_g, :]
        rg = (lax.broadcasted_iota(jnp.int32, (per_g, tm), 0) + g * per_g).astype(F32)
        t1, i1 = first_argmax(mg, rg, float(ne))
        t2 = jnp.max(jnp.where(rg == i1, neg, mg), axis=0, keepdims=True)
        gs = jnp.where(row_g == g, t1 + t2, gs)
    grp_e = lax.shift_right_logical(lax.broadcasted_iota(jnp.int32, (ne, tm), 0),
                                    per_g.bit_length() - 1).astype(F32)
    allowed = jnp.zeros((ne, tm), F32)
    for _ in range(TOPK_GROUPS):
        _, gi = first_argmax(gs, row_g, float(N_EXPERT_GROUPS))
        allowed = jnp.where(grp_e == gi, 1.0, allowed)
        gs = jnp.where(row_g == gi, neg, gs)
    mb = jnp.where(allowed > 0.0, biased, neg)
    row_k = lax.broadcasted_iota(jnp.int32, (TOP_K, tm), 0)
    eacc = jnp.zeros((TOP_K, tm), F32)
    wacc = jnp.zeros((TOP_K, tm), F32)
    chosen = jnp.zeros((ne, tm), F32)
    for k in range(TOP_K):
        _, ik = first_argmax(mb, row_e, float(ne))
        sel = row_e == ik
        wk = jnp.sum(jnp.where(sel, scores, 0.0), axis=0, keepdims=True)
        eacc = jnp.where(row_k == k, ik, eacc)
        wacc = jnp.where(row_k == k, wk, wacc)
        chosen = jnp.where(sel, 1.0, chosen)
        mb = jnp.where(sel, neg, mb)
    wsum = jnp.sum(wacc, axis=0, keepdims=True)
    e_ref[...] = eacc.astype(jnp.int32)
    w_ref[...] = wacc / (wsum + 1e-20) * ROUTED_SCALE
    chosen_b = chosen.astype(BF16)
    m_ref[...] = chosen_b

    @pl.when(pl.program_id(0) == 0)
    def _():
        cnt_ref[...] = jnp.zeros_like(cnt_ref)

    cnt_ref[...] += jnp.dot(chosen_b, jnp.ones((tm, LANES), BF16), preferred_element_type=F32)


def _moe_pre(xp1, xs1, sc_rows, sh_rows, g, wr, rb, swg, swu, swd):
    tp, d = xp1.shape
    ts = xs1.shape[0]
    t = tp + ts
    ne = wr.shape[1]
    ff = swg.shape[1]
    tm = _row_tile(ts, (256,))
    assert tp % tm == 0 and tm % (SUBLANES * MOD_ROWS) == 0
    ntp = tp // tm
    mr = tm // MOD_ROWS
    nsub = _token_rows(d)
    row = lambda i: (i, 0)
    col = lambda i: (0, i)
    const = lambda i: (0, 0)
    rb_col = jnp.broadcast_to(rb.reshape(ne, 1), (ne, LANES))
    return pl.pallas_call(
        functools.partial(_moe_pre_kernel, ntp=ntp),
        grid=(t // tm,),
        in_specs=_two_part_specs(tm, d, ntp) + [
                  pl.BlockSpec((mr, d), row), pl.BlockSpec((mr, d), row),
                  pl.BlockSpec((1, d), const), pl.BlockSpec((ne, d), const), pl.BlockSpec((ne, LANES), const),
                  pl.BlockSpec((d, ff), const), pl.BlockSpec((d, ff), const), pl.BlockSpec((ff, d), const)],
        out_specs=[pl.BlockSpec((tm * nsub, LANES), row), pl.BlockSpec((tm, d), row),
                   pl.BlockSpec((TOP_K, tm), col), pl.BlockSpec((TOP_K, tm), col),
                   pl.BlockSpec((ne, tm), col), pl.BlockSpec((ne, LANES), const)],
        out_shape=[jax.ShapeDtypeStruct((t * nsub, LANES), ROW_DTYPE), jax.ShapeDtypeStruct((t, d), F32),
                   jax.ShapeDtypeStruct((TOP_K, t), jnp.int32), jax.ShapeDtypeStruct((TOP_K, t), F32),
                   jax.ShapeDtypeStruct((ne, t), BF16), jax.ShapeDtypeStruct((ne, LANES), F32)],
        compiler_params=pltpu.CompilerParams(
            dimension_semantics=("arbitrary",), vmem_limit_bytes=VMEM_LIMIT),
        name="moe_pre",
    )(xp1, xs1, sc_rows, sh_rows, g.reshape(1, d), wr.T, rb_col, swg, swu, swd)


ROW_DTYPE = F32


def _token_rows(d):
    return d // LANES


def _read_rows(buf, nrows, nsub):
    return jnp.concatenate([buf[pl.ds(c, nrows, stride=nsub), :] for c in range(nsub)], axis=-1)


def _write_rows(ref, v):
    nrows, width = v.shape
    nsub = _token_rows(width)
    for c in range(nsub):
        ref[pl.ds(c, nrows, stride=nsub), :] = v[:, c * LANES:(c + 1) * LANES]


def _rank_kernel(m_ref, e_ref, ps_ref, d_ref, carry):
    i = pl.program_id(0)

    @pl.when(i == 0)
    def _():
        carry[...] = jnp.zeros_like(carry)

    m = m_ref[...]
    ne, tr = m.shape
    r_i = lax.broadcasted_iota(jnp.int32, (tr, tr), 0)
    c_i = lax.broadcasted_iota(jnp.int32, (tr, tr), 1)
    earlier = (r_i < c_i).astype(BF16)
    base = carry[...] + ps_ref[...]
    slot = jnp.dot(m, earlier, preferred_element_type=F32) + jnp.concatenate([base] * (tr // LANES), axis=1)
    row_e = lax.broadcasted_iota(jnp.int32, (ne, tr), 0)
    d_rows = [jnp.sum(jnp.where(row_e == e_ref[k:k + 1, :], slot, 0.0), axis=0, keepdims=True)
              for k in range(TOP_K)]
    d_ref[...] = jnp.concatenate(d_rows, axis=0).astype(jnp.int32)
    carry[...] += jnp.dot(m, jnp.ones((tr, LANES), BF16), preferred_element_type=F32)


def _rank(mask_t, eidx_t, pstart):
    ne, t = mask_t.shape
    tr = _row_tile(t, (512, 256))
    col = lambda i: (0, i)
    return pl.pallas_call(
        _rank_kernel,
        grid=(t // tr,),
        in_specs=[pl.BlockSpec((ne, tr), col), pl.BlockSpec((TOP_K, tr), col),
                  pl.BlockSpec((ne, LANES), lambda i: (0, 0))],
        out_specs=pl.BlockSpec((TOP_K, tr), col),
        out_shape=jax.ShapeDtypeStruct((TOP_K, t), jnp.int32),
        scratch_shapes=[pltpu.VMEM((ne, LANES), F32)],
        compiler_params=pltpu.CompilerParams(
            dimension_semantics=("arbitrary",), vmem_limit_bytes=VMEM_LIMIT),
        name="rank",
    )(mask_t, eidx_t, pstart)


def _dispatch_kernel(ps_ref, pl_ref, nu_ref, d_ref, h_hbm, xs_hbm, zbuf, stage, lsem, sem, zsem,
                     *, tt, nsub, bm, nb):
    i = pl.program_id(0)
    n = pl.num_programs(0)
    ne = ps_ref.shape[0]
    pieces = [1 << b for b in reversed(range(bm.bit_length() - 1))]

    def zero_fill(op):
        def pad(e, carry):
            start = ps_ref[e]
            length = pl_ref[e]
            for p in pieces:
                off = length - lax.rem(length, 2 * p)

                @pl.when(lax.rem(length, 2 * p) >= p)
                def _():
                    dst = xs_hbm.at[pl.ds(pl.multiple_of((start + off) * nsub, nsub), p * nsub)]
                    op(pltpu.make_async_copy(zbuf.at[pl.ds(0, p * nsub)], dst, zsem))
            return carry
        lax.fori_loop(0, ne, pad, 0)

        def tail(b, carry):
            dst = xs_hbm.at[pl.ds(pl.multiple_of(b * (bm * nsub), bm * nsub), bm * nsub)]
            op(pltpu.make_async_copy(zbuf, dst, zsem))
            return carry
        lax.fori_loop(nu_ref[0], nb, tail, 0)

    stages = stage.shape[0]
    rows = tt * nsub

    def load(tile):
        src = h_hbm.at[pl.ds(pl.multiple_of(tile * rows, rows), rows)]
        return pltpu.make_async_copy(src, stage.at[lax.rem(tile, stages)], lsem.at[lax.rem(tile, stages)])

    def wait_rows(tile):
        s = lax.rem(tile, stages)
        for _ in range(TOP_K):
            pltpu.make_async_copy(stage.at[s], xs_hbm.at[pl.ds(0, rows)], sem.at[s]).wait()

    @pl.when(i == 0)
    def _():
        zbuf[...] = jnp.zeros_like(zbuf)
        zero_fill(lambda c: c.start())
        load(0).start()

    @pl.when(i >= stages - 1)
    def _():
        wait_rows(i - (stages - 1))

    @pl.when(i + 1 < n)
    def _():
        load(i + 1).start()

    load(i).wait()
    s = lax.rem(i, stages)

    def body(r2, carry):
        for u in range(ROW_UNROLL):
            r = r2 * ROW_UNROLL + u
            src = stage.at[s, pl.ds(pl.multiple_of(r * nsub, nsub), nsub)]
            for k in range(TOP_K):
                dst = xs_hbm.at[pl.ds(pl.multiple_of(d_ref[0, 0, r * TOP_K + k] * nsub, nsub), nsub)]
                pltpu.make_async_copy(src, dst, sem.at[s]).start(priority=k % 2)
        return carry

    lax.fori_loop(0, tt // ROW_UNROLL, body, 0)

    @pl.when(i == n - 1)
    def _():
        for back in range(stages - 2, -1, -1):
            @pl.when(i >= back)
            def _():
                wait_rows(i - back)
        zero_fill(lambda c: c.wait())


def _dispatch(dest, h2_tiles, pad_start, pad_len, n_used, nb, nsub):
    t = dest.shape[0]
    tt = _row_tile(t, (256,))
    nt = t // tt
    bm = EXPERT_ROWS
    d3 = dest.reshape(nt, 1, tt * TOP_K)
    gs = pltpu.PrefetchScalarGridSpec(
        num_scalar_prefetch=3,
        grid=(nt,),
        in_specs=[pl.BlockSpec((1, 1, tt * TOP_K), lambda i, *_: (i, 0, 0), memory_space=pltpu.SMEM),
                  pl.BlockSpec(memory_space=pl.ANY)],
        out_specs=pl.BlockSpec(memory_space=pl.ANY),
        scratch_shapes=[pltpu.VMEM((bm * nsub, LANES), ROW_DTYPE),
                        pltpu.VMEM((DISPATCH_STAGES, tt * nsub, LANES), ROW_DTYPE),
                        pltpu.SemaphoreType.DMA((DISPATCH_STAGES,)),
                        pltpu.SemaphoreType.DMA((DISPATCH_STAGES,)),
                        pltpu.SemaphoreType.DMA],
    )
    return pl.pallas_call(
        functools.partial(_dispatch_kernel, tt=tt, nsub=nsub, bm=bm, nb=nb),
        grid_spec=gs,
        out_shape=jax.ShapeDtypeStruct((nb * bm * nsub, LANES), ROW_DTYPE),
        compiler_params=pltpu.CompilerParams(dimension_semantics=("arbitrary",)),
        name="dispatch",
    )(pad_start, pad_len, n_used, d3, h2_tiles)


def _grouped_kernel(b0_ref, nk_ref, nu_ref, x_hbm, wg_ref, wu_ref, wd_ref, y_hbm,
                    xbuf, ybuf, zbuf, xsem, ysem, zsem, wgb, wub, wdb, *, bm, nsub, nb):
    e = pl.program_id(0)
    ring = xbuf.shape[0]
    rows = bm * nsub
    nu = nu_ref[0]

    def x_copy(gb, slot):
        src = x_hbm.at[pl.ds(pl.multiple_of(gb * rows, rows), rows)]
        return pltpu.make_async_copy(src, xbuf.at[slot], xsem.at[slot])

    def y_copy(gb, slot):
        dst = y_hbm.at[pl.ds(pl.multiple_of(gb * rows, rows), rows)]
        return pltpu.make_async_copy(ybuf.at[slot], dst, ysem.at[slot])

    def tail_copy(gb):
        dst = y_hbm.at[pl.ds(pl.multiple_of(gb * rows, rows), rows)]
        return pltpu.make_async_copy(zbuf, dst, zsem)

    def for_tail(op):
        def body(gb, carry):
            op(tail_copy(gb))
            return carry
        lax.fori_loop(nu, nb, body, 0)

    @pl.when(e == 0)
    def _():
        for g0 in range(ring - 1):
            @pl.when(g0 < nu)
            def _():
                x_copy(g0, g0).start()
        zbuf[...] = jnp.zeros_like(zbuf)
        for_tail(lambda c: c.start())

    @pl.when(nk_ref[e] > 0)
    def _():
        wgb[...] = wg_ref[0].astype(BF16)
        wub[...] = wu_ref[0].astype(BF16)
        wdb[...] = wd_ref[0].astype(BF16)

        def block(b, carry):
            gb = b0_ref[e] + b
            slot = lax.rem(gb, ring)

            @pl.when(gb + ring - 1 < nu)
            def _():
                x_copy(gb + ring - 1, lax.rem(gb + ring - 1, ring)).start()

            x_copy(gb, slot).wait()

            @pl.when(gb >= ring)
            def _():
                y_copy(gb - ring, slot).wait()

            x = _read_rows(xbuf.at[slot], bm, nsub).astype(BF16)
            hg = jnp.dot(x, wgb[...], preferred_element_type=F32)
            hu = jnp.dot(x, wub[...], preferred_element_type=F32)
            y = jnp.dot((_silu(hg) * hu).astype(BF16), wdb[...], preferred_element_type=F32)
            _write_rows(ybuf.at[slot], y)
            y_copy(gb, slot).start()
            return carry

        lax.fori_loop(0, nk_ref[e], block, 0)

    @pl.when(e == pl.num_programs(0) - 1)
    def _():
        for back in range(ring, 0, -1):
            @pl.when(nu >= back)
            def _():
                y_copy(nu - back, lax.rem(nu - back, ring)).wait()

        for_tail(lambda c: c.wait())


def _grouped(xs, blk_start, blk_count, n_used, wg, wu, wd):
    ne, d, ff = wg.shape
    nsub = _token_rows(d)
    bm = EXPERT_ROWS
    nb = xs.shape[0] // (bm * nsub)
    wspec = lambda shape: pl.BlockSpec(shape, lambda e, *_: (e, 0, 0))
    gs = pltpu.PrefetchScalarGridSpec(
        num_scalar_prefetch=3,
        grid=(ne,),
        in_specs=[pl.BlockSpec(memory_space=pl.ANY),
                  wspec((1, d, ff)), wspec((1, d, ff)), wspec((1, ff, d))],
        out_specs=pl.BlockSpec(memory_space=pl.ANY),
        scratch_shapes=[pltpu.VMEM((GROUP_RING, bm * nsub, LANES), ROW_DTYPE),
                        pltpu.VMEM((GROUP_RING, bm * nsub, LANES), ROW_DTYPE),
                        pltpu.VMEM((bm * nsub, LANES), ROW_DTYPE),
                        pltpu.SemaphoreType.DMA((GROUP_RING,)), pltpu.SemaphoreType.DMA((GROUP_RING,)),
                        pltpu.SemaphoreType.DMA,
                        pltpu.VMEM((d, ff), BF16), pltpu.VMEM((d, ff), BF16), pltpu.VMEM((ff, d), BF16)],
    )
    return pl.pallas_call(
        functools.partial(_grouped_kernel, bm=bm, nsub=nsub, nb=nb),
        grid_spec=gs,
        out_shape=jax.ShapeDtypeStruct(xs.shape, xs.dtype),
        compiler_params=pltpu.CompilerParams(
            dimension_semantics=("arbitrary",), vmem_limit_bytes=VMEM_LIMIT),
        name="grouped",
    )(blk_start, blk_count, n_used, xs, wg, wu, wd)


def _combine_kernel(d_ref, dn_ref, y_hbm, w_ref, s_ref, xp_ref, xs_ref, g2_ref, fg_ref, op_ref, os_ref,
                    buf_a, buf_b, obuf, sem_a, sem_b, *, tt, nsub, ntp):
    i = pl.program_id(0)
    n = pl.num_programs(0)

    def issue(dref, buf, sem, r):
        for k in range(TOP_K):
            src = y_hbm.at[pl.ds(pl.multiple_of(dref[0, 0, r * TOP_K + k] * nsub, nsub), nsub)]
            pltpu.make_async_copy(src, buf.at[k, pl.ds(pl.multiple_of(r * nsub, nsub), nsub)],
                                  sem).start(priority=k % 2)

    def wait_tile(buf, sem):
        for k in range(TOP_K):
            pltpu.make_async_copy(y_hbm.at[pl.ds(0, tt * nsub)], buf.at[k], sem).wait()

    @pl.when(i == 0)
    def _():
        def first(r, carry):
            issue(d_ref, buf_a, sem_a, r)
            return carry
        lax.fori_loop(0, tt, first, 0)

    is_prompt = i < ntp

    def step(cur, cur_sem, nxt, nxt_sem):
        wait_tile(cur, cur_sem)

        def chunk(c, carry):
            r0 = pl.multiple_of(c * SUBLANES, SUBLANES)
            for u in range(SUBLANES):
                issue(dn_ref, nxt, nxt_sem, r0 + u)
            acc = s_ref[pl.ds(r0, SUBLANES), :]
            w8 = w_ref[pl.ds(r0, SUBLANES), :]
            for k in range(TOP_K):
                rows = jnp.concatenate(
                    [cur[k, pl.ds(r0 * nsub + cc, SUBLANES, stride=nsub), :] for cc in range(nsub)], axis=-1)
                acc = acc + w8[:, k:k + 1] * rows
            g2 = g2_ref[pl.ds(c // (MOD_ROWS // SUBLANES), 1), :]
            x1 = jnp.where(is_prompt, xp_ref[pl.ds(r0, SUBLANES), :], xs_ref[pl.ds(r0, SUBLANES), :])
            obuf[pl.ds(r0, SUBLANES), :] = _rms(x1 + acc * g2, fg_ref[...], RMS_EPS)
            return carry

        lax.fori_loop(0, tt // SUBLANES, chunk, 0)

        @pl.when(i == n - 1)
        def _():
            wait_tile(nxt, nxt_sem)

    @pl.when(lax.rem(i, 2) == 0)
    def _():
        step(buf_a, sem_a, buf_b, sem_b)

    @pl.when(lax.rem(i, 2) == 1)
    def _():
        step(buf_b, sem_b, buf_a, sem_a)

    @pl.when(is_prompt)
    def _():
        op_ref[...] = obuf[...]

    @pl.when(jnp.logical_not(is_prompt))
    def _():
        os_ref[...] = obuf[...]


def _combine(dest, y_tiles, wts, shared, xp1, xs1, g2_rows, final_g):
    tp, d = xp1.shape
    ts = xs1.shape[0]
    nsub = _token_rows(d)
    tt = _row_tile(ts, (256,))
    assert tp % tt == 0 and tt % (SUBLANES * MOD_ROWS) == 0
    ntp = tp // tt
    mr = tt // MOD_ROWS
    nt = (tp + ts) // tt
    d3 = dest.reshape(nt, 1, tt * TOP_K)
    kern = functools.partial(_combine_kernel, tt=tt, nsub=nsub, ntp=ntp)
    smem_blk = lambda f: pl.BlockSpec((1, 1, tt * TOP_K), f, memory_space=pltpu.SMEM)
    row = lambda i: (i, 0)
    two = _two_part_specs(tt, d, ntp)
    return pl.pallas_call(
        kern,
        grid=(nt,),
        in_specs=[smem_blk(lambda i: (i, 0, 0)),
                  smem_blk(lambda i: (jnp.minimum(i + 1, nt - 1), 0, 0)),
                  pl.BlockSpec(memory_space=pl.ANY),
                  pl.BlockSpec((tt, LANES), row), pl.BlockSpec((tt, d), row)] + two + [
                  pl.BlockSpec((mr, d), row), pl.BlockSpec((1, d), lambda i: (0, 0))],
        out_specs=two,
        out_shape=[jax.ShapeDtypeStruct((tp, d), F32), jax.ShapeDtypeStruct((ts, d), F32)],
        scratch_shapes=[pltpu.VMEM((TOP_K, tt * nsub, LANES), ROW_DTYPE),
                        pltpu.VMEM((TOP_K, tt * nsub, LANES), ROW_DTYPE), pltpu.VMEM((tt, d), F32),
                        pltpu.SemaphoreType.DMA, pltpu.SemaphoreType.DMA],
        compiler_params=pltpu.CompilerParams(
            dimension_semantics=("arbitrary",), vmem_limit_bytes=VMEM_LIMIT),
        name="combine",
    )(d3, d3, y_tiles, wts, shared, xp1, xs1, g2_rows, final_g.reshape(1, d))


def _plan(counts, n_assign):
    ne = counts.shape[0]
    bm = EXPERT_ROWS
    counts = counts.astype(jnp.int32)
    nblk = (counts + bm - 1) // bm
    bend = jnp.cumsum(nblk)
    pstart = jnp.broadcast_to(((bend - nblk) * bm).astype(F32).reshape(ne, 1), (ne, LANES))
    nb = -(-(n_assign + ne * (bm - 1)) // bm)
    n_used = bend[-1:].astype(jnp.int32)
    pad_start = ((bend - nblk) * bm + counts).astype(jnp.int32)
    pad_len = (nblk * bm - counts).astype(jnp.int32)
    return pstart, (bend - nblk).astype(jnp.int32), nblk, n_used, pad_start, pad_len, nb


def per_g_pow2(ne):
    per_g = ne // N_EXPERT_GROUPS
    return per_g * N_EXPERT_GROUPS == ne and per_g & (per_g - 1) == 0


def _mod_rows(m, nbp, lp):
    return jnp.concatenate([jnp.repeat(m[:nbp], lp // MOD_ROWS, axis=0), m[nbp:]], axis=0)


def kernel(x_prompt, x_sample, state_ssm, state_conv, state_pool, c_prompt, c_sample, ln1_g, ln2_g, w_ada, b_ada, w_in, conv_w, conv_b, dt_bias, a_log, d_skip, ssd_norm_g, w_ssd_out, pool_w, pool_scale, w_out, w_router, router_bias, moe_w_gate, moe_w_up, moe_w_down, shared_w_gate, shared_w_up, shared_w_down, final_g):
    bp, lp, dm = x_prompt.shape
    bs, ls, _ = x_sample.shape
    depth = ln1_g.shape[0]
    heads, hd, ns = state_ssm.shape[2], state_ssm.shape[3], state_ssm.shape[4]
    di = heads * hd
    gn = SSD_GROUPS * ns
    cch = di + 2 * gn
    assert depth == 1 and ls == MOD_ROWS and lp % MOD_ROWS == 0 and heads <= LANES
    assert per_g_pow2(w_router.shape[2])
    assert ns == 2 * hd and hd & (hd - 1) == 0 and (heads // SSD_GROUPS) % 2 == 0
    assert all(w & (w - 1) == 0 for w in POOL_WINDOWS) and state_pool.shape[2] == max(POOL_WINDOWS) - 1
    tp, ts = bp * lp, bs * ls
    expand3 = (jnp.arange(3 * LANES)[:, None] % LANES == jnp.arange(di)[None, :] // hd).astype(BF16)

    xp, xs = x_prompt.reshape(tp, dm), x_sample.reshape(ts, dm)
    c_all = jnp.concatenate([c_prompt, c_sample], axis=0)

    o1, o2, o3, o4 = di, di + cch, di + cch + heads, di + cch + heads + dm
    cols = (0, di, 2 * di, di + cch, di + cch + dm)

    ssm_p, conv_p, pool_p, ssm_s, conv_s, pool_s = [], [], [], [], [], []
    for l in range(depth):
        wi = w_in[l]
        wcat = jnp.concatenate(
            [wi[:, :o1], wi[:, o1:o2], wi[:, o3:o4], wi[:, o4:],
             jnp.pad(wi[:, o2:o3], ((0, 0), (0, LANES - heads)))], axis=1).astype(BF16)
        pad_h = lambda v: jnp.pad(v.reshape(1, heads), ((0, 0), (0, LANES - heads)))
        seq_w = (conv_w[l], conv_b[l].reshape(1, cch), pad_h(dt_bias[l]), pad_h(a_log[l]),
                 jnp.repeat(d_skip[l], hd).reshape(1, di), ssd_norm_g[l].reshape(1, di),
                 w_ssd_out[l].astype(BF16), pool_w[l].astype(BF16), pool_scale[l].reshape(1, dm),
                 w_out[l].astype(BF16), expand3)

        mod = _ada(c_all, w_ada[l], b_ada[l])
        sh1, sc1, g1, sh2, sc2, g2 = jnp.split(mod, 6, axis=-1)

        proj, dtp = _inproj(xp, xs, _mod_rows(sc1, bp, lp), _mod_rows(sh1, bp, lp), ln1_g[l], wcat)

        zc = jnp.zeros((bp,) + state_conv.shape[2:], F32)
        zs = jnp.zeros((bp, heads, hd, ns), F32)
        zp = jnp.zeros((bp,) + state_pool.shape[2:], F32)
        xp1, ns_p, nc_p, np_p = _seq(proj, dtp, xp, g1[:bp], zc, zs, zp, seq_w,
                                     row0=0, nb=bp, seqlen=lp, pos0=0, cols=cols)
        xs1, ns_s, nc_s, np_s = _seq(proj, dtp, xs, g1[bp:], state_conv[l], state_ssm[l], state_pool[l],
                                     seq_w, row0=tp, nb=bs, seqlen=ls, pos0=PAST_LEN, cols=cols)
        conv_p.append(nc_p)
        conv_s.append(nc_s)
        pool_p.append(np_p)
        pool_s.append(np_s)
        ssm_p.append(ns_p)
        ssm_s.append(ns_s)

        h2_tiles, shared, eidx_t, wts_t, mask_t, counts = _moe_pre(
            xp1, xs1, _mod_rows(sc2, bp, lp), _mod_rows(sh2, bp, lp), ln2_g[l],
            w_router[l].astype(BF16), router_bias[l],
            shared_w_gate[l].astype(BF16), shared_w_up[l].astype(BF16), shared_w_down[l].astype(BF16))
        nsub = _token_rows(dm)
        pstart, blk_start, blk_count, n_used, pad_start, pad_len, nb = _plan(counts[:, 0], (tp + ts) * TOP_K)
        dest = _rank(mask_t, eidx_t, pstart).T
        wts = jnp.pad(wts_t.T, ((0, 0), (0, LANES - TOP_K)))
        x_sorted = _dispatch(dest, h2_tiles, pad_start, pad_len, n_used, nb, nsub)
        y_tiles = _grouped(x_sorted, blk_start, blk_count, n_used,
                           moe_w_gate[l], moe_w_up[l], moe_w_down[l])
        xp, xs = _combine(dest, y_tiles, wts, shared, xp1, xs1, _mod_rows(g2, bp, lp), final_g)

    y_prompt = xp.reshape(bp, lp, dm)
    y_sample = xs.reshape(bs, ls, dm)
    return (y_prompt, y_sample, jnp.stack(ssm_p), jnp.stack(conv_p), jnp.stack(pool_p),
            jnp.stack(ssm_s), jnp.stack(conv_s), jnp.stack(pool_s))
```

```python
import functools

import jax
import jax.numpy as jnp
from jax import lax
from jax.experimental import pallas as pl
from jax.experimental.pallas import tpu as pltpu

F32 = jnp.float32
BF16 = jnp.bfloat16
HIGHEST = lax.Precision.HIGHEST

RMS_EPS = 1e-6
SSD_NORM_EPS = 1e-5
CHUNK = 64
SSD_GROUPS = 4
POOL_WINDOWS = (2, 4, 8, 16)
PAST_LEN = 1024
TOP_K = 8
N_EXPERT_GROUPS = 8
TOPK_GROUPS = 4
ROUTED_SCALE = 2.5

LANES = 128
SUBLANES = 8
MOD_ROWS = 32
VMEM_LIMIT = 56 * 1024 * 1024
EXPERT_ROWS = 128
SEQ_TILE = 256
GROUP_RING = 8
ROW_UNROLL = 4
DISPATCH_STAGES = 3


def _sigmoid(x):
    return 0.5 * jnp.tanh(0.5 * x) + 0.5


def _silu(x):
    h = 0.5 * x
    return h + h * jnp.tanh(h)


def _softplus(x):
    return jnp.maximum(x, 0.0) + jnp.log1p(jnp.exp(-jnp.abs(x)))


def _row_tile(n, prefs):
    for t in prefs:
        if n % t == 0:
            return t
    return n


def _modulate(y, sc_ref, sh_ref):
    rows, d = y.shape
    y3 = y.reshape(rows // MOD_ROWS, MOD_ROWS, d)
    y3 = y3 * (1.0 + sc_ref[...][:, None, :]) + sh_ref[...][:, None, :]
    return y3.reshape(rows, d)


def _rms(x, g, eps):
    return x * lax.rsqrt(jnp.mean(x * x, axis=-1, keepdims=True) + eps) * g


def _ada_kernel(c_ref, w_ref, b_ref, o_ref):
    s = _silu(c_ref[...])
    o_ref[...] = jnp.dot(s, w_ref[...], preferred_element_type=F32, precision=HIGHEST) + b_ref[...]


def _ada(c_all, w_ada, b_ada):
    n, d = c_all.shape
    dout = w_ada.shape[1]
    tn = _row_tile(dout, (1024, 512, 256, 128))
    return pl.pallas_call(
        _ada_kernel,
        grid=(dout // tn,),
        in_specs=[pl.BlockSpec((n, d), lambda j: (0, 0)),
                  pl.BlockSpec((d, tn), lambda j: (0, j)),
                  pl.BlockSpec((1, tn), lambda j: (0, j))],
        out_specs=pl.BlockSpec((n, tn), lambda j: (0, j)),
        out_shape=jax.ShapeDtypeStruct((n, dout), F32),
        name="ada",
    )(c_all, w_ada, b_ada.reshape(1, dout))


def _two_part_specs(tm, d, ntp):
    return [pl.BlockSpec((tm, d), lambda i, *_: (jnp.minimum(i, ntp - 1), 0)),
            pl.BlockSpec((tm, d), lambda i, *_: (jnp.maximum(i - ntp, 0), 0))]


def _inproj_kernel(xp_ref, xs_ref, sc_ref, sh_ref, g_ref, w_ref, o_ref, dt_ref, *, ntp, tn):
    x = jnp.where(pl.program_id(0) < ntp, xp_ref[...], xs_ref[...])
    h = _modulate(_rms(x, g_ref[...], RMS_EPS), sc_ref, sh_ref).astype(BF16)
    n_main = o_ref.shape[1]
    for c0 in range(0, n_main, tn):
        o_ref[:, c0:c0 + tn] = jnp.dot(h, w_ref[:, c0:c0 + tn], preferred_element_type=F32).astype(BF16)
    dt_ref[...] = jnp.dot(h, w_ref[:, n_main:], preferred_element_type=F32)


def _inproj(xp, xs, sc_rows, sh_rows, g, wcat):
    tp, d = xp.shape
    ts = xs.shape[0]
    n = wcat.shape[1]
    n_main = n - LANES
    tm = _row_tile(ts, (512, 256))
    assert tp % tm == 0 and tm % (SUBLANES * MOD_ROWS) == 0
    ntp = tp // tm
    nt = ntp + ts // tm
    mr = tm // MOD_ROWS
    tn = _row_tile(n_main, (2048, 1024, 512, 256, 128))
    row = lambda i: (i, 0)
    return pl.pallas_call(
        functools.partial(_inproj_kernel, ntp=ntp, tn=tn),
        grid=(nt,),
        in_specs=_two_part_specs(tm, d, ntp) + [
            pl.BlockSpec((mr, d), row), pl.BlockSpec((mr, d), row),
            pl.BlockSpec((1, d), lambda i: (0, 0)),
            pl.BlockSpec((d, n), lambda i: (0, 0), pipeline_mode=pl.Buffered(1))],
        out_specs=[pl.BlockSpec((tm, n_main), row), pl.BlockSpec((tm, LANES), row)],
        out_shape=[jax.ShapeDtypeStruct((tp + ts, n_main), BF16),
                   jax.ShapeDtypeStruct((tp + ts, LANES), F32)],
        compiler_params=pltpu.CompilerParams(
            dimension_semantics=("parallel",), vmem_limit_bytes=VMEM_LIMIT),
        name="inproj",
    )(xp, xs, sc_rows, sh_rows, g.reshape(1, d), wcat)


def _seq_kernel(z_ref, xp_ref, bc_ref, u_ref, gt_ref, dt_ref, x_ref, g1_ref,
                cst_ref, sst_ref, pst_ref, cw_ref, cb_ref, dtb_ref, alog_ref, dx_ref, ng_ref,
                wssd_ref, pw_ref, ps_ref, wout_ref, e3_ref,
                xo_ref, so_ref, co_ref, po_ref,
                cbuf, pbuf, st_scr, y_scr, xbc_scr, csx_scr, wvx_scr,
                *, tq, q, pos0, heads, hd, ns, dm):
    j = pl.program_id(1)
    di = heads * hd
    gn = SSD_GROUPS * ns
    hpg = heads // SSD_GROUPS
    cw = cbuf.shape[0] - tq
    kw = cw_ref.shape[0]
    ph = pbuf.shape[0] - tq

    @pl.when(j == 0)
    def _():
        cbuf[0:cw - (kw - 1), :] = jnp.zeros((cw - (kw - 1), cbuf.shape[1]), F32)
        cbuf[cw - (kw - 1):cw, :] = cst_ref[0]
        pbuf[0:1, :] = jnp.zeros((1, dm), F32)
        pbuf[1:ph, :] = pst_ref[0]
        st_scr[...] = sst_ref[0].T

    cbuf[cw:cw + tq, 0:di] = xp_ref[...].astype(F32)
    cbuf[cw:cw + tq, di:di + 2 * gn] = bc_ref[...].astype(F32)
    ext = cbuf[...]
    acc = cb_ref[...] + ext[cw:cw + tq, :] * cw_ref[kw - 1:kw, :]
    for k in range(kw - 1):
        acc = acc + pltpu.roll(ext, kw - 1 - k, axis=0)[cw:cw + tq, :] * cw_ref[k:k + 1, :]
    xbc_scr[...] = _silu(acc)
    cbuf[cw - (kw - 1):cw, :] = cbuf[cw + tq - (kw - 1):cw + tq, :]

    dt = _softplus(dt_ref[...] + dtb_ref[...])
    dta = dt * (-jnp.exp(alog_ref[...]))
    lq = q.bit_length() - 1
    r_i = lax.broadcasted_iota(jnp.int32, (tq, tq), 0)
    c_i = lax.broadcasted_iota(jnp.int32, (tq, tq), 1)
    same = lax.shift_right_logical(r_i, lq) == lax.shift_right_logical(c_i, lq)
    tril = jnp.where(same, (r_i >= c_i).astype(F32), 0.0)
    cs = jnp.dot(tril, dta, preferred_element_type=F32, precision=HIGHEST)
    cs_end = jnp.dot(same.astype(F32), dta, preferred_element_type=F32, precision=HIGHEST)
    wv = dt * jnp.exp(cs_end - cs)

    both = jnp.concatenate([cs, wv], axis=0)
    hi = both.astype(BF16)
    r1 = both - hi.astype(F32)
    mid = r1.astype(BF16)
    lo = (r1 - mid.astype(F32)).astype(BF16)
    ex = jnp.dot(jnp.concatenate([hi, mid, lo], axis=1), e3_ref[...], preferred_element_type=F32)
    csx_scr[...] = ex[0:tq]
    wvx_scr[...] = ex[tq:2 * tq]
    cs_t = cs.T
    dt_t = dt.T

    lane = lax.broadcasted_iota(jnp.int32, (q, 2 * hd), 1)
    kpos = jnp.bitwise_and(lane, hd - 1)
    causal2 = jnp.logical_and(lax.broadcasted_iota(jnp.int32, (q, 2 * hd), 0) >= kpos, kpos < q)
    first = lane < hd
    zrow = jnp.zeros((1, hd - q), F32)
    zblk = jnp.zeros((hd - q, 2 * hd), BF16)

    def pair_row(t, p, r0):
        parts = []
        for h in (2 * p, 2 * p + 1):
            parts.append(t[h:h + 1, r0:r0 + q])
            if q < hd:
                parts.append(zrow)
        return jnp.concatenate(parts, axis=1)

    def pair_rows(a, b):
        blocks = [a, zblk, b, zblk] if q < hd else [a, b]
        return jnp.concatenate(blocks, axis=0)

    gw = di // SSD_GROUPS
    for c in range(tq // q):
        r0 = c * q
        for g in range(SSD_GROUPS):
            bgb = xbc_scr[r0:r0 + q, di + g * ns:di + (g + 1) * ns].astype(BF16)
            cgb = xbc_scr[r0:r0 + q, di + gn + g * ns:di + gn + (g + 1) * ns].astype(BF16)
            cb2 = lax.dot_general(cgb, pair_rows(bgb, bgb), (((1,), (1,)), ((), ())),
                                  preferred_element_type=F32)
            stg = st_scr[:, g * gw:(g + 1) * gw]
            ecs = jnp.exp(csx_scr[r0:r0 + q, g * gw:(g + 1) * gw])
            yo = jnp.dot(cgb, stg.astype(BF16), preferred_element_type=F32) * ecs
            for pp in range(hpg // 2):
                p = g * (hpg // 2) + pp
                lo_l = p * 2 * hd
                seg = csx_scr[r0:r0 + q, lo_l:lo_l + 2 * hd] - pair_row(cs_t, p, r0)
                lm = jnp.exp(jnp.where(causal2, seg, -jnp.inf))
                m2 = (cb2 * lm * pair_row(dt_t, p, r0)).astype(BF16)
                xpair = xbc_scr[r0:r0 + q, lo_l:lo_l + 2 * hd]
                rhs = pair_rows(jnp.where(first, xpair, 0.0).astype(BF16),
                                jnp.where(first, 0.0, xpair).astype(BF16))
                yd = jnp.dot(m2, rhs, preferred_element_type=F32)
                y_scr[r0:r0 + q, lo_l:lo_l + 2 * hd] = yd + yo[:, pp * 2 * hd:(pp + 1) * 2 * hd]
            xw = (xbc_scr[r0:r0 + q, g * gw:(g + 1) * gw]
                  * wvx_scr[r0:r0 + q, g * gw:(g + 1) * gw]).astype(BF16)
            upd = lax.dot_general(bgb, xw, (((0,), (0,)), ((), ())), preferred_element_type=F32)
            st_scr[:, g * gw:(g + 1) * gw] = stg * ecs[q - 1:q, :] + upd

    @pl.when(j == pl.num_programs(1) - 1)
    def _():
        so_ref[0] = st_scr[...].T
        co_ref[0] = cbuf[cw - (kw - 1):cw, :]

    y = y_scr[...] + xbc_scr[:, 0:di] * dx_ref[...]
    v = y * _silu(z_ref[...].astype(F32))
    gw = di // SSD_GROUPS
    parts = []
    for g in range(SSD_GROUPS):
        vg = v[:, g * gw:(g + 1) * gw]
        parts.append(vg * lax.rsqrt(jnp.mean(vg * vg, axis=-1, keepdims=True) + SSD_NORM_EPS))
    yn = (jnp.concatenate(parts, axis=-1) * ng_ref[...]).astype(BF16)
    a_br = jnp.dot(yn, wssd_ref[...], preferred_element_type=F32)

    pbuf[ph:ph + tq, :] = u_ref[...].astype(F32)
    pos = pos0 + j * tq + lax.broadcasted_iota(jnp.int32, (tq, 1), 0)
    pgd = dm // len(POOL_WINDOWS)
    b_parts = []
    for gi, w in enumerate(POOL_WINDOWS):
        lo = gi * pgd
        s = pbuf[:, lo:lo + pgd]
        step = 1
        while step < w:
            s = s + pltpu.roll(s, step, axis=0)
            step *= 2
        cnt = jnp.minimum(w, pos + 1).astype(F32)
        pooled = s[ph:ph + tq, :] / cnt - pbuf[ph:ph + tq, lo:lo + pgd]
        b_parts.append(jnp.dot(pooled.astype(BF16), pw_ref[gi], preferred_element_type=F32))
    b_br = jnp.concatenate(b_parts, axis=-1) * ps_ref[...]
    pbuf[1:ph, :] = pbuf[tq + 1:tq + ph, :]

    @pl.when(j == pl.num_programs(1) - 1)
    def _():
        po_ref[0] = pbuf[1:ph, :]

    gates = _sigmoid(gt_ref[...].astype(F32))
    merged = gates[:, 0:dm] * a_br + gates[:, dm:2 * dm] * b_br
    mix = jnp.dot(merged.astype(BF16), wout_ref[...], preferred_element_type=F32)
    xo_ref[...] = x_ref[...] + g1_ref[0] * mix


def _seq(proj, dtp, x, g1, conv_state, ssm_state, pool_state, wts, *, row0, nb, seqlen, pos0, cols):
    dm = x.shape[1]
    heads, hd, ns = ssm_state.shape[1], ssm_state.shape[2], ssm_state.shape[3]
    di = heads * hd
    gn = SSD_GROUPS * ns
    q = min(CHUNK, seqlen)
    tq = _row_tile(seqlen, (SEQ_TILE, 128, 64, 32))
    nj = seqlen // tq
    rb0 = row0 // tq
    oz, ox, obc, ou, og = cols

    def rows(b, j):
        return rb0 + b * nj + j

    full = lambda a: pl.BlockSpec(a.shape, lambda b, j: (0,) * a.ndim, pipeline_mode=pl.Buffered(1))
    kern = functools.partial(_seq_kernel, tq=tq, q=q, pos0=pos0, heads=heads, hd=hd, ns=ns, dm=dm)
    kw1 = conv_state.shape[1]
    pst = pool_state.shape[1]
    xo, so, co, po = pl.pallas_call(
        kern,
        grid=(nb, nj),
        in_specs=[
            pl.BlockSpec((tq, di), lambda b, j: (rows(b, j), oz // di)),
            pl.BlockSpec((tq, di), lambda b, j: (rows(b, j), ox // di)),
            pl.BlockSpec((tq, 2 * gn), lambda b, j: (rows(b, j), obc // (2 * gn))),
            pl.BlockSpec((tq, dm), lambda b, j: (rows(b, j), ou // dm)),
            pl.BlockSpec((tq, 2 * dm), lambda b, j: (rows(b, j), og // (2 * dm))),
            pl.BlockSpec((tq, LANES), lambda b, j: (rows(b, j), 0)),
            pl.BlockSpec((tq, dm), lambda b, j: (b * nj + j, 0)),
            pl.BlockSpec((1, 1, dm), lambda b, j: (b, 0, 0)),
            pl.BlockSpec((1, kw1, di + 2 * gn), lambda b, j: (b, 0, 0)),
            pl.BlockSpec((1, di, ns), lambda b, j: (b, 0, 0)),
            pl.BlockSpec((1, pst, dm), lambda b, j: (b, 0, 0)),
        ] + [full(w) for w in wts],
        out_specs=[pl.BlockSpec((tq, dm), lambda b, j: (b * nj + j, 0)),
                   pl.BlockSpec((1, di, ns), lambda b, j: (b, 0, 0)),
                   pl.BlockSpec((1, kw1, di + 2 * gn), lambda b, j: (b, 0, 0)),
                   pl.BlockSpec((1, pst, dm), lambda b, j: (b, 0, 0))],
        out_shape=[jax.ShapeDtypeStruct((nb * seqlen, dm), F32),
                   jax.ShapeDtypeStruct((nb, di, ns), F32),
                   jax.ShapeDtypeStruct((nb, kw1, di + 2 * gn), F32),
                   jax.ShapeDtypeStruct((nb, pst, dm), F32)],
        scratch_shapes=[pltpu.VMEM((tq + SUBLANES, di + 2 * gn), F32),
                        pltpu.VMEM((tq + 2 * SUBLANES, dm), F32),
                        pltpu.VMEM((ns, di), F32),
                        pltpu.VMEM((tq, di), F32),
                        pltpu.VMEM((tq, di + 2 * gn), F32),
                        pltpu.VMEM((tq, di), F32),
                        pltpu.VMEM((tq, di), F32)],
        compiler_params=pltpu.CompilerParams(
            dimension_semantics=("parallel", "arbitrary"), vmem_limit_bytes=VMEM_LIMIT),
        name="seq",
    )(proj, proj, proj, proj, proj, dtp, x, g1.reshape(nb, 1, dm),
      conv_state, ssm_state.reshape(nb, di, ns), pool_state, *wts)
    return xo, so.reshape(nb, heads, hd, ns), co, po


def _moe_pre_kernel(xp_ref, xs_ref, sc_ref, sh_ref, g_ref, wr_ref, rb_ref, swg_ref, swu_ref, swd_ref,
                    h_ref, s_ref, e_ref, w_ref, m_ref, cnt_ref, *, ntp):
    x = jnp.where(pl.program_id(0) < ntp, xp_ref[...], xs_ref[...])
    y = _rms(x, g_ref[...], RMS_EPS)
    h = _modulate(y, sc_ref, sh_ref)
    _write_rows(h_ref, h)
    hb = h.astype(BF16)

    sg = jnp.dot(hb, swg_ref[...], preferred_element_type=F32)
    su = jnp.dot(hb, swu_ref[...], preferred_element_type=F32)
    s_ref[...] = jnp.dot((_silu(sg) * su).astype(BF16), swd_ref[...], preferred_element_type=F32)

    scores = jax.nn.sigmoid(lax.dot_general(wr_ref[...], hb, (((1,), (1,)), ((), ())),
                                            preferred_element_type=F32))
    biased = scores + rb_ref[:, 0:1]
    ne, tm = scores.shape
    per_g = ne // N_EXPERT_GROUPS
    neg = -jnp.inf
    row_e = lax.broadcasted_iota(jnp.int32, (ne, tm), 0).astype(F32)
    row_g = lax.broadcasted_iota(jnp.int32, (N_EXPERT_GROUPS, tm), 0).astype(F32)

    def first_argmax(v, idx, big):
        top = jnp.max(v, axis=0, keepdims=True)
        return top, jnp.min(jnp.where(v == top, idx, big), axis=0, keepdims=True)

    gs = jnp.full((N_EXPERT_GROUPS, tm), neg, F32)
    for g in range(N_EXPERT_GROUPS):
        mg = biased[g * per_g:(g + 1) * per_g, :]
        rg = (lax.broadcasted_iota(jnp.int32, (per_g, tm), 0) + g * per_g).astype(F32)
        t1, i1 = first_argmax(mg, rg, float(ne))
        t2 = jnp.max(jnp.where(rg == i1, neg, mg), axis=0, keepdims=True)
        gs = jnp.where(row_g == g, t1 + t2, gs)
    grp_e = lax.shift_right_logical(lax.broadcasted_iota(jnp.int32, (ne, tm), 0),
                                    per_g.bit_length() - 1).astype(F32)
    allowed = jnp.zeros((ne, tm), F32)
    for _ in range(TOPK_GROUPS):
        _, gi = first_argmax(gs, row_g, float(N_EXPERT_GROUPS))
        allowed = jnp.where(grp_e == gi, 1.0, allowed)
        gs = jnp.where(row_g == gi, neg, gs)
    mb = jnp.where(allowed > 0.0, biased, neg)
    row_k = lax.broadcasted_iota(jnp.int32, (TOP_K, tm), 0)
    eacc = jnp.zeros((TOP_K, tm), F32)
    wacc = jnp.zeros((TOP_K, tm), F32)
    chosen = jnp.zeros((ne, tm), F32)
    for k in range(TOP_K):
        _, ik = first_argmax(mb, row_e, float(ne))
        sel = row_e == ik
        wk = jnp.sum(jnp.where(sel, scores, 0.0), axis=0, keepdims=True)
        eacc = jnp.where(row_k == k, ik, eacc)
        wacc = jnp.where(row_k == k, wk, wacc)
        chosen = jnp.where(sel, 1.0, chosen)
        mb = jnp.where(sel, neg, mb)
    wsum = jnp.sum(wacc, axis=0, keepdims=True)
    e_ref[...] = eacc.astype(jnp.int32)
    w_ref[...] = wacc / (wsum + 1e-20) * ROUTED_SCALE
    chosen_b = chosen.astype(BF16)
    m_ref[...] = chosen_b

    @pl.when(pl.program_id(0) == 0)
    def _():
        cnt_ref[...] = jnp.zeros_like(cnt_ref)

    cnt_ref[...] += jnp.dot(chosen_b, jnp.ones((tm, LANES), BF16), preferred_element_type=F32)


def _moe_pre(xp1, xs1, sc_rows, sh_rows, g, wr, rb, swg, swu, swd):
    tp, d = xp1.shape
    ts = xs1.shape[0]
    t = tp + ts
    ne = wr.shape[1]
    ff = swg.shape[1]
    tm = _row_tile(ts, (256,))
    assert tp % tm == 0 and tm % (SUBLANES * MOD_ROWS) == 0
    ntp = tp // tm
    mr = tm // MOD_ROWS
    nsub = _token_rows(d)
    row = lambda i: (i, 0)
    col = lambda i: (0, i)
    const = lambda i: (0, 0)
    rb_col = jnp.broadcast_to(rb.reshape(ne, 1), (ne, LANES))
    return pl.pallas_call(
        functools.partial(_moe_pre_kernel, ntp=ntp),
        grid=(t // tm,),
        in_specs=_two_part_specs(tm, d, ntp) + [
                  pl.BlockSpec((mr, d), row), pl.BlockSpec((mr, d), row),
                  pl.BlockSpec((1, d), const), pl.BlockSpec((ne, d), const), pl.BlockSpec((ne, LANES), const),
                  pl.BlockSpec((d, ff), const), pl.BlockSpec((d, ff), const), pl.BlockSpec((ff, d), const)],
        out_specs=[pl.BlockSpec((tm * nsub, LANES), row), pl.BlockSpec((tm, d), row),
                   pl.BlockSpec((TOP_K, tm), col), pl.BlockSpec((TOP_K, tm), col),
                   pl.BlockSpec((ne, tm), col), pl.BlockSpec((ne, LANES), const)],
        out_shape=[jax.ShapeDtypeStruct((t * nsub, LANES), ROW_DTYPE), jax.ShapeDtypeStruct((t, d), F32),
                   jax.ShapeDtypeStruct((TOP_K, t), jnp.int32), jax.ShapeDtypeStruct((TOP_K, t), F32),
                   jax.ShapeDtypeStruct((ne, t), BF16), jax.ShapeDtypeStruct((ne, LANES), F32)],
        compiler_params=pltpu.CompilerParams(
            dimension_semantics=("arbitrary",), vmem_limit_bytes=VMEM_LIMIT),
        name="moe_pre",
    )(xp1, xs1, sc_rows, sh_rows, g.reshape(1, d), wr.T, rb_col, swg, swu, swd)


ROW_DTYPE = F32


def _token_rows(d):
    return d // LANES


def _read_rows(buf, nrows, nsub):
    return jnp.concatenate([buf[pl.ds(c, nrows, stride=nsub), :] for c in range(nsub)], axis=-1)


def _write_rows(ref, v):
    nrows, width = v.shape
    nsub = _token_rows(width)
    for c in range(nsub):
        ref[pl.ds(c, nrows, stride=nsub), :] = v[:, c * LANES:(c + 1) * LANES]


def _rank_kernel(m_ref, e_ref, ps_ref, d_ref, carry):
    i = pl.program_id(0)

    @pl.when(i == 0)
    def _():
        carry[...] = jnp.zeros_like(carry)

    m = m_ref[...]
    ne, tr = m.shape
    r_i = lax.broadcasted_iota(jnp.int32, (tr, tr), 0)
    c_i = lax.broadcasted_iota(jnp.int32, (tr, tr), 1)
    earlier = (r_i < c_i).astype(BF16)
    base = carry[...] + ps_ref[...]
    slot = jnp.dot(m, earlier, preferred_element_type=F32) + jnp.concatenate([base] * (tr // LANES), axis=1)
    row_e = lax.broadcasted_iota(jnp.int32, (ne, tr), 0)
    d_rows = [jnp.sum(jnp.where(row_e == e_ref[k:k + 1, :], slot, 0.0), axis=0, keepdims=True)
              for k in range(TOP_K)]
    d_ref[...] = jnp.concatenate(d_rows, axis=0).astype(jnp.int32)
    carry[...] += jnp.dot(m, jnp.ones((tr, LANES), BF16), preferred_element_type=F32)


def _rank(mask_t, eidx_t, pstart):
    ne, t = mask_t.shape
    tr = _row_tile(t, (512, 256))
    col = lambda i: (0, i)
    return pl.pallas_call(
        _rank_kernel,
        grid=(t // tr,),
        in_specs=[pl.BlockSpec((ne, tr), col), pl.BlockSpec((TOP_K, tr), col),
                  pl.BlockSpec((ne, LANES), lambda i: (0, 0))],
        out_specs=pl.BlockSpec((TOP_K, tr), col),
        out_shape=jax.ShapeDtypeStruct((TOP_K, t), jnp.int32),
        scratch_shapes=[pltpu.VMEM((ne, LANES), F32)],
        compiler_params=pltpu.CompilerParams(
            dimension_semantics=("arbitrary",), vmem_limit_bytes=VMEM_LIMIT),
        name="rank",
    )(mask_t, eidx_t, pstart)


def _dispatch_kernel(ps_ref, pl_ref, nu_ref, d_ref, h_hbm, xs_hbm, zbuf, stage, lsem, sem, zsem,
                     *, tt, nsub, bm, nb):
    i = pl.program_id(0)
    n = pl.num_programs(0)
    ne = ps_ref.shape[0]
    pieces = [1 << b for b in reversed(range(bm.bit_length() - 1))]

    def zero_fill(op):
        def pad(e, carry):
            start = ps_ref[e]
            length = pl_ref[e]
            for p in pieces:
                off = length - lax.rem(length, 2 * p)

                @pl.when(lax.rem(length, 2 * p) >= p)
                def _():
                    dst = xs_hbm.at[pl.ds(pl.multiple_of((start + off) * nsub, nsub), p * nsub)]
                    op(pltpu.make_async_copy(zbuf.at[pl.ds(0, p * nsub)], dst, zsem))
            return carry
        lax.fori_loop(0, ne, pad, 0)

        def tail(b, carry):
            dst = xs_hbm.at[pl.ds(pl.multiple_of(b * (bm * nsub), bm * nsub), bm * nsub)]
            op(pltpu.make_async_copy(zbuf, dst, zsem))
            return carry
        lax.fori_loop(nu_ref[0], nb, tail, 0)

    stages = stage.shape[0]
    rows = tt * nsub

    def load(tile):
        src = h_hbm.at[pl.ds(pl.multiple_of(tile * rows, rows), rows)]
        return pltpu.make_async_copy(src, stage.at[lax.rem(tile, stages)], lsem.at[lax.rem(tile, stages)])

    def wait_rows(tile):
        s = lax.rem(tile, stages)
        for _ in range(TOP_K):
            pltpu.make_async_copy(stage.at[s], xs_hbm.at[pl.ds(0, rows)], sem.at[s]).wait()

    @pl.when(i == 0)
    def _():
        zbuf[...] = jnp.zeros_like(zbuf)
        zero_fill(lambda c: c.start())
        load(0).start()

    @pl.when(i >= stages - 1)
    def _():
        wait_rows(i - (stages - 1))

    @pl.when(i + 1 < n)
    def _():
        load(i + 1).start()

    load(i).wait()
    s = lax.rem(i, stages)

    def body(r2, carry):
        for u in range(ROW_UNROLL):
            r = r2 * ROW_UNROLL + u
            src = stage.at[s, pl.ds(pl.multiple_of(r * nsub, nsub), nsub)]
            for k in range(TOP_K):
                dst = xs_hbm.at[pl.ds(pl.multiple_of(d_ref[0, 0, r * TOP_K + k] * nsub, nsub), nsub)]
                pltpu.make_async_copy(src, dst, sem.at[s]).start(priority=k % 2)
        return carry

    lax.fori_loop(0, tt // ROW_UNROLL, body, 0)

    @pl.when(i == n - 1)
    def _():
        for back in range(stages - 2, -1, -1):
            @pl.when(i >= back)
            def _():
                wait_rows(i - back)
        zero_fill(lambda c: c.wait())


def _dispatch(dest, h2_tiles, pad_start, pad_len, n_used, nb, nsub):
    t = dest.shape[0]
    tt = _row_tile(t, (256,))
    nt = t // tt
    bm = EXPERT_ROWS
    d3 = dest.reshape(nt, 1, tt * TOP_K)
    gs = pltpu.PrefetchScalarGridSpec(
        num_scalar_prefetch=3,
        grid=(nt,),
        in_specs=[pl.BlockSpec((1, 1, tt * TOP_K), lambda i, *_: (i, 0, 0), memory_space=pltpu.SMEM),
                  pl.BlockSpec(memory_space=pl.ANY)],
        out_specs=pl.BlockSpec(memory_space=pl.ANY),
        scratch_shapes=[pltpu.VMEM((bm * nsub, LANES), ROW_DTYPE),
                        pltpu.VMEM((DISPATCH_STAGES, tt * nsub, LANES), ROW_DTYPE),
                        pltpu.SemaphoreType.DMA((DISPATCH_STAGES,)),
                        pltpu.SemaphoreType.DMA((DISPATCH_STAGES,)),
                        pltpu.SemaphoreType.DMA],
    )
    return pl.pallas_call(
        functools.partial(_dispatch_kernel, tt=tt, nsub=nsub, bm=bm, nb=nb),
        grid_spec=gs,
        out_shape=jax.ShapeDtypeStruct((nb * bm * nsub, LANES), ROW_DTYPE),
        compiler_params=pltpu.CompilerParams(dimension_semantics=("arbitrary",)),
        name="dispatch",
    )(pad_start, pad_len, n_used, d3, h2_tiles)


def _grouped_kernel(b0_ref, nk_ref, nu_ref, x_hbm, wg_ref, wu_ref, wd_ref, y_hbm,
                    xbuf, ybuf, zbuf, xsem, ysem, zsem, wgb, wub, wdb, *, bm, nsub, nb):
    e = pl.program_id(0)
    ring = xbuf.shape[0]
    rows = bm * nsub
    nu = nu_ref[0]

    def x_copy(gb, slot):
        src = x_hbm.at[pl.ds(pl.multiple_of(gb * rows, rows), rows)]
        return pltpu.make_async_copy(src, xbuf.at[slot], xsem.at[slot])

    def y_copy(gb, slot):
        dst = y_hbm.at[pl.ds(pl.multiple_of(gb * rows, rows), rows)]
        return pltpu.make_async_copy(ybuf.at[slot], dst, ysem.at[slot])

    def tail_copy(gb):
        dst = y_hbm.at[pl.ds(pl.multiple_of(gb * rows, rows), rows)]
        return pltpu.make_async_copy(zbuf, dst, zsem)

    def for_tail(op):
        def body(gb, carry):
            op(tail_copy(gb))
            return carry
        lax.fori_loop(nu, nb, body, 0)

    @pl.when(e == 0)
    def _():
        for g0 in range(ring - 1):
            @pl.when(g0 < nu)
            def _():
                x_copy(g0, g0).start()
        zbuf[...] = jnp.zeros_like(zbuf)
        for_tail(lambda c: c.start())

    @pl.when(nk_ref[e] > 0)
    def _():
        wgb[...] = wg_ref[0].astype(BF16)
        wub[...] = wu_ref[0].astype(BF16)
        wdb[...] = wd_ref[0].astype(BF16)

        def block(b, carry):
            gb = b0_ref[e] + b
            slot = lax.rem(gb, ring)

            @pl.when(gb + ring - 1 < nu)
            def _():
                x_copy(gb + ring - 1, lax.rem(gb + ring - 1, ring)).start()

            x_copy(gb, slot).wait()

            @pl.when(gb >= ring)
            def _():
                y_copy(gb - ring, slot).wait()

            x = _read_rows(xbuf.at[slot], bm, nsub).astype(BF16)
            hg = jnp.dot(x, wgb[...], preferred_element_type=F32)
            hu = jnp.dot(x, wub[...], preferred_element_type=F32)
            y = jnp.dot((_silu(hg) * hu).astype(BF16), wdb[...], preferred_element_type=F32)
            _write_rows(ybuf.at[slot], y)
            y_copy(gb, slot).start()
            return carry

        lax.fori_loop(0, nk_ref[e], block, 0)

    @pl.when(e == pl.num_programs(0) - 1)
    def _():
        for back in range(ring, 0, -1):
            @pl.when(nu >= back)
            def _():
                y_copy(nu - back, lax.rem(nu - back, ring)).wait()

        for_tail(lambda c: c.wait())


def _grouped(xs, blk_start, blk_count, n_used, wg, wu, wd):
    ne, d, ff = wg.shape
    nsub = _token_rows(d)
    bm = EXPERT_ROWS
    nb = xs.shape[0] // (bm * nsub)
    wspec = lambda shape: pl.BlockSpec(shape, lambda e, *_: (e, 0, 0))
    gs = pltpu.PrefetchScalarGridSpec(
        num_scalar_prefetch=3,
        grid=(ne,),
        in_specs=[pl.BlockSpec(memory_space=pl.ANY),
                  wspec((1, d, ff)), wspec((1, d, ff)), wspec((1, ff, d))],
        out_specs=pl.BlockSpec(memory_space=pl.ANY),
        scratch_shapes=[pltpu.VMEM((GROUP_RING, bm * nsub, LANES), ROW_DTYPE),
                        pltpu.VMEM((GROUP_RING, bm * nsub, LANES), ROW_DTYPE),
                        pltpu.VMEM((bm * nsub, LANES), ROW_DTYPE),
                        pltpu.SemaphoreType.DMA((GROUP_RING,)), pltpu.SemaphoreType.DMA((GROUP_RING,)),
                        pltpu.SemaphoreType.DMA,
                        pltpu.VMEM((d, ff), BF16), pltpu.VMEM((d, ff), BF16), pltpu.VMEM((ff, d), BF16)],
    )
    return pl.pallas_call(
        functools.partial(_grouped_kernel, bm=bm, nsub=nsub, nb=nb),
        grid_spec=gs,
        out_shape=jax.ShapeDtypeStruct(xs.shape, xs.dtype),
        compiler_params=pltpu.CompilerParams(
            dimension_semantics=("arbitrary",), vmem_limit_bytes=VMEM_LIMIT),
        name="grouped",
    )(blk_start, blk_count, n_used, xs, wg, wu, wd)


def _combine_kernel(d_ref, dn_ref, y_hbm, w_ref, s_ref, xp_ref, xs_ref, g2_ref, fg_ref, op_ref, os_ref,
                    buf_a, buf_b, obuf, sem_a, sem_b, *, tt, nsub, ntp):
    i = pl.program_id(0)
    n = pl.num_programs(0)

    def issue(dref, buf, sem, r):
        for k in range(TOP_K):
            src = y_hbm.at[pl.ds(pl.multiple_of(dref[0, 0, r * TOP_K + k] * nsub, nsub), nsub)]
            pltpu.make_async_copy(src, buf.at[k, pl.ds(pl.multiple_of(r * nsub, nsub), nsub)],
                                  sem).start(priority=k % 2)

    def wait_tile(buf, sem):
        for k in range(TOP_K):
            pltpu.make_async_copy(y_hbm.at[pl.ds(0, tt * nsub)], buf.at[k], sem).wait()

    @pl.when(i == 0)
    def _():
        def first(r, carry):
            issue(d_ref, buf_a, sem_a, r)
            return carry
        lax.fori_loop(0, tt, first, 0)

    is_prompt = i < ntp

    def step(cur, cur_sem, nxt, nxt_sem):
        wait_tile(cur, cur_sem)

        def chunk(c, carry):
            r0 = pl.multiple_of(c * SUBLANES, SUBLANES)
            for u in range(SUBLANES):
                issue(dn_ref, nxt, nxt_sem, r0 + u)
            acc = s_ref[pl.ds(r0, SUBLANES), :]
            w8 = w_ref[pl.ds(r0, SUBLANES), :]
            for k in range(TOP_K):
                rows = jnp.concatenate(
                    [cur[k, pl.ds(r0 * nsub + cc, SUBLANES, stride=nsub), :] for cc in range(nsub)], axis=-1)
                acc = acc + w8[:, k:k + 1] * rows
            g2 = g2_ref[pl.ds(c // (MOD_ROWS // SUBLANES), 1), :]
            x1 = jnp.where(is_prompt, xp_ref[pl.ds(r0, SUBLANES), :], xs_ref[pl.ds(r0, SUBLANES), :])
            obuf[pl.ds(r0, SUBLANES), :] = _rms(x1 + acc * g2, fg_ref[...], RMS_EPS)
            return carry

        lax.fori_loop(0, tt // SUBLANES, chunk, 0)

        @pl.when(i == n - 1)
        def _():
            wait_tile(nxt, nxt_sem)

    @pl.when(lax.rem(i, 2) == 0)
    def _():
        step(buf_a, sem_a, buf_b, sem_b)

    @pl.when(lax.rem(i, 2) == 1)
    def _():
        step(buf_b, sem_b, buf_a, sem_a)

    @pl.when(is_prompt)
    def _():
        op_ref[...] = obuf[...]

    @pl.when(jnp.logical_not(is_prompt))
    def _():
        os_ref[...] = obuf[...]


def _combine(dest, y_tiles, wts, shared, xp1, xs1, g2_rows, final_g):
    tp, d = xp1.shape
    ts = xs1.shape[0]
    nsub = _token_rows(d)
    tt = _row_tile(ts, (256,))
    assert tp % tt == 0 and tt % (SUBLANES * MOD_ROWS) == 0
    ntp = tp // tt
    mr = tt // MOD_ROWS
    nt = (tp + ts) // tt
    d3 = dest.reshape(nt, 1, tt * TOP_K)
    kern = functools.partial(_combine_kernel, tt=tt, nsub=nsub, ntp=ntp)
    smem_blk = lambda f: pl.BlockSpec((1, 1, tt * TOP_K), f, memory_space=pltpu.SMEM)
    row = lambda i: (i, 0)
    two = _two_part_specs(tt, d, ntp)
    return pl.pallas_call(
        kern,
        grid=(nt,),
        in_specs=[smem_blk(lambda i: (i, 0, 0)),
                  smem_blk(lambda i: (jnp.minimum(i + 1, nt - 1), 0, 0)),
                  pl.BlockSpec(memory_space=pl.ANY),
                  pl.BlockSpec((tt, LANES), row), pl.BlockSpec((tt, d), row)] + two + [
                  pl.BlockSpec((mr, d), row), pl.BlockSpec((1, d), lambda i: (0, 0))],
        out_specs=two,
        out_shape=[jax.ShapeDtypeStruct((tp, d), F32), jax.ShapeDtypeStruct((ts, d), F32)],
        scratch_shapes=[pltpu.VMEM((TOP_K, tt * nsub, LANES), ROW_DTYPE),
                        pltpu.VMEM((TOP_K, tt * nsub, LANES), ROW_DTYPE), pltpu.VMEM((tt, d), F32),
                        pltpu.SemaphoreType.DMA, pltpu.SemaphoreType.DMA],
        compiler_params=pltpu.CompilerParams(
            dimension_semantics=("arbitrary",), vmem_limit_bytes=VMEM_LIMIT),
        name="combine",
    )(d3, d3, y_tiles, wts, shared, xp1, xs1, g2_rows, final_g.reshape(1, d))


def _plan(counts, n_assign):
    ne = counts.shape[0]
    bm = EXPERT_ROWS
    counts = counts.astype(jnp.int32)
    nblk = (counts + bm - 1) // bm
    bend = jnp.cumsum(nblk)
    pstart = jnp.broadcast_to(((bend - nblk) * bm).astype(F32).reshape(ne, 1), (ne, LANES))
    nb = -(-(n_assign + ne * (bm - 1)) // bm)
    n_used = bend[-1:].astype(jnp.int32)
    pad_start = ((bend - nblk) * bm + counts).astype(jnp.int32)
    pad_len = (nblk * bm - counts).astype(jnp.int32)
    return pstart, (bend - nblk).astype(jnp.int32), nblk, n_used, pad_start, pad_len, nb


def per_g_pow2(ne):
    per_g = ne // N_EXPERT_GROUPS
    return per_g * N_EXPERT_GROUPS == ne and per_g & (per_g - 1) == 0


def _mod_rows(m, nbp, lp):
    return jnp.concatenate([jnp.repeat(m[:nbp], lp // MOD_ROWS, axis=0), m[nbp:]], axis=0)


def kernel(x_prompt, x_sample, state_ssm, state_conv, state_pool, c_prompt, c_sample, ln1_g, ln2_g, w_ada, b_ada, w_in, conv_w, conv_b, dt_bias, a_log, d_skip, ssd_norm_g, w_ssd_out, pool_w, pool_scale, w_out, w_router, router_bias, moe_w_gate, moe_w_up, moe_w_down, shared_w_gate, shared_w_up, shared_w_down, final_g):
    bp, lp, dm = x_prompt.shape
    bs, ls, _ = x_sample.shape
    depth = ln1_g.shape[0]
    heads, hd, ns = state_ssm.shape[2], state_ssm.shape[3], state_ssm.shape[4]
    di = heads * hd
    gn = SSD_GROUPS * ns
    cch = di + 2 * gn
    assert depth == 1 and ls == MOD_ROWS and lp % MOD_ROWS == 0 and heads <= LANES
    assert per_g_pow2(w_router.shape[2])
    assert ns == 2 * hd and hd & (hd - 1) == 0 and (heads // SSD_GROUPS) % 2 == 0
    assert all(w & (w - 1) == 0 for w in POOL_WINDOWS) and state_pool.shape[2] == max(POOL_WINDOWS) - 1
    tp, ts = bp * lp, bs * ls
    expand3 = (jnp.arange(3 * LANES)[:, None] % LANES == jnp.arange(di)[None, :] // hd).astype(BF16)

    xp, xs = x_prompt.reshape(tp, dm), x_sample.reshape(ts, dm)
    c_all = jnp.concatenate([c_prompt, c_sample], axis=0)

    o1, o2, o3, o4 = di, di + cch, di + cch + heads, di + cch + heads + dm
    cols = (0, di, 2 * di, di + cch, di + cch + dm)

    ssm_p, conv_p, pool_p, ssm_s, conv_s, pool_s = [], [], [], [], [], []
    for l in range(depth):
        wi = w_in[l]
        wcat = jnp.concatenate(
            [wi[:, :o1], wi[:, o1:o2], wi[:, o3:o4], wi[:, o4:],
             jnp.pad(wi[:, o2:o3], ((0, 0), (0, LANES - heads)))], axis=1).astype(BF16)
        pad_h = lambda v: jnp.pad(v.reshape(1, heads), ((0, 0), (0, LANES - heads)))
        seq_w = (conv_w[l], conv_b[l].reshape(1, cch), pad_h(dt_bias[l]), pad_h(a_log[l]),
                 jnp.repeat(d_skip[l], hd).reshape(1, di), ssd_norm_g[l].reshape(1, di),
                 w_ssd_out[l].astype(BF16), pool_w[l].astype(BF16), pool_scale[l].reshape(1, dm),
                 w_out[l].astype(BF16), expand3)

        mod = _ada(c_all, w_ada[l], b_ada[l])
        sh1, sc1, g1, sh2, sc2, g2 = jnp.split(mod, 6, axis=-1)

        proj, dtp = _inproj(xp, xs, _mod_rows(sc1, bp, lp), _mod_rows(sh1, bp, lp), ln1_g[l], wcat)

        zc = jnp.zeros((bp,) + state_conv.shape[2:], F32)
        zs = jnp.zeros((bp, heads, hd, ns), F32)
        zp = jnp.zeros((bp,) + state_pool.shape[2:], F32)
        xp1, ns_p, nc_p, np_p = _seq(proj, dtp, xp, g1[:bp], zc, zs, zp, seq_w,
                                     row0=0, nb=bp, seqlen=lp, pos0=0, cols=cols)
        xs1, ns_s, nc_s, np_s = _seq(proj, dtp, xs, g1[bp:], state_conv[l], state_ssm[l], state_pool[l],
                                     seq_w, row0=tp, nb=bs, seqlen=ls, pos0=PAST_LEN, cols=cols)
        conv_p.append(nc_p)
        conv_s.append(nc_s)
        pool_p.append(np_p)
        pool_s.append(np_s)
        ssm_p.append(ns_p)
        ssm_s.append(ns_s)

        h2_tiles, shared, eidx_t, wts_t, mask_t, counts = _moe_pre(
            xp1, xs1, _mod_rows(sc2, bp, lp), _mod_rows(sh2, bp, lp), ln2_g[l],
            w_router[l].astype(BF16), router_bias[l],
            shared_w_gate[l].astype(BF16), shared_w_up[l].astype(BF16), shared_w_down[l].astype(BF16))
        nsub = _token_rows(dm)
        pstart, blk_start, blk_count, n_used, pad_start, pad_len, nb = _plan(counts[:, 0], (tp + ts) * TOP_K)
        dest = _rank(mask_t, eidx_t, pstart).T
        wts = jnp.pad(wts_t.T, ((0, 0), (0, LANES - TOP_K)))
        x_sorted = _dispatch(dest, h2_tiles, pad_start, pad_len, n_used, nb, nsub)
        y_tiles = _grouped(x_sorted, blk_start, blk_count, n_used,
                           moe_w_gate[l], moe_w_up[l], moe_w_down[l])
        xp, xs = _combine(dest, y_tiles, wts, shared, xp1, xs1, _mod_rows(g2, bp, lp), final_g)

    y_prompt = xp.reshape(bp, lp, dm)
    y_sample = xs.reshape(bs, ls, dm)
    return (y_prompt, y_sample, jnp.stack(ssm_p), jnp.stack(conv_p), jnp.stack(pool_p),
            jnp.stack(ssm_s), jnp.stack(conv_s), jnp.stack(pool_s))
```

```python
import functools

import jax
import jax.numpy as jnp
from jax import lax
from jax.experimental import pallas as pl
from jax.experimental.pallas import tpu as pltpu

F32 = jnp.float32
BF16 = jnp.bfloat16
HIGHEST = lax.Precision.HIGHEST

RMS_EPS = 1e-6
SSD_NORM_EPS = 1e-5
CHUNK = 64
SSD_GROUPS = 4
POOL_WINDOWS = (2, 4, 8, 16)
PAST_LEN = 1024
TOP_K = 8
N_EXPERT_GROUPS = 8
TOPK_GROUPS = 4
ROUTED_SCALE = 2.5

LANES = 128
SUBLANES = 8
MOD_ROWS = 32
VMEM_LIMIT = 56 * 1024 * 1024
EXPERT_ROWS = 256
SEQ_TILE = 256
GROUP_RING = 6
ROW_UNROLL = 4
DISPATCH_STAGES = 3


def _sigmoid(x):
    return 0.5 * jnp.tanh(0.5 * x) + 0.5


def _silu(x):
    h = 0.5 * x
    return h + h * jnp.tanh(h)


def _softplus(x):
    return jnp.maximum(x, 0.0) + jnp.log1p(jnp.exp(-jnp.abs(x)))


def _row_tile(n, prefs):
    for t in prefs:
        if n % t == 0:
            return t
    return n


def _modulate(y, sc_ref, sh_ref):
    rows, d = y.shape
    y3 = y.reshape(rows // MOD_ROWS, MOD_ROWS, d)
    y3 = y3 * (1.0 + sc_ref[...][:, None, :]) + sh_ref[...][:, None, :]
    return y3.reshape(rows, d)


def _rms(x, g, eps):
    return x * lax.rsqrt(jnp.mean(x * x, axis=-1, keepdims=True) + eps) * g


def _ada_kernel(c_ref, w_ref, b_ref, o_ref):
    s = _silu(c_ref[...])
    o_ref[...] = jnp.dot(s, w_ref[...], preferred_element_type=F32, precision=HIGHEST) + b_ref[...]


def _ada(c_all, w_ada, b_ada):
    n, d = c_all.shape
    dout = w_ada.shape[1]
    tn = _row_tile(dout, (1024, 512, 256, 128))
    return pl.pallas_call(
        _ada_kernel,
        grid=(dout // tn,),
        in_specs=[pl.BlockSpec((n, d), lambda j: (0, 0)),
                  pl.BlockSpec((d, tn), lambda j: (0, j)),
                  pl.BlockSpec((1, tn), lambda j: (0, j))],
        out_specs=pl.BlockSpec((n, tn), lambda j: (0, j)),
        out_shape=jax.ShapeDtypeStruct((n, dout), F32),
        name="ada",
    )(c_all, w_ada, b_ada.reshape(1, dout))


def _two_part_specs(tm, d, ntp):
    return [pl.BlockSpec((tm, d), lambda i, *_: (jnp.minimum(i, ntp - 1), 0)),
            pl.BlockSpec((tm, d), lambda i, *_: (jnp.maximum(i - ntp, 0), 0))]


def _inproj_kernel(xp_ref, xs_ref, sc_ref, sh_ref, g_ref, w_ref, o_ref, dt_ref, *, ntp, tn):
    x = jnp.where(pl.program_id(0) < ntp, xp_ref[...], xs_ref[...])
    h = _modulate(_rms(x, g_ref[...], RMS_EPS), sc_ref, sh_ref).astype(BF16)
    n_main = o_ref.shape[1]
    for c0 in range(0, n_main, tn):
        o_ref[:, c0:c0 + tn] = jnp.dot(h, w_ref[:, c0:c0 + tn], preferred_element_type=F32).astype(BF16)
    dt_ref[...] = jnp.dot(h, w_ref[:, n_main:], preferred_element_type=F32)


def _inproj(xp, xs, sc_rows, sh_rows, g, wcat):
    tp, d = xp.shape
    ts = xs.shape[0]
    n = wcat.shape[1]
    n_main = n - LANES
    tm = _row_tile(ts, (512, 256))
    assert tp % tm == 0 and tm % (SUBLANES * MOD_ROWS) == 0
    ntp = tp // tm
    nt = ntp + ts // tm
    mr = tm // MOD_ROWS
    tn = _row_tile(n_main, (2048, 1024, 512, 256, 128))
    row = lambda i: (i, 0)
    return pl.pallas_call(
        functools.partial(_inproj_kernel, ntp=ntp, tn=tn),
        grid=(nt,),
        in_specs=_two_part_specs(tm, d, ntp) + [
            pl.BlockSpec((mr, d), row), pl.BlockSpec((mr, d), row),
            pl.BlockSpec((1, d), lambda i: (0, 0)),
            pl.BlockSpec((d, n), lambda i: (0, 0), pipeline_mode=pl.Buffered(1))],
        out_specs=[pl.BlockSpec((tm, n_main), row), pl.BlockSpec((tm, LANES), row)],
        out_shape=[jax.ShapeDtypeStruct((tp + ts, n_main), BF16),
                   jax.ShapeDtypeStruct((tp + ts, LANES), F32)],
        compiler_params=pltpu.CompilerParams(
            dimension_semantics=("parallel",), vmem_limit_bytes=VMEM_LIMIT),
        name="inproj",
    )(xp, xs, sc_rows, sh_rows, g.reshape(1, d), wcat)


def _seq_kernel(z_ref, xp_ref, bc_ref, u_ref, gt_ref, dt_ref, x_ref, g1_ref,
                cst_ref, sst_ref, pst_ref, cw_ref, cb_ref, dtb_ref, alog_ref, dx_ref, ng_ref,
                wssd_ref, pw_ref, ps_ref, wout_ref, e3_ref,
                xo_ref, so_ref, co_ref, po_ref,
                cbuf, pbuf, st_scr, y_scr, xbc_scr, csx_scr, wvx_scr,
                *, tq, q, pos0, heads, hd, ns, dm):
    j = pl.program_id(1)
    di = heads * hd
    gn = SSD_GROUPS * ns
    hpg = heads // SSD_GROUPS
    cw = cbuf.shape[0] - tq
    kw = cw_ref.shape[0]
    ph = pbuf.shape[0] - tq

    @pl.when(j == 0)
    def _():
        cbuf[0:cw - (kw - 1), :] = jnp.zeros((cw - (kw - 1), cbuf.shape[1]), F32)
        cbuf[cw - (kw - 1):cw, :] = cst_ref[0]
        pbuf[0:1, :] = jnp.zeros((1, dm), F32)
        pbuf[1:ph, :] = pst_ref[0]
        st_scr[...] = sst_ref[0].T

    cbuf[cw:cw + tq, 0:di] = xp_ref[...].astype(F32)
    cbuf[cw:cw + tq, di:di + 2 * gn] = bc_ref[...].astype(F32)
    ext = cbuf[...]
    acc = cb_ref[...] + ext[cw:cw + tq, :] * cw_ref[kw - 1:kw, :]
    for k in range(kw - 1):
        acc = acc + pltpu.roll(ext, kw - 1 - k, axis=0)[cw:cw + tq, :] * cw_ref[k:k + 1, :]
    xbc_scr[...] = _silu(acc)
    cbuf[cw - (kw - 1):cw, :] = cbuf[cw + tq - (kw - 1):cw + tq, :]

    dt = _softplus(dt_ref[...] + dtb_ref[...])
    dta = dt * (-jnp.exp(alog_ref[...]))
    lq = q.bit_length() - 1
    r_i = lax.broadcasted_iota(jnp.int32, (tq, tq), 0)
    c_i = lax.broadcasted_iota(jnp.int32, (tq, tq), 1)
    same = lax.shift_right_logical(r_i, lq) == lax.shift_right_logical(c_i, lq)
    tril = jnp.where(same, (r_i >= c_i).astype(F32), 0.0)
    cs = jnp.dot(tril, dta, preferred_element_type=F32, precision=HIGHEST)
    cs_end = jnp.dot(same.astype(F32), dta, preferred_element_type=F32, precision=HIGHEST)
    wv = dt * jnp.exp(cs_end - cs)

    both = jnp.concatenate([cs, wv], axis=0)
    hi = both.astype(BF16)
    r1 = both - hi.astype(F32)
    mid = r1.astype(BF16)
    lo = (r1 - mid.astype(F32)).astype(BF16)
    ex = jnp.dot(jnp.concatenate([hi, mid, lo], axis=1), e3_ref[...], preferred_element_type=F32)
    csx_scr[...] = ex[0:tq]
    wvx_scr[...] = ex[tq:2 * tq]
    cs_t = cs.T
    dt_t = dt.T

    lane = lax.broadcasted_iota(jnp.int32, (q, 2 * hd), 1)
    kpos = jnp.bitwise_and(lane, hd - 1)
    causal2 = jnp.logical_and(lax.broadcasted_iota(jnp.int32, (q, 2 * hd), 0) >= kpos, kpos < q)
    first = lane < hd
    zrow = jnp.zeros((1, hd - q), F32)
    zblk = jnp.zeros((hd - q, 2 * hd), BF16)

    def pair_row(t, p, r0):
        parts = []
        for h in (2 * p, 2 * p + 1):
            parts.append(t[h:h + 1, r0:r0 + q])
            if q < hd:
                parts.append(zrow)
        return jnp.concatenate(parts, axis=1)

    def pair_rows(a, b):
        blocks = [a, zblk, b, zblk] if q < hd else [a, b]
        return jnp.concatenate(blocks, axis=0)

    gw = di // SSD_GROUPS
    for c in range(tq // q):
        r0 = c * q
        for g in range(SSD_GROUPS):
            bgb = xbc_scr[r0:r0 + q, di + g * ns:di + (g + 1) * ns].astype(BF16)
            cgb = xbc_scr[r0:r0 + q, di + gn + g * ns:di + gn + (g + 1) * ns].astype(BF16)
            cb2 = lax.dot_general(cgb, pair_rows(bgb, bgb), (((1,), (1,)), ((), ())),
                                  preferred_element_type=F32)
            stg = st_scr[:, g * gw:(g + 1) * gw]
            ecs = jnp.exp(csx_scr[r0:r0 + q, g * gw:(g + 1) * gw])
            yo = jnp.dot(cgb, stg.astype(BF16), preferred_element_type=F32) * ecs
            for pp in range(hpg // 2):
                p = g * (hpg // 2) + pp
                lo_l = p * 2 * hd
                seg = csx_scr[r0:r0 + q, lo_l:lo_l + 2 * hd] - pair_row(cs_t, p, r0)
                lm = jnp.exp(jnp.where(causal2, seg, -jnp.inf))
                m2 = (cb2 * lm * pair_row(dt_t, p, r0)).astype(BF16)
                xpair = xbc_scr[r0:r0 + q, lo_l:lo_l + 2 * hd]
                rhs = pair_rows(jnp.where(first, xpair, 0.0).astype(BF16),
                                jnp.where(first, 0.0, xpair).astype(BF16))
                yd = jnp.dot(m2, rhs, preferred_element_type=F32)
                y_scr[r0:r0 + q, lo_l:lo_l + 2 * hd] = yd + yo[:, pp * 2 * hd:(pp + 1) * 2 * hd]
            xw = (xbc_scr[r0:r0 + q, g * gw:(g + 1) * gw]
                  * wvx_scr[r0:r0 + q, g * gw:(g + 1) * gw]).astype(BF16)
            upd = lax.dot_general(bgb, xw, (((0,), (0,)), ((), ())), preferred_element_type=F32)
            st_scr[:, g * gw:(g + 1) * gw] = stg * ecs[q - 1:q, :] + upd

    @pl.when(j == pl.num_programs(1) - 1)
    def _():
        so_ref[0] = st_scr[...].T
        co_ref[0] = cbuf[cw - (kw - 1):cw, :]

    y = y_scr[...] + xbc_scr[:, 0:di] * dx_ref[...]
    v = y * _silu(z_ref[...].astype(F32))
    gw = di // SSD_GROUPS
    parts = []
    for g in range(SSD_GROUPS):
        vg = v[:, g * gw:(g + 1) * gw]
        parts.append(vg * lax.rsqrt(jnp.mean(vg * vg, axis=-1, keepdims=True) + SSD_NORM_EPS))
    yn = (jnp.concatenate(parts, axis=-1) * ng_ref[...]).astype(BF16)
    a_br = jnp.dot(yn, wssd_ref[...], preferred_element_type=F32)

    pbuf[ph:ph + tq, :] = u_ref[...].astype(F32)
    pos = pos0 + j * tq + lax.broadcasted_iota(jnp.int32, (tq, 1), 0)
    pgd = dm // len(POOL_WINDOWS)
    b_parts = []
    for gi, w in enumerate(POOL_WINDOWS):
        lo = gi * pgd
        s = pbuf[:, lo:lo + pgd]
        step = 1
        while step < w:
            s = s + pltpu.roll(s, step, axis=0)
            step *= 2
        cnt = jnp.minimum(w, pos + 1).astype(F32)
        pooled = s[ph:ph + tq, :] / cnt - pbuf[ph:ph + tq, lo:lo + pgd]
        b_parts.append(jnp.dot(pooled.astype(BF16), pw_ref[gi], preferred_element_type=F32))
    b_br = jnp.concatenate(b_parts, axis=-1) * ps_ref[...]
    pbuf[1:ph, :] = pbuf[tq + 1:tq + ph, :]

    @pl.when(j == pl.num_programs(1) - 1)
    def _():
        po_ref[0] = pbuf[1:ph, :]

    gates = _sigmoid(gt_ref[...].astype(F32))
    merged = gates[:, 0:dm] * a_br + gates[:, dm:2 * dm] * b_br
    mix = jnp.dot(merged.astype(BF16), wout_ref[...], preferred_element_type=F32)
    xo_ref[...] = x_ref[...] + g1_ref[0] * mix


def _seq(proj, dtp, x, g1, conv_state, ssm_state, pool_state, wts, *, row0, nb, seqlen, pos0, cols):
    dm = x.shape[1]
    heads, hd, ns = ssm_state.shape[1], ssm_state.shape[2], ssm_state.shape[3]
    di = heads * hd
    gn = SSD_GROUPS * ns
    q = min(CHUNK, seqlen)
    tq = _row_tile(seqlen, (SEQ_TILE, 128, 64, 32))
    nj = seqlen // tq
    rb0 = row0 // tq
    oz, ox, obc, ou, og = cols

    def rows(b, j):
        return rb0 + b * nj + j

    full = lambda a: pl.BlockSpec(a.shape, lambda b, j: (0,) * a.ndim, pipeline_mode=pl.Buffered(1))
    kern = functools.partial(_seq_kernel, tq=tq, q=q, pos0=pos0, heads=heads, hd=hd, ns=ns, dm=dm)
    kw1 = conv_state.shape[1]
    pst = pool_state.shape[1]
    xo, so, co, po = pl.pallas_call(
        kern,
        grid=(nb, nj),
        in_specs=[
            pl.BlockSpec((tq, di), lambda b, j: (rows(b, j), oz // di)),
            pl.BlockSpec((tq, di), lambda b, j: (rows(b, j), ox // di)),
            pl.BlockSpec((tq, 2 * gn), lambda b, j: (rows(b, j), obc // (2 * gn))),
            pl.BlockSpec((tq, dm), lambda b, j: (rows(b, j), ou // dm)),
            pl.BlockSpec((tq, 2 * dm), lambda b, j: (rows(b, j), og // (2 * dm))),
            pl.BlockSpec((tq, LANES), lambda b, j: (rows(b, j), 0)),
            pl.BlockSpec((tq, dm), lambda b, j: (b * nj + j, 0)),
            pl.BlockSpec((1, 1, dm), lambda b, j: (b, 0, 0)),
            pl.BlockSpec((1, kw1, di + 2 * gn), lambda b, j: (b, 0, 0)),
            pl.BlockSpec((1, di, ns), lambda b, j: (b, 0, 0)),
            pl.BlockSpec((1, pst, dm), lambda b, j: (b, 0, 0)),
        ] + [full(w) for w in wts],
        out_specs=[pl.BlockSpec((tq, dm), lambda b, j: (b * nj + j, 0)),
                   pl.BlockSpec((1, di, ns), lambda b, j: (b, 0, 0)),
                   pl.BlockSpec((1, kw1, di + 2 * gn), lambda b, j: (b, 0, 0)),
                   pl.BlockSpec((1, pst, dm), lambda b, j: (b, 0, 0))],
        out_shape=[jax.ShapeDtypeStruct((nb * seqlen, dm), F32),
                   jax.ShapeDtypeStruct((nb, di, ns), F32),
                   jax.ShapeDtypeStruct((nb, kw1, di + 2 * gn), F32),
                   jax.ShapeDtypeStruct((nb, pst, dm), F32)],
        scratch_shapes=[pltpu.VMEM((tq + SUBLANES, di + 2 * gn), F32),
                        pltpu.VMEM((tq + 2 * SUBLANES, dm), F32),
                        pltpu.VMEM((ns, di), F32),
                        pltpu.VMEM((tq, di), F32),
                        pltpu.VMEM((tq, di + 2 * gn), F32),
                        pltpu.VMEM((tq, di), F32),
                        pltpu.VMEM((tq, di), F32)],
        compiler_params=pltpu.CompilerParams(
            dimension_semantics=("parallel", "arbitrary"), vmem_limit_bytes=VMEM_LIMIT),
        name="seq",
    )(proj, proj, proj, proj, proj, dtp, x, g1.reshape(nb, 1, dm),
      conv_state, ssm_state.reshape(nb, di, ns), pool_state, *wts)
    return xo, so.reshape(nb, heads, hd, ns), co, po


def _moe_pre_kernel(xp_ref, xs_ref, sc_ref, sh_ref, g_ref, wr_ref, rb_ref, swg_ref, swu_ref, swd_ref,
                    h_ref, s_ref, e_ref, w_ref, m_ref, cnt_ref, *, ntp):
    x = jnp.where(pl.program_id(0) < ntp, xp_ref[...], xs_ref[...])
    y = _rms(x, g_ref[...], RMS_EPS)
    h = _modulate(y, sc_ref, sh_ref)
    _write_rows(h_ref, h)
    hb = h.astype(BF16)

    sg = jnp.dot(hb, swg_ref[...], preferred_element_type=F32)
    su = jnp.dot(hb, swu_ref[...], preferred_element_type=F32)
    s_ref[...] = jnp.dot((_silu(sg) * su).astype(BF16), swd_ref[...], preferred_element_type=F32)

    scores = jax.nn.sigmoid(lax.dot_general(wr_ref[...], hb, (((1,), (1,)), ((), ())),
                                            preferred_element_type=F32))
    biased = scores + rb_ref[:, 0:1]
    ne, tm = scores.shape
    per_g = ne // N_EXPERT_GROUPS
    neg = -jnp.inf
    row_e = lax.broadcasted_iota(jnp.int32, (ne, tm), 0).astype(F32)
    row_g = lax.broadcasted_iota(jnp.int32, (N_EXPERT_GROUPS, tm), 0).astype(F32)

    def first_argmax(v, idx, big):
        top = jnp.max(v, axis=0, keepdims=True)
        return top, jnp.min(jnp.where(v == top, idx, big), axis=0, keepdims=True)

    gs = jnp.full((N_EXPERT_GROUPS, tm), neg, F32)
    for g in range(N_EXPERT_GROUPS):
        mg = biased[g * per_g:(g + 1) * per_g, :]
        rg = (lax.broadcasted_iota(jnp.int32, (per_g, tm), 0) + g * per_g).astype(F32)
        t1, i1 = first_argmax(mg, rg, float(ne))
        t2 = jnp.max(jnp.where(rg == i1, neg, mg), axis=0, keepdims=True)
        gs = jnp.where(row_g == g, t1 + t2, gs)
    grp_e = lax.shift_right_logical(lax.broadcasted_iota(jnp.int32, (ne, tm), 0),
                                    per_g.bit_length() - 1).astype(F32)
    allowed = jnp.zeros((ne, tm), F32)
    for _ in range(TOPK_GROUPS):
        _, gi = first_argmax(gs, row_g, float(N_EXPERT_GROUPS))
        allowed = jnp.where(grp_e == gi, 1.0, allowed)
        gs = jnp.where(row_g == gi, neg, gs)
    mb = jnp.where(allowed > 0.0, biased, neg)
    row_k = lax.broadcasted_iota(jnp.int32, (TOP_K, tm), 0)
    eacc = jnp.zeros((TOP_K, tm), F32)
    wacc = jnp.zeros((TOP_K, tm), F32)
    chosen = jnp.zeros((ne, tm), F32)
    for k in range(TOP_K):
        _, ik = first_argmax(mb, row_e, float(ne))
        sel = row_e == ik
        wk = jnp.sum(jnp.where(sel, scores, 0.0), axis=0, keepdims=True)
        eacc = jnp.where(row_k == k, ik, eacc)
        wacc = jnp.where(row_k == k, wk, wacc)
        chosen = jnp.where(sel, 1.0, chosen)
        mb = jnp.where(sel, neg, mb)
    wsum = jnp.sum(wacc, axis=0, keepdims=True)
    e_ref[...] = eacc.astype(jnp.int32)
    w_ref[...] = wacc / (wsum + 1e-20) * ROUTED_SCALE
    chosen_b = chosen.astype(BF16)
    m_ref[...] = chosen_b

    @pl.when(pl.program_id(0) == 0)
    def _():
        cnt_ref[...] = jnp.zeros_like(cnt_ref)

    cnt_ref[...] += jnp.dot(chosen_b, jnp.ones((tm, LANES), BF16), preferred_element_type=F32)


def _moe_pre(xp1, xs1, sc_rows, sh_rows, g, wr, rb, swg, swu, swd):
    tp, d = xp1.shape
    ts = xs1.shape[0]
    t = tp + ts
    ne = wr.shape[1]
    ff = swg.shape[1]
    tm = _row_tile(ts, (512, 256))
    assert tp % tm == 0 and tm % (SUBLANES * MOD_ROWS) == 0
    ntp = tp // tm
    mr = tm // MOD_ROWS
    nsub = _token_rows(d)
    row = lambda i: (i, 0)
    col = lambda i: (0, i)
    const = lambda i: (0, 0)
    rb_col = jnp.broadcast_to(rb.reshape(ne, 1), (ne, LANES))
    return pl.pallas_call(
        functools.partial(_moe_pre_kernel, ntp=ntp),
        grid=(t // tm,),
        in_specs=_two_part_specs(tm, d, ntp) + [
                  pl.BlockSpec((mr, d), row), pl.BlockSpec((mr, d), row),
                  pl.BlockSpec((1, d), const), pl.BlockSpec((ne, d), const), pl.BlockSpec((ne, LANES), const),
                  pl.BlockSpec((d, ff), const), pl.BlockSpec((d, ff), const), pl.BlockSpec((ff, d), const)],
        out_specs=[pl.BlockSpec((tm * nsub, LANES), row), pl.BlockSpec((tm, d), row),
                   pl.BlockSpec((TOP_K, tm), col), pl.BlockSpec((TOP_K, tm), col),
                   pl.BlockSpec((ne, tm), col), pl.BlockSpec((ne, LANES), const)],
        out_shape=[jax.ShapeDtypeStruct((t * nsub, LANES), ROW_DTYPE), jax.ShapeDtypeStruct((t, d), F32),
                   jax.ShapeDtypeStruct((TOP_K, t), jnp.int32), jax.ShapeDtypeStruct((TOP_K, t), F32),
                   jax.ShapeDtypeStruct((ne, t), BF16), jax.ShapeDtypeStruct((ne, LANES), F32)],
        compiler_params=pltpu.CompilerParams(
            dimension_semantics=("arbitrary",), vmem_limit_bytes=VMEM_LIMIT),
        name="moe_pre",
    )(xp1, xs1, sc_rows, sh_rows, g.reshape(1, d), wr.T, rb_col, swg, swu, swd)


ROW_DTYPE = F32


def _token_rows(d):
    return d // LANES


def _read_rows(buf, nrows, nsub):
    return jnp.concatenate([buf[pl.ds(c, nrows, stride=nsub), :] for c in range(nsub)], axis=-1)


def _write_rows(ref, v):
    nrows, width = v.shape
    nsub = _token_rows(width)
    for c in range(nsub):
        ref[pl.ds(c, nrows, stride=nsub), :] = v[:, c * LANES:(c + 1) * LANES]


def _rank_kernel(m_ref, e_ref, ps_ref, d_ref, carry):
    i = pl.program_id(0)

    @pl.when(i == 0)
    def _():
        carry[...] = jnp.zeros_like(carry)

    m = m_ref[...]
    ne, tr = m.shape
    r_i = lax.broadcasted_iota(jnp.int32, (tr, tr), 0)
    c_i = lax.broadcasted_iota(jnp.int32, (tr, tr), 1)
    earlier = (r_i < c_i).astype(BF16)
    base = carry[...] + ps_ref[...]
    slot = jnp.dot(m, earlier, preferred_element_type=F32) + jnp.concatenate([base] * (tr // LANES), axis=1)
    row_e = lax.broadcasted_iota(jnp.int32, (ne, tr), 0)
    d_rows = [jnp.sum(jnp.where(row_e == e_ref[k:k + 1, :], slot, 0.0), axis=0, keepdims=True)
              for k in range(TOP_K)]
    d_ref[...] = jnp.concatenate(d_rows, axis=0).astype(jnp.int32)
    carry[...] += jnp.dot(m, jnp.ones((tr, LANES), BF16), preferred_element_type=F32)


def _rank(mask_t, eidx_t, pstart):
    ne, t = mask_t.shape
    tr = _row_tile(t, (512, 256))
    col = lambda i: (0, i)
    return pl.pallas_call(
        _rank_kernel,
        grid=(t // tr,),
        in_specs=[pl.BlockSpec((ne, tr), col), pl.BlockSpec((TOP_K, tr), col),
                  pl.BlockSpec((ne, LANES), lambda i: (0, 0))],
        out_specs=pl.BlockSpec((TOP_K, tr), col),
        out_shape=jax.ShapeDtypeStruct((TOP_K, t), jnp.int32),
        scratch_shapes=[pltpu.VMEM((ne, LANES), F32)],
        compiler_params=pltpu.CompilerParams(
            dimension_semantics=("arbitrary",), vmem_limit_bytes=VMEM_LIMIT),
        name="rank",
    )(mask_t, eidx_t, pstart)


def _dispatch_kernel(ps_ref, pl_ref, nu_ref, d_ref, h_hbm, xs_hbm, zbuf, stage, lsem, sem, zsem,
                     *, tt, nsub, bm, nb):
    i = pl.program_id(0)
    n = pl.num_programs(0)
    ne = ps_ref.shape[0]
    pieces = [1 << b for b in reversed(range(bm.bit_length() - 1))]

    def zero_fill(op):
        def pad(e, carry):
            start = ps_ref[e]
            length = pl_ref[e]
            for p in pieces:
                off = length - lax.rem(length, 2 * p)

                @pl.when(lax.rem(length, 2 * p) >= p)
                def _():
                    dst = xs_hbm.at[pl.ds(pl.multiple_of((start + off) * nsub, nsub), p * nsub)]
                    op(pltpu.make_async_copy(zbuf.at[pl.ds(0, p * nsub)], dst, zsem))
            return carry
        lax.fori_loop(0, ne, pad, 0)

        def tail(b, carry):
            dst = xs_hbm.at[pl.ds(pl.multiple_of(b * (bm * nsub), bm * nsub), bm * nsub)]
            op(pltpu.make_async_copy(zbuf, dst, zsem))
            return carry
        lax.fori_loop(nu_ref[0], nb, tail, 0)

    stages = stage.shape[0]
    rows = tt * nsub

    def load(tile):
        src = h_hbm.at[pl.ds(pl.multiple_of(tile * rows, rows), rows)]
        return pltpu.make_async_copy(src, stage.at[lax.rem(tile, stages)], lsem.at[lax.rem(tile, stages)])

    def wait_rows(tile):
        s = lax.rem(tile, stages)
        for _ in range(TOP_K):
            pltpu.make_async_copy(stage.at[s], xs_hbm.at[pl.ds(0, rows)], sem.at[s]).wait()

    @pl.when(i == 0)
    def _():
        zbuf[...] = jnp.zeros_like(zbuf)
        zero_fill(lambda c: c.start())
        load(0).start()

    @pl.when(i >= stages - 1)
    def _():
        wait_rows(i - (stages - 1))

    @pl.when(i + 1 < n)
    def _():
        load(i + 1).start()

    load(i).wait()
    s = lax.rem(i, stages)

    def body(r2, carry):
        for u in range(ROW_UNROLL):
            r = r2 * ROW_UNROLL + u
            src = stage.at[s, pl.ds(pl.multiple_of(r * nsub, nsub), nsub)]
            for k in range(TOP_K):
                dst = xs_hbm.at[pl.ds(pl.multiple_of(d_ref[0, 0, r * TOP_K + k] * nsub, nsub), nsub)]
                pltpu.make_async_copy(src, dst, sem.at[s]).start(priority=k % 2)
        return carry

    lax.fori_loop(0, tt // ROW_UNROLL, body, 0)

    @pl.when(i == n - 1)
    def _():
        for back in range(stages - 2, -1, -1):
            @pl.when(i >= back)
            def _():
                wait_rows(i - back)
        zero_fill(lambda c: c.wait())


def _dispatch(dest, h2_tiles, pad_start, pad_len, n_used, nb, nsub):
    t = dest.shape[0]
    tt = _row_tile(t, (256,))
    nt = t // tt
    bm = EXPERT_ROWS
    d3 = dest.reshape(nt, 1, tt * TOP_K)
    gs = pltpu.PrefetchScalarGridSpec(
        num_scalar_prefetch=3,
        grid=(nt,),
        in_specs=[pl.BlockSpec((1, 1, tt * TOP_K), lambda i, *_: (i, 0, 0), memory_space=pltpu.SMEM),
                  pl.BlockSpec(memory_space=pl.ANY)],
        out_specs=pl.BlockSpec(memory_space=pl.ANY),
        scratch_shapes=[pltpu.VMEM((bm * nsub, LANES), ROW_DTYPE),
                        pltpu.VMEM((DISPATCH_STAGES, tt * nsub, LANES), ROW_DTYPE),
                        pltpu.SemaphoreType.DMA((DISPATCH_STAGES,)),
                        pltpu.SemaphoreType.DMA((DISPATCH_STAGES,)),
                        pltpu.SemaphoreType.DMA],
    )
    return pl.pallas_call(
        functools.partial(_dispatch_kernel, tt=tt, nsub=nsub, bm=bm, nb=nb),
        grid_spec=gs,
        out_shape=jax.ShapeDtypeStruct((nb * bm * nsub, LANES), ROW_DTYPE),
        compiler_params=pltpu.CompilerParams(dimension_semantics=("arbitrary",)),
        name="dispatch",
    )(pad_start, pad_len, n_used, d3, h2_tiles)


def _grouped_kernel(b0_ref, nk_ref, nu_ref, x_hbm, wg_ref, wu_ref, wd_ref, y_hbm,
                    xbuf, ybuf, zbuf, xsem, ysem, zsem, wgb, wub, wdb, *, bm, nsub, nb):
    e = pl.program_id(0)
    ring = xbuf.shape[0]
    rows = bm * nsub
    nu = nu_ref[0]

    def x_copy(gb, slot):
        src = x_hbm.at[pl.ds(pl.multiple_of(gb * rows, rows), rows)]
        return pltpu.make_async_copy(src, xbuf.at[slot], xsem.at[slot])

    def y_copy(gb, slot):
        dst = y_hbm.at[pl.ds(pl.multiple_of(gb * rows, rows), rows)]
        return pltpu.make_async_copy(ybuf.at[slot], dst, ysem.at[slot])

    def tail_copy(gb):
        dst = y_hbm.at[pl.ds(pl.multiple_of(gb * rows, rows), rows)]
        return pltpu.make_async_copy(zbuf, dst, zsem)

    def for_tail(op):
        def body(gb, carry):
            op(tail_copy(gb))
            return carry
        lax.fori_loop(nu, nb, body, 0)

    @pl.when(e == 0)
    def _():
        for g0 in range(ring - 1):
            @pl.when(g0 < nu)
            def _():
                x_copy(g0, g0).start()
        zbuf[...] = jnp.zeros_like(zbuf)
        for_tail(lambda c: c.start())

    @pl.when(nk_ref[e] > 0)
    def _():
        wgb[...] = wg_ref[0].astype(BF16)
        wub[...] = wu_ref[0].astype(BF16)
        wdb[...] = wd_ref[0].astype(BF16)

        def block(b, carry):
            gb = b0_ref[e] + b
            slot = lax.rem(gb, ring)

            @pl.when(gb + ring - 1 < nu)
            def _():
                x_copy(gb + ring - 1, lax.rem(gb + ring - 1, ring)).start()

            x_copy(gb, slot).wait()

            @pl.when(gb >= ring)
            def _():
                y_copy(gb - ring, slot).wait()

            x = _read_rows(xbuf.at[slot], bm, nsub).astype(BF16)
            hg = jnp.dot(x, wgb[...], preferred_element_type=F32)
            hu = jnp.dot(x, wub[...], preferred_element_type=F32)
            y = jnp.dot((_silu(hg) * hu).astype(BF16), wdb[...], preferred_element_type=F32)
            _write_rows(ybuf.at[slot], y)
            y_copy(gb, slot).start()
            return carry

        lax.fori_loop(0, nk_ref[e], block, 0)

    @pl.when(e == pl.num_programs(0) - 1)
    def _():
        for back in range(ring, 0, -1):
            @pl.when(nu >= back)
            def _():
                y_copy(nu - back, lax.rem(nu - back, ring)).wait()

        for_tail(lambda c: c.wait())


def _grouped(xs, blk_start, blk_count, n_used, wg, wu, wd):
    ne, d, ff = wg.shape
    nsub = _token_rows(d)
    bm = EXPERT_ROWS
    nb = xs.shape[0] // (bm * nsub)
    wspec = lambda shape: pl.BlockSpec(shape, lambda e, *_: (e, 0, 0))
    gs = pltpu.PrefetchScalarGridSpec(
        num_scalar_prefetch=3,
        grid=(ne,),
        in_specs=[pl.BlockSpec(memory_space=pl.ANY),
                  wspec((1, d, ff)), wspec((1, d, ff)), wspec((1, ff, d))],
        out_specs=pl.BlockSpec(memory_space=pl.ANY),
        scratch_shapes=[pltpu.VMEM((GROUP_RING, bm * nsub, LANES), ROW_DTYPE),
                        pltpu.VMEM((GROUP_RING, bm * nsub, LANES), ROW_DTYPE),
                        pltpu.VMEM((bm * nsub, LANES), ROW_DTYPE),
                        pltpu.SemaphoreType.DMA((GROUP_RING,)), pltpu.SemaphoreType.DMA((GROUP_RING,)),
                        pltpu.SemaphoreType.DMA,
                        pltpu.VMEM((d, ff), BF16), pltpu.VMEM((d, ff), BF16), pltpu.VMEM((ff, d), BF16)],
    )
    return pl.pallas_call(
        functools.partial(_grouped_kernel, bm=bm, nsub=nsub, nb=nb),
        grid_spec=gs,
        out_shape=jax.ShapeDtypeStruct(xs.shape, xs.dtype),
        compiler_params=pltpu.CompilerParams(
            dimension_semantics=("arbitrary",), vmem_limit_bytes=VMEM_LIMIT),
        name="grouped",
    )(blk_start, blk_count, n_used, xs, wg, wu, wd)


def _combine_kernel(d_ref, dn_ref, y_hbm, w_ref, s_ref, xp_ref, xs_ref, g2_ref, fg_ref, op_ref, os_ref,
                    buf_a, buf_b, obuf, sem_a, sem_b, *, tt, nsub, ntp):
    i = pl.program_id(0)
    n = pl.num_programs(0)

    def issue(dref, buf, sem, r):
        for k in range(TOP_K):
            src = y_hbm.at[pl.ds(pl.multiple_of(dref[0, 0, r * TOP_K + k] * nsub, nsub), nsub)]
            pltpu.make_async_copy(src, buf.at[k, pl.ds(pl.multiple_of(r * nsub, nsub), nsub)],
                                  sem).start(priority=k % 2)

    def wait_tile(buf, sem):
        for k in range(TOP_K):
            pltpu.make_async_copy(y_hbm.at[pl.ds(0, tt * nsub)], buf.at[k], sem).wait()

    @pl.when(i == 0)
    def _():
        def first(r, carry):
            issue(d_ref, buf_a, sem_a, r)
            return carry
        lax.fori_loop(0, tt, first, 0)

    is_prompt = i < ntp

    def step(cur, cur_sem, nxt, nxt_sem):
        wait_tile(cur, cur_sem)

        def chunk(c, carry):
            r0 = pl.multiple_of(c * SUBLANES, SUBLANES)
            for u in range(SUBLANES):
                issue(dn_ref, nxt, nxt_sem, r0 + u)
            acc = s_ref[pl.ds(r0, SUBLANES), :]
            w8 = w_ref[pl.ds(r0, SUBLANES), :]
            for k in range(TOP_K):
                rows = jnp.concatenate(
                    [cur[k, pl.ds(r0 * nsub + cc, SUBLANES, stride=nsub), :] for cc in range(nsub)], axis=-1)
                acc = acc + w8[:, k:k + 1] * rows
            g2 = g2_ref[pl.ds(c // (MOD_ROWS // SUBLANES), 1), :]
            x1 = jnp.where(is_prompt, xp_ref[pl.ds(r0, SUBLANES), :], xs_ref[pl.ds(r0, SUBLANES), :])
            obuf[pl.ds(r0, SUBLANES), :] = _rms(x1 + acc * g2, fg_ref[...], RMS_EPS)
            return carry

        lax.fori_loop(0, tt // SUBLANES, chunk, 0)

        @pl.when(i == n - 1)
        def _():
            wait_tile(nxt, nxt_sem)

    @pl.when(lax.rem(i, 2) == 0)
    def _():
        step(buf_a, sem_a, buf_b, sem_b)

    @pl.when(lax.rem(i, 2) == 1)
    def _():
        step(buf_b, sem_b, buf_a, sem_a)

    @pl.when(is_prompt)
    def _():
        op_ref[...] = obuf[...]

    @pl.when(jnp.logical_not(is_prompt))
    def _():
        os_ref[...] = obuf[...]


def _combine(dest, y_tiles, wts, shared, xp1, xs1, g2_rows, final_g):
    tp, d = xp1.shape
    ts = xs1.shape[0]
    nsub = _token_rows(d)
    tt = _row_tile(ts, (256,))
    assert tp % tt == 0 and tt % (SUBLANES * MOD_ROWS) == 0
    ntp = tp // tt
    mr = tt // MOD_ROWS
    nt = (tp + ts) // tt
    d3 = dest.reshape(nt, 1, tt * TOP_K)
    kern = functools.partial(_combine_kernel, tt=tt, nsub=nsub, ntp=ntp)
    smem_blk = lambda f: pl.BlockSpec((1, 1, tt * TOP_K), f, memory_space=pltpu.SMEM)
    row = lambda i: (i, 0)
    two = _two_part_specs(tt, d, ntp)
    return pl.pallas_call(
        kern,
        grid=(nt,),
        in_specs=[smem_blk(lambda i: (i, 0, 0)),
                  smem_blk(lambda i: (jnp.minimum(i + 1, nt - 1), 0, 0)),
                  pl.BlockSpec(memory_space=pl.ANY),
                  pl.BlockSpec((tt, LANES), row), pl.BlockSpec((tt, d), row)] + two + [
                  pl.BlockSpec((mr, d), row), pl.BlockSpec((1, d), lambda i: (0, 0))],
        out_specs=two,
        out_shape=[jax.ShapeDtypeStruct((tp, d), F32), jax.ShapeDtypeStruct((ts, d), F32)],
        scratch_shapes=[pltpu.VMEM((TOP_K, tt * nsub, LANES), ROW_DTYPE),
                        pltpu.VMEM((TOP_K, tt * nsub, LANES), ROW_DTYPE), pltpu.VMEM((tt, d), F32),
                        pltpu.SemaphoreType.DMA, pltpu.SemaphoreType.DMA],
        compiler_params=pltpu.CompilerParams(
            dimension_semantics=("arbitrary",), vmem_limit_bytes=VMEM_LIMIT),
        name="combine",
    )(d3, d3, y_tiles, wts, shared, xp1, xs1, g2_rows, final_g.reshape(1, d))


def _plan(counts, n_assign):
    ne = counts.shape[0]
    bm = EXPERT_ROWS
    counts = counts.astype(jnp.int32)
    nblk = (counts + bm - 1) // bm
    bend = jnp.cumsum(nblk)
    pstart = jnp.broadcast_to(((bend - nblk) * bm).astype(F32).reshape(ne, 1), (ne, LANES))
    nb = -(-(n_assign + ne * (bm - 1)) // bm)
    n_used = bend[-1:].astype(jnp.int32)
    pad_start = ((bend - nblk) * bm + counts).astype(jnp.int32)
    pad_len = (nblk * bm - counts).astype(jnp.int32)
    return pstart, (bend - nblk).astype(jnp.int32), nblk, n_used, pad_start, pad_len, nb


def per_g_pow2(ne):
    per_g = ne // N_EXPERT_GROUPS
    return per_g * N_EXPERT_GROUPS == ne and per_g & (per_g - 1) == 0


def _mod_rows(m, nbp, lp):
    return jnp.concatenate([jnp.repeat(m[:nbp], lp // MOD_ROWS, axis=0), m[nbp:]], axis=0)


def kernel(x_prompt, x_sample, state_ssm, state_conv, state_pool, c_prompt, c_sample, ln1_g, ln2_g, w_ada, b_ada, w_in, conv_w, conv_b, dt_bias, a_log, d_skip, ssd_norm_g, w_ssd_out, pool_w, pool_scale, w_out, w_router, router_bias, moe_w_gate, moe_w_up, moe_w_down, shared_w_gate, shared_w_up, shared_w_down, final_g):
    bp, lp, dm = x_prompt.shape
    bs, ls, _ = x_sample.shape
    depth = ln1_g.shape[0]
    heads, hd, ns = state_ssm.shape[2], state_ssm.shape[3], state_ssm.shape[4]
    di = heads * hd
    gn = SSD_GROUPS * ns
    cch = di + 2 * gn
    assert depth == 1 and ls == MOD_ROWS and lp % MOD_ROWS == 0 and heads <= LANES
    assert per_g_pow2(w_router.shape[2])
    assert ns == 2 * hd and hd & (hd - 1) == 0 and (heads // SSD_GROUPS) % 2 == 0
    assert all(w & (w - 1) == 0 for w in POOL_WINDOWS) and state_pool.shape[2] == max(POOL_WINDOWS) - 1
    tp, ts = bp * lp, bs * ls
    expand3 = (jnp.arange(3 * LANES)[:, None] % LANES == jnp.arange(di)[None, :] // hd).astype(BF16)

    xp, xs = x_prompt.reshape(tp, dm), x_sample.reshape(ts, dm)
    c_all = jnp.concatenate([c_prompt, c_sample], axis=0)

    o1, o2, o3, o4 = di, di + cch, di + cch + heads, di + cch + heads + dm
    cols = (0, di, 2 * di, di + cch, di + cch + dm)

    ssm_p, conv_p, pool_p, ssm_s, conv_s, pool_s = [], [], [], [], [], []
    for l in range(depth):
        wi = w_in[l]
        wcat = jnp.concatenate(
            [wi[:, :o1], wi[:, o1:o2], wi[:, o3:o4], wi[:, o4:],
             jnp.pad(wi[:, o2:o3], ((0, 0), (0, LANES - heads)))], axis=1).astype(BF16)
        pad_h = lambda v: jnp.pad(v.reshape(1, heads), ((0, 0), (0, LANES - heads)))
        seq_w = (conv_w[l], conv_b[l].reshape(1, cch), pad_h(dt_bias[l]), pad_h(a_log[l]),
                 jnp.repeat(d_skip[l], hd).reshape(1, di), ssd_norm_g[l].reshape(1, di),
                 w_ssd_out[l].astype(BF16), pool_w[l].astype(BF16), pool_scale[l].reshape(1, dm),
                 w_out[l].astype(BF16), expand3)

        mod = _ada(c_all, w_ada[l], b_ada[l])
        sh1, sc1, g1, sh2, sc2, g2 = jnp.split(mod, 6, axis=-1)

        proj, dtp = _inproj(xp, xs, _mod_rows(sc1, bp, lp), _mod_rows(sh1, bp, lp), ln1_g[l], wcat)

        zc = jnp.zeros((bp,) + state_conv.shape[2:], F32)
        zs = jnp.zeros((bp, heads, hd, ns), F32)
        zp = jnp.zeros((bp,) + state_pool.shape[2:], F32)
        xp1, ns_p, nc_p, np_p = _seq(proj, dtp, xp, g1[:bp], zc, zs, zp, seq_w,
                                     row0=0, nb=bp, seqlen=lp, pos0=0, cols=cols)
        xs1, ns_s, nc_s, np_s = _seq(proj, dtp, xs, g1[bp:], state_conv[l], state_ssm[l], state_pool[l],
                                     seq_w, row0=tp, nb=bs, seqlen=ls, pos0=PAST_LEN, cols=cols)
        conv_p.append(nc_p)
        conv_s.append(nc_s)
        pool_p.append(np_p)
        pool_s.append(np_s)
        ssm_p.append(ns_p)
        ssm_s.append(ns_s)

        h2_tiles, shared, eidx_t, wts_t, mask_t, counts = _moe_pre(
            xp1, xs1, _mod_rows(sc2, bp, lp), _mod_rows(sh2, bp, lp), ln2_g[l],
            w_router[l].astype(BF16), router_bias[l],
            shared_w_gate[l].astype(BF16), shared_w_up[l].astype(BF16), shared_w_down[l].astype(BF16))
        nsub = _token_rows(dm)
        pstart, blk_start, blk_count, n_used, pad_start, pad_len, nb = _plan(counts[:, 0], (tp + ts) * TOP_K)
        dest = _rank(mask_t, eidx_t, pstart).T
        wts = jnp.pad(wts_t.T, ((0, 0), (0, LANES - TOP_K)))
        x_sorted = _dispatch(dest, h2_tiles, pad_start, pad_len, n_used, nb, nsub)
        y_tiles = _grouped(x_sorted, blk_start, blk_count, n_used,
                           moe_w_gate[l], moe_w_up[l], moe_w_down[l])
        xp, xs = _combine(dest, y_tiles, wts, shared, xp1, xs1, _mod_rows(g2, bp, lp), final_g)

    y_prompt = xp.reshape(bp, lp, dm)
    y_sample = xs.reshape(bs, ls, dm)
    return (y_prompt, y_sample, jnp.stack(ssm_p), jnp.stack(conv_p), jnp.stack(pool_p),
            jnp.stack(ssm_s), jnp.stack(conv_s), jnp.stack(pool_s))
```

```python
import functools

import jax
import jax.numpy as jnp
from jax import lax
from jax.experimental import pallas as pl
from jax.experimental.pallas import tpu as pltpu

F32 = jnp.float32
BF16 = jnp.bfloat16
HIGHEST = lax.Precision.HIGHEST

RMS_EPS = 1e-6
SSD_NORM_EPS = 1e-5
CHUNK = 64
SSD_GROUPS = 4
POOL_WINDOWS = (2, 4, 8, 16)
PAST_LEN = 1024
TOP_K = 8
N_EXPERT_GROUPS = 8
TOPK_GROUPS = 4
ROUTED_SCALE = 2.5

LANES = 128
SUBLANES = 8
MOD_ROWS = 32
VMEM_LIMIT = 56 * 1024 * 1024
EXPERT_ROWS = 256
SEQ_TILE = 256
GROUP_RING = 6
ROW_UNROLL = 4
DISPATCH_STAGES = 3


def _sigmoid(x):
    return 0.5 * jnp.tanh(0.5 * x) + 0.5


def _silu(x):
    h = 0.5 * x
    return h + h * jnp.tanh(h)


def _softplus(x):
    return jnp.maximum(x, 0.0) + jnp.log1p(jnp.exp(-jnp.abs(x)))


def _row_tile(n, prefs):
    for t in prefs:
        if n % t == 0:
            return t
    return n


def _modulate(y, sc_ref, sh_ref):
    rows, d = y.shape
    y3 = y.reshape(rows // MOD_ROWS, MOD_ROWS, d)
    y3 = y3 * (1.0 + sc_ref[...][:, None, :]) + sh_ref[...][:, None, :]
    return y3.reshape(rows, d)


def _rms(x, g, eps):
    return x * lax.rsqrt(jnp.mean(x * x, axis=-1, keepdims=True) + eps) * g


def _ada_kernel(c_ref, w_ref, b_ref, o_ref):
    s = _silu(c_ref[...])
    o_ref[...] = jnp.dot(s, w_ref[...], preferred_element_type=F32, precision=HIGHEST) + b_ref[...]


def _ada(c_all, w_ada, b_ada):
    n, d = c_all.shape
    dout = w_ada.shape[1]
    tn = _row_tile(dout, (1024, 512, 256, 128))
    return pl.pallas_call(
        _ada_kernel,
        grid=(dout // tn,),
        in_specs=[pl.BlockSpec((n, d), lambda j: (0, 0)),
                  pl.BlockSpec((d, tn), lambda j: (0, j)),
                  pl.BlockSpec((1, tn), lambda j: (0, j))],
        out_specs=pl.BlockSpec((n, tn), lambda j: (0, j)),
        out_shape=jax.ShapeDtypeStruct((n, dout), F32),
        name="ada",
    )(c_all, w_ada, b_ada.reshape(1, dout))


def _two_part_specs(tm, d, ntp):
    return [pl.BlockSpec((tm, d), lambda i, *_: (jnp.minimum(i, ntp - 1), 0)),
            pl.BlockSpec((tm, d), lambda i, *_: (jnp.maximum(i - ntp, 0), 0))]


def _inproj_kernel(xp_ref, xs_ref, sc_ref, sh_ref, g_ref, w_ref, o_ref, dt_ref, *, ntp, tn):
    x = jnp.where(pl.program_id(0) < ntp, xp_ref[...], xs_ref[...])
    h = _modulate(_rms(x, g_ref[...], RMS_EPS), sc_ref, sh_ref).astype(BF16)
    n_main = o_ref.shape[1]
    for c0 in range(0, n_main, tn):
        o_ref[:, c0:c0 + tn] = jnp.dot(h, w_ref[:, c0:c0 + tn], preferred_element_type=F32).astype(BF16)
    dt_ref[...] = jnp.dot(h, w_ref[:, n_main:], preferred_element_type=F32)


def _inproj(xp, xs, sc_rows, sh_rows, g, wcat):
    tp, d = xp.shape
    ts = xs.shape[0]
    n = wcat.shape[1]
    n_main = n - LANES
    tm = _row_tile(ts, (512, 256))
    assert tp % tm == 0 and tm % (SUBLANES * MOD_ROWS) == 0
    ntp = tp // tm
    nt = ntp + ts // tm
    mr = tm // MOD_ROWS
    tn = _row_tile(n_main, (2048, 1024, 512, 256, 128))
    row = lambda i: (i, 0)
    return pl.pallas_call(
        functools.partial(_inproj_kernel, ntp=ntp, tn=tn),
        grid=(nt,),
        in_specs=_two_part_specs(tm, d, ntp) + [
            pl.BlockSpec((mr, d), row), pl.BlockSpec((mr, d), row),
            pl.BlockSpec((1, d), lambda i: (0, 0)),
            pl.BlockSpec((d, n), lambda i: (0, 0), pipeline_mode=pl.Buffered(1))],
        out_specs=[pl.BlockSpec((tm, n_main), row), pl.BlockSpec((tm, LANES), row)],
        out_shape=[jax.ShapeDtypeStruct((tp + ts, n_main), BF16),
                   jax.ShapeDtypeStruct((tp + ts, LANES), F32)],
        compiler_params=pltpu.CompilerParams(
            dimension_semantics=("parallel",), vmem_limit_bytes=VMEM_LIMIT),
        name="inproj",
    )(xp, xs, sc_rows, sh_rows, g.reshape(1, d), wcat)


def _seq_kernel(z_ref, xp_ref, bc_ref, u_ref, gt_ref, dt_ref, x_ref, g1_ref,
                cst_ref, sst_ref, pst_ref, cw_ref, cb_ref, dtb_ref, alog_ref, dx_ref, ng_ref,
                wssd_ref, pw_ref, ps_ref, wout_ref, e3_ref,
                xo_ref, so_ref, co_ref, po_ref,
                cbuf, pbuf, st_scr, y_scr, xbc_scr, csx_scr, wvx_scr,
                *, tq, q, pos0, heads, hd, ns, dm):
    j = pl.program_id(1)
    di = heads * hd
    gn = SSD_GROUPS * ns
    hpg = heads // SSD_GROUPS
    cw = cbuf.shape[0] - tq
    kw = cw_ref.shape[0]
    ph = pbuf.shape[0] - tq

    @pl.when(j == 0)
    def _():
        cbuf[0:cw - (kw - 1), :] = jnp.zeros((cw - (kw - 1), cbuf.shape[1]), F32)
        cbuf[cw - (kw - 1):cw, :] = cst_ref[0]
        pbuf[0:1, :] = jnp.zeros((1, dm), F32)
        pbuf[1:ph, :] = pst_ref[0]
        st_scr[...] = sst_ref[0].T

    cbuf[cw:cw + tq, 0:di] = xp_ref[...].astype(F32)
    cbuf[cw:cw + tq, di:di + 2 * gn] = bc_ref[...].astype(F32)
    ext = cbuf[...]
    acc = cb_ref[...] + ext[cw:cw + tq, :] * cw_ref[kw - 1:kw, :]
    for k in range(kw - 1):
        acc = acc + pltpu.roll(ext, kw - 1 - k, axis=0)[cw:cw + tq, :] * cw_ref[k:k + 1, :]
    xbc_scr[...] = _silu(acc)
    cbuf[cw - (kw - 1):cw, :] = cbuf[cw + tq - (kw - 1):cw + tq, :]

    dt = _softplus(dt_ref[...] + dtb_ref[...])
    dta = dt * (-jnp.exp(alog_ref[...]))
    lq = q.bit_length() - 1
    r_i = lax.broadcasted_iota(jnp.int32, (tq, tq), 0)
    c_i = lax.broadcasted_iota(jnp.int32, (tq, tq), 1)
    same = lax.shift_right_logical(r_i, lq) == lax.shift_right_logical(c_i, lq)
    tril = jnp.where(same, (r_i >= c_i).astype(F32), 0.0)
    cs = jnp.dot(tril, dta, preferred_element_type=F32, precision=HIGHEST)
    cs_end = jnp.dot(same.astype(F32), dta, preferred_element_type=F32, precision=HIGHEST)
    wv = dt * jnp.exp(cs_end - cs)

    both = jnp.concatenate([cs, wv], axis=0)
    hi = both.astype(BF16)
    r1 = both - hi.astype(F32)
    mid = r1.astype(BF16)
    lo = (r1 - mid.astype(F32)).astype(BF16)
    ex = jnp.dot(jnp.concatenate([hi, mid, lo], axis=1), e3_ref[...], preferred_element_type=F32)
    csx_scr[...] = ex[0:tq]
    wvx_scr[...] = ex[tq:2 * tq]
    cs_t = cs.T
    dt_t = dt.T

    lane = lax.broadcasted_iota(jnp.int32, (q, 2 * hd), 1)
    kpos = jnp.bitwise_and(lane, hd - 1)
    causal2 = jnp.logical_and(lax.broadcasted_iota(jnp.int32, (q, 2 * hd), 0) >= kpos, kpos < q)
    first = lane < hd
    zrow = jnp.zeros((1, hd - q), F32)
    zblk = jnp.zeros((hd - q, 2 * hd), BF16)

    def pair_row(t, p, r0):
        parts = []
        for h in (2 * p, 2 * p + 1):
            parts.append(t[h:h + 1, r0:r0 + q])
            if q < hd:
                parts.append(zrow)
        return jnp.concatenate(parts, axis=1)

    def pair_rows(a, b):
        blocks = [a, zblk, b, zblk] if q < hd else [a, b]
        return jnp.concatenate(blocks, axis=0)

    gw = di // SSD_GROUPS
    for c in range(tq // q):
        r0 = c * q
        for g in range(SSD_GROUPS):
            bgb = xbc_scr[r0:r0 + q, di + g * ns:di + (g + 1) * ns].astype(BF16)
            cgb = xbc_scr[r0:r0 + q, di + gn + g * ns:di + gn + (g + 1) * ns].astype(BF16)
            cb2 = lax.dot_general(cgb, pair_rows(bgb, bgb), (((1,), (1,)), ((), ())),
                                  preferred_element_type=F32)
            stg = st_scr[:, g * gw:(g + 1) * gw]
            ecs = jnp.exp(csx_scr[r0:r0 + q, g * gw:(g + 1) * gw])
            yo = jnp.dot(cgb, stg.astype(BF16), preferred_element_type=F32) * ecs
            for pp in range(hpg // 2):
                p = g * (hpg // 2) + pp
                lo_l = p * 2 * hd
                seg = csx_scr[r0:r0 + q, lo_l:lo_l + 2 * hd] - pair_row(cs_t, p, r0)
                lm = jnp.exp(jnp.where(causal2, seg, -jnp.inf))
                m2 = (cb2 * lm * pair_row(dt_t, p, r0)).astype(BF16)
                xpair = xbc_scr[r0:r0 + q, lo_l:lo_l + 2 * hd]
                rhs = pair_rows(jnp.where(first, xpair, 0.0).astype(BF16),
                                jnp.where(first, 0.0, xpair).astype(BF16))
                yd = jnp.dot(m2, rhs, preferred_element_type=F32)
                y_scr[r0:r0 + q, lo_l:lo_l + 2 * hd] = yd + yo[:, pp * 2 * hd:(pp + 1) * 2 * hd]
            xw = (xbc_scr[r0:r0 + q, g * gw:(g + 1) * gw]
                  * wvx_scr[r0:r0 + q, g * gw:(g + 1) * gw]).astype(BF16)
            upd = lax.dot_general(bgb, xw, (((0,), (0,)), ((), ())), preferred_element_type=F32)
            st_scr[:, g * gw:(g + 1) * gw] = stg * ecs[q - 1:q, :] + upd

    @pl.when(j == pl.num_programs(1) - 1)
    def _():
        so_ref[0] = st_scr[...].T
        co_ref[0] = cbuf[cw - (kw - 1):cw, :]

    y = y_scr[...] + xbc_scr[:, 0:di] * dx_ref[...]
    v = y * _silu(z_ref[...].astype(F32))
    gw = di // SSD_GROUPS
    parts = []
    for g in range(SSD_GROUPS):
        vg = v[:, g * gw:(g + 1) * gw]
        parts.append(vg * lax.rsqrt(jnp.mean(vg * vg, axis=-1, keepdims=True) + SSD_NORM_EPS))
    yn = (jnp.concatenate(parts, axis=-1) * ng_ref[...]).astype(BF16)
    a_br = jnp.dot(yn, wssd_ref[...], preferred_element_type=F32)

    pbuf[ph:ph + tq, :] = u_ref[...].astype(F32)
    pos = pos0 + j * tq + lax.broadcasted_iota(jnp.int32, (tq, 1), 0)
    pgd = dm // len(POOL_WINDOWS)
    b_parts = []
    for gi, w in enumerate(POOL_WINDOWS):
        lo = gi * pgd
        s = pbuf[:, lo:lo + pgd]
        step = 1
        while step < w:
            s = s + pltpu.roll(s, step, axis=0)
            step *= 2
        cnt = jnp.minimum(w, pos + 1).astype(F32)
        pooled = s[ph:ph + tq, :] / cnt - pbuf[ph:ph + tq, lo:lo + pgd]
        b_parts.append(jnp.dot(pooled.astype(BF16), pw_ref[gi], preferred_element_type=F32))
    b_br = jnp.concatenate(b_parts, axis=-1) * ps_ref[...]
    pbuf[1:ph, :] = pbuf[tq + 1:tq + ph, :]

    @pl.when(j == pl.num_programs(1) - 1)
    def _():
        po_ref[0] = pbuf[1:ph, :]

    gates = _sigmoid(gt_ref[...].astype(F32))
    merged = gates[:, 0:dm] * a_br + gates[:, dm:2 * dm] * b_br
    mix = jnp.dot(merged.astype(BF16), wout_ref[...], preferred_element_type=F32)
    xo_ref[...] = x_ref[...] + g1_ref[0] * mix


def _seq(proj, dtp, x, g1, conv_state, ssm_state, pool_state, wts, *, row0, nb, seqlen, pos0, cols):
    dm = x.shape[1]
    heads, hd, ns = ssm_state.shape[1], ssm_state.shape[2], ssm_state.shape[3]
    di = heads * hd
    gn = SSD_GROUPS * ns
    q = min(CHUNK, seqlen)
    tq = _row_tile(seqlen, (SEQ_TILE, 128, 64, 32))
    nj = seqlen // tq
    rb0 = row0 // tq
    oz, ox, obc, ou, og = cols

    def rows(b, j):
        return rb0 + b * nj + j

    full = lambda a: pl.BlockSpec(a.shape, lambda b, j: (0,) * a.ndim, pipeline_mode=pl.Buffered(1))
    kern = functools.partial(_seq_kernel, tq=tq, q=q, pos0=pos0, heads=heads, hd=hd, ns=ns, dm=dm)
    kw1 = conv_state.shape[1]
    pst = pool_state.shape[1]
    xo, so, co, po = pl.pallas_call(
        kern,
        grid=(nb, nj),
        in_specs=[
            pl.BlockSpec((tq, di), lambda b, j: (rows(b, j), oz // di)),
            pl.BlockSpec((tq, di), lambda b, j: (rows(b, j), ox // di)),
            pl.BlockSpec((tq, 2 * gn), lambda b, j: (rows(b, j), obc // (2 * gn))),
            pl.BlockSpec((tq, dm), lambda b, j: (rows(b, j), ou // dm)),
            pl.BlockSpec((tq, 2 * dm), lambda b, j: (rows(b, j), og // (2 * dm))),
            pl.BlockSpec((tq, LANES), lambda b, j: (rows(b, j), 0)),
            pl.BlockSpec((tq, dm), lambda b, j: (b * nj + j, 0)),
            pl.BlockSpec((1, 1, dm), lambda b, j: (b, 0, 0)),
            pl.BlockSpec((1, kw1, di + 2 * gn), lambda b, j: (b, 0, 0)),
            pl.BlockSpec((1, di, ns), lambda b, j: (b, 0, 0)),
            pl.BlockSpec((1, pst, dm), lambda b, j: (b, 0, 0)),
        ] + [full(w) for w in wts],
        out_specs=[pl.BlockSpec((tq, dm), lambda b, j: (b * nj + j, 0)),
                   pl.BlockSpec((1, di, ns), lambda b, j: (b, 0, 0)),
                   pl.BlockSpec((1, kw1, di + 2 * gn), lambda b, j: (b, 0, 0)),
                   pl.BlockSpec((1, pst, dm), lambda b, j: (b, 0, 0))],
        out_shape=[jax.ShapeDtypeStruct((nb * seqlen, dm), F32),
                   jax.ShapeDtypeStruct((nb, di, ns), F32),
                   jax.ShapeDtypeStruct((nb, kw1, di + 2 * gn), F32),
                   jax.ShapeDtypeStruct((nb, pst, dm), F32)],
        scratch_shapes=[pltpu.VMEM((tq + SUBLANES, di + 2 * gn), F32),
                        pltpu.VMEM((tq + 2 * SUBLANES, dm), F32),
                        pltpu.VMEM((ns, di), F32),
                        pltpu.VMEM((tq, di), F32),
                        pltpu.VMEM((tq, di + 2 * gn), F32),
                        pltpu.VMEM((tq, di), F32),
                        pltpu.VMEM((tq, di), F32)],
        compiler_params=pltpu.CompilerParams(
            dimension_semantics=("parallel", "arbitrary"), vmem_limit_bytes=VMEM_LIMIT),
        name="seq",
    )(proj, proj, proj, proj, proj, dtp, x, g1.reshape(nb, 1, dm),
      conv_state, ssm_state.reshape(nb, di, ns), pool_state, *wts)
    return xo, so.reshape(nb, heads, hd, ns), co, po


def _moe_pre_kernel(xp_ref, xs_ref, sc_ref, sh_ref, g_ref, wr_ref, rb_ref, swg_ref, swu_ref, swd_ref,
                    h_ref, s_ref, e_ref, w_ref, m_ref, cnt_ref, *, ntp):
    x = jnp.where(pl.program_id(0) < ntp, xp_ref[...], xs_ref[...])
    y = _rms(x, g_ref[...], RMS_EPS)
    h = _modulate(y, sc_ref, sh_ref)
    _write_rows(h_ref, h)
    hb = h.astype(BF16)

    sg = jnp.dot(hb, swg_ref[...], preferred_element_type=F32)
    su = jnp.dot(hb, swu_ref[...], preferred_element_type=F32)
    s_ref[...] = jnp.dot((_silu(sg) * su).astype(BF16), swd_ref[...], preferred_element_type=F32)

    scores = jax.nn.sigmoid(lax.dot_general(wr_ref[...], hb, (((1,), (1,)), ((), ())),
                                            preferred_element_type=F32))
    biased = scores + rb_ref[:, 0:1]
    ne, tm = scores.shape
    per_g = ne // N_EXPERT_GROUPS
    neg = -jnp.inf
    row_e = lax.broadcasted_iota(jnp.int32, (ne, tm), 0).astype(F32)
    row_g = lax.broadcasted_iota(jnp.int32, (N_EXPERT_GROUPS, tm), 0).astype(F32)

    def first_argmax(v, idx, big):
        top = jnp.max(v, axis=0, keepdims=True)
        return top, jnp.min(jnp.where(v == top, idx, big), axis=0, keepdims=True)

    gs = jnp.full((N_EXPERT_GROUPS, tm), neg, F32)
    for g in range(N_EXPERT_GROUPS):
        mg = biased[g * per_g:(g + 1) * per_g, :]
        rg = (lax.broadcasted_iota(jnp.int32, (per_g, tm), 0) + g * per_g).astype(F32)
        t1, i1 = first_argmax(mg, rg, float(ne))
        t2 = jnp.max(jnp.where(rg == i1, neg, mg), axis=0, keepdims=True)
        gs = jnp.where(row_g == g, t1 + t2, gs)
    grp_e = lax.shift_right_logical(lax.broadcasted_iota(jnp.int32, (ne, tm), 0),
                                    per_g.bit_length() - 1).astype(F32)
    allowed = jnp.zeros((ne, tm), F32)
    for _ in range(TOPK_GROUPS):
        _, gi = first_argmax(gs, row_g, float(N_EXPERT_GROUPS))
        allowed = jnp.where(grp_e == gi, 1.0, allowed)
        gs = jnp.where(row_g == gi, neg, gs)
    mb = jnp.where(allowed > 0.0, biased, neg)
    row_k = lax.broadcasted_iota(jnp.int32, (TOP_K, tm), 0)
    eacc = jnp.zeros((TOP_K, tm), F32)
    wacc = jnp.zeros((TOP_K, tm), F32)
    chosen = jnp.zeros((ne, tm), F32)
    for k in range(TOP_K):
        _, ik = first_argmax(mb, row_e, float(ne))
        sel = row_e == ik
        wk = jnp.sum(jnp.where(sel, scores, 0.0), axis=0, keepdims=True)
        eacc = jnp.where(row_k == k, ik, eacc)
        wacc = jnp.where(row_k == k, wk, wacc)
        chosen = jnp.where(sel, 1.0, chosen)
        mb = jnp.where(sel, neg, mb)
    wsum = jnp.sum(wacc, axis=0, keepdims=True)
    e_ref[...] = eacc.astype(jnp.int32)
    w_ref[...] = wacc / (wsum + 1e-20) * ROUTED_SCALE
    chosen_b = chosen.astype(BF16)
    m_ref[...] = chosen_b

    @pl.when(pl.program_id(0) == 0)
    def _():
        cnt_ref[...] = jnp.zeros_like(cnt_ref)

    cnt_ref[...] += jnp.dot(chosen_b, jnp.ones((tm, LANES), BF16), preferred_element_type=F32)


def _moe_pre(xp1, xs1, sc_rows, sh_rows, g, wr, rb, swg, swu, swd):
    tp, d = xp1.shape
    ts = xs1.shape[0]
    t = tp + ts
    ne = wr.shape[1]
    ff = swg.shape[1]
    tm = _row_tile(ts, (1024, 512, 256))
    assert tp % tm == 0 and tm % (SUBLANES * MOD_ROWS) == 0
    ntp = tp // tm
    mr = tm // MOD_ROWS
    nsub = _token_rows(d)
    row = lambda i: (i, 0)
    col = lambda i: (0, i)
    const = lambda i: (0, 0)
    rb_col = jnp.broadcast_to(rb.reshape(ne, 1), (ne, LANES))
    return pl.pallas_call(
        functools.partial(_moe_pre_kernel, ntp=ntp),
        grid=(t // tm,),
        in_specs=_two_part_specs(tm, d, ntp) + [
                  pl.BlockSpec((mr, d), row), pl.BlockSpec((mr, d), row),
                  pl.BlockSpec((1, d), const), pl.BlockSpec((ne, d), const), pl.BlockSpec((ne, LANES), const),
                  pl.BlockSpec((d, ff), const), pl.BlockSpec((d, ff), const), pl.BlockSpec((ff, d), const)],
        out_specs=[pl.BlockSpec((tm * nsub, LANES), row), pl.BlockSpec((tm, d), row),
                   pl.BlockSpec((TOP_K, tm), col), pl.BlockSpec((TOP_K, tm), col),
                   pl.BlockSpec((ne, tm), col), pl.BlockSpec((ne, LANES), const)],
        out_shape=[jax.ShapeDtypeStruct((t * nsub, LANES), ROW_DTYPE), jax.ShapeDtypeStruct((t, d), F32),
                   jax.ShapeDtypeStruct((TOP_K, t), jnp.int32), jax.ShapeDtypeStruct((TOP_K, t), F32),
                   jax.ShapeDtypeStruct((ne, t), BF16), jax.ShapeDtypeStruct((ne, LANES), F32)],
        compiler_params=pltpu.CompilerParams(
            dimension_semantics=("arbitrary",), vmem_limit_bytes=VMEM_LIMIT),
        name="moe_pre",
    )(xp1, xs1, sc_rows, sh_rows, g.reshape(1, d), wr.T, rb_col, swg, swu, swd)


ROW_DTYPE = F32


def _token_rows(d):
    return d // LANES


def _read_rows(buf, nrows, nsub):
    return jnp.concatenate([buf[pl.ds(c, nrows, stride=nsub), :] for c in range(nsub)], axis=-1)


def _write_rows(ref, v):
    nrows, width = v.shape
    nsub = _token_rows(width)
    for c in range(nsub):
        ref[pl.ds(c, nrows, stride=nsub), :] = v[:, c * LANES:(c + 1) * LANES]


def _rank_kernel(m_ref, e_ref, ps_ref, d_ref, carry):
    i = pl.program_id(0)

    @pl.when(i == 0)
    def _():
        carry[...] = jnp.zeros_like(carry)

    m = m_ref[...]
    ne, tr = m.shape
    r_i = lax.broadcasted_iota(jnp.int32, (tr, tr), 0)
    c_i = lax.broadcasted_iota(jnp.int32, (tr, tr), 1)
    earlier = (r_i < c_i).astype(BF16)
    base = carry[...] + ps_ref[...]
    slot = jnp.dot(m, earlier, preferred_element_type=F32) + jnp.concatenate([base] * (tr // LANES), axis=1)
    row_e = lax.broadcasted_iota(jnp.int32, (ne, tr), 0)
    d_rows = [jnp.sum(jnp.where(row_e == e_ref[k:k + 1, :], slot, 0.0), axis=0, keepdims=True)
              for k in range(TOP_K)]
    d_ref[...] = jnp.concatenate(d_rows, axis=0).astype(jnp.int32)
    carry[...] += jnp.dot(m, jnp.ones((tr, LANES), BF16), preferred_element_type=F32)


def _rank(mask_t, eidx_t, pstart):
    ne, t = mask_t.shape
    tr = _row_tile(t, (512, 256))
    col = lambda i: (0, i)
    return pl.pallas_call(
        _rank_kernel,
        grid=(t // tr,),
        in_specs=[pl.BlockSpec((ne, tr), col), pl.BlockSpec((TOP_K, tr), col),
                  pl.BlockSpec((ne, LANES), lambda i: (0, 0))],
        out_specs=pl.BlockSpec((TOP_K, tr), col),
        out_shape=jax.ShapeDtypeStruct((TOP_K, t), jnp.int32),
        scratch_shapes=[pltpu.VMEM((ne, LANES), F32)],
        compiler_params=pltpu.CompilerParams(
            dimension_semantics=("arbitrary",), vmem_limit_bytes=VMEM_LIMIT),
        name="rank",
    )(mask_t, eidx_t, pstart)


def _dispatch_kernel(ps_ref, pl_ref, nu_ref, d_ref, h_hbm, xs_hbm, zbuf, stage, lsem, sem, zsem,
                     *, tt, nsub, bm, nb):
    i = pl.program_id(0)
    n = pl.num_programs(0)
    ne = ps_ref.shape[0]
    pieces = [1 << b for b in reversed(range(bm.bit_length() - 1))]

    def zero_fill(op):
        def pad(e, carry):
            start = ps_ref[e]
            length = pl_ref[e]
            for p in pieces:
                off = length - lax.rem(length, 2 * p)

                @pl.when(lax.rem(length, 2 * p) >= p)
                def _():
                    dst = xs_hbm.at[pl.ds(pl.multiple_of((start + off) * nsub, nsub), p * nsub)]
                    op(pltpu.make_async_copy(zbuf.at[pl.ds(0, p * nsub)], dst, zsem))
            return carry
        lax.fori_loop(0, ne, pad, 0)

        def tail(b, carry):
            dst = xs_hbm.at[pl.ds(pl.multiple_of(b * (bm * nsub), bm * nsub), bm * nsub)]
            op(pltpu.make_async_copy(zbuf, dst, zsem))
            return carry
        lax.fori_loop(nu_ref[0], nb, tail, 0)

    stages = stage.shape[0]
    rows = tt * nsub

    def load(tile):
        src = h_hbm.at[pl.ds(pl.multiple_of(tile * rows, rows), rows)]
        return pltpu.make_async_copy(src, stage.at[lax.rem(tile, stages)], lsem.at[lax.rem(tile, stages)])

    def wait_rows(tile):
        s = lax.rem(tile, stages)
        for _ in range(TOP_K):
            pltpu.make_async_copy(stage.at[s], xs_hbm.at[pl.ds(0, rows)], sem.at[s]).wait()

    @pl.when(i == 0)
    def _():
        zbuf[...] = jnp.zeros_like(zbuf)
        zero_fill(lambda c: c.start())
        load(0).start()

    @pl.when(i >= stages - 1)
    def _():
        wait_rows(i - (stages - 1))

    @pl.when(i + 1 < n)
    def _():
        load(i + 1).start()

    load(i).wait()
    s = lax.rem(i, stages)

    def body(r2, carry):
        for u in range(ROW_UNROLL):
            r = r2 * ROW_UNROLL + u
            src = stage.at[s, pl.ds(pl.multiple_of(r * nsub, nsub), nsub)]
            for k in range(TOP_K):
                dst = xs_hbm.at[pl.ds(pl.multiple_of(d_ref[0, 0, r * TOP_K + k] * nsub, nsub), nsub)]
                pltpu.make_async_copy(src, dst, sem.at[s]).start(priority=k % 2)
        return carry

    lax.fori_loop(0, tt // ROW_UNROLL, body, 0)

    @pl.when(i == n - 1)
    def _():
        for back in range(stages - 2, -1, -1):
            @pl.when(i >= back)
            def _():
                wait_rows(i - back)
        zero_fill(lambda c: c.wait())


def _dispatch(dest, h2_tiles, pad_start, pad_len, n_used, nb, nsub):
    t = dest.shape[0]
    tt = _row_tile(t, (256,))
    nt = t // tt
    bm = EXPERT_ROWS
    d3 = dest.reshape(nt, 1, tt * TOP_K)
    gs = pltpu.PrefetchScalarGridSpec(
        num_scalar_prefetch=3,
        grid=(nt,),
        in_specs=[pl.BlockSpec((1, 1, tt * TOP_K), lambda i, *_: (i, 0, 0), memory_space=pltpu.SMEM),
                  pl.BlockSpec(memory_space=pl.ANY)],
        out_specs=pl.BlockSpec(memory_space=pl.ANY),
        scratch_shapes=[pltpu.VMEM((bm * nsub, LANES), ROW_DTYPE),
                        pltpu.VMEM((DISPATCH_STAGES, tt * nsub, LANES), ROW_DTYPE),
                        pltpu.SemaphoreType.DMA((DISPATCH_STAGES,)),
                        pltpu.SemaphoreType.DMA((DISPATCH_STAGES,)),
                        pltpu.SemaphoreType.DMA],
    )
    return pl.pallas_call(
        functools.partial(_dispatch_kernel, tt=tt, nsub=nsub, bm=bm, nb=nb),
        grid_spec=gs,
        out_shape=jax.ShapeDtypeStruct((nb * bm * nsub, LANES), ROW_DTYPE),
        compiler_params=pltpu.CompilerParams(dimension_semantics=("arbitrary",)),
        name="dispatch",
    )(pad_start, pad_len, n_used, d3, h2_tiles)


def _grouped_kernel(b0_ref, nk_ref, nu_ref, x_hbm, wg_ref, wu_ref, wd_ref, y_hbm,
                    xbuf, ybuf, zbuf, xsem, ysem, zsem, wgb, wub, wdb, *, bm, nsub, nb):
    e = pl.program_id(0)
    ring = xbuf.shape[0]
    rows = bm * nsub
    nu = nu_ref[0]

    def x_copy(gb, slot):
        src = x_hbm.at[pl.ds(pl.multiple_of(gb * rows, rows), rows)]
        return pltpu.make_async_copy(src, xbuf.at[slot], xsem.at[slot])

    def y_copy(gb, slot):
        dst = y_hbm.at[pl.ds(pl.multiple_of(gb * rows, rows), rows)]
        return pltpu.make_async_copy(ybuf.at[slot], dst, ysem.at[slot])

    def tail_copy(gb):
        dst = y_hbm.at[pl.ds(pl.multiple_of(gb * rows, rows), rows)]
        return pltpu.make_async_copy(zbuf, dst, zsem)

    def for_tail(op):
        def body(gb, carry):
            op(tail_copy(gb))
            return carry
        lax.fori_loop(nu, nb, body, 0)

    @pl.when(e == 0)
    def _():
        for g0 in range(ring - 1):
            @pl.when(g0 < nu)
            def _():
                x_copy(g0, g0).start()
        zbuf[...] = jnp.zeros_like(zbuf)
        for_tail(lambda c: c.start())

    @pl.when(nk_ref[e] > 0)
    def _():
        wgb[...] = wg_ref[0].astype(BF16)
        wub[...] = wu_ref[0].astype(BF16)
        wdb[...] = wd_ref[0].astype(BF16)

        def block(b, carry):
            gb = b0_ref[e] + b
            slot = lax.rem(gb, ring)

            @pl.when(gb + ring - 1 < nu)
            def _():
                x_copy(gb + ring - 1, lax.rem(gb + ring - 1, ring)).start()

            x_copy(gb, slot).wait()

            @pl.when(gb >= ring)
            def _():
                y_copy(gb - ring, slot).wait()

            x = _read_rows(xbuf.at[slot], bm, nsub).astype(BF16)
            hg = jnp.dot(x, wgb[...], preferred_element_type=F32)
            hu = jnp.dot(x, wub[...], preferred_element_type=F32)
            y = jnp.dot((_silu(hg) * hu).astype(BF16), wdb[...], preferred_element_type=F32)
            _write_rows(ybuf.at[slot], y)
            y_copy(gb, slot).start()
            return carry

        lax.fori_loop(0, nk_ref[e], block, 0)

    @pl.when(e == pl.num_programs(0) - 1)
    def _():
        for back in range(ring, 0, -1):
            @pl.when(nu >= back)
            def _():
                y_copy(nu - back, lax.rem(nu - back, ring)).wait()

        for_tail(lambda c: c.wait())


def _grouped(xs, blk_start, blk_count, n_used, wg, wu, wd):
    ne, d, ff = wg.shape
    nsub = _token_rows(d)
    bm = EXPERT_ROWS
    nb = xs.shape[0] // (bm * nsub)
    wspec = lambda shape: pl.BlockSpec(shape, lambda e, *_: (e, 0, 0))
    gs = pltpu.PrefetchScalarGridSpec(
        num_scalar_prefetch=3,
        grid=(ne,),
        in_specs=[pl.BlockSpec(memory_space=pl.ANY),
                  wspec((1, d, ff)), wspec((1, d, ff)), wspec((1, ff, d))],
        out_specs=pl.BlockSpec(memory_space=pl.ANY),
        scratch_shapes=[pltpu.VMEM((GROUP_RING, bm * nsub, LANES), ROW_DTYPE),
                        pltpu.VMEM((GROUP_RING, bm * nsub, LANES), ROW_DTYPE),
                        pltpu.VMEM((bm * nsub, LANES), ROW_DTYPE),
                        pltpu.SemaphoreType.DMA((GROUP_RING,)), pltpu.SemaphoreType.DMA((GROUP_RING,)),
                        pltpu.SemaphoreType.DMA,
                        pltpu.VMEM((d, ff), BF16), pltpu.VMEM((d, ff), BF16), pltpu.VMEM((ff, d), BF16)],
    )
    return pl.pallas_call(
        functools.partial(_grouped_kernel, bm=bm, nsub=nsub, nb=nb),
        grid_spec=gs,
        out_shape=jax.ShapeDtypeStruct(xs.shape, xs.dtype),
        compiler_params=pltpu.CompilerParams(
            dimension_semantics=("arbitrary",), vmem_limit_bytes=VMEM_LIMIT),
        name="grouped",
    )(blk_start, blk_count, n_used, xs, wg, wu, wd)


def _combine_kernel(d_ref, dn_ref, y_hbm, w_ref, s_ref, xp_ref, xs_ref, g2_ref, fg_ref, op_ref, os_ref,
                    buf_a, buf_b, obuf, sem_a, sem_b, *, tt, nsub, ntp):
    i = pl.program_id(0)
    n = pl.num_programs(0)

    def issue(dref, buf, sem, r):
        for k in range(TOP_K):
            src = y_hbm.at[pl.ds(pl.multiple_of(dref[0, 0, r * TOP_K + k] * nsub, nsub), nsub)]
            pltpu.make_async_copy(src, buf.at[k, pl.ds(pl.multiple_of(r * nsub, nsub), nsub)],
                                  sem).start(priority=k % 2)

    def wait_tile(buf, sem):
        for k in range(TOP_K):
            pltpu.make_async_copy(y_hbm.at[pl.ds(0, tt * nsub)], buf.at[k], sem).wait()

    @pl.when(i == 0)
    def _():
        def first(r, carry):
            issue(d_ref, buf_a, sem_a, r)
            return carry
        lax.fori_loop(0, tt, first, 0)

    is_prompt = i < ntp

    def step(cur, cur_sem, nxt, nxt_sem):
        wait_tile(cur, cur_sem)

        def chunk(c, carry):
            r0 = pl.multiple_of(c * SUBLANES, SUBLANES)
            for u in range(SUBLANES):
                issue(dn_ref, nxt, nxt_sem, r0 + u)
            acc = s_ref[pl.ds(r0, SUBLANES), :]
            w8 = w_ref[pl.ds(r0, SUBLANES), :]
            for k in range(TOP_K):
                rows = jnp.concatenate(
                    [cur[k, pl.ds(r0 * nsub + cc, SUBLANES, stride=nsub), :] for cc in range(nsub)], axis=-1)
                acc = acc + w8[:, k:k + 1] * rows
            g2 = g2_ref[pl.ds(c // (MOD_ROWS // SUBLANES), 1), :]
            x1 = jnp.where(is_prompt, xp_ref[pl.ds(r0, SUBLANES), :], xs_ref[pl.ds(r0, SUBLANES), :])
            obuf[pl.ds(r0, SUBLANES), :] = _rms(x1 + acc * g2, fg_ref[...], RMS_EPS)
            return carry

        lax.fori_loop(0, tt // SUBLANES, chunk, 0)

        @pl.when(i == n - 1)
        def _():
            wait_tile(nxt, nxt_sem)

    @pl.when(lax.rem(i, 2) == 0)
    def _():
        step(buf_a, sem_a, buf_b, sem_b)

    @pl.when(lax.rem(i, 2) == 1)
    def _():
        step(buf_b, sem_b, buf_a, sem_a)

    @pl.when(is_prompt)
    def _():
        op_ref[...] = obuf[...]

    @pl.when(jnp.logical_not(is_prompt))
    def _():
        os_ref[...] = obuf[...]


def _combine(dest, y_tiles, wts, shared, xp1, xs1, g2_rows, final_g):
    tp, d = xp1.shape
    ts = xs1.shape[0]
    nsub = _token_rows(d)
    tt = _row_tile(ts, (256,))
    assert tp % tt == 0 and tt % (SUBLANES * MOD_ROWS) == 0
    ntp = tp // tt
    mr = tt // MOD_ROWS
    nt = (tp + ts) // tt
    d3 = dest.reshape(nt, 1, tt * TOP_K)
    kern = functools.partial(_combine_kernel, tt=tt, nsub=nsub, ntp=ntp)
    smem_blk = lambda f: pl.BlockSpec((1, 1, tt * TOP_K), f, memory_space=pltpu.SMEM)
    row = lambda i: (i, 0)
    two = _two_part_specs(tt, d, ntp)
    return pl.pallas_call(
        kern,
        grid=(nt,),
        in_specs=[smem_blk(lambda i: (i, 0, 0)),
                  smem_blk(lambda i: (jnp.minimum(i + 1, nt - 1), 0, 0)),
                  pl.BlockSpec(memory_space=pl.ANY),
                  pl.BlockSpec((tt, LANES), row), pl.BlockSpec((tt, d), row)] + two + [
                  pl.BlockSpec((mr, d), row), pl.BlockSpec((1, d), lambda i: (0, 0))],
        out_specs=two,
        out_shape=[jax.ShapeDtypeStruct((tp, d), F32), jax.ShapeDtypeStruct((ts, d), F32)],
        scratch_shapes=[pltpu.VMEM((TOP_K, tt * nsub, LANES), ROW_DTYPE),
                        pltpu.VMEM((TOP_K, tt * nsub, LANES), ROW_DTYPE), pltpu.VMEM((tt, d), F32),
                        pltpu.SemaphoreType.DMA, pltpu.SemaphoreType.DMA],
        compiler_params=pltpu.CompilerParams(
            dimension_semantics=("arbitrary",), vmem_limit_bytes=VMEM_LIMIT),
        name="combine",
    )(d3, d3, y_tiles, wts, shared, xp1, xs1, g2_rows, final_g.reshape(1, d))


def _plan(counts, n_assign):
    ne = counts.shape[0]
    bm = EXPERT_ROWS
    counts = counts.astype(jnp.int32)
    nblk = (counts + bm - 1) // bm
    bend = jnp.cumsum(nblk)
    pstart = jnp.broadcast_to(((bend - nblk) * bm).astype(F32).reshape(ne, 1), (ne, LANES))
    nb = -(-(n_assign + ne * (bm - 1)) // bm)
    n_used = bend[-1:].astype(jnp.int32)
    pad_start = ((bend - nblk) * bm + counts).astype(jnp.int32)
    pad_len = (nblk * bm - counts).astype(jnp.int32)
    return pstart, (bend - nblk).astype(jnp.int32), nblk, n_used, pad_start, pad_len, nb


def per_g_pow2(ne):
    per_g = ne // N_EXPERT_GROUPS
    return per_g * N_EXPERT_GROUPS == ne and per_g & (per_g - 1) == 0


def _mod_rows(m, nbp, lp):
    return jnp.concatenate([jnp.repeat(m[:nbp], lp // MOD_ROWS, axis=0), m[nbp:]], axis=0)


def kernel(x_prompt, x_sample, state_ssm, state_conv, state_pool, c_prompt, c_sample, ln1_g, ln2_g, w_ada, b_ada, w_in, conv_w, conv_b, dt_bias, a_log, d_skip, ssd_norm_g, w_ssd_out, pool_w, pool_scale, w_out, w_router, router_bias, moe_w_gate, moe_w_up, moe_w_down, shared_w_gate, shared_w_up, shared_w_down, final_g):
    bp, lp, dm = x_prompt.shape
    bs, ls, _ = x_sample.shape
    depth = ln1_g.shape[0]
    heads, hd, ns = state_ssm.shape[2], state_ssm.shape[3], state_ssm.shape[4]
    di = heads * hd
    gn = SSD_GROUPS * ns
    cch = di + 2 * gn
    assert depth == 1 and ls == MOD_ROWS and lp % MOD_ROWS == 0 and heads <= LANES
    assert per_g_pow2(w_router.shape[2])
    assert ns == 2 * hd and hd & (hd - 1) == 0 and (heads // SSD_GROUPS) % 2 == 0
    assert all(w & (w - 1) == 0 for w in POOL_WINDOWS) and state_pool.shape[2] == max(POOL_WINDOWS) - 1
    tp, ts = bp * lp, bs * ls
    expand3 = (jnp.arange(3 * LANES)[:, None] % LANES == jnp.arange(di)[None, :] // hd).astype(BF16)

    xp, xs = x_prompt.reshape(tp, dm), x_sample.reshape(ts, dm)
    c_all = jnp.concatenate([c_prompt, c_sample], axis=0)

    o1, o2, o3, o4 = di, di + cch, di + cch + heads, di + cch + heads + dm
    cols = (0, di, 2 * di, di + cch, di + cch + dm)

    ssm_p, conv_p, pool_p, ssm_s, conv_s, pool_s = [], [], [], [], [], []
    for l in range(depth):
        wi = w_in[l]
        wcat = jnp.concatenate(
            [wi[:, :o1], wi[:, o1:o2], wi[:, o3:o4], wi[:, o4:],
             jnp.pad(wi[:, o2:o3], ((0, 0), (0, LANES - heads)))], axis=1).astype(BF16)
        pad_h = lambda v: jnp.pad(v.reshape(1, heads), ((0, 0), (0, LANES - heads)))
        seq_w = (conv_w[l], conv_b[l].reshape(1, cch), pad_h(dt_bias[l]), pad_h(a_log[l]),
                 jnp.repeat(d_skip[l], hd).reshape(1, di), ssd_norm_g[l].reshape(1, di),
                 w_ssd_out[l].astype(BF16), pool_w[l].astype(BF16), pool_scale[l].reshape(1, dm),
                 w_out[l].astype(BF16), expand3)

        mod = _ada(c_all, w_ada[l], b_ada[l])
        sh1, sc1, g1, sh2, sc2, g2 = jnp.split(mod, 6, axis=-1)

        proj, dtp = _inproj(xp, xs, _mod_rows(sc1, bp, lp), _mod_rows(sh1, bp, lp), ln1_g[l], wcat)

        zc = jnp.zeros((bp,) + state_conv.shape[2:], F32)
        zs = jnp.zeros((bp, heads, hd, ns), F32)
        zp = jnp.zeros((bp,) + state_pool.shape[2:], F32)
        xp1, ns_p, nc_p, np_p = _seq(proj, dtp, xp, g1[:bp], zc, zs, zp, seq_w,
                                     row0=0, nb=bp, seqlen=lp, pos0=0, cols=cols)
        xs1, ns_s, nc_s, np_s = _seq(proj, dtp, xs, g1[bp:], state_conv[l], state_ssm[l], state_pool[l],
                                     seq_w, row0=tp, nb=bs, seqlen=ls, pos0=PAST_LEN, cols=cols)
        conv_p.append(nc_p)
        conv_s.append(nc_s)
        pool_p.append(np_p)
        pool_s.append(np_s)
        ssm_p.append(ns_p)
        ssm_s.append(ns_s)

        h2_tiles, shared, eidx_t, wts_t, mask_t, counts = _moe_pre(
            xp1, xs1, _mod_rows(sc2, bp, lp), _mod_rows(sh2, bp, lp), ln2_g[l],
            w_router[l].astype(BF16), router_bias[l],
            shared_w_gate[l].astype(BF16), shared_w_up[l].astype(BF16), shared_w_down[l].astype(BF16))
        nsub = _token_rows(dm)
        pstart, blk_start, blk_count, n_used, pad_start, pad_len, nb = _plan(counts[:, 0], (tp + ts) * TOP_K)
        dest = _rank(mask_t, eidx_t, pstart).T
        wts = jnp.pad(wts_t.T, ((0, 0), (0, LANES - TOP_K)))
        x_sorted = _dispatch(dest, h2_tiles, pad_start, pad_len, n_used, nb, nsub)
        y_tiles = _grouped(x_sorted, blk_start, blk_count, n_used,
                           moe_w_gate[l], moe_w_up[l], moe_w_down[l])
        xp, xs = _combine(dest, y_tiles, wts, shared, xp1, xs1, _mod_rows(g2, bp, lp), final_g)

    y_prompt = xp.reshape(bp, lp, dm)
    y_sample = xs.reshape(bs, ls, dm)
    return (y_prompt, y_sample, jnp.stack(ssm_p), jnp.stack(conv_p), jnp.stack(pool_p),
            jnp.stack(ssm_s), jnp.stack(conv_s), jnp.stack(pool_s))
```
